```python
import math
import jax, jax.numpy as jnp
from jax import lax
import numpy as np

D_MODEL = 1024
BATCH = 4
SEQ = 4096
DEPTH = 1

D_MIX = D_MODEL
D_LRU = D_MIX // 2
D_ATTN = D_MIX - D_LRU
N_LRU_BLOCKS = 8
LRU_BLOCK = D_LRU // N_LRU_BLOCKS
CONV_WIDTH = 4
LRU_C = 8.0
ATTN_HEAD_DIM = 64
N_ATTN_HEADS = D_ATTN // (2 * ATTN_HEAD_DIM)
V_HEAD_DIM = 2 * ATTN_HEAD_DIM
D_FF = 4 * D_MODEL
Q_BLOCK = 128
EPS = 1e-6
D_IN = 3 * D_ATTN + 2 * D_LRU

kernel_name = "hybrid_diffattn_rglru_block"


def _alibi_slopes(n_heads):
    return np.array([2.0 ** (-8.0 * (i + 1) / n_heads) for i in range(n_heads)], dtype=np.float32)


def rmsnorm(x, g):
    xf = x.astype(jnp.float32)
    y = xf * lax.rsqrt(jnp.mean(xf * xf, axis=-1, keepdims=True) + EPS)
    return (y * g.astype(jnp.float32)).astype(x.dtype)


def causal_depthwise_conv(u, w, b):
    S = u.shape[1]
    up = jnp.pad(u, ((0, 0), (CONV_WIDTH - 1, 0), (0, 0)))
    out = b
    for k in range(CONV_WIDTH):
        out = out + up[:, k:k + S] * w[k]
    return out


def rg_lru(u, w_rg, b_rg, w_ig, b_ig, lru_L):
    B, S, _ = u.shape
    ub = u.reshape(B, S, N_LRU_BLOCKS, LRU_BLOCK)
    r = jax.nn.sigmoid(jnp.einsum('bsnc,ncd->bsnd', ub, w_rg) + b_rg).reshape(B, S, D_LRU)
    i = jax.nn.sigmoid(jnp.einsum('bsnc,ncd->bsnd', ub, w_ig) + b_ig).reshape(B, S, D_LRU)
    log_a = LRU_C * r.astype(jnp.float32) * jax.nn.log_sigmoid(lru_L.astype(jnp.float32))
    a = jnp.exp(log_a)
    mult = jnp.sqrt(-jnp.expm1(2.0 * log_a))
    bx = mult * (i * u).astype(jnp.float32)

    def combine(c1, c2):
        a1, b1 = c1
        a2, b2 = c2
        return a1 * a2, a2 * b1 + b2

    _, h = lax.associative_scan(combine, (a, bx), axis=1)
    return h.astype(u.dtype)


def diff_attention(q, k, v, lam, subln_g, lambda_init):
    B, S = q.shape[0], q.shape[1]
    H, d = N_ATTN_HEADS, ATTN_HEAD_DIM
    nb = S // Q_BLOCK
    qb = (q * (d ** -0.5)).reshape(B, nb, Q_BLOCK, H, 2, d).transpose(1, 0, 3, 4, 2, 5)
    kt = k.transpose(0, 2, 3, 1, 4)
    vt = v.transpose(0, 2, 1, 3)
    slopes = jnp.asarray(_alibi_slopes(H))[:, None, None, None]
    kpos = jnp.arange(S)

    def one_block(args):
        blk, qblk = args
        qpos = blk * Q_BLOCK + jnp.arange(Q_BLOCK)
        s = jnp.einsum('bhiqd,bhikd->bhiqk', qblk, kt).astype(jnp.float32)
        dist = (qpos[:, None] - kpos[None, :]).astype(jnp.float32)
        s = jnp.where(dist >= 0, s - slopes * dist, -jnp.inf)
        p = jax.nn.softmax(s, axis=-1)
        p = p[:, :, 0] - lam * p[:, :, 1]
        return jnp.einsum('bhqk,bhkd->bhqd', p.astype(v.dtype), vt)

    o = lax.map(one_block, (jnp.arange(nb), qb))
    o = o.transpose(1, 0, 3, 2, 4).reshape(B, S, H, V_HEAD_DIM)
    o = rmsnorm(o, subln_g) * (1.0 - lambda_init)
    return o.reshape(B, S, H * V_HEAD_DIM)


def setup_inputs(seed: int = 0) -> dict:
    key = jax.random.key(seed)
    ks = jax.random.split(key, 24)
    f32 = jnp.float32
    L = DEPTH

    def nrm(k, shape, scale):
        return jax.random.normal(k, shape, f32) * scale

    a0 = jax.random.uniform(ks[12], (L, D_LRU), f32, 0.9, 0.999)
    return {
        "x": jax.random.normal(ks[0], (BATCH, SEQ, D_MODEL), f32),
        "norm_mix_g": 1.0 + nrm(ks[1], (L, D_MODEL), 0.02),
        "w_in": nrm(ks[2], (L, D_MODEL, D_IN), D_MODEL ** -0.5),
        "conv_w": nrm(ks[3], (L, CONV_WIDTH, D_LRU), CONV_WIDTH ** -0.5),
        "conv_b": nrm(ks[4], (L, D_LRU), 0.02),
        "w_rg": nrm(ks[5], (L, N_LRU_BLOCKS, LRU_BLOCK, LRU_BLOCK), LRU_BLOCK ** -0.5),
        "b_rg": nrm(ks[6], (L, N_LRU_BLOCKS, LRU_BLOCK), 0.02),
        "w_ig": nrm(ks[7], (L, N_LRU_BLOCKS, LRU_BLOCK, LRU_BLOCK), LRU_BLOCK ** -0.5),
        "b_ig": nrm(ks[8], (L, N_LRU_BLOCKS, LRU_BLOCK), 0.02),
        "lru_L": jnp.log(a0) - jnp.log1p(-a0),
        "lambda_q1": nrm(ks[13], (L, ATTN_HEAD_DIM), 0.1),
        "lambda_k1": nrm(ks[14], (L, ATTN_HEAD_DIM), 0.1),
        "lambda_q2": nrm(ks[15], (L, ATTN_HEAD_DIM), 0.1),
        "lambda_k2": nrm(ks[16], (L, ATTN_HEAD_DIM), 0.1),
        "subln_g": 1.0 + nrm(ks[17], (L, V_HEAD_DIM), 0.02),
        "w_out": nrm(ks[18], (L, D_MIX, D_MODEL), D_MIX ** -0.5),
        "norm_mlp_g": 1.0 + nrm(ks[19], (L, D_MODEL), 0.02),
        "w_up": nrm(ks[20], (L, D_MODEL, D_FF), D_MODEL ** -0.5),
        "w_down": nrm(ks[21], (L, D_FF, D_MODEL), D_FF ** -0.5),
        "final_g": 1.0 + nrm(ks[22], (D_MODEL,), 0.02),
    }


def reference(x, norm_mix_g, w_in, conv_w, conv_b, w_rg, b_rg, w_ig, b_ig, lru_L,
              lambda_q1, lambda_k1, lambda_q2, lambda_k2, subln_g, w_out,
              norm_mlp_g, w_up, w_down, final_g):
    B, S, _ = x.shape
    H, d = N_ATTN_HEADS, ATTN_HEAD_DIM
    for l in range(DEPTH):
        h = rmsnorm(x, norm_mix_g[l])
        proj = h @ w_in[l]
        q, k, v, lru_u, lru_g = jnp.split(
            proj, [D_ATTN, 2 * D_ATTN, 3 * D_ATTN, 3 * D_ATTN + D_LRU], axis=-1)

        lambda_init = 0.8 - 0.6 * math.exp(-0.3 * l)
        lam = (jnp.exp(jnp.sum(lambda_q1[l].astype(jnp.float32) * lambda_k1[l].astype(jnp.float32)))
               - jnp.exp(jnp.sum(lambda_q2[l].astype(jnp.float32) * lambda_k2[l].astype(jnp.float32)))
               + lambda_init)
        attn_out = diff_attention(q.reshape(B, S, H, 2, d), k.reshape(B, S, H, 2, d),
                                  v.reshape(B, S, H, V_HEAD_DIM), lam, subln_g[l], lambda_init)

        u = causal_depthwise_conv(lru_u, conv_w[l], conv_b[l])
        lru_out = rg_lru(u, w_rg[l], b_rg[l], w_ig[l], b_ig[l], lru_L[l]) * jax.nn.gelu(lru_g)

        mix = jnp.concatenate([attn_out, lru_out], axis=-1)
        x = x + mix @ w_out[l]

        hm = rmsnorm(x, norm_mlp_g[l])
        x = x + jnp.square(jax.nn.relu(hm @ w_up[l])) @ w_down[l]
    return rmsnorm(x, final_g)
```

```python
import functools
import math

import numpy as np
import jax
import jax.numpy as jnp
from jax import lax
from jax.experimental import pallas as pl
from jax.experimental.pallas import tpu as pltpu

EPS = 1e-6
N_LRU_BLOCKS = 8
CONV_WIDTH = 4
LRU_C = 8.0
HEAD_DIM = 64
V_HEAD_DIM = 2 * HEAD_DIM
LAMBDA_INIT = 0.8 - 0.6 * math.exp(-0.3 * 0)

V7X_MXU_COLS = 256
V7X_VMEM_LIMIT_BYTES = 56 * 1024 * 1024

ATTN_BLOCK = V7X_MXU_COLS
PROJ_TOKENS = 512
LRU_TOKENS = 256
MLP_TOKENS = 512
FF_CHUNK = 1024

_NT_DIMS = (((1,), (1,)), ((), ()))


def _bf16(x):
    return x.astype(jnp.bfloat16)


def _rmsnorm_rows(x, g):
    return x * lax.rsqrt(jnp.mean(x * x, axis=-1, keepdims=True) + EPS) * g


def _in_proj_kernel(x_ref, g_ref, wqt_ref, wk_ref, wvt_ref, wug_ref,
                    qt_ref, k_ref, vt_ref, u_ref, gate_ref):
    h = _bf16(_rmsnorm_rows(x_ref[...], g_ref[...]))
    k_ref[...] = _bf16(jnp.dot(h, wk_ref[...], preferred_element_type=jnp.float32))
    d_lru = u_ref.shape[-1]
    ug = jnp.dot(h, wug_ref[...], preferred_element_type=jnp.float32)
    u_ref[...] = ug[:, :d_lru]
    gate_ref[...] = ug[:, d_lru:]
    qt = _bf16(lax.dot_general(wqt_ref[...], h, _NT_DIMS, preferred_element_type=jnp.float32))
    vt = _bf16(lax.dot_general(wvt_ref[...], h, _NT_DIMS, preferred_element_type=jnp.float32))
    for j in range(qt_ref.shape[0]):
        qt_ref[j] = qt[:, j * ATTN_BLOCK:(j + 1) * ATTN_BLOCK]
        vt_ref[j] = vt[:, j * ATTN_BLOCK:(j + 1) * ATTN_BLOCK]


def _in_proj(x, g, wqt, wk, wvt, wug):
    B, S, D = x.shape
    d_attn = wk.shape[1]
    d_lru = wug.shape[1] // 2
    tm = PROJ_TOKENS
    nblk = tm // ATTN_BLOCK
    const = lambda b, i: (0, 0)
    return pl.pallas_call(
        _in_proj_kernel,
        grid=(B, S // tm),
        in_specs=[
            pl.BlockSpec((None, tm, D), lambda b, i: (b, i, 0)),
            pl.BlockSpec((1, D), const),
            pl.BlockSpec(wqt.shape, const),
            pl.BlockSpec(wk.shape, const),
            pl.BlockSpec(wvt.shape, const),
            pl.BlockSpec(wug.shape, const),
        ],
        out_specs=[
            pl.BlockSpec((None, nblk, d_attn, ATTN_BLOCK), lambda b, i: (b, i, 0, 0)),
            pl.BlockSpec((None, tm, d_attn), lambda b, i: (b, i, 0)),
            pl.BlockSpec((None, nblk, d_attn, ATTN_BLOCK), lambda b, i: (b, i, 0, 0)),
            pl.BlockSpec((None, tm, d_lru), lambda b, i: (b, i, 0)),
            pl.BlockSpec((None, tm, d_lru), lambda b, i: (b, i, 0)),
        ],
        out_shape=[
            jax.ShapeDtypeStruct((B, S // ATTN_BLOCK, d_attn, ATTN_BLOCK), jnp.bfloat16),
            jax.ShapeDtypeStruct((B, S, d_attn), jnp.bfloat16),
            jax.ShapeDtypeStruct((B, S // ATTN_BLOCK, d_attn, ATTN_BLOCK), jnp.bfloat16),
            jax.ShapeDtypeStruct((B, S, d_lru), jnp.float32),
            jax.ShapeDtypeStruct((B, S, d_lru), jnp.float32),
        ],
        compiler_params=pltpu.CompilerParams(
            dimension_semantics=("parallel", "parallel"),
            vmem_limit_bytes=V7X_VMEM_LIMIT_BYTES),
        name="in_proj",
    )(x, g, wqt, wk, wvt, wug)


def _diff_attn_kernel(qt_ref, k_ref, vt_ref, bias_ref, dbias_ref, lamv_ref, sg_ref, o_ref,
                      m_ref, l_ref, acc_ref, *, slopes):
    h = pl.program_id(1)
    qi = pl.program_id(2)
    blk = ATTN_BLOCK
    qt = qt_ref[...]
    row = lax.broadcasted_iota(jnp.int32, qt.shape, 0)
    zero = jnp.zeros_like(qt)
    q_maps = (jnp.where(row < HEAD_DIM, qt, zero), jnp.where(row >= HEAD_DIM, qt, zero))

    slope = jnp.float32(slopes[0])
    for i in range(1, len(slopes)):
        slope = jnp.where(h == i, jnp.float32(slopes[i]), slope)

    m_ref[...] = jnp.full(m_ref.shape, -jnp.inf, jnp.float32)
    l_ref[...] = jnp.zeros(l_ref.shape, jnp.float32)
    acc_ref[...] = jnp.zeros(acc_ref.shape, jnp.float32)

    def block_step(kb, bias):
        c = slope * (blk * (kb - qi)).astype(jnp.float32)
        start = pl.multiple_of(kb * blk, blk)
        k_blk = k_ref[pl.ds(start, blk), :]
        vt_blk = vt_ref[kb]
        for mp in range(2):
            s = jnp.dot(k_blk, q_maps[mp], preferred_element_type=jnp.float32) + bias
            m_old = m_ref[mp]
            m_new = jnp.maximum(m_old, jnp.max(s, axis=0, keepdims=True) + c)
            alpha = jnp.exp(m_old - m_new)
            p = jnp.exp(s - (m_new - c))
            l_ref[mp] = alpha * l_ref[mp] + jnp.sum(p, axis=0, keepdims=True)
            acc_ref[mp] = alpha * acc_ref[mp] + jnp.dot(
                vt_blk, _bf16(p), preferred_element_type=jnp.float32)
            m_ref[mp] = m_new

    def full_block(kb, carry):
        block_step(kb, bias_ref[...])
        return carry

    lax.fori_loop(0, qi, full_block, 0)
    block_step(qi, dbias_ref[...])

    lamv = lamv_ref[...]
    lam = (jnp.exp(jnp.sum(lamv[0:1] * lamv[1:2], axis=-1, keepdims=True))
           - jnp.exp(jnp.sum(lamv[2:3] * lamv[3:4], axis=-1, keepdims=True))
           + LAMBDA_INIT)
    o = acc_ref[0] / l_ref[0] - lam * (acc_ref[1] / l_ref[1])
    o = o * lax.rsqrt(jnp.mean(o * o, axis=0, keepdims=True) + EPS)
    o = o * sg_ref[...] * (1.0 - LAMBDA_INIT)
    o_ref[...] = _bf16(o.T)


def _alibi_slopes(n_heads):
    return [2.0 ** (-8.0 * (i + 1) / n_heads) for i in range(n_heads)]


def _alibi_block_bias(n_heads):
    slopes = np.asarray(_alibi_slopes(n_heads), np.float32)[:, None, None]
    j = np.arange(ATTN_BLOCK, dtype=np.float32)[None, :, None]
    i = np.arange(ATTN_BLOCK, dtype=np.float32)[None, None, :]
    bias = (-slopes * (i - j)).astype(np.float32)
    diag = np.where(i - j >= 0, bias, -np.inf).astype(np.float32)
    return jnp.asarray(bias), jnp.asarray(diag)


def _diff_attn(qt, k, vt, lamv, subln_g):
    B, nblk, d_attn, blk = qt.shape
    S = k.shape[1]
    H = d_attn // V_HEAD_DIM
    bias, dbias = _alibi_block_bias(H)
    kernel = functools.partial(_diff_attn_kernel, slopes=tuple(_alibi_slopes(H)))
    return pl.pallas_call(
        kernel,
        grid=(B, H, nblk),
        in_specs=[
            pl.BlockSpec((None, None, V_HEAD_DIM, blk), lambda b, h, q: (b, q, h, 0)),
            pl.BlockSpec((None, S, V_HEAD_DIM), lambda b, h, q: (b, 0, h)),
            pl.BlockSpec((None, nblk, V_HEAD_DIM, blk), lambda b, h, q: (b, 0, h, 0)),
            pl.BlockSpec((None, blk, blk), lambda b, h, q: (h, 0, 0)),
            pl.BlockSpec((None, blk, blk), lambda b, h, q: (h, 0, 0)),
            pl.BlockSpec(lamv.shape, lambda b, h, q: (0, 0)),
            pl.BlockSpec(subln_g.shape, lambda b, h, q: (0, 0)),
        ],
        out_specs=pl.BlockSpec((None, blk, V_HEAD_DIM), lambda b, h, q: (b, q, h)),
        out_shape=jax.ShapeDtypeStruct((B, S, d_attn), jnp.bfloat16),
        scratch_shapes=[
            pltpu.VMEM((2, 1, blk), jnp.float32),
            pltpu.VMEM((2, 1, blk), jnp.float32),
            pltpu.VMEM((2, V_HEAD_DIM, blk), jnp.float32),
        ],
        compiler_params=pltpu.CompilerParams(
            dimension_semantics=("parallel", "parallel", "arbitrary"),
            vmem_limit_bytes=V7X_VMEM_LIMIT_BYTES),
        name="diff_attn",
    )(qt, k, vt, bias, dbias, lamv, subln_g)


def _gelu_tanh(x):
    c = math.sqrt(2.0 / math.pi)
    return 0.5 * x * (1.0 + jnp.tanh(c * (x + 0.044715 * (x * x * x))))


def _log_sigmoid(x):
    return jnp.minimum(x, 0.0) - jnp.log1p(jnp.exp(-jnp.abs(x)))


def _rg_lru_kernel(u_ref, gate_ref, cw_ref, cb_ref, wg_ref, bg_ref, lru_l_ref, o_ref,
                   tail_ref, carry_ref):
    ts, d_lru = u_ref.shape

    @pl.when(pl.program_id(1) == 0)
    def _():
        tail_ref[...] = jnp.zeros(tail_ref.shape, jnp.float32)
        carry_ref[...] = jnp.zeros(carry_ref.shape, jnp.float32)

    u_raw = u_ref[...]
    ext = jnp.concatenate([tail_ref[...], u_raw], axis=0)
    tail_ref[...] = u_raw[ts - 8:, :]
    cw = cw_ref[...]
    u = cb_ref[...]
    for kk in range(CONV_WIDTH):
        off = 8 - (CONV_WIDTH - 1) + kk
        u = u + ext[off:off + ts, :] * cw[kk:kk + 1, :]

    gates = jnp.dot(_bf16(u), wg_ref[...], preferred_element_type=jnp.float32) + bg_ref[...]
    r = jax.nn.sigmoid(gates[:, :d_lru])
    i_gate = jax.nn.sigmoid(gates[:, d_lru:])
    log_a = LRU_C * r * _log_sigmoid(lru_l_ref[...])
    a = jnp.exp(log_a)
    mult = jnp.sqrt(-jnp.tanh(log_a) * (a * a + 1.0))
    b = mult * (i_gate * u)

    rows = lax.broadcasted_iota(jnp.int32, (ts, d_lru), 0)
    shift = 1
    while shift < ts:
        valid = rows >= shift
        a_prev = jnp.where(valid, pltpu.roll(a, shift, axis=0), 1.0)
        b_prev = jnp.where(valid, pltpu.roll(b, shift, axis=0), 0.0)
        b = a * b_prev + b
        a = a * a_prev
        shift *= 2
    hstate = b + a * carry_ref[0:1, :]
    carry_ref[...] = jnp.broadcast_to(hstate[ts - 1:ts, :], carry_ref.shape)
    o_ref[...] = _bf16(hstate * _gelu_tanh(gate_ref[...]))


def _rg_lru(u, gate, conv_w, conv_b, w_gates, b_gates, lru_l):
    B, S, d_lru = u.shape
    ts = LRU_TOKENS
    const = lambda b, i: (0, 0)
    return pl.pallas_call(
        _rg_lru_kernel,
        grid=(B, S // ts),
        in_specs=[
            pl.BlockSpec((None, ts, d_lru), lambda b, i: (b, i, 0)),
            pl.BlockSpec((None, ts, d_lru), lambda b, i: (b, i, 0)),
            pl.BlockSpec(conv_w.shape, const),
            pl.BlockSpec(conv_b.shape, const),
            pl.BlockSpec(w_gates.shape, const),
            pl.BlockSpec(b_gates.shape, const),
            pl.BlockSpec(lru_l.shape, const),
        ],
        out_specs=pl.BlockSpec((None, ts, d_lru), lambda b, i: (b, i, 0)),
        out_shape=jax.ShapeDtypeStruct((B, S, d_lru), jnp.bfloat16),
        scratch_shapes=[
            pltpu.VMEM((8, d_lru), jnp.float32),
            pltpu.VMEM((8, d_lru), jnp.float32),
        ],
        compiler_params=pltpu.CompilerParams(
            dimension_semantics=("parallel", "arbitrary"),
            vmem_limit_bytes=V7X_VMEM_LIMIT_BYTES),
        name="rg_lru",
    )(u, gate, conv_w, conv_b, w_gates, b_gates, lru_l)


def _out_mlp_kernel(x_ref, attn_ref, lru_ref, wo_ref, gm_ref, wup_ref, wdn_ref, gf_ref, y_ref):
    mix = jnp.concatenate([attn_ref[...], lru_ref[...]], axis=-1)
    x1 = x_ref[...] + jnp.dot(mix, wo_ref[...], preferred_element_type=jnp.float32)
    hm = _bf16(_rmsnorm_rows(x1, gm_ref[...]))
    d_ff = wup_ref.shape[1]
    mlp = jnp.zeros_like(x1)
    for c in range(d_ff // FF_CHUNK):
        cols = slice(c * FF_CHUNK, (c + 1) * FF_CHUNK)
        up = jnp.dot(hm, wup_ref[:, cols], preferred_element_type=jnp.float32)
        act = _bf16(jnp.square(jnp.maximum(up, 0.0)))
        mlp = mlp + jnp.dot(act, wdn_ref[cols, :], preferred_element_type=jnp.float32)
    y_ref[...] = _rmsnorm_rows(x1 + mlp, gf_ref[...])


def _out_mlp(x, attn, lru, w_out, g_mlp, w_up, w_down, g_final):
    B, S, D = x.shape
    T = B * S
    tm = MLP_TOKENS
    x2 = x.reshape(T, D)
    attn2 = attn.reshape(T, attn.shape[-1])
    lru2 = lru.reshape(T, lru.shape[-1])
    const = lambda i: (0, 0)
    resident = pl.Buffered(1)
    y = pl.pallas_call(
        _out_mlp_kernel,
        grid=(T // tm,),
        in_specs=[
            pl.BlockSpec((tm, D), lambda i: (i, 0)),
            pl.BlockSpec((tm, attn2.shape[1]), lambda i: (i, 0)),
            pl.BlockSpec((tm, lru2.shape[1]), lambda i: (i, 0)),
            pl.BlockSpec(w_out.shape, const, pipeline_mode=resident),
            pl.BlockSpec(g_mlp.shape, const),
            pl.BlockSpec(w_up.shape, const, pipeline_mode=resident),
            pl.BlockSpec(w_down.shape, const, pipeline_mode=resident),
            pl.BlockSpec(g_final.shape, const),
        ],
        out_specs=pl.BlockSpec((tm, D), lambda i: (i, 0)),
        out_shape=jax.ShapeDtypeStruct((T, D), jnp.float32),
        compiler_params=pltpu.CompilerParams(
            dimension_semantics=("parallel",),
            vmem_limit_bytes=V7X_VMEM_LIMIT_BYTES),
        name="out_mlp",
    )(x2, attn2, lru2, w_out, g_mlp, w_up, w_down, g_final)
    return y.reshape(B, S, D)


def _block_diag(w):
    n, c, d = w.shape
    eye = jnp.eye(n, dtype=w.dtype)
    return (eye[:, None, :, None] * w[:, :, None, :]).reshape(n * c, n * d)


def kernel(x, norm_mix_g, w_in, conv_w, conv_b, w_rg, b_rg, w_ig, b_ig, lru_L, lambda_q1, lambda_k1, lambda_q2, lambda_k2, subln_g, w_out, norm_mlp_g, w_up, w_down, final_g):
    B, S, D = x.shape
    d_lru = lru_L.shape[-1]
    d_attn = (w_in.shape[-1] - 2 * d_lru) // 3
    depth = w_in.shape[0]
    assert depth == 1, "LAMBDA_INIT is specialised to a single layer"
    assert S % PROJ_TOKENS == 0 and S % LRU_TOKENS == 0 and (B * S) % MLP_TOKENS == 0
    l = 0
    row = lambda v: v.reshape(1, -1)

    w = w_in[l]
    wqt = _bf16(w[:, :d_attn] * (HEAD_DIM ** -0.5)).T
    wk = _bf16(w[:, d_attn:2 * d_attn])
    wvt = _bf16(w[:, 2 * d_attn:3 * d_attn]).T
    wug = _bf16(w[:, 3 * d_attn:])
    w_gates = _bf16(jnp.concatenate([_block_diag(w_rg[l]), _block_diag(w_ig[l])], axis=1))
    b_gates = jnp.concatenate([b_rg[l].reshape(1, -1), b_ig[l].reshape(1, -1)], axis=1)
    lamv = jnp.stack([lambda_q1[l], lambda_k1[l], lambda_q2[l], lambda_k2[l]]).astype(jnp.float32)

    qt, k, vt, u, gate = _in_proj(x, row(norm_mix_g[l]), wqt, wk, wvt, wug)
    attn = _diff_attn(qt, k, vt, lamv, subln_g[l].reshape(-1, 1))
    lru = _rg_lru(u, gate, conv_w[l], row(conv_b[l]), w_gates, b_gates, row(lru_L[l]))
    return _out_mlp(x, attn, lru, _bf16(w_out[l]), row(norm_mlp_g[l]), _bf16(w_up[l]),
                    _bf16(w_down[l]), row(final_g))
```

```python
import functools
import math

import numpy as np
import jax
import jax.numpy as jnp
from jax import lax
from jax.experimental import pallas as pl
from jax.experimental.pallas import tpu as pltpu

EPS = 1e-6
N_LRU_BLOCKS = 8
CONV_WIDTH = 4
LRU_C = 8.0
HEAD_DIM = 64
V_HEAD_DIM = 2 * HEAD_DIM
LAMBDA_INIT = 0.8 - 0.6 * math.exp(-0.3 * 0)

V7X_MXU_COLS = 256
V7X_VMEM_LIMIT_BYTES = 56 * 1024 * 1024

ATTN_BLOCK = V7X_MXU_COLS
PROJ_TOKENS = 512
LRU_TOKENS = 256
MLP_TOKENS = 512
FF_CHUNK = 1024

_NT_DIMS = (((1,), (1,)), ((), ()))


def _bf16(x):
    return x.astype(jnp.bfloat16)


def _rmsnorm_rows(x, g):
    return x * lax.rsqrt(jnp.mean(x * x, axis=-1, keepdims=True) + EPS) * g


def _in_proj_kernel(x_ref, g_ref, wqt_ref, wk_ref, wvt_ref, wug_ref,
                    qt_ref, k_ref, vt_ref, u_ref, gate_ref):
    h = _bf16(_rmsnorm_rows(x_ref[...], g_ref[...]))
    k_ref[...] = _bf16(jnp.dot(h, wk_ref[...], preferred_element_type=jnp.float32))
    d_lru = u_ref.shape[-1]
    ug = jnp.dot(h, wug_ref[...], preferred_element_type=jnp.float32)
    u_ref[...] = ug[:, :d_lru]
    gate_ref[...] = ug[:, d_lru:]
    qt = _bf16(lax.dot_general(wqt_ref[...], h, _NT_DIMS, preferred_element_type=jnp.float32))
    vt = _bf16(lax.dot_general(wvt_ref[...], h, _NT_DIMS, preferred_element_type=jnp.float32))
    for j in range(qt_ref.shape[0]):
        qt_ref[j] = qt[:, j * ATTN_BLOCK:(j + 1) * ATTN_BLOCK]
        vt_ref[j] = vt[:, j * ATTN_BLOCK:(j + 1) * ATTN_BLOCK]


def _in_proj(x, g, wqt, wk, wvt, wug):
    B, S, D = x.shape
    d_attn = wk.shape[1]
    d_lru = wug.shape[1] // 2
    tm = PROJ_TOKENS
    nblk = tm // ATTN_BLOCK
    const = lambda b, i: (0, 0)
    return pl.pallas_call(
        _in_proj_kernel,
        grid=(B, S // tm),
        in_specs=[
            pl.BlockSpec((None, tm, D), lambda b, i: (b, i, 0)),
            pl.BlockSpec((1, D), const),
            pl.BlockSpec(wqt.shape, const),
            pl.BlockSpec(wk.shape, const),
            pl.BlockSpec(wvt.shape, const),
            pl.BlockSpec(wug.shape, const),
        ],
        out_specs=[
            pl.BlockSpec((None, nblk, d_attn, ATTN_BLOCK), lambda b, i: (b, i, 0, 0)),
            pl.BlockSpec((None, tm, d_attn), lambda b, i: (b, i, 0)),
            pl.BlockSpec((None, nblk, d_attn, ATTN_BLOCK), lambda b, i: (b, i, 0, 0)),
            pl.BlockSpec((None, tm, d_lru), lambda b, i: (b, i, 0)),
            pl.BlockSpec((None, tm, d_lru), lambda b, i: (b, i, 0)),
        ],
        out_shape=[
            jax.ShapeDtypeStruct((B, S // ATTN_BLOCK, d_attn, ATTN_BLOCK), jnp.bfloat16),
            jax.ShapeDtypeStruct((B, S, d_attn), jnp.bfloat16),
            jax.ShapeDtypeStruct((B, S // ATTN_BLOCK, d_attn, ATTN_BLOCK), jnp.bfloat16),
            jax.ShapeDtypeStruct((B, S, d_lru), jnp.float32),
            jax.ShapeDtypeStruct((B, S, d_lru), jnp.float32),
        ],
        compiler_params=pltpu.CompilerParams(
            dimension_semantics=("parallel", "parallel"),
            vmem_limit_bytes=V7X_VMEM_LIMIT_BYTES),
        name="in_proj",
    )(x, g, wqt, wk, wvt, wug)


N_POS_COLS = 128
N_ONES_ROWS = 16


def _diff_attn_kernel(qt_ref, k_ref, pos_ref, vt_ref, lamv_ref, sg_ref, o_ref, s_ref, *, slopes):
    h = pl.program_id(1)
    qi = pl.program_id(2)
    blk = ATTN_BLOCK
    n_blocks = vt_ref.shape[0]

    slope = jnp.float32(slopes[0])
    for i in range(1, len(slopes)):
        slope = jnp.where(h == i, jnp.float32(slopes[i]), slope)

    r = lax.broadcasted_iota(jnp.int32, (N_POS_COLS, blk), 0)
    lane = lax.broadcasted_iota(jnp.int32, (N_POS_COLS, blk), 1).astype(jnp.float32)
    q_off = (qi * blk).astype(jnp.float32)
    alibi = jnp.where(r == 0, slope,
            jnp.where(r == 1, slope * blk,
            jnp.where(r == 2, -slope * lane,
            jnp.where(r == 3, -slope * q_off, 0.0))))
    alibi = _bf16(alibi)
    qt = qt_ref[...]
    zeros = jnp.zeros((HEAD_DIM, blk), qt.dtype)
    q_aug = jnp.concatenate(
        [jnp.concatenate([qt[:HEAD_DIM], zeros, alibi], axis=0),
         jnp.concatenate([zeros, qt[HEAD_DIM:], alibi], axis=0)], axis=1)

    key_in_blk = lax.broadcasted_iota(jnp.int32, (blk, 2 * blk), 0)
    qry_in_blk = lax.broadcasted_iota(jnp.int32, (blk, 2 * blk), 1) % blk
    causal = key_in_blk <= qry_in_blk
    ones_row = lax.broadcasted_iota(jnp.int32, (N_ONES_ROWS, blk), 0) == 0
    ones_blk = jnp.where(ones_row, 1.0, 0.0).astype(jnp.bfloat16)

    lamv = lamv_ref[...]
    lam = (jnp.exp(jnp.sum(lamv[0:1] * lamv[1:2], axis=-1, keepdims=True))
           - jnp.exp(jnp.sum(lamv[2:3] * lamv[3:4], axis=-1, keepdims=True))
           + LAMBDA_INIT)

    def q_block(n):
        length = n * blk

        def run():
            k_aug = jnp.concatenate([k_ref[0:length, :], pos_ref[0:length, :]], axis=1)
            vt_aug = jnp.concatenate(
                [jnp.concatenate([vt_ref[kb] for kb in range(n)], axis=1),
                 jnp.concatenate([ones_blk] * n, axis=1)], axis=0)
            s = jnp.dot(k_aug, q_aug, preferred_element_type=jnp.float32)
            diag = jnp.where(causal, s[length - blk:], -jnp.inf)
            m = jnp.max(diag, axis=0, keepdims=True)
            if n > 1:
                m = jnp.maximum(m, jnp.max(s[:length - blk], axis=0, keepdims=True))
                s_ref[0:length - blk, :] = s[:length - blk]
            s_ref[length - blk:length, :] = diag
            p = _bf16(jnp.exp(s_ref[0:length, :] - m))
            acc = jnp.dot(vt_aug, p, preferred_element_type=jnp.float32)
            heads = acc[:V_HEAD_DIM] / acc[V_HEAD_DIM:V_HEAD_DIM + 1]
            o = heads[:, :blk] - lam * heads[:, blk:]
            o = o * lax.rsqrt(jnp.mean(o * o, axis=0, keepdims=True) + EPS)
            o = o * sg_ref[...] * (1.0 - LAMBDA_INIT)
            o_ref[...] = _bf16(o.T)

        return run

    lax.switch(qi, [q_block(n) for n in range(1, n_blocks + 1)])


def _alibi_slopes(n_heads):
    return [2.0 ** (-8.0 * (i + 1) / n_heads) for i in range(n_heads)]


def _key_position_columns(seq_len):
    j = np.arange(seq_len)
    pos = np.zeros((seq_len, N_POS_COLS), np.float32)
    pos[:, 0] = j % ATTN_BLOCK
    pos[:, 1] = j // ATTN_BLOCK
    pos[:, 2:4] = 1.0
    assert ATTN_BLOCK <= 256 and seq_len // ATTN_BLOCK <= 256
    return jnp.asarray(pos, jnp.bfloat16)


def _diff_attn(qt, k, vt, lamv, subln_g):
    B, nblk, d_attn, blk = qt.shape
    S = k.shape[1]
    H = d_attn // V_HEAD_DIM
    pos = _key_position_columns(S)
    kernel = functools.partial(_diff_attn_kernel, slopes=tuple(_alibi_slopes(H)))
    return pl.pallas_call(
        kernel,
        grid=(B, H, nblk),
        in_specs=[
            pl.BlockSpec((None, None, V_HEAD_DIM, blk), lambda b, h, q: (b, q, h, 0)),
            pl.BlockSpec((None, S, V_HEAD_DIM), lambda b, h, q: (b, 0, h)),
            pl.BlockSpec(pos.shape, lambda b, h, q: (0, 0)),
            pl.BlockSpec((None, nblk, V_HEAD_DIM, blk), lambda b, h, q: (b, 0, h, 0)),
            pl.BlockSpec(lamv.shape, lambda b, h, q: (0, 0)),
            pl.BlockSpec(subln_g.shape, lambda b, h, q: (0, 0)),
        ],
        out_specs=pl.BlockSpec((None, blk, V_HEAD_DIM), lambda b, h, q: (b, q, h)),
        out_shape=jax.ShapeDtypeStruct((B, S, d_attn), jnp.bfloat16),
        scratch_shapes=[pltpu.VMEM((S, 2 * blk), jnp.float32)],
        compiler_params=pltpu.CompilerParams(
            dimension_semantics=("parallel", "parallel", "arbitrary"),
            vmem_limit_bytes=V7X_VMEM_LIMIT_BYTES),
        name="diff_attn",
    )(qt, k, pos, vt, lamv, subln_g)


def _gelu_tanh(x):
    c = math.sqrt(2.0 / math.pi)
    return 0.5 * x * (1.0 + jnp.tanh(c * (x + 0.044715 * (x * x * x))))


def _log_sigmoid(x):
    return jnp.minimum(x, 0.0) - jnp.log1p(jnp.exp(-jnp.abs(x)))


def _rg_lru_kernel(u_ref, gate_ref, cw_ref, cb_ref, wg_ref, bg_ref, lru_l_ref, o_ref,
                   tail_ref, carry_ref):
    ts, d_lru = u_ref.shape

    @pl.when(pl.program_id(1) == 0)
    def _():
        tail_ref[...] = jnp.zeros(tail_ref.shape, jnp.float32)
        carry_ref[...] = jnp.zeros(carry_ref.shape, jnp.float32)

    u_raw = u_ref[...]
    ext = jnp.concatenate([tail_ref[...], u_raw], axis=0)
    tail_ref[...] = u_raw[ts - 8:, :]
    cw = cw_ref[...]
    u = cb_ref[...]
    for kk in range(CONV_WIDTH):
        off = 8 - (CONV_WIDTH - 1) + kk
        u = u + ext[off:off + ts, :] * cw[kk:kk + 1, :]

    gates = jnp.dot(_bf16(u), wg_ref[...], preferred_element_type=jnp.float32) + bg_ref[...]
    r = jax.nn.sigmoid(gates[:, :d_lru])
    i_gate = jax.nn.sigmoid(gates[:, d_lru:])
    log_a = LRU_C * r * _log_sigmoid(lru_l_ref[...])
    a = jnp.exp(log_a)
    mult = jnp.sqrt(-jnp.tanh(log_a) * (a * a + 1.0))
    b = mult * (i_gate * u)

    rows = lax.broadcasted_iota(jnp.int32, (ts, d_lru), 0)
    shift = 1
    while shift < ts:
        valid = rows >= shift
        a_prev = jnp.where(valid, pltpu.roll(a, shift, axis=0), 1.0)
        b_prev = jnp.where(valid, pltpu.roll(b, shift, axis=0), 0.0)
        b = a * b_prev + b
        a = a * a_prev
        shift *= 2
    hstate = b + a * carry_ref[0:1, :]
    carry_ref[...] = jnp.broadcast_to(hstate[ts - 1:ts, :], carry_ref.shape)
    o_ref[...] = _bf16(hstate * _gelu_tanh(gate_ref[...]))


def _rg_lru(u, gate, conv_w, conv_b, w_gates, b_gates, lru_l):
    B, S, d_lru = u.shape
    ts = LRU_TOKENS
    const = lambda b, i: (0, 0)
    return pl.pallas_call(
        _rg_lru_kernel,
        grid=(B, S // ts),
        in_specs=[
            pl.BlockSpec((None, ts, d_lru), lambda b, i: (b, i, 0)),
            pl.BlockSpec((None, ts, d_lru), lambda b, i: (b, i, 0)),
            pl.BlockSpec(conv_w.shape, const),
            pl.BlockSpec(conv_b.shape, const),
            pl.BlockSpec(w_gates.shape, const),
            pl.BlockSpec(b_gates.shape, const),
            pl.BlockSpec(lru_l.shape, const),
        ],
        out_specs=pl.BlockSpec((None, ts, d_lru), lambda b, i: (b, i, 0)),
        out_shape=jax.ShapeDtypeStruct((B, S, d_lru), jnp.bfloat16),
        scratch_shapes=[
            pltpu.VMEM((8, d_lru), jnp.float32),
            pltpu.VMEM((8, d_lru), jnp.float32),
        ],
        compiler_params=pltpu.CompilerParams(
            dimension_semantics=("parallel", "arbitrary"),
            vmem_limit_bytes=V7X_VMEM_LIMIT_BYTES),
        name="rg_lru",
    )(u, gate, conv_w, conv_b, w_gates, b_gates, lru_l)


def _out_mlp_kernel(x_ref, attn_ref, lru_ref, wo_ref, gm_ref, wup_ref, wdn_ref, gf_ref, y_ref):
    mix = jnp.concatenate([attn_ref[...], lru_ref[...]], axis=-1)
    x1 = x_ref[...] + jnp.dot(mix, wo_ref[...], preferred_element_type=jnp.float32)
    hm = _bf16(_rmsnorm_rows(x1, gm_ref[...]))
    d_ff = wup_ref.shape[1]
    mlp = jnp.zeros_like(x1)
    for c in range(d_ff // FF_CHUNK):
        cols = slice(c * FF_CHUNK, (c + 1) * FF_CHUNK)
        up = jnp.dot(hm, wup_ref[:, cols], preferred_element_type=jnp.float32)
        act = _bf16(jnp.square(jnp.maximum(up, 0.0)))
        mlp = mlp + jnp.dot(act, wdn_ref[cols, :], preferred_element_type=jnp.float32)
    y_ref[...] = _rmsnorm_rows(x1 + mlp, gf_ref[...])


def _out_mlp(x, attn, lru, w_out, g_mlp, w_up, w_down, g_final):
    B, S, D = x.shape
    T = B * S
    tm = MLP_TOKENS
    x2 = x.reshape(T, D)
    attn2 = attn.reshape(T, attn.shape[-1])
    lru2 = lru.reshape(T, lru.shape[-1])
    const = lambda i: (0, 0)
    resident = pl.Buffered(1)
    y = pl.pallas_call(
        _out_mlp_kernel,
        grid=(T // tm,),
        in_specs=[
            pl.BlockSpec((tm, D), lambda i: (i, 0)),
            pl.BlockSpec((tm, attn2.shape[1]), lambda i: (i, 0)),
            pl.BlockSpec((tm, lru2.shape[1]), lambda i: (i, 0)),
            pl.BlockSpec(w_out.shape, const, pipeline_mode=resident),
            pl.BlockSpec(g_mlp.shape, const),
            pl.BlockSpec(w_up.shape, const, pipeline_mode=resident),
            pl.BlockSpec(w_down.shape, const, pipeline_mode=resident),
            pl.BlockSpec(g_final.shape, const),
        ],
        out_specs=pl.BlockSpec((tm, D), lambda i: (i, 0)),
        out_shape=jax.ShapeDtypeStruct((T, D), jnp.float32),
        compiler_params=pltpu.CompilerParams(
            dimension_semantics=("parallel",),
            vmem_limit_bytes=V7X_VMEM_LIMIT_BYTES),
        name="out_mlp",
    )(x2, attn2, lru2, w_out, g_mlp, w_up, w_down, g_final)
    return y.reshape(B, S, D)


def _block_diag(w):
    n, c, d = w.shape
    eye = jnp.eye(n, dtype=w.dtype)
    return (eye[:, None, :, None] * w[:, :, None, :]).reshape(n * c, n * d)


def kernel(x, norm_mix_g, w_in, conv_w, conv_b, w_rg, b_rg, w_ig, b_ig, lru_L, lambda_q1, lambda_k1, lambda_q2, lambda_k2, subln_g, w_out, norm_mlp_g, w_up, w_down, final_g):
    B, S, D = x.shape
    d_lru = lru_L.shape[-1]
    d_attn = (w_in.shape[-1] - 2 * d_lru) // 3
    depth = w_in.shape[0]
    assert depth == 1, "LAMBDA_INIT is specialised to a single layer"
    assert S % PROJ_TOKENS == 0 and S % LRU_TOKENS == 0 and (B * S) % MLP_TOKENS == 0
    l = 0
    row = lambda v: v.reshape(1, -1)

    w = w_in[l]
    wqt = _bf16(w[:, :d_attn] * (HEAD_DIM ** -0.5)).T
    wk = _bf16(w[:, d_attn:2 * d_attn])
    wvt = _bf16(w[:, 2 * d_attn:3 * d_attn]).T
    wug = _bf16(w[:, 3 * d_attn:])
    w_gates = _bf16(jnp.concatenate([_block_diag(w_rg[l]), _block_diag(w_ig[l])], axis=1))
    b_gates = jnp.concatenate([b_rg[l].reshape(1, -1), b_ig[l].reshape(1, -1)], axis=1)
    lamv = jnp.stack([lambda_q1[l], lambda_k1[l], lambda_q2[l], lambda_k2[l]]).astype(jnp.float32)

    qt, k, vt, u, gate = _in_proj(x, row(norm_mix_g[l]), wqt, wk, wvt, wug)
    attn = _diff_attn(qt, k, vt, lamv, subln_g[l].reshape(-1, 1))
    lru = _rg_lru(u, gate, conv_w[l], row(conv_b[l]), w_gates, b_gates, row(lru_L[l]))
    return _out_mlp(x, attn, lru, _bf16(w_out[l]), row(norm_mlp_g[l]), _bf16(w_up[l]),
                    _bf16(w_down[l]), row(final_g))
```

```python
import functools
import math

import numpy as np
import jax
import jax.numpy as jnp
from jax import lax
from jax.experimental import pallas as pl
from jax.experimental.pallas import tpu as pltpu

EPS = 1e-6
N_LRU_BLOCKS = 8
CONV_WIDTH = 4
LRU_C = 8.0
HEAD_DIM = 64
V_HEAD_DIM = 2 * HEAD_DIM
LAMBDA_INIT = 0.8 - 0.6 * math.exp(-0.3 * 0)

V7X_MXU_COLS = 256
V7X_VMEM_LIMIT_BYTES = 56 * 1024 * 1024

ATTN_BLOCK = V7X_MXU_COLS
PROJ_TOKENS = 512
LRU_TOKENS = 256
MLP_TOKENS = 512
FF_CHUNK = 1024

_NT_DIMS = (((1,), (1,)), ((), ()))


def _bf16(x):
    return x.astype(jnp.bfloat16)


def _rmsnorm_rows(x, g):
    return x * lax.rsqrt(jnp.mean(x * x, axis=-1, keepdims=True) + EPS) * g


def _in_proj_kernel(x_ref, g_ref, wqt_ref, wk_ref, wvt_ref, wug_ref,
                    qt_ref, k_ref, vt_ref, u_ref, gate_ref):
    h = _bf16(_rmsnorm_rows(x_ref[...], g_ref[...]))
    k_ref[...] = _bf16(jnp.dot(h, wk_ref[...], preferred_element_type=jnp.float32))
    d_lru = u_ref.shape[-1]
    ug = jnp.dot(h, wug_ref[...], preferred_element_type=jnp.float32)
    u_ref[...] = ug[:, :d_lru]
    gate_ref[...] = ug[:, d_lru:]
    qt = _bf16(lax.dot_general(wqt_ref[...], h, _NT_DIMS, preferred_element_type=jnp.float32))
    vt = _bf16(lax.dot_general(wvt_ref[...], h, _NT_DIMS, preferred_element_type=jnp.float32))
    for j in range(qt_ref.shape[0]):
        qt_ref[j] = qt[:, j * ATTN_BLOCK:(j + 1) * ATTN_BLOCK]
        vt_ref[j] = vt[:, j * ATTN_BLOCK:(j + 1) * ATTN_BLOCK]


def _in_proj(x, g, wqt, wk, wvt, wug):
    B, S, D = x.shape
    d_attn = wk.shape[1]
    d_lru = wug.shape[1] // 2
    tm = PROJ_TOKENS
    nblk = tm // ATTN_BLOCK
    const = lambda b, i: (0, 0)
    return pl.pallas_call(
        _in_proj_kernel,
        grid=(B, S // tm),
        in_specs=[
            pl.BlockSpec((None, tm, D), lambda b, i: (b, i, 0)),
            pl.BlockSpec((1, D), const),
            pl.BlockSpec(wqt.shape, const),
            pl.BlockSpec(wk.shape, const),
            pl.BlockSpec(wvt.shape, const),
            pl.BlockSpec(wug.shape, const),
        ],
        out_specs=[
            pl.BlockSpec((None, nblk, d_attn, ATTN_BLOCK), lambda b, i: (b, i, 0, 0)),
            pl.BlockSpec((None, tm, d_attn), lambda b, i: (b, i, 0)),
            pl.BlockSpec((None, nblk, d_attn, ATTN_BLOCK), lambda b, i: (b, i, 0, 0)),
            pl.BlockSpec((None, tm, d_lru), lambda b, i: (b, i, 0)),
            pl.BlockSpec((None, tm, d_lru), lambda b, i: (b, i, 0)),
        ],
        out_shape=[
            jax.ShapeDtypeStruct((B, S // ATTN_BLOCK, d_attn, ATTN_BLOCK), jnp.bfloat16),
            jax.ShapeDtypeStruct((B, S, d_attn), jnp.bfloat16),
            jax.ShapeDtypeStruct((B, S // ATTN_BLOCK, d_attn, ATTN_BLOCK), jnp.bfloat16),
            jax.ShapeDtypeStruct((B, S, d_lru), jnp.float32),
            jax.ShapeDtypeStruct((B, S, d_lru), jnp.float32),
        ],
        compiler_params=pltpu.CompilerParams(
            dimension_semantics=("parallel", "parallel"),
            vmem_limit_bytes=V7X_VMEM_LIMIT_BYTES),
        name="in_proj",
    )(x, g, wqt, wk, wvt, wug)


N_POS_COLS = 128
N_ONES_ROWS = 16


def _diff_attn_kernel(qt0_ref, qtn_ref, k_ref, pos_ref, vt_ref, lamv_ref, sg_ref, o_ref,
                      s_ref, m_ref, *, slopes):
    h = pl.program_id(1)
    qi = pl.program_id(2)
    blk = ATTN_BLOCK
    n_blocks = vt_ref.shape[0]

    slope = jnp.float32(slopes[0])
    for i in range(1, len(slopes)):
        slope = jnp.where(h == i, jnp.float32(slopes[i]), slope)

    r = lax.broadcasted_iota(jnp.int32, (N_POS_COLS, blk), 0)
    lane = lax.broadcasted_iota(jnp.int32, (N_POS_COLS, blk), 1).astype(jnp.float32)
    zeros = jnp.zeros((HEAD_DIM, blk), qt0_ref.dtype)
    key_in_blk = lax.broadcasted_iota(jnp.int32, (blk, 2 * blk), 0)
    qry_in_blk = lax.broadcasted_iota(jnp.int32, (blk, 2 * blk), 1) % blk
    causal = key_in_blk <= qry_in_blk
    ones_row = lax.broadcasted_iota(jnp.int32, (N_ONES_ROWS, blk), 0) == 0
    ones_blk = jnp.where(ones_row, 1.0, 0.0).astype(jnp.bfloat16)

    def q_operand(qt, q_index):
        alibi = _bf16(jnp.where(r == 0, slope,
                      jnp.where(r == 1, slope * blk,
                      jnp.where(r == 2, -slope * lane,
                      jnp.where(r == 3, -slope * float(q_index * blk), 0.0)))))
        return jnp.concatenate(
            [jnp.concatenate([qt[:HEAD_DIM], zeros, alibi], axis=0),
             jnp.concatenate([zeros, qt[HEAD_DIM:], alibi], axis=0)], axis=1)

    def score_block(q_aug, j, diagonal):
        rows = slice(j * blk, (j + 1) * blk)
        k_aug = jnp.concatenate([k_ref[rows, :], pos_ref[rows, :]], axis=1)
        s = jnp.dot(k_aug, q_aug, preferred_element_type=jnp.float32)
        return jnp.where(causal, s, -jnp.inf) if diagonal else s

    def q_block(n):
        cur, nxt = (n - 1) % 2, n % 2

        def run():
            if n == 1:
                s0 = score_block(q_operand(qt0_ref[...], 0), 0, True)
                s_ref[cur, 0:blk, :] = s0
                m_ref[cur] = jnp.max(s0, axis=0, keepdims=True)
            m_cur = m_ref[cur]
            q_next = q_operand(qtn_ref[...], n) if n < n_blocks else None
            m_next = None
            acc = jnp.zeros((V_HEAD_DIM + N_ONES_ROWS, 2 * blk), jnp.float32)
            for j in range(n + 1):
                if q_next is not None:
                    s = score_block(q_next, j, j == n)
                    s_ref[nxt, j * blk:(j + 1) * blk, :] = s
                    col_max = jnp.max(s, axis=0, keepdims=True)
                    m_next = col_max if m_next is None else jnp.maximum(m_next, col_max)
                if j < n:
                    p = _bf16(jnp.exp(s_ref[cur, j * blk:(j + 1) * blk, :] - m_cur))
                    vt_aug = jnp.concatenate([vt_ref[j], ones_blk], axis=0)
                    acc = acc + jnp.dot(vt_aug, p, preferred_element_type=jnp.float32)
            if q_next is not None:
                m_ref[nxt] = m_next

            lamv = lamv_ref[...]
            lam = (jnp.exp(jnp.sum(lamv[0:1] * lamv[1:2], axis=-1, keepdims=True))
                   - jnp.exp(jnp.sum(lamv[2:3] * lamv[3:4], axis=-1, keepdims=True))
                   + LAMBDA_INIT)
            heads = acc[:V_HEAD_DIM] / acc[V_HEAD_DIM:V_HEAD_DIM + 1]
            o = heads[:, :blk] - lam * heads[:, blk:]
            o = o * lax.rsqrt(jnp.mean(o * o, axis=0, keepdims=True) + EPS)
            o = o * sg_ref[...] * (1.0 - LAMBDA_INIT)
            o_ref[...] = _bf16(o.T)

        return run

    lax.switch(qi, [q_block(n) for n in range(1, n_blocks + 1)])


def _alibi_slopes(n_heads):
    return [2.0 ** (-8.0 * (i + 1) / n_heads) for i in range(n_heads)]


def _key_position_columns(seq_len):
    j = np.arange(seq_len)
    pos = np.zeros((seq_len, N_POS_COLS), np.float32)
    pos[:, 0] = j % ATTN_BLOCK
    pos[:, 1] = j // ATTN_BLOCK
    pos[:, 2:4] = 1.0
    assert ATTN_BLOCK <= 256 and seq_len // ATTN_BLOCK <= 256
    return jnp.asarray(pos, jnp.bfloat16)


def _diff_attn(qt, k, vt, lamv, subln_g):
    B, nblk, d_attn, blk = qt.shape
    S = k.shape[1]
    H = d_attn // V_HEAD_DIM
    pos = _key_position_columns(S)
    kernel = functools.partial(_diff_attn_kernel, slopes=tuple(_alibi_slopes(H)))
    last = nblk - 1
    return pl.pallas_call(
        kernel,
        grid=(B, H, nblk),
        in_specs=[
            pl.BlockSpec((None, None, V_HEAD_DIM, blk), lambda b, h, q: (b, 0, h, 0)),
            pl.BlockSpec((None, None, V_HEAD_DIM, blk),
                         lambda b, h, q: (b, jnp.minimum(q + 1, last), h, 0)),
            pl.BlockSpec((None, S, V_HEAD_DIM), lambda b, h, q: (b, 0, h)),
            pl.BlockSpec(pos.shape, lambda b, h, q: (0, 0)),
            pl.BlockSpec((None, nblk, V_HEAD_DIM, blk), lambda b, h, q: (b, 0, h, 0)),
            pl.BlockSpec(lamv.shape, lambda b, h, q: (0, 0)),
            pl.BlockSpec(subln_g.shape, lambda b, h, q: (0, 0)),
        ],
        out_specs=pl.BlockSpec((None, blk, V_HEAD_DIM), lambda b, h, q: (b, q, h)),
        out_shape=jax.ShapeDtypeStruct((B, S, d_attn), jnp.bfloat16),
        scratch_shapes=[pltpu.VMEM((2, S, 2 * blk), jnp.float32),
                        pltpu.VMEM((2, 1, 2 * blk), jnp.float32)],
        compiler_params=pltpu.CompilerParams(
            dimension_semantics=("parallel", "parallel", "arbitrary"),
            vmem_limit_bytes=V7X_VMEM_LIMIT_BYTES),
        name="diff_attn",
    )(qt, qt, k, pos, vt, lamv, subln_g)


def _gelu_tanh(x):
    c = math.sqrt(2.0 / math.pi)
    return 0.5 * x * (1.0 + jnp.tanh(c * (x + 0.044715 * (x * x * x))))


def _log_sigmoid(x):
    return jnp.minimum(x, 0.0) - jnp.log1p(jnp.exp(-jnp.abs(x)))


def _rg_lru_kernel(u_ref, gate_ref, cw_ref, cb_ref, wg_ref, bg_ref, lru_l_ref, o_ref,
                   tail_ref, carry_ref):
    ts, d_lru = u_ref.shape

    @pl.when(pl.program_id(1) == 0)
    def _():
        tail_ref[...] = jnp.zeros(tail_ref.shape, jnp.float32)
        carry_ref[...] = jnp.zeros(carry_ref.shape, jnp.float32)

    u_raw = u_ref[...]
    ext = jnp.concatenate([tail_ref[...], u_raw], axis=0)
    tail_ref[...] = u_raw[ts - 8:, :]
    cw = cw_ref[...]
    u = cb_ref[...]
    for kk in range(CONV_WIDTH):
        off = 8 - (CONV_WIDTH - 1) + kk
        u = u + ext[off:off + ts, :] * cw[kk:kk + 1, :]

    gates = jnp.dot(_bf16(u), wg_ref[...], preferred_element_type=jnp.float32) + bg_ref[...]
    r = jax.nn.sigmoid(gates[:, :d_lru])
    i_gate = jax.nn.sigmoid(gates[:, d_lru:])
    log_a = LRU_C * r * _log_sigmoid(lru_l_ref[...])
    a = jnp.exp(log_a)
    mult = jnp.sqrt(-jnp.tanh(log_a) * (a * a + 1.0))
    b = mult * (i_gate * u)

    rows = lax.broadcasted_iota(jnp.int32, (ts, d_lru), 0)
    shift = 1
    while shift < ts:
        valid = rows >= shift
        a_prev = jnp.where(valid, pltpu.roll(a, shift, axis=0), 1.0)
        b_prev = jnp.where(valid, pltpu.roll(b, shift, axis=0), 0.0)
        b = a * b_prev + b
        a = a * a_prev
        shift *= 2
    hstate = b + a * carry_ref[0:1, :]
    carry_ref[...] = jnp.broadcast_to(hstate[ts - 1:ts, :], carry_ref.shape)
    o_ref[...] = _bf16(hstate * _gelu_tanh(gate_ref[...]))


def _rg_lru(u, gate, conv_w, conv_b, w_gates, b_gates, lru_l):
    B, S, d_lru = u.shape
    ts = LRU_TOKENS
    const = lambda b, i: (0, 0)
    return pl.pallas_call(
        _rg_lru_kernel,
        grid=(B, S // ts),
        in_specs=[
            pl.BlockSpec((None, ts, d_lru), lambda b, i: (b, i, 0)),
            pl.BlockSpec((None, ts, d_lru), lambda b, i: (b, i, 0)),
            pl.BlockSpec(conv_w.shape, const),
            pl.BlockSpec(conv_b.shape, const),
            pl.BlockSpec(w_gates.shape, const),
            pl.BlockSpec(b_gates.shape, const),
            pl.BlockSpec(lru_l.shape, const),
        ],
        out_specs=pl.BlockSpec((None, ts, d_lru), lambda b, i: (b, i, 0)),
        out_shape=jax.ShapeDtypeStruct((B, S, d_lru), jnp.bfloat16),
        scratch_shapes=[
            pltpu.VMEM((8, d_lru), jnp.float32),
            pltpu.VMEM((8, d_lru), jnp.float32),
        ],
        compiler_params=pltpu.CompilerParams(
            dimension_semantics=("parallel", "arbitrary"),
            vmem_limit_bytes=V7X_VMEM_LIMIT_BYTES),
        name="rg_lru",
    )(u, gate, conv_w, conv_b, w_gates, b_gates, lru_l)


def _out_mlp_kernel(x_ref, attn_ref, lru_ref, wo_ref, gm_ref, wup_ref, wdn_ref, gf_ref, y_ref):
    mix = jnp.concatenate([attn_ref[...], lru_ref[...]], axis=-1)
    x1 = x_ref[...] + jnp.dot(mix, wo_ref[...], preferred_element_type=jnp.float32)
    hm = _bf16(_rmsnorm_rows(x1, gm_ref[...]))
    d_ff = wup_ref.shape[1]
    mlp = jnp.zeros_like(x1)
    for c in range(d_ff // FF_CHUNK):
        cols = slice(c * FF_CHUNK, (c + 1) * FF_CHUNK)
        up = jnp.dot(hm, wup_ref[:, cols], preferred_element_type=jnp.float32)
        act = _bf16(jnp.square(jnp.maximum(up, 0.0)))
        mlp = mlp + jnp.dot(act, wdn_ref[cols, :], preferred_element_type=jnp.float32)
    y_ref[...] = _rmsnorm_rows(x1 + mlp, gf_ref[...])


def _out_mlp(x, attn, lru, w_out, g_mlp, w_up, w_down, g_final):
    B, S, D = x.shape
    T = B * S
    tm = MLP_TOKENS
    x2 = x.reshape(T, D)
    attn2 = attn.reshape(T, attn.shape[-1])
    lru2 = lru.reshape(T, lru.shape[-1])
    const = lambda i: (0, 0)
    resident = pl.Buffered(1)
    y = pl.pallas_call(
        _out_mlp_kernel,
        grid=(T // tm,),
        in_specs=[
            pl.BlockSpec((tm, D), lambda i: (i, 0)),
            pl.BlockSpec((tm, attn2.shape[1]), lambda i: (i, 0)),
            pl.BlockSpec((tm, lru2.shape[1]), lambda i: (i, 0)),
            pl.BlockSpec(w_out.shape, const, pipeline_mode=resident),
            pl.BlockSpec(g_mlp.shape, const),
            pl.BlockSpec(w_up.shape, const, pipeline_mode=resident),
            pl.BlockSpec(w_down.shape, const, pipeline_mode=resident),
            pl.BlockSpec(g_final.shape, const),
        ],
        out_specs=pl.BlockSpec((tm, D), lambda i: (i, 0)),
        out_shape=jax.ShapeDtypeStruct((T, D), jnp.float32),
        compiler_params=pltpu.CompilerParams(
            dimension_semantics=("parallel",),
            vmem_limit_bytes=V7X_VMEM_LIMIT_BYTES),
        name="out_mlp",
    )(x2, attn2, lru2, w_out, g_mlp, w_up, w_down, g_final)
    return y.reshape(B, S, D)


def _block_diag(w):
    n, c, d = w.shape
    eye = jnp.eye(n, dtype=w.dtype)
    return (eye[:, None, :, None] * w[:, :, None, :]).reshape(n * c, n * d)


def kernel(x, norm_mix_g, w_in, conv_w, conv_b, w_rg, b_rg, w_ig, b_ig, lru_L, lambda_q1, lambda_k1, lambda_q2, lambda_k2, subln_g, w_out, norm_mlp_g, w_up, w_down, final_g):
    B, S, D = x.shape
    d_lru = lru_L.shape[-1]
    d_attn = (w_in.shape[-1] - 2 * d_lru) // 3
    depth = w_in.shape[0]
    assert depth == 1, "LAMBDA_INIT is specialised to a single layer"
    assert S % PROJ_TOKENS == 0 and S % LRU_TOKENS == 0 and (B * S) % MLP_TOKENS == 0
    l = 0
    row = lambda v: v.reshape(1, -1)

    w = w_in[l]
    wqt = _bf16(w[:, :d_attn] * (HEAD_DIM ** -0.5)).T
    wk = _bf16(w[:, d_attn:2 * d_attn])
    wvt = _bf16(w[:, 2 * d_attn:3 * d_attn]).T
    wug = _bf16(w[:, 3 * d_attn:])
    w_gates = _bf16(jnp.concatenate([_block_diag(w_rg[l]), _block_diag(w_ig[l])], axis=1))
    b_gates = jnp.concatenate([b_rg[l].reshape(1, -1), b_ig[l].reshape(1, -1)], axis=1)
    lamv = jnp.stack([lambda_q1[l], lambda_k1[l], lambda_q2[l], lambda_k2[l]]).astype(jnp.float32)

    qt, k, vt, u, gate = _in_proj(x, row(norm_mix_g[l]), wqt, wk, wvt, wug)
    attn = _diff_attn(qt, k, vt, lamv, subln_g[l].reshape(-1, 1))
    lru = _rg_lru(u, gate, conv_w[l], row(conv_b[l]), w_gates, b_gates, row(lru_L[l]))
    return _out_mlp(x, attn, lru, _bf16(w_out[l]), row(norm_mlp_g[l]), _bf16(w_up[l]),
                    _bf16(w_down[l]), row(final_g))
```

```python
import functools
import math

import numpy as np
import jax
import jax.numpy as jnp
from jax import lax
from jax.experimental import pallas as pl
from jax.experimental.pallas import tpu as pltpu

EPS = 1e-6
CONV_WIDTH = 4
LRU_C = 8.0
HEAD_DIM = 64
V_HEAD_DIM = 2 * HEAD_DIM
LAMBDA_INIT = 0.8 - 0.6 * math.exp(-0.3 * 0)

V7X_MXU_COLS = 256
LANES = 128
SUBLANES = 8
V7X_VMEM_LIMIT_BYTES = 56 * 1024 * 1024

ATTN_BLOCK = V7X_MXU_COLS
PROJ_TOKENS = 512
MLP_TOKENS = 512
FF_CHUNK = 1024
CHAIN = 4
SUB_ROWS = SUBLANES * CHAIN

_NT_DIMS = (((1,), (1,)), ((), ()))


def _bf16(x):
    return x.astype(jnp.bfloat16)


def _rmsnorm_rows(x, g):
    return x * lax.rsqrt(jnp.mean(x * x, axis=-1, keepdims=True) + EPS) * g


def _log_sigmoid(x):
    return jnp.minimum(x, 0.0) - jnp.log1p(jnp.exp(-jnp.abs(x)))


def _in_proj_lru_kernel(x_ref, g_ref, wug_ref, wk_ref, wqt_ref, wvt_ref, cw_ref, cb_ref,
                        wg_ref, bg_ref, lru_l_ref,
                        qt_ref, k_ref, vt_ref, lru_ref,
                        h_scr, ug_slab, u_scr, ub_scr, z_scr, gelu_scr, out_slab, tail_ref,
                        carry_ref):
    tm = x_ref.shape[0]
    d_lru = lru_ref.shape[-1]
    n_slab = d_lru // LANES
    n_sub = tm // SUB_ROWS
    f32 = jnp.float32

    @pl.when(pl.program_id(1) == 0)
    def _():
        tail_ref[...] = jnp.zeros(tail_ref.shape, f32)
        carry_ref[...] = jnp.zeros(carry_ref.shape, f32)

    h_scr[...] = _bf16(_rmsnorm_rows(x_ref[...], g_ref[...]))

    def ug_half(half):
        cols = slice(half * d_lru, (half + 1) * d_lru)
        nat = jnp.dot(h_scr[...], wug_ref[:, cols], preferred_element_type=f32)
        for l in range(n_slab):
            ug_slab[half * n_slab + l] = nat[:, l * LANES:(l + 1) * LANES]

    ug_half(0)

    def piece(slab, s, c):
        return ug_slab[slab, pl.ds(s * SUB_ROWS + c, SUBLANES, stride=CHAIN), :]

    row = lax.broadcasted_iota(jnp.int32, (SUBLANES, LANES), 0)

    for l in range(n_slab):
        lanes = slice(l * LANES, (l + 1) * LANES)
        cw = cw_ref[:, lanes]
        cb = cb_ref[:, lanes]
        rolled_prev = [pltpu.roll(tail_ref[c, :, lanes], 1, axis=0) for c in range(CHAIN)]
        for s in range(n_sub):
            cur = [piece(l, s, c) for c in range(CHAIN)]
            rolled = [pltpu.roll(cur[c], 1, axis=0) for c in range(CHAIN)]
            shifted = [jnp.where(row == 0, rolled_prev[c], rolled[c]) for c in range(CHAIN)]
            rolled_prev = rolled
            if s == n_sub - 1:
                for c in range(CHAIN):
                    tail_ref[c, :, lanes] = cur[c]
            conv = []
            for c in range(CHAIN):
                acc = cb
                for kk in range(CONV_WIDTH):
                    d = CONV_WIDTH - 1 - kk
                    src = cur[c - d] if c >= d else shifted[c - d + CHAIN]
                    acc = acc + src * cw[kk:kk + 1, :]
                conv.append(acc)
            u_sub = jnp.concatenate(conv, axis=0)
            u_scr[l, s * SUB_ROWS:(s + 1) * SUB_ROWS, :] = u_sub
            ub_scr[s * SUB_ROWS:(s + 1) * SUB_ROWS, lanes] = _bf16(u_sub)

    ug_half(1)
    z = jnp.dot(ub_scr[...], wg_ref[...], preferred_element_type=f32) + bg_ref[...]
    for l in range(2 * n_slab):
        z_scr[l] = z[:, l * LANES:(l + 1) * LANES]

    def k_piece(part):
        cols = slice(part * ATTN_BLOCK, (part + 1) * ATTN_BLOCK)
        k_ref[:, cols] = _bf16(jnp.dot(h_scr[...], wk_ref[:, cols], preferred_element_type=f32))

    def nt_piece(w_ref, o_ref, part):
        rows = slice(part * ATTN_BLOCK, (part + 1) * ATTN_BLOCK)
        o_ref[part] = _bf16(lax.dot_general(w_ref[...], h_scr[rows, :], _NT_DIMS,
                                            preferred_element_type=f32))

    side_work = {0: lambda: k_piece(0), 2: lambda: k_piece(1),
                 5: lambda: nt_piece(wqt_ref, qt_ref, 0), 8: lambda: nt_piece(wqt_ref, qt_ref, 1),
                 11: lambda: nt_piece(wvt_ref, vt_ref, 0), 13: lambda: nt_piece(wvt_ref, vt_ref, 1)}
    assert n_sub > max(side_work) and qt_ref.shape[0] == 2 and k_ref.shape[1] == 2 * ATTN_BLOCK

    c_gelu = math.sqrt(2.0 / math.pi)
    for l in range(n_slab):
        for s in range(n_sub):
            gate = jnp.concatenate([piece(n_slab + l, s, c) for c in range(CHAIN)], axis=0)
            gelu_scr[l, s * SUB_ROWS:(s + 1) * SUB_ROWS, :] = (0.5 * gate) * (1.0 + jnp.tanh(
                gate * (c_gelu + (c_gelu * 0.044715) * (gate * gate))))

    c2_row = (0.5 * LRU_C * math.log2(math.e)) * _log_sigmoid(lru_l_ref[...])

    carry = [carry_ref[:, l * LANES:(l + 1) * LANES] for l in range(n_slab)]
    for s in range(n_sub):
        if s in side_work:
            side_work[s]()
        base = s * SUB_ROWS
        rows = slice(base, base + SUB_ROWS)
        for l in range(n_slab):
            c2 = c2_row[:, l * LANES:(l + 1) * LANES]
            u = u_scr[l, rows, :]
            t_r = jnp.tanh(z_scr[l, rows, :])
            t_i = jnp.tanh(z_scr[n_slab + l, rows, :])
            log2_a = c2 + c2 * t_r
            a = jnp.exp2(log2_a)
            quarter = (jnp.tanh(log2_a * math.log(2.0)) * -0.25) * (a * a + 1.0)
            half_mult = jnp.where(quarter > 0.0, quarter * lax.rsqrt(quarter), 0.0)
            b = (half_mult * u) * (1.0 + t_i)
            gelu = gelu_scr[l, rows, :]

            a_c = [a[c * SUBLANES:(c + 1) * SUBLANES] for c in range(CHAIN)]
            b_c = [b[c * SUBLANES:(c + 1) * SUBLANES] for c in range(CHAIN)]
            h_loc, p_loc = [b_c[0]], [a_c[0]]
            for c in range(1, CHAIN):
                h_loc.append(a_c[c] * h_loc[c - 1] + b_c[c])
                p_loc.append(a_c[c] * p_loc[c - 1])
            p_cum, h_cum = p_loc[CHAIN - 1], h_loc[CHAIN - 1]
            shift = 1
            while shift < SUBLANES:
                valid = row >= shift
                p_prev = jnp.where(valid, pltpu.roll(p_cum, shift, axis=0), 1.0)
                h_prev = jnp.where(valid, pltpu.roll(h_cum, shift, axis=0), 0.0)
                h_cum = p_cum * h_prev + h_cum
                p_cum = p_cum * p_prev
                shift *= 2
            chain_end = h_cum + p_cum * carry[l]
            chain_in = jnp.where(row == 0, carry[l], pltpu.roll(chain_end, 1, axis=0))
            carry[l] = jnp.broadcast_to(chain_end[SUBLANES - 1:SUBLANES, :], (SUBLANES, LANES))
            for c in range(CHAIN):
                out = (h_loc[c] + p_loc[c] * chain_in) * gelu[c * SUBLANES:(c + 1) * SUBLANES]
                out_slab[l, pl.ds(base + c, SUBLANES, stride=CHAIN), :] = out
    for l in range(n_slab):
        carry_ref[:, l * LANES:(l + 1) * LANES] = carry[l]
    lru_ref[...] = _bf16(jnp.concatenate([out_slab[l] for l in range(n_slab)], axis=1))


def _in_proj_lru(x, g, wug, wk, wqt, wvt, conv_w, conv_b, w_gates, b_gates, lru_l):
    B, S, D = x.shape
    d_attn = wk.shape[1]
    d_lru = wug.shape[1] // 2
    tm = PROJ_TOKENS
    nblk = tm // ATTN_BLOCK
    n_slab = d_lru // LANES
    const = lambda b, i: (0, 0)
    return pl.pallas_call(
        _in_proj_lru_kernel,
        grid=(B, S // tm),
        in_specs=[
            pl.BlockSpec((None, tm, D), lambda b, i: (b, i, 0)),
            pl.BlockSpec((1, D), const),
            pl.BlockSpec(wug.shape, const),
            pl.BlockSpec(wk.shape, const),
            pl.BlockSpec(wqt.shape, const),
            pl.BlockSpec(wvt.shape, const),
            pl.BlockSpec(conv_w.shape, const),
            pl.BlockSpec(conv_b.shape, const),
            pl.BlockSpec(w_gates.shape, const),
            pl.BlockSpec(b_gates.shape, const),
            pl.BlockSpec(lru_l.shape, const),
        ],
        out_specs=[
            pl.BlockSpec((None, nblk, d_attn, ATTN_BLOCK), lambda b, i: (b, i, 0, 0)),
            pl.BlockSpec((None, tm, d_attn), lambda b, i: (b, i, 0)),
            pl.BlockSpec((None, nblk, d_attn, ATTN_BLOCK), lambda b, i: (b, i, 0, 0)),
            pl.BlockSpec((None, tm, d_lru), lambda b, i: (b, i, 0)),
        ],
        out_shape=[
            jax.ShapeDtypeStruct((B, S // ATTN_BLOCK, d_attn, ATTN_BLOCK), jnp.bfloat16),
            jax.ShapeDtypeStruct((B, S, d_attn), jnp.bfloat16),
            jax.ShapeDtypeStruct((B, S // ATTN_BLOCK, d_attn, ATTN_BLOCK), jnp.bfloat16),
            jax.ShapeDtypeStruct((B, S, d_lru), jnp.bfloat16),
        ],
        scratch_shapes=[
            pltpu.VMEM((tm, D), jnp.bfloat16),
            pltpu.VMEM((2 * n_slab, tm, LANES), jnp.float32),
            pltpu.VMEM((n_slab, tm, LANES), jnp.float32),
            pltpu.VMEM((tm, d_lru), jnp.bfloat16),
            pltpu.VMEM((2 * n_slab, tm, LANES), jnp.float32),
            pltpu.VMEM((n_slab, tm, LANES), jnp.float32),
            pltpu.VMEM((n_slab, tm, LANES), jnp.float32),
            pltpu.VMEM((CHAIN, SUBLANES, d_lru), jnp.float32),
            pltpu.VMEM((SUBLANES, d_lru), jnp.float32),
        ],
        compiler_params=pltpu.CompilerParams(
            dimension_semantics=("parallel", "arbitrary"),
            vmem_limit_bytes=V7X_VMEM_LIMIT_BYTES),
        name="in_proj_lru",
    )(x, g, wug, wk, wqt, wvt, conv_w, conv_b, w_gates, b_gates, lru_l)


N_POS_COLS = 128
N_ONES_ROWS = 16


def _diff_attn_kernel(qt0_ref, qtn_ref, k_ref, pos_ref, vt_ref, lamv_ref, sg_ref, o_ref,
                      s_ref, m_ref, *, slopes):
    h = pl.program_id(1)
    qi = pl.program_id(2)
    blk = ATTN_BLOCK
    n_blocks = vt_ref.shape[0]

    slope = jnp.float32(slopes[0])
    for i in range(1, len(slopes)):
        slope = jnp.where(h == i, jnp.float32(slopes[i]), slope)

    r = lax.broadcasted_iota(jnp.int32, (N_POS_COLS, blk), 0)
    lane = lax.broadcasted_iota(jnp.int32, (N_POS_COLS, blk), 1).astype(jnp.float32)
    zeros = jnp.zeros((HEAD_DIM, blk), qt0_ref.dtype)
    key_in_blk = lax.broadcasted_iota(jnp.int32, (blk, 2 * blk), 0)
    qry_in_blk = lax.broadcasted_iota(jnp.int32, (blk, 2 * blk), 1) % blk
    causal = key_in_blk <= qry_in_blk
    ones_row = lax.broadcasted_iota(jnp.int32, (N_ONES_ROWS, blk), 0) == 0
    ones_blk = jnp.where(ones_row, 1.0, 0.0).astype(jnp.bfloat16)

    def q_operand(qt, q_index):
        alibi = _bf16(jnp.where(r == 0, slope,
                      jnp.where(r == 1, slope * blk,
                      jnp.where(r == 2, -slope * lane,
                      jnp.where(r == 3, -slope * float(q_index * blk), 0.0)))))
        return jnp.concatenate(
            [jnp.concatenate([qt[:HEAD_DIM], zeros, alibi], axis=0),
             jnp.concatenate([zeros, qt[HEAD_DIM:], alibi], axis=0)], axis=1)

    def score_block(q_aug, j, diagonal):
        rows = slice(j * blk, (j + 1) * blk)
        k_aug = jnp.concatenate([k_ref[rows, :], pos_ref[rows, :]], axis=1)
        s = jnp.dot(k_aug, q_aug, preferred_element_type=jnp.float32)
        return jnp.where(causal, s, -jnp.inf) if diagonal else s

    def q_block(n):
        cur, nxt = (n - 1) % 2, n % 2

        def run():
            if n == 1:
                s0 = score_block(q_operand(qt0_ref[...], 0), 0, True)
                s_ref[cur, 0:blk, :] = s0
                m_ref[cur] = jnp.max(s0, axis=0, keepdims=True)
            m_cur = m_ref[cur]
            q_next = q_operand(qtn_ref[...], n) if n < n_blocks else None
            m_next = None
            acc = jnp.zeros((V_HEAD_DIM + N_ONES_ROWS, 2 * blk), jnp.float32)
            for j in range(n + 1):
                if q_next is not None:
                    s = score_block(q_next, j, j == n)
                    s_ref[nxt, j * blk:(j + 1) * blk, :] = s
                    col_max = jnp.max(s, axis=0, keepdims=True)
                    m_next = col_max if m_next is None else jnp.maximum(m_next, col_max)
                if j < n:
                    p = _bf16(jnp.exp(s_ref[cur, j * blk:(j + 1) * blk, :] - m_cur))
                    vt_aug = jnp.concatenate([vt_ref[j], ones_blk], axis=0)
                    acc = acc + jnp.dot(vt_aug, p, preferred_element_type=jnp.float32)
            if q_next is not None:
                m_ref[nxt] = m_next

            lamv = lamv_ref[...]
            lam = (jnp.exp(jnp.sum(lamv[0:1] * lamv[1:2], axis=-1, keepdims=True))
                   - jnp.exp(jnp.sum(lamv[2:3] * lamv[3:4], axis=-1, keepdims=True))
                   + LAMBDA_INIT)
            heads = acc[:V_HEAD_DIM] / acc[V_HEAD_DIM:V_HEAD_DIM + 1]
            o = heads[:, :blk] - lam * heads[:, blk:]
            o = o * lax.rsqrt(jnp.mean(o * o, axis=0, keepdims=True) + EPS)
            o = o * sg_ref[...] * (1.0 - LAMBDA_INIT)
            o_ref[...] = _bf16(o.T)

        return run

    lax.switch(qi, [q_block(n) for n in range(1, n_blocks + 1)])


def _alibi_slopes(n_heads):
    return [2.0 ** (-8.0 * (i + 1) / n_heads) for i in range(n_heads)]


def _key_position_columns(seq_len):
    j = np.arange(seq_len)
    pos = np.zeros((seq_len, N_POS_COLS), np.float32)
    pos[:, 0] = j % ATTN_BLOCK
    pos[:, 1] = j // ATTN_BLOCK
    pos[:, 2:4] = 1.0
    assert ATTN_BLOCK <= 256 and seq_len // ATTN_BLOCK <= 256
    return jnp.asarray(pos, jnp.bfloat16)


def _diff_attn(qt, k, vt, lamv, subln_g):
    B, nblk, d_attn, blk = qt.shape
    S = k.shape[1]
    H = d_attn // V_HEAD_DIM
    pos = _key_position_columns(S)
    kernel = functools.partial(_diff_attn_kernel, slopes=tuple(_alibi_slopes(H)))
    last = nblk - 1
    return pl.pallas_call(
        kernel,
        grid=(B, H, nblk),
        in_specs=[
            pl.BlockSpec((None, None, V_HEAD_DIM, blk), lambda b, h, q: (b, 0, h, 0)),
            pl.BlockSpec((None, None, V_HEAD_DIM, blk),
                         lambda b, h, q: (b, jnp.minimum(q + 1, last), h, 0)),
            pl.BlockSpec((None, S, V_HEAD_DIM), lambda b, h, q: (b, 0, h)),
            pl.BlockSpec(pos.shape, lambda b, h, q: (0, 0)),
            pl.BlockSpec((None, nblk, V_HEAD_DIM, blk), lambda b, h, q: (b, 0, h, 0)),
            pl.BlockSpec(lamv.shape, lambda b, h, q: (0, 0)),
            pl.BlockSpec(subln_g.shape, lambda b, h, q: (0, 0)),
        ],
        out_specs=pl.BlockSpec((None, blk, V_HEAD_DIM), lambda b, h, q: (b, q, h)),
        out_shape=jax.ShapeDtypeStruct((B, S, d_attn), jnp.bfloat16),
        scratch_shapes=[pltpu.VMEM((2, S, 2 * blk), jnp.float32),
                        pltpu.VMEM((2, 1, 2 * blk), jnp.float32)],
        compiler_params=pltpu.CompilerParams(
            dimension_semantics=("parallel", "parallel", "arbitrary"),
            vmem_limit_bytes=V7X_VMEM_LIMIT_BYTES),
        name="diff_attn",
    )(qt, qt, k, pos, vt, lamv, subln_g)


def _out_mlp_kernel(x_ref, attn_ref, lru_ref, wo_ref, gm_ref, wup_ref, wdn_ref, gf_ref, y_ref):
    mix = jnp.concatenate([attn_ref[...], lru_ref[...]], axis=-1)
    x1 = x_ref[...] + jnp.dot(mix, wo_ref[...], preferred_element_type=jnp.float32)
    hm = _bf16(_rmsnorm_rows(x1, gm_ref[...]))
    d_ff = wup_ref.shape[1]
    mlp = jnp.zeros_like(x1)
    for c in range(d_ff // FF_CHUNK):
        cols = slice(c * FF_CHUNK, (c + 1) * FF_CHUNK)
        up = jnp.dot(hm, wup_ref[:, cols], preferred_element_type=jnp.float32)
        act = _bf16(jnp.square(jnp.maximum(up, 0.0)))
        mlp = mlp + jnp.dot(act, wdn_ref[cols, :], preferred_element_type=jnp.float32)
    y_ref[...] = _rmsnorm_rows(x1 + mlp, gf_ref[...])


def _out_mlp(x, attn, lru, w_out, g_mlp, w_up, w_down, g_final):
    B, S, D = x.shape
    T = B * S
    tm = MLP_TOKENS
    x2 = x.reshape(T, D)
    attn2 = attn.reshape(T, attn.shape[-1])
    lru2 = lru.reshape(T, lru.shape[-1])
    const = lambda i: (0, 0)
    resident = pl.Buffered(1)
    y = pl.pallas_call(
        _out_mlp_kernel,
        grid=(T // tm,),
        in_specs=[
            pl.BlockSpec((tm, D), lambda i: (i, 0)),
            pl.BlockSpec((tm, attn2.shape[1]), lambda i: (i, 0)),
            pl.BlockSpec((tm, lru2.shape[1]), lambda i: (i, 0)),
            pl.BlockSpec(w_out.shape, const, pipeline_mode=resident),
            pl.BlockSpec(g_mlp.shape, const),
            pl.BlockSpec(w_up.shape, const, pipeline_mode=resident),
            pl.BlockSpec(w_down.shape, const, pipeline_mode=resident),
            pl.BlockSpec(g_final.shape, const),
        ],
        out_specs=pl.BlockSpec((tm, D), lambda i: (i, 0)),
        out_shape=jax.ShapeDtypeStruct((T, D), jnp.float32),
        compiler_params=pltpu.CompilerParams(
            dimension_semantics=("parallel",),
            vmem_limit_bytes=V7X_VMEM_LIMIT_BYTES),
        name="out_mlp",
    )(x2, attn2, lru2, w_out, g_mlp, w_up, w_down, g_final)
    return y.reshape(B, S, D)


def _block_diag(w):
    n, c, d = w.shape
    eye = jnp.eye(n, dtype=w.dtype)
    return (eye[:, None, :, None] * w[:, :, None, :]).reshape(n * c, n * d)


def kernel(x, norm_mix_g, w_in, conv_w, conv_b, w_rg, b_rg, w_ig, b_ig, lru_L, lambda_q1, lambda_k1, lambda_q2, lambda_k2, subln_g, w_out, norm_mlp_g, w_up, w_down, final_g):
    B, S, D = x.shape
    d_lru = lru_L.shape[-1]
    d_attn = (w_in.shape[-1] - 2 * d_lru) // 3
    depth = w_in.shape[0]
    assert depth == 1, "LAMBDA_INIT is specialised to a single layer"
    assert S % PROJ_TOKENS == 0 and (B * S) % MLP_TOKENS == 0
    assert PROJ_TOKENS % SUB_ROWS == 0 and d_lru % LANES == 0
    l = 0
    row = lambda v: v.reshape(1, -1)

    w = w_in[l]
    wqt = _bf16(w[:, :d_attn] * (HEAD_DIM ** -0.5)).T
    wk = _bf16(w[:, d_attn:2 * d_attn])
    wvt = _bf16(w[:, 2 * d_attn:3 * d_attn]).T
    wug = _bf16(w[:, 3 * d_attn:])
    w_gates = _bf16(0.5 * jnp.concatenate([_block_diag(w_rg[l]), _block_diag(w_ig[l])], axis=1))
    b_gates = 0.5 * jnp.concatenate([b_rg[l].reshape(1, -1), b_ig[l].reshape(1, -1)], axis=1)
    lamv = jnp.stack([lambda_q1[l], lambda_k1[l], lambda_q2[l], lambda_k2[l]]).astype(jnp.float32)

    qt, k, vt, lru = _in_proj_lru(x, row(norm_mix_g[l]), wug, wk, wqt, wvt, conv_w[l],
                                  row(conv_b[l]), w_gates, b_gates, row(lru_L[l]))
    attn = _diff_attn(qt, k, vt, lamv, subln_g[l].reshape(-1, 1))
    return _out_mlp(x, attn, lru, _bf16(w_out[l]), row(norm_mlp_g[l]), _bf16(w_up[l]),
                    _bf16(w_down[l]), row(final_g))
```

```python
import functools
import math

import numpy as np
import jax
import jax.numpy as jnp
from jax import lax
from jax.experimental import pallas as pl
from jax.experimental.pallas import tpu as pltpu

EPS = 1e-6
CONV_WIDTH = 4
LRU_C = 8.0
HEAD_DIM = 64
V_HEAD_DIM = 2 * HEAD_DIM
LAMBDA_INIT = 0.8 - 0.6 * math.exp(-0.3 * 0)

V7X_MXU_COLS = 256
LANES = 128
SUBLANES = 8
V7X_VMEM_LIMIT_BYTES = 56 * 1024 * 1024

ATTN_BLOCK = V7X_MXU_COLS
PROJ_TOKENS = 512
MLP_TOKENS = 512
FF_CHUNK = 1024
CHAIN = 4
SUB_ROWS = SUBLANES * CHAIN

_NT_DIMS = (((1,), (1,)), ((), ()))


def _bf16(x):
    return x.astype(jnp.bfloat16)


def _rmsnorm_rows(x, g):
    return x * lax.rsqrt(jnp.mean(x * x, axis=-1, keepdims=True) + EPS) * g


def _log_sigmoid(x):
    return jnp.minimum(x, 0.0) - jnp.log1p(jnp.exp(-jnp.abs(x)))


def _in_proj_lru_kernel(x_ref, g_ref, wug_ref, wk_ref, wqt_ref, wvt_ref, cw_ref, cb_ref,
                        wg_ref, bg_ref, lru_l_ref,
                        qt_ref, k_ref, vt_ref, lru_ref,
                        h_scr, ug_slab, u_scr, ub_scr, z_scr, gelu_scr, out_slab, tail_ref,
                        carry_ref):
    tm = x_ref.shape[0]
    d_lru = lru_ref.shape[-1]
    n_slab = d_lru // LANES
    n_sub = tm // SUB_ROWS
    f32 = jnp.float32

    @pl.when(pl.program_id(1) == 0)
    def _():
        tail_ref[...] = jnp.zeros(tail_ref.shape, f32)
        carry_ref[...] = jnp.zeros(carry_ref.shape, f32)

    h_scr[...] = _bf16(_rmsnorm_rows(x_ref[...], g_ref[...]))

    def ug_half(half):
        cols = slice(half * d_lru, (half + 1) * d_lru)
        nat = jnp.dot(h_scr[...], wug_ref[:, cols], preferred_element_type=f32)
        for l in range(n_slab):
            ug_slab[half * n_slab + l] = nat[:, l * LANES:(l + 1) * LANES]

    ug_half(0)

    def piece(slab, s, c):
        return ug_slab[slab, pl.ds(s * SUB_ROWS + c, SUBLANES, stride=CHAIN), :]

    row = lax.broadcasted_iota(jnp.int32, (SUBLANES, LANES), 0)

    for l in range(n_slab):
        lanes = slice(l * LANES, (l + 1) * LANES)
        cw = cw_ref[:, lanes]
        cb = cb_ref[:, lanes]
        rolled_prev = [pltpu.roll(tail_ref[c, :, lanes], 1, axis=0) for c in range(CHAIN)]
        for s in range(n_sub):
            cur = [piece(l, s, c) for c in range(CHAIN)]
            rolled = [pltpu.roll(cur[c], 1, axis=0) for c in range(CHAIN)]
            shifted = [jnp.where(row == 0, rolled_prev[c], rolled[c]) for c in range(CHAIN)]
            rolled_prev = rolled
            if s == n_sub - 1:
                for c in range(CHAIN):
                    tail_ref[c, :, lanes] = cur[c]
            conv = []
            for c in range(CHAIN):
                acc = cb
                for kk in range(CONV_WIDTH):
                    d = CONV_WIDTH - 1 - kk
                    src = cur[c - d] if c >= d else shifted[c - d + CHAIN]
                    acc = acc + src * cw[kk:kk + 1, :]
                conv.append(acc)
            u_sub = jnp.concatenate(conv, axis=0)
            u_scr[l, s * SUB_ROWS:(s + 1) * SUB_ROWS, :] = u_sub
            ub_scr[s * SUB_ROWS:(s + 1) * SUB_ROWS, lanes] = _bf16(u_sub)

    ug_half(1)
    z = jnp.dot(ub_scr[...], wg_ref[...], preferred_element_type=f32) + bg_ref[...]
    for l in range(2 * n_slab):
        z_scr[l] = z[:, l * LANES:(l + 1) * LANES]

    def k_piece(part):
        cols = slice(part * ATTN_BLOCK, (part + 1) * ATTN_BLOCK)
        k_ref[:, cols] = _bf16(jnp.dot(h_scr[...], wk_ref[:, cols], preferred_element_type=f32))

    def nt_piece(w_ref, o_ref, part):
        rows = slice(part * ATTN_BLOCK, (part + 1) * ATTN_BLOCK)
        o_ref[part] = _bf16(lax.dot_general(w_ref[...], h_scr[rows, :], _NT_DIMS,
                                            preferred_element_type=f32))

    side_work = {0: lambda: k_piece(0), 2: lambda: k_piece(1),
                 5: lambda: nt_piece(wqt_ref, qt_ref, 0), 8: lambda: nt_piece(wqt_ref, qt_ref, 1),
                 11: lambda: nt_piece(wvt_ref, vt_ref, 0), 13: lambda: nt_piece(wvt_ref, vt_ref, 1)}
    assert n_sub > max(side_work) and qt_ref.shape[0] == 2 and k_ref.shape[1] == 2 * ATTN_BLOCK

    c_gelu = math.sqrt(2.0 / math.pi)
    for l in range(n_slab):
        for s in range(n_sub):
            gate = jnp.concatenate([piece(n_slab + l, s, c) for c in range(CHAIN)], axis=0)
            gelu_scr[l, s * SUB_ROWS:(s + 1) * SUB_ROWS, :] = (0.5 * gate) * (1.0 + jnp.tanh(
                gate * (c_gelu + (c_gelu * 0.044715) * (gate * gate))))

    c2_row = (0.5 * LRU_C * math.log2(math.e)) * _log_sigmoid(lru_l_ref[...])

    carry = [carry_ref[:, l * LANES:(l + 1) * LANES] for l in range(n_slab)]
    for s in range(n_sub):
        if s in side_work:
            side_work[s]()
        base = s * SUB_ROWS
        rows = slice(base, base + SUB_ROWS)
        for l in range(n_slab):
            c2 = c2_row[:, l * LANES:(l + 1) * LANES]
            u = u_scr[l, rows, :]
            t_r = jnp.tanh(z_scr[l, rows, :])
            t_i = jnp.tanh(z_scr[n_slab + l, rows, :])
            log2_a = c2 + c2 * t_r
            a = jnp.exp2(log2_a)
            quarter = (jnp.tanh(log2_a * math.log(2.0)) * -0.25) * (a * a + 1.0)
            half_mult = jnp.where(quarter > 0.0, quarter * lax.rsqrt(quarter), 0.0)
            b = (half_mult * u) * (1.0 + t_i)
            gelu = gelu_scr[l, rows, :]

            a_c = [a[c * SUBLANES:(c + 1) * SUBLANES] for c in range(CHAIN)]
            b_c = [b[c * SUBLANES:(c + 1) * SUBLANES] for c in range(CHAIN)]
            h_loc, p_loc = [b_c[0]], [a_c[0]]
            for c in range(1, CHAIN):
                h_loc.append(a_c[c] * h_loc[c - 1] + b_c[c])
                p_loc.append(a_c[c] * p_loc[c - 1])
            p_cum, h_cum = p_loc[CHAIN - 1], h_loc[CHAIN - 1]
            shift = 1
            while shift < SUBLANES:
                valid = row >= shift
                p_prev = jnp.where(valid, pltpu.roll(p_cum, shift, axis=0), 1.0)
                h_prev = jnp.where(valid, pltpu.roll(h_cum, shift, axis=0), 0.0)
                h_cum = p_cum * h_prev + h_cum
                p_cum = p_cum * p_prev
                shift *= 2
            chain_end = h_cum + p_cum * carry[l]
            chain_in = jnp.where(row == 0, carry[l], pltpu.roll(chain_end, 1, axis=0))
            carry[l] = jnp.broadcast_to(chain_end[SUBLANES - 1:SUBLANES, :], (SUBLANES, LANES))
            for c in range(CHAIN):
                out = (h_loc[c] + p_loc[c] * chain_in) * gelu[c * SUBLANES:(c + 1) * SUBLANES]
                out_slab[l, pl.ds(base + c, SUBLANES, stride=CHAIN), :] = out
    for l in range(n_slab):
        carry_ref[:, l * LANES:(l + 1) * LANES] = carry[l]
    lru_ref[...] = _bf16(jnp.concatenate([out_slab[l] for l in range(n_slab)], axis=1))


def _in_proj_lru(x, g, wug, wk, wqt, wvt, conv_w, conv_b, w_gates, b_gates, lru_l):
    B, S, D = x.shape
    d_attn = wk.shape[1]
    d_lru = wug.shape[1] // 2
    tm = PROJ_TOKENS
    nblk = tm // ATTN_BLOCK
    n_slab = d_lru // LANES
    const = lambda b, i: (0, 0)
    return pl.pallas_call(
        _in_proj_lru_kernel,
        grid=(B, S // tm),
        in_specs=[
            pl.BlockSpec((None, tm, D), lambda b, i: (b, i, 0)),
            pl.BlockSpec((1, D), const),
            pl.BlockSpec(wug.shape, const),
            pl.BlockSpec(wk.shape, const),
            pl.BlockSpec(wqt.shape, const),
            pl.BlockSpec(wvt.shape, const),
            pl.BlockSpec(conv_w.shape, const),
            pl.BlockSpec(conv_b.shape, const),
            pl.BlockSpec(w_gates.shape, const),
            pl.BlockSpec(b_gates.shape, const),
            pl.BlockSpec(lru_l.shape, const),
        ],
        out_specs=[
            pl.BlockSpec((None, nblk, d_attn, ATTN_BLOCK), lambda b, i: (b, i, 0, 0)),
            pl.BlockSpec((None, tm, d_attn), lambda b, i: (b, i, 0)),
            pl.BlockSpec((None, nblk, d_attn, ATTN_BLOCK), lambda b, i: (b, i, 0, 0)),
            pl.BlockSpec((None, tm, d_lru), lambda b, i: (b, i, 0)),
        ],
        out_shape=[
            jax.ShapeDtypeStruct((B, S // ATTN_BLOCK, d_attn, ATTN_BLOCK), jnp.bfloat16),
            jax.ShapeDtypeStruct((B, S, d_attn), jnp.bfloat16),
            jax.ShapeDtypeStruct((B, S // ATTN_BLOCK, d_attn, ATTN_BLOCK), jnp.bfloat16),
            jax.ShapeDtypeStruct((B, S, d_lru), jnp.bfloat16),
        ],
        scratch_shapes=[
            pltpu.VMEM((tm, D), jnp.bfloat16),
            pltpu.VMEM((2 * n_slab, tm, LANES), jnp.float32),
            pltpu.VMEM((n_slab, tm, LANES), jnp.float32),
            pltpu.VMEM((tm, d_lru), jnp.bfloat16),
            pltpu.VMEM((2 * n_slab, tm, LANES), jnp.float32),
            pltpu.VMEM((n_slab, tm, LANES), jnp.float32),
            pltpu.VMEM((n_slab, tm, LANES), jnp.float32),
            pltpu.VMEM((CHAIN, SUBLANES, d_lru), jnp.float32),
            pltpu.VMEM((SUBLANES, d_lru), jnp.float32),
        ],
        compiler_params=pltpu.CompilerParams(
            dimension_semantics=("parallel", "arbitrary"),
            vmem_limit_bytes=V7X_VMEM_LIMIT_BYTES),
        name="in_proj_lru",
    )(x, g, wug, wk, wqt, wvt, conv_w, conv_b, w_gates, b_gates, lru_l)


N_POS_COLS = 128
N_ONES_ROWS = 16


def _diff_attn_kernel(qt0_ref, qtn_ref, k_ref, pos_ref, vt_ref, lamv_ref, sg_ref, o_ref,
                      s_ref, m_ref, acc_ref, *, slopes):
    h = pl.program_id(1)
    qi = pl.program_id(2)
    blk = ATTN_BLOCK
    n_blocks = vt_ref.shape[0]

    slope = jnp.float32(slopes[0])
    for i in range(1, len(slopes)):
        slope = jnp.where(h == i, jnp.float32(slopes[i]), slope)

    r = lax.broadcasted_iota(jnp.int32, (N_POS_COLS, blk), 0)
    lane = lax.broadcasted_iota(jnp.int32, (N_POS_COLS, blk), 1).astype(jnp.float32)
    zeros = jnp.zeros((HEAD_DIM, blk), qt0_ref.dtype)
    key_in_blk = lax.broadcasted_iota(jnp.int32, (blk, 2 * blk), 0)
    qry_in_blk = lax.broadcasted_iota(jnp.int32, (blk, 2 * blk), 1) % blk
    causal = key_in_blk <= qry_in_blk
    ones_row = lax.broadcasted_iota(jnp.int32, (N_ONES_ROWS, blk), 0) == 0
    ones_blk = jnp.where(ones_row, 1.0, 0.0).astype(jnp.bfloat16)

    def q_operand(qt, q_index):
        alibi = _bf16(jnp.where(r == 0, slope,
                      jnp.where(r == 1, slope * blk,
                      jnp.where(r == 2, -slope * lane,
                      jnp.where(r == 3, -slope * float(q_index * blk), 0.0)))))
        return jnp.concatenate(
            [jnp.concatenate([qt[:HEAD_DIM], zeros, alibi], axis=0),
             jnp.concatenate([zeros, qt[HEAD_DIM:], alibi], axis=0)], axis=1)

    def score_block(q_aug, j, diagonal):
        rows = slice(j * blk, (j + 1) * blk)
        k_aug = jnp.concatenate([k_ref[rows, :], pos_ref[rows, :]], axis=1)
        s = jnp.dot(k_aug, q_aug, preferred_element_type=jnp.float32)
        return jnp.where(causal, s, -jnp.inf) if diagonal else s

    def finish_previous():
        acc = acc_ref[...]
        lamv = lamv_ref[...]
        lam = (jnp.exp(jnp.sum(lamv[0:1] * lamv[1:2], axis=-1, keepdims=True))
               - jnp.exp(jnp.sum(lamv[2:3] * lamv[3:4], axis=-1, keepdims=True))
               + LAMBDA_INIT)
        heads = acc[:V_HEAD_DIM] / acc[V_HEAD_DIM:V_HEAD_DIM + 1]
        o = heads[:, :blk] - lam * heads[:, blk:]
        o = o * lax.rsqrt(jnp.mean(o * o, axis=0, keepdims=True) + EPS)
        o = o * sg_ref[...] * (1.0 - LAMBDA_INIT)
        o_ref[...] = _bf16(o.T)

    def grid_step(t):
        n = t + 1
        cur, nxt = (n - 1) % 2, n % 2

        def run():
            if t >= 1:
                finish_previous()
            if t == n_blocks:
                return
            if n == 1:
                s0 = score_block(q_operand(qt0_ref[...], 0), 0, True)
                s_ref[cur, 0:blk, :] = s0
                m_ref[cur] = jnp.max(s0, axis=0, keepdims=True)
            m_cur = m_ref[cur]
            q_next = q_operand(qtn_ref[...], n) if n < n_blocks else None
            m_next = None
            acc = jnp.zeros((V_HEAD_DIM + N_ONES_ROWS, 2 * blk), jnp.float32)
            for j in range(n + 1):
                if q_next is not None:
                    s = score_block(q_next, j, j == n)
                    s_ref[nxt, j * blk:(j + 1) * blk, :] = s
                    col_max = jnp.max(s, axis=0, keepdims=True)
                    m_next = col_max if m_next is None else jnp.maximum(m_next, col_max)
                if j < n:
                    p = _bf16(jnp.exp(s_ref[cur, j * blk:(j + 1) * blk, :] - m_cur))
                    vt_aug = jnp.concatenate([vt_ref[j], ones_blk], axis=0)
                    acc = acc + jnp.dot(vt_aug, p, preferred_element_type=jnp.float32)
            if q_next is not None:
                m_ref[nxt] = m_next
            acc_ref[...] = acc

        return run

    lax.switch(qi, [grid_step(t) for t in range(n_blocks + 1)])


def _alibi_slopes(n_heads):
    return [2.0 ** (-8.0 * (i + 1) / n_heads) for i in range(n_heads)]


def _key_position_columns(seq_len):
    j = np.arange(seq_len)
    pos = np.zeros((seq_len, N_POS_COLS), np.float32)
    pos[:, 0] = j % ATTN_BLOCK
    pos[:, 1] = j // ATTN_BLOCK
    pos[:, 2:4] = 1.0
    assert ATTN_BLOCK <= 256 and seq_len // ATTN_BLOCK <= 256
    return jnp.asarray(pos, jnp.bfloat16)


def _diff_attn(qt, k, vt, lamv, subln_g):
    B, nblk, d_attn, blk = qt.shape
    S = k.shape[1]
    H = d_attn // V_HEAD_DIM
    pos = _key_position_columns(S)
    kernel = functools.partial(_diff_attn_kernel, slopes=tuple(_alibi_slopes(H)))
    last = nblk - 1
    return pl.pallas_call(
        kernel,
        grid=(B, H, nblk + 1),
        in_specs=[
            pl.BlockSpec((None, None, V_HEAD_DIM, blk), lambda b, h, q: (b, 0, h, 0)),
            pl.BlockSpec((None, None, V_HEAD_DIM, blk),
                         lambda b, h, q: (b, jnp.minimum(q + 1, last), h, 0)),
            pl.BlockSpec((None, S, V_HEAD_DIM), lambda b, h, q: (b, 0, h)),
            pl.BlockSpec(pos.shape, lambda b, h, q: (0, 0)),
            pl.BlockSpec((None, nblk, V_HEAD_DIM, blk), lambda b, h, q: (b, 0, h, 0)),
            pl.BlockSpec(lamv.shape, lambda b, h, q: (0, 0)),
            pl.BlockSpec(subln_g.shape, lambda b, h, q: (0, 0)),
        ],
        out_specs=pl.BlockSpec((None, blk, V_HEAD_DIM),
                               lambda b, h, q: (b, jnp.maximum(q - 1, 0), h)),
        out_shape=jax.ShapeDtypeStruct((B, S, d_attn), jnp.bfloat16),
        scratch_shapes=[pltpu.VMEM((2, S, 2 * blk), jnp.float32),
                        pltpu.VMEM((2, 1, 2 * blk), jnp.float32),
                        pltpu.VMEM((V_HEAD_DIM + N_ONES_ROWS, 2 * blk), jnp.float32)],
        compiler_params=pltpu.CompilerParams(
            dimension_semantics=("parallel", "parallel", "arbitrary"),
            vmem_limit_bytes=V7X_VMEM_LIMIT_BYTES),
        name="diff_attn",
    )(qt, qt, k, pos, vt, lamv, subln_g)


def _out_mlp_kernel(x_ref, attn_ref, lru_ref, wo_ref, gm_ref, wup_ref, wdn_ref, gf_ref, y_ref):
    mix = jnp.concatenate([attn_ref[...], lru_ref[...]], axis=-1)
    x1 = x_ref[...] + jnp.dot(mix, wo_ref[...], preferred_element_type=jnp.float32)
    hm = _bf16(_rmsnorm_rows(x1, gm_ref[...]))
    d_ff = wup_ref.shape[1]
    mlp = jnp.zeros_like(x1)
    for c in range(d_ff // FF_CHUNK):
        cols = slice(c * FF_CHUNK, (c + 1) * FF_CHUNK)
        up = jnp.dot(hm, wup_ref[:, cols], preferred_element_type=jnp.float32)
        act = _bf16(jnp.square(jnp.maximum(up, 0.0)))
        mlp = mlp + jnp.dot(act, wdn_ref[cols, :], preferred_element_type=jnp.float32)
    y_ref[...] = _rmsnorm_rows(x1 + mlp, gf_ref[...])


def _out_mlp(x, attn, lru, w_out, g_mlp, w_up, w_down, g_final):
    B, S, D = x.shape
    T = B * S
    tm = MLP_TOKENS
    x2 = x.reshape(T, D)
    attn2 = attn.reshape(T, attn.shape[-1])
    lru2 = lru.reshape(T, lru.shape[-1])
    const = lambda i: (0, 0)
    resident = pl.Buffered(1)
    y = pl.pallas_call(
        _out_mlp_kernel,
        grid=(T // tm,),
        in_specs=[
            pl.BlockSpec((tm, D), lambda i: (i, 0)),
            pl.BlockSpec((tm, attn2.shape[1]), lambda i: (i, 0)),
            pl.BlockSpec((tm, lru2.shape[1]), lambda i: (i, 0)),
            pl.BlockSpec(w_out.shape, const, pipeline_mode=resident),
            pl.BlockSpec(g_mlp.shape, const),
            pl.BlockSpec(w_up.shape, const, pipeline_mode=resident),
            pl.BlockSpec(w_down.shape, const, pipeline_mode=resident),
            pl.BlockSpec(g_final.shape, const),
        ],
        out_specs=pl.BlockSpec((tm, D), lambda i: (i, 0)),
        out_shape=jax.ShapeDtypeStruct((T, D), jnp.float32),
        compiler_params=pltpu.CompilerParams(
            dimension_semantics=("parallel",),
            vmem_limit_bytes=V7X_VMEM_LIMIT_BYTES),
        name="out_mlp",
    )(x2, attn2, lru2, w_out, g_mlp, w_up, w_down, g_final)
    return y.reshape(B, S, D)


def _block_diag(w):
    n, c, d = w.shape
    eye = jnp.eye(n, dtype=w.dtype)
    return (eye[:, None, :, None] * w[:, :, None, :]).reshape(n * c, n * d)


def kernel(x, norm_mix_g, w_in, conv_w, conv_b, w_rg, b_rg, w_ig, b_ig, lru_L, lambda_q1, lambda_k1, lambda_q2, lambda_k2, subln_g, w_out, norm_mlp_g, w_up, w_down, final_g):
    B, S, D = x.shape
    d_lru = lru_L.shape[-1]
    d_attn = (w_in.shape[-1] - 2 * d_lru) // 3
    depth = w_in.shape[0]
    assert depth == 1, "LAMBDA_INIT is specialised to a single layer"
    assert S % PROJ_TOKENS == 0 and (B * S) % MLP_TOKENS == 0
    assert PROJ_TOKENS % SUB_ROWS == 0 and d_lru % LANES == 0
    l = 0
    row = lambda v: v.reshape(1, -1)

    w = w_in[l]
    wqt = _bf16(w[:, :d_attn] * (HEAD_DIM ** -0.5)).T
    wk = _bf16(w[:, d_attn:2 * d_attn])
    wvt = _bf16(w[:, 2 * d_attn:3 * d_attn]).T
    wug = _bf16(w[:, 3 * d_attn:])
    w_gates = _bf16(0.5 * jnp.concatenate([_block_diag(w_rg[l]), _block_diag(w_ig[l])], axis=1))
    b_gates = 0.5 * jnp.concatenate([b_rg[l].reshape(1, -1), b_ig[l].reshape(1, -1)], axis=1)
    lamv = jnp.stack([lambda_q1[l], lambda_k1[l], lambda_q2[l], lambda_k2[l]]).astype(jnp.float32)

    qt, k, vt, lru = _in_proj_lru(x, row(norm_mix_g[l]), wug, wk, wqt, wvt, conv_w[l],
                                  row(conv_b[l]), w_gates, b_gates, row(lru_L[l]))
    attn = _diff_attn(qt, k, vt, lamv, subln_g[l].reshape(-1, 1))
    return _out_mlp(x, attn, lru, _bf16(w_out[l]), row(norm_mlp_g[l]), _bf16(w_up[l]),
                    _bf16(w_down[l]), row(final_g))
```

```python
import functools
import math

import numpy as np
import jax
import jax.numpy as jnp
from jax import lax
from jax.experimental import pallas as pl
from jax.experimental.pallas import tpu as pltpu

EPS = 1e-6
CONV_WIDTH = 4
LRU_C = 8.0
HEAD_DIM = 64
V_HEAD_DIM = 2 * HEAD_DIM
LAMBDA_INIT = 0.8 - 0.6 * math.exp(-0.3 * 0)

V7X_MXU_COLS = 256
LANES = 128
SUBLANES = 8
V7X_VMEM_LIMIT_BYTES = 56 * 1024 * 1024

ATTN_BLOCK = V7X_MXU_COLS
PROJ_TOKENS = 512
MLP_TOKENS = 512
FF_CHUNK = 1024
CHAIN = 4
SUB_ROWS = SUBLANES * CHAIN

_NT_DIMS = (((1,), (1,)), ((), ()))


def _bf16(x):
    return x.astype(jnp.bfloat16)


def _rmsnorm_rows(x, g):
    return x * lax.rsqrt(jnp.mean(x * x, axis=-1, keepdims=True) + EPS) * g


def _log_sigmoid(x):
    return jnp.minimum(x, 0.0) - jnp.log1p(jnp.exp(-jnp.abs(x)))


def _in_proj_lru_kernel(x_ref, g_ref, wug_ref, wk_ref, wqt_ref, wvt_ref, cw_ref, cb_ref,
                        wg_ref, bg_ref, lru_l_ref,
                        qt_ref, k_ref, vt_ref, lru_ref,
                        h_scr, ug_slab, u_scr, ub_scr, z_scr, gelu_scr, out_slab, tail_ref,
                        carry_ref):
    tm = x_ref.shape[0]
    d_lru = lru_ref.shape[-1]
    n_slab = d_lru // LANES
    n_sub = tm // SUB_ROWS
    f32 = jnp.float32

    @pl.when(pl.program_id(1) == 0)
    def _():
        tail_ref[...] = jnp.zeros(tail_ref.shape, f32)
        carry_ref[...] = jnp.zeros(carry_ref.shape, f32)

    h_scr[...] = _bf16(_rmsnorm_rows(x_ref[...], g_ref[...]))

    def ug_half(half):
        cols = slice(half * d_lru, (half + 1) * d_lru)
        nat = jnp.dot(h_scr[...], wug_ref[:, cols], preferred_element_type=f32)
        for l in range(n_slab):
            ug_slab[half * n_slab + l] = nat[:, l * LANES:(l + 1) * LANES]

    ug_half(0)

    def piece(slab, s, c):
        return ug_slab[slab, pl.ds(s * SUB_ROWS + c, SUBLANES, stride=CHAIN), :]

    row = lax.broadcasted_iota(jnp.int32, (SUBLANES, LANES), 0)

    for l in range(n_slab):
        lanes = slice(l * LANES, (l + 1) * LANES)
        cw = cw_ref[:, lanes]
        cb = cb_ref[:, lanes]
        rolled_prev = [pltpu.roll(tail_ref[c, :, lanes], 1, axis=0) for c in range(CHAIN)]
        for s in range(n_sub):
            cur = [piece(l, s, c) for c in range(CHAIN)]
            rolled = [pltpu.roll(cur[c], 1, axis=0) for c in range(CHAIN)]
            shifted = [jnp.where(row == 0, rolled_prev[c], rolled[c]) for c in range(CHAIN)]
            rolled_prev = rolled
            if s == n_sub - 1:
                for c in range(CHAIN):
                    tail_ref[c, :, lanes] = cur[c]
            conv = []
            for c in range(CHAIN):
                acc = cb
                for kk in range(CONV_WIDTH):
                    d = CONV_WIDTH - 1 - kk
                    src = cur[c - d] if c >= d else shifted[c - d + CHAIN]
                    acc = acc + src * cw[kk:kk + 1, :]
                conv.append(acc)
            u_sub = jnp.concatenate(conv, axis=0)
            u_scr[l, s * SUB_ROWS:(s + 1) * SUB_ROWS, :] = u_sub
            ub_scr[s * SUB_ROWS:(s + 1) * SUB_ROWS, lanes] = _bf16(u_sub)

    ug_half(1)
    z = jnp.dot(ub_scr[...], wg_ref[...], preferred_element_type=f32) + bg_ref[...]
    for l in range(2 * n_slab):
        z_scr[l] = z[:, l * LANES:(l + 1) * LANES]

    def k_piece(part):
        cols = slice(part * ATTN_BLOCK, (part + 1) * ATTN_BLOCK)
        k_ref[:, cols] = _bf16(jnp.dot(h_scr[...], wk_ref[:, cols], preferred_element_type=f32))

    def nt_piece(w_ref, o_ref, part):
        rows = slice(part * ATTN_BLOCK, (part + 1) * ATTN_BLOCK)
        o_ref[part] = _bf16(lax.dot_general(w_ref[...], h_scr[rows, :], _NT_DIMS,
                                            preferred_element_type=f32))

    side_work = {0: lambda: k_piece(0), 2: lambda: k_piece(1),
                 5: lambda: nt_piece(wqt_ref, qt_ref, 0), 8: lambda: nt_piece(wqt_ref, qt_ref, 1),
                 11: lambda: nt_piece(wvt_ref, vt_ref, 0), 13: lambda: nt_piece(wvt_ref, vt_ref, 1)}
    assert n_sub > max(side_work) and qt_ref.shape[0] == 2 and k_ref.shape[1] == 2 * ATTN_BLOCK

    c_gelu = math.sqrt(2.0 / math.pi)
    for l in range(n_slab):
        for s in range(n_sub):
            gate = jnp.concatenate([piece(n_slab + l, s, c) for c in range(CHAIN)], axis=0)
            gelu_scr[l, s * SUB_ROWS:(s + 1) * SUB_ROWS, :] = (0.5 * gate) * (1.0 + jnp.tanh(
                gate * (c_gelu + (c_gelu * 0.044715) * (gate * gate))))

    c2_row = (0.5 * LRU_C * math.log2(math.e)) * _log_sigmoid(lru_l_ref[...])

    carry = [carry_ref[:, l * LANES:(l + 1) * LANES] for l in range(n_slab)]
    for s in range(n_sub):
        if s in side_work:
            side_work[s]()
        base = s * SUB_ROWS
        rows = slice(base, base + SUB_ROWS)
        for l in range(n_slab):
            c2 = c2_row[:, l * LANES:(l + 1) * LANES]
            u = u_scr[l, rows, :]
            t_r = jnp.tanh(z_scr[l, rows, :])
            t_i = jnp.tanh(z_scr[n_slab + l, rows, :])
            log2_a = c2 + c2 * t_r
            a = jnp.exp2(log2_a)
            quarter = (jnp.tanh(log2_a * math.log(2.0)) * -0.25) * (a * a + 1.0)
            half_mult = jnp.where(quarter > 0.0, quarter * lax.rsqrt(quarter), 0.0)
            b = (half_mult * u) * (1.0 + t_i)
            gelu = gelu_scr[l, rows, :]

            a_c = [a[c * SUBLANES:(c + 1) * SUBLANES] for c in range(CHAIN)]
            b_c = [b[c * SUBLANES:(c + 1) * SUBLANES] for c in range(CHAIN)]
            h_loc, p_loc = [b_c[0]], [a_c[0]]
            for c in range(1, CHAIN):
                h_loc.append(a_c[c] * h_loc[c - 1] + b_c[c])
                p_loc.append(a_c[c] * p_loc[c - 1])
            p_cum, h_cum = p_loc[CHAIN - 1], h_loc[CHAIN - 1]
            shift = 1
            while shift < SUBLANES:
                valid = row >= shift
                p_prev = jnp.where(valid, pltpu.roll(p_cum, shift, axis=0), 1.0)
                h_prev = jnp.where(valid, pltpu.roll(h_cum, shift, axis=0), 0.0)
                h_cum = p_cum * h_prev + h_cum
                p_cum = p_cum * p_prev
                shift *= 2
            chain_end = h_cum + p_cum * carry[l]
            chain_in = jnp.where(row == 0, carry[l], pltpu.roll(chain_end, 1, axis=0))
            carry[l] = jnp.broadcast_to(chain_end[SUBLANES - 1:SUBLANES, :], (SUBLANES, LANES))
            for c in range(CHAIN):
                out = (h_loc[c] + p_loc[c] * chain_in) * gelu[c * SUBLANES:(c + 1) * SUBLANES]
                out_slab[l, pl.ds(base + c, SUBLANES, stride=CHAIN), :] = out
    for l in range(n_slab):
        carry_ref[:, l * LANES:(l + 1) * LANES] = carry[l]
    lru_ref[...] = _bf16(jnp.concatenate([out_slab[l] for l in range(n_slab)], axis=1))


def _in_proj_lru(x, g, wug, wk, wqt, wvt, conv_w, conv_b, w_gates, b_gates, lru_l):
    B, S, D = x.shape
    d_attn = wk.shape[1]
    d_lru = wug.shape[1] // 2
    tm = PROJ_TOKENS
    nblk = tm // ATTN_BLOCK
    n_slab = d_lru // LANES
    const = lambda b, i: (0, 0)
    return pl.pallas_call(
        _in_proj_lru_kernel,
        grid=(B, S // tm),
        in_specs=[
            pl.BlockSpec((None, tm, D), lambda b, i: (b, i, 0)),
            pl.BlockSpec((1, D), const),
            pl.BlockSpec(wug.shape, const),
            pl.BlockSpec(wk.shape, const),
            pl.BlockSpec(wqt.shape, const),
            pl.BlockSpec(wvt.shape, const),
            pl.BlockSpec(conv_w.shape, const),
            pl.BlockSpec(conv_b.shape, const),
            pl.BlockSpec(w_gates.shape, const),
            pl.BlockSpec(b_gates.shape, const),
            pl.BlockSpec(lru_l.shape, const),
        ],
        out_specs=[
            pl.BlockSpec((None, nblk, d_attn, ATTN_BLOCK), lambda b, i: (b, i, 0, 0)),
            pl.BlockSpec((None, tm, d_attn), lambda b, i: (b, i, 0)),
            pl.BlockSpec((None, nblk, d_attn, ATTN_BLOCK), lambda b, i: (b, i, 0, 0)),
            pl.BlockSpec((None, tm, d_lru), lambda b, i: (b, i, 0)),
        ],
        out_shape=[
            jax.ShapeDtypeStruct((B, S // ATTN_BLOCK, d_attn, ATTN_BLOCK), jnp.bfloat16),
            jax.ShapeDtypeStruct((B, S, d_attn), jnp.bfloat16),
            jax.ShapeDtypeStruct((B, S // ATTN_BLOCK, d_attn, ATTN_BLOCK), jnp.bfloat16),
            jax.ShapeDtypeStruct((B, S, d_lru), jnp.bfloat16),
        ],
        scratch_shapes=[
            pltpu.VMEM((tm, D), jnp.bfloat16),
            pltpu.VMEM((2 * n_slab, tm, LANES), jnp.float32),
            pltpu.VMEM((n_slab, tm, LANES), jnp.float32),
            pltpu.VMEM((tm, d_lru), jnp.bfloat16),
            pltpu.VMEM((2 * n_slab, tm, LANES), jnp.float32),
            pltpu.VMEM((n_slab, tm, LANES), jnp.float32),
            pltpu.VMEM((n_slab, tm, LANES), jnp.float32),
            pltpu.VMEM((CHAIN, SUBLANES, d_lru), jnp.float32),
            pltpu.VMEM((SUBLANES, d_lru), jnp.float32),
        ],
        compiler_params=pltpu.CompilerParams(
            dimension_semantics=("parallel", "arbitrary"),
            vmem_limit_bytes=V7X_VMEM_LIMIT_BYTES),
        name="in_proj_lru",
    )(x, g, wug, wk, wqt, wvt, conv_w, conv_b, w_gates, b_gates, lru_l)


N_POS_COLS = 128
N_ONES_ROWS = 16
ATTN_Q_PER_STEP = 6


def _diff_attn_kernel(qt_ref, k_ref, pos_ref, vt_ref, lamv_ref, sg_ref, o_ref,
                      s_ref, m_ref, acc_ref, *, slopes, q_per_step):
    h = pl.program_id(1)
    blk = ATTN_BLOCK
    n_blocks = vt_ref.shape[0]

    slope = jnp.float32(slopes[0])
    for i in range(1, len(slopes)):
        slope = jnp.where(h == i, jnp.float32(slopes[i]), slope)

    r = lax.broadcasted_iota(jnp.int32, (N_POS_COLS, blk), 0)
    lane = lax.broadcasted_iota(jnp.int32, (N_POS_COLS, blk), 1).astype(jnp.float32)
    zeros = jnp.zeros((HEAD_DIM, blk), qt_ref.dtype)
    key_in_blk = lax.broadcasted_iota(jnp.int32, (blk, 2 * blk), 0)
    qry_in_blk = lax.broadcasted_iota(jnp.int32, (blk, 2 * blk), 1) % blk
    causal = key_in_blk <= qry_in_blk
    ones_row = lax.broadcasted_iota(jnp.int32, (N_ONES_ROWS, blk), 0) == 0
    ones_blk = jnp.where(ones_row, 1.0, 0.0).astype(jnp.bfloat16)

    def q_operand(qt, q_index):
        alibi = _bf16(jnp.where(r == 0, slope,
                      jnp.where(r == 1, slope * blk,
                      jnp.where(r == 2, -slope * lane,
                      jnp.where(r == 3, -slope * float(q_index * blk), 0.0)))))
        return jnp.concatenate(
            [jnp.concatenate([qt[:HEAD_DIM], zeros, alibi], axis=0),
             jnp.concatenate([zeros, qt[HEAD_DIM:], alibi], axis=0)], axis=1)

    def score_block(q_aug, j, diagonal):
        rows = slice(j * blk, (j + 1) * blk)
        k_aug = jnp.concatenate([k_ref[rows, :], pos_ref[rows, :]], axis=1)
        s = jnp.dot(k_aug, q_aug, preferred_element_type=jnp.float32)
        return jnp.where(causal, s, -jnp.inf) if diagonal else s

    def finish(t):
        acc = acc_ref[...]
        lamv = lamv_ref[...]
        lam = (jnp.exp(jnp.sum(lamv[0:1] * lamv[1:2], axis=-1, keepdims=True))
               - jnp.exp(jnp.sum(lamv[2:3] * lamv[3:4], axis=-1, keepdims=True))
               + LAMBDA_INIT)
        heads = acc[:V_HEAD_DIM] / acc[V_HEAD_DIM:V_HEAD_DIM + 1]
        o = heads[:, :blk] - lam * heads[:, blk:]
        o = o * lax.rsqrt(jnp.mean(o * o, axis=0, keepdims=True) + EPS)
        o = o * sg_ref[...] * (1.0 - LAMBDA_INIT)
        o_ref[t * blk:(t + 1) * blk, :] = _bf16(o.T)

    def sub_step(t):
        n = t + 1
        cur, nxt = (n - 1) % 2, n % 2
        if t >= 1:
            finish(t - 1)
        if t == n_blocks:
            return
        if n == 1:
            s0 = score_block(q_operand(qt_ref[0], 0), 0, True)
            s_ref[cur, 0:blk, :] = s0
            m_ref[cur] = jnp.max(s0, axis=0, keepdims=True)
        m_cur = m_ref[cur]
        q_next = q_operand(qt_ref[n], n) if n < n_blocks else None
        m_next = None
        acc = jnp.zeros((V_HEAD_DIM + N_ONES_ROWS, 2 * blk), jnp.float32)
        for j in range(n + 1):
            if q_next is not None:
                s = score_block(q_next, j, j == n)
                s_ref[nxt, j * blk:(j + 1) * blk, :] = s
                col_max = jnp.max(s, axis=0, keepdims=True)
                m_next = col_max if m_next is None else jnp.maximum(m_next, col_max)
            if j < n:
                p = _bf16(jnp.exp(s_ref[cur, j * blk:(j + 1) * blk, :] - m_cur))
                vt_aug = jnp.concatenate([vt_ref[j], ones_blk], axis=0)
                acc = acc + jnp.dot(vt_aug, p, preferred_element_type=jnp.float32)
        if q_next is not None:
            m_ref[nxt] = m_next
        acc_ref[...] = acc

    def grid_step(first):
        def run():
            for t in range(first, min(first + q_per_step, n_blocks + 1)):
                sub_step(t)
        return run

    steps = [grid_step(first) for first in range(0, n_blocks + 1, q_per_step)]
    if len(steps) == 1:
        steps[0]()
    else:
        lax.switch(pl.program_id(2), steps)


def _alibi_slopes(n_heads):
    return [2.0 ** (-8.0 * (i + 1) / n_heads) for i in range(n_heads)]


def _key_position_columns(seq_len):
    j = np.arange(seq_len)
    pos = np.zeros((seq_len, N_POS_COLS), np.float32)
    pos[:, 0] = j % ATTN_BLOCK
    pos[:, 1] = j // ATTN_BLOCK
    pos[:, 2:4] = 1.0
    assert ATTN_BLOCK <= 256 and seq_len // ATTN_BLOCK <= 256
    return jnp.asarray(pos, jnp.bfloat16)


def _diff_attn(qt, k, vt, lamv, subln_g):
    B, nblk, d_attn, blk = qt.shape
    S = k.shape[1]
    H = d_attn // V_HEAD_DIM
    pos = _key_position_columns(S)
    kernel = functools.partial(_diff_attn_kernel, slopes=tuple(_alibi_slopes(H)),
                               q_per_step=ATTN_Q_PER_STEP)
    n_steps = pl.cdiv(nblk + 1, ATTN_Q_PER_STEP)
    return pl.pallas_call(
        kernel,
        grid=(B, H, n_steps),
        in_specs=[
            pl.BlockSpec((None, nblk, V_HEAD_DIM, blk), lambda b, h, q: (b, 0, h, 0)),
            pl.BlockSpec((None, S, V_HEAD_DIM), lambda b, h, q: (b, 0, h)),
            pl.BlockSpec(pos.shape, lambda b, h, q: (0, 0)),
            pl.BlockSpec((None, nblk, V_HEAD_DIM, blk), lambda b, h, q: (b, 0, h, 0)),
            pl.BlockSpec(lamv.shape, lambda b, h, q: (0, 0)),
            pl.BlockSpec(subln_g.shape, lambda b, h, q: (0, 0)),
        ],
        out_specs=pl.BlockSpec((None, S, V_HEAD_DIM), lambda b, h, q: (b, 0, h)),
        out_shape=jax.ShapeDtypeStruct((B, S, d_attn), jnp.bfloat16),
        scratch_shapes=[pltpu.VMEM((2, S, 2 * blk), jnp.float32),
                        pltpu.VMEM((2, 1, 2 * blk), jnp.float32),
                        pltpu.VMEM((V_HEAD_DIM + N_ONES_ROWS, 2 * blk), jnp.float32)],
        compiler_params=pltpu.CompilerParams(
            dimension_semantics=("parallel", "parallel", "arbitrary"),
            vmem_limit_bytes=V7X_VMEM_LIMIT_BYTES),
        name="diff_attn",
    )(qt, k, pos, vt, lamv, subln_g)


def _out_mlp_kernel(x_ref, attn_ref, lru_ref, wo_ref, gm_ref, wup_ref, wdn_ref, gf_ref, y_ref):
    mix = jnp.concatenate([attn_ref[...], lru_ref[...]], axis=-1)
    x1 = x_ref[...] + jnp.dot(mix, wo_ref[...], preferred_element_type=jnp.float32)
    hm = _bf16(_rmsnorm_rows(x1, gm_ref[...]))
    d_ff = wup_ref.shape[1]
    mlp = jnp.zeros_like(x1)
    for c in range(d_ff // FF_CHUNK):
        cols = slice(c * FF_CHUNK, (c + 1) * FF_CHUNK)
        up = jnp.dot(hm, wup_ref[:, cols], preferred_element_type=jnp.float32)
        act = _bf16(jnp.square(jnp.maximum(up, 0.0)))
        mlp = mlp + jnp.dot(act, wdn_ref[cols, :], preferred_element_type=jnp.float32)
    y_ref[...] = _rmsnorm_rows(x1 + mlp, gf_ref[...])


def _out_mlp(x, attn, lru, w_out, g_mlp, w_up, w_down, g_final):
    B, S, D = x.shape
    T = B * S
    tm = MLP_TOKENS
    x2 = x.reshape(T, D)
    attn2 = attn.reshape(T, attn.shape[-1])
    lru2 = lru.reshape(T, lru.shape[-1])
    const = lambda i: (0, 0)
    resident = pl.Buffered(1)
    y = pl.pallas_call(
        _out_mlp_kernel,
        grid=(T // tm,),
        in_specs=[
            pl.BlockSpec((tm, D), lambda i: (i, 0)),
            pl.BlockSpec((tm, attn2.shape[1]), lambda i: (i, 0)),
            pl.BlockSpec((tm, lru2.shape[1]), lambda i: (i, 0)),
            pl.BlockSpec(w_out.shape, const, pipeline_mode=resident),
            pl.BlockSpec(g_mlp.shape, const),
            pl.BlockSpec(w_up.shape, const, pipeline_mode=resident),
            pl.BlockSpec(w_down.shape, const, pipeline_mode=resident),
            pl.BlockSpec(g_final.shape, const),
        ],
        out_specs=pl.BlockSpec((tm, D), lambda i: (i, 0)),
        out_shape=jax.ShapeDtypeStruct((T, D), jnp.float32),
        compiler_params=pltpu.CompilerParams(
            dimension_semantics=("parallel",),
            vmem_limit_bytes=V7X_VMEM_LIMIT_BYTES),
        name="out_mlp",
    )(x2, attn2, lru2, w_out, g_mlp, w_up, w_down, g_final)
    return y.reshape(B, S, D)


def _block_diag(w):
    n, c, d = w.shape
    eye = jnp.eye(n, dtype=w.dtype)
    return (eye[:, None, :, None] * w[:, :, None, :]).reshape(n * c, n * d)


def kernel(x, norm_mix_g, w_in, conv_w, conv_b, w_rg, b_rg, w_ig, b_ig, lru_L, lambda_q1, lambda_k1, lambda_q2, lambda_k2, subln_g, w_out, norm_mlp_g, w_up, w_down, final_g):
    B, S, D = x.shape
    d_lru = lru_L.shape[-1]
    d_attn = (w_in.shape[-1] - 2 * d_lru) // 3
    depth = w_in.shape[0]
    assert depth == 1, "LAMBDA_INIT is specialised to a single layer"
    assert S % PROJ_TOKENS == 0 and (B * S) % MLP_TOKENS == 0
    assert PROJ_TOKENS % SUB_ROWS == 0 and d_lru % LANES == 0
    l = 0
    row = lambda v: v.reshape(1, -1)

    w = w_in[l]
    wqt = _bf16(w[:, :d_attn] * (HEAD_DIM ** -0.5)).T
    wk = _bf16(w[:, d_attn:2 * d_attn])
    wvt = _bf16(w[:, 2 * d_attn:3 * d_attn]).T
    wug = _bf16(w[:, 3 * d_attn:])
    w_gates = _bf16(0.5 * jnp.concatenate([_block_diag(w_rg[l]), _block_diag(w_ig[l])], axis=1))
    b_gates = 0.5 * jnp.concatenate([b_rg[l].reshape(1, -1), b_ig[l].reshape(1, -1)], axis=1)
    lamv = jnp.stack([lambda_q1[l], lambda_k1[l], lambda_q2[l], lambda_k2[l]]).astype(jnp.float32)

    qt, k, vt, lru = _in_proj_lru(x, row(norm_mix_g[l]), wug, wk, wqt, wvt, conv_w[l],
                                  row(conv_b[l]), w_gates, b_gates, row(lru_L[l]))
    attn = _diff_attn(qt, k, vt, lamv, subln_g[l].reshape(-1, 1))
    return _out_mlp(x, attn, lru, _bf16(w_out[l]), row(norm_mlp_g[l]), _bf16(w_up[l]),
                    _bf16(w_down[l]), row(final_g))
```

```python
import functools
import math

import numpy as np
import jax
import jax.numpy as jnp
from jax import lax
from jax.experimental import pallas as pl
from jax.experimental.pallas import tpu as pltpu

EPS = 1e-6
CONV_WIDTH = 4
LRU_C = 8.0
HEAD_DIM = 64
V_HEAD_DIM = 2 * HEAD_DIM
LAMBDA_INIT = 0.8 - 0.6 * math.exp(-0.3 * 0)

V7X_MXU_COLS = 256
LANES = 128
SUBLANES = 8
V7X_VMEM_LIMIT_BYTES = 56 * 1024 * 1024

ATTN_BLOCK = V7X_MXU_COLS
PROJ_TOKENS = 512
MLP_TOKENS = 512
FF_CHUNK = 1024
CHAIN = 4
SUB_ROWS = SUBLANES * CHAIN


def _bf16(x):
    return x.astype(jnp.bfloat16)


def _rmsnorm_rows(x, g):
    return x * lax.rsqrt(jnp.mean(x * x, axis=-1, keepdims=True) + EPS) * g


def _log_sigmoid(x):
    return jnp.minimum(x, 0.0) - jnp.log1p(jnp.exp(-jnp.abs(x)))


def _in_proj_lru_kernel(x_ref, g_ref, w_ref, cw_ref, cb_ref, wg_ref, bg_ref, lru_l_ref,
                        qt_ref, k_ref, vt_ref, lru_ref,
                        h_scr, ug_slab, u_scr, ub_scr, z_scr, gelu_scr, out_slab, tail_ref,
                        carry_ref):
    tm = x_ref.shape[0]
    d_lru = lru_ref.shape[-1]
    n_slab = d_lru // LANES
    n_sub = tm // SUB_ROWS
    d_attn = k_ref.shape[-1]
    f32 = jnp.float32

    @pl.when(pl.program_id(1) == 0)
    def _():
        tail_ref[...] = jnp.zeros(tail_ref.shape, f32)
        carry_ref[...] = jnp.zeros(carry_ref.shape, f32)

    h_scr[...] = _bf16(_rmsnorm_rows(x_ref[...], g_ref[...]))

    def ug_half(half):
        cols = slice(3 * d_attn + half * d_lru, 3 * d_attn + (half + 1) * d_lru)
        nat = jnp.dot(h_scr[...], w_ref[:, cols], preferred_element_type=f32)
        for l in range(n_slab):
            ug_slab[half * n_slab + l] = nat[:, l * LANES:(l + 1) * LANES]

    ug_half(0)

    def piece(slab, s, c):
        return ug_slab[slab, pl.ds(s * SUB_ROWS + c, SUBLANES, stride=CHAIN), :]

    row = lax.broadcasted_iota(jnp.int32, (SUBLANES, LANES), 0)

    for l in range(n_slab):
        lanes = slice(l * LANES, (l + 1) * LANES)
        cw = cw_ref[:, lanes]
        cb = cb_ref[:, lanes]
        rolled_prev = [pltpu.roll(tail_ref[c, :, lanes], 1, axis=0) for c in range(CHAIN)]
        for s in range(n_sub):
            cur = [piece(l, s, c) for c in range(CHAIN)]
            rolled = [pltpu.roll(cur[c], 1, axis=0) for c in range(CHAIN)]
            shifted = [jnp.where(row == 0, rolled_prev[c], rolled[c]) for c in range(CHAIN)]
            rolled_prev = rolled
            if s == n_sub - 1:
                for c in range(CHAIN):
                    tail_ref[c, :, lanes] = cur[c]
            conv = []
            for c in range(CHAIN):
                acc = cb
                for kk in range(CONV_WIDTH):
                    d = CONV_WIDTH - 1 - kk
                    src = cur[c - d] if c >= d else shifted[c - d + CHAIN]
                    acc = acc + src * cw[kk:kk + 1, :]
                conv.append(acc)
            u_sub = jnp.concatenate(conv, axis=0)
            u_scr[l, s * SUB_ROWS:(s + 1) * SUB_ROWS, :] = u_sub
            ub_scr[s * SUB_ROWS:(s + 1) * SUB_ROWS, lanes] = _bf16(u_sub)

    ug_half(1)
    for l in range(n_slab):
        z = jnp.dot(ub_scr[:, l * LANES:(l + 1) * LANES], wg_ref[l],
                    preferred_element_type=f32) + bg_ref[l]
        z_scr[l] = z[:, :LANES]
        z_scr[n_slab + l] = z[:, LANES:]

    def k_piece(part):
        cols = slice(part * ATTN_BLOCK, (part + 1) * ATTN_BLOCK)
        k_ref[:, cols] = _bf16(jnp.dot(h_scr[...], w_ref[:, d_attn + part * ATTN_BLOCK:
                                                          d_attn + (part + 1) * ATTN_BLOCK],
                                       preferred_element_type=f32))

    def transposed_piece(first_col, o_ref, part):
        rows = slice(part * ATTN_BLOCK, (part + 1) * ATTN_BLOCK)
        t = jnp.dot(h_scr[rows, :], w_ref[:, first_col:first_col + d_attn],
                    preferred_element_type=f32)
        o_ref[part] = _bf16(t.T)

    side_work = {0: lambda: k_piece(0), 2: lambda: k_piece(1),
                 5: lambda: transposed_piece(0, qt_ref, 0),
                 8: lambda: transposed_piece(0, qt_ref, 1),
                 11: lambda: transposed_piece(2 * d_attn, vt_ref, 0),
                 13: lambda: transposed_piece(2 * d_attn, vt_ref, 1)}
    assert n_sub > max(side_work) and qt_ref.shape[0] == 2 and d_attn == 2 * ATTN_BLOCK

    c_gelu = math.sqrt(2.0 / math.pi)
    for l in range(n_slab):
        for s in range(n_sub):
            gate = jnp.concatenate([piece(n_slab + l, s, c) for c in range(CHAIN)], axis=0)
            gelu_scr[l, s * SUB_ROWS:(s + 1) * SUB_ROWS, :] = (0.5 * gate) * (1.0 + jnp.tanh(
                gate * (c_gelu + (c_gelu * 0.044715) * (gate * gate))))

    c2_row = (0.5 * LRU_C * math.log2(math.e)) * _log_sigmoid(lru_l_ref[...])

    carry = [carry_ref[:, l * LANES:(l + 1) * LANES] for l in range(n_slab)]
    for s in range(n_sub):
        if s in side_work:
            side_work[s]()
        base = s * SUB_ROWS
        rows = slice(base, base + SUB_ROWS)
        for l in range(n_slab):
            c2 = c2_row[:, l * LANES:(l + 1) * LANES]
            u = u_scr[l, rows, :]
            t_r = jnp.tanh(z_scr[l, rows, :])
            t_i = jnp.tanh(z_scr[n_slab + l, rows, :])
            log2_a = c2 + c2 * t_r
            a = jnp.exp2(log2_a)
            quarter = (jnp.tanh(log2_a * math.log(2.0)) * -0.25) * (a * a + 1.0)
            half_mult = jnp.where(quarter > 0.0, quarter * lax.rsqrt(quarter), 0.0)
            b = (half_mult * u) * (1.0 + t_i)
            gelu = gelu_scr[l, rows, :]

            a_c = [a[c * SUBLANES:(c + 1) * SUBLANES] for c in range(CHAIN)]
            b_c = [b[c * SUBLANES:(c + 1) * SUBLANES] for c in range(CHAIN)]
            h_loc, p_loc = [b_c[0]], [a_c[0]]
            for c in range(1, CHAIN):
                h_loc.append(a_c[c] * h_loc[c - 1] + b_c[c])
                p_loc.append(a_c[c] * p_loc[c - 1])
            p_cum, h_cum = p_loc[CHAIN - 1], h_loc[CHAIN - 1]
            shift = 1
            while shift < SUBLANES:
                valid = row >= shift
                p_prev = jnp.where(valid, pltpu.roll(p_cum, shift, axis=0), 1.0)
                h_prev = jnp.where(valid, pltpu.roll(h_cum, shift, axis=0), 0.0)
                h_cum = p_cum * h_prev + h_cum
                p_cum = p_cum * p_prev
                shift *= 2
            chain_end = h_cum + p_cum * carry[l]
            chain_in = jnp.where(row == 0, carry[l], pltpu.roll(chain_end, 1, axis=0))
            carry[l] = jnp.broadcast_to(chain_end[SUBLANES - 1:SUBLANES, :], (SUBLANES, LANES))
            for c in range(CHAIN):
                out = (h_loc[c] + p_loc[c] * chain_in) * gelu[c * SUBLANES:(c + 1) * SUBLANES]
                out_slab[l, pl.ds(base + c, SUBLANES, stride=CHAIN), :] = out
    for l in range(n_slab):
        carry_ref[:, l * LANES:(l + 1) * LANES] = carry[l]
    lru_ref[...] = _bf16(jnp.concatenate([out_slab[l] for l in range(n_slab)], axis=1))


def _in_proj_lru(x, g, w, conv_w, conv_b, w_gates, b_gates, lru_l):
    B, S, D = x.shape
    d_lru = lru_l.shape[-1]
    d_attn = (w.shape[1] - 2 * d_lru) // 3
    tm = PROJ_TOKENS
    nblk = tm // ATTN_BLOCK
    n_slab = d_lru // LANES
    const = lambda b, i: (0, 0)
    return pl.pallas_call(
        _in_proj_lru_kernel,
        grid=(B, S // tm),
        in_specs=[
            pl.BlockSpec((None, tm, D), lambda b, i: (b, i, 0)),
            pl.BlockSpec((1, D), const),
            pl.BlockSpec(w.shape, const),
            pl.BlockSpec(conv_w.shape, const),
            pl.BlockSpec(conv_b.shape, const),
            pl.BlockSpec(w_gates.shape, lambda b, i: (0, 0, 0)),
            pl.BlockSpec(b_gates.shape, lambda b, i: (0, 0, 0)),
            pl.BlockSpec(lru_l.shape, const),
        ],
        out_specs=[
            pl.BlockSpec((None, nblk, d_attn, ATTN_BLOCK), lambda b, i: (b, i, 0, 0)),
            pl.BlockSpec((None, tm, d_attn), lambda b, i: (b, i, 0)),
            pl.BlockSpec((None, nblk, d_attn, ATTN_BLOCK), lambda b, i: (b, i, 0, 0)),
            pl.BlockSpec((None, tm, d_lru), lambda b, i: (b, i, 0)),
        ],
        out_shape=[
            jax.ShapeDtypeStruct((B, S // ATTN_BLOCK, d_attn, ATTN_BLOCK), jnp.bfloat16),
            jax.ShapeDtypeStruct((B, S, d_attn), jnp.bfloat16),
            jax.ShapeDtypeStruct((B, S // ATTN_BLOCK, d_attn, ATTN_BLOCK), jnp.bfloat16),
            jax.ShapeDtypeStruct((B, S, d_lru), jnp.bfloat16),
        ],
        scratch_shapes=[
            pltpu.VMEM((tm, D), jnp.bfloat16),
            pltpu.VMEM((2 * n_slab, tm, LANES), jnp.float32),
            pltpu.VMEM((n_slab, tm, LANES), jnp.float32),
            pltpu.VMEM((tm, d_lru), jnp.bfloat16),
            pltpu.VMEM((2 * n_slab, tm, LANES), jnp.float32),
            pltpu.VMEM((n_slab, tm, LANES), jnp.float32),
            pltpu.VMEM((n_slab, tm, LANES), jnp.float32),
            pltpu.VMEM((CHAIN, SUBLANES, d_lru), jnp.float32),
            pltpu.VMEM((SUBLANES, d_lru), jnp.float32),
        ],
        compiler_params=pltpu.CompilerParams(
            dimension_semantics=("parallel", "arbitrary"),
            vmem_limit_bytes=V7X_VMEM_LIMIT_BYTES),
        name="in_proj_lru",
    )(x, g, w, conv_w, conv_b, w_gates, b_gates, lru_l)


N_POS_COLS = 128
N_ONES_ROWS = 16
ATTN_Q_PER_STEP = 6


def _diff_attn_kernel(qt_ref, k_ref, pos_ref, vt_ref, lamv_ref, sg_ref, o_ref,
                      s_ref, m_ref, acc_ref, *, slopes, q_per_step):
    h = pl.program_id(1)
    blk = ATTN_BLOCK
    n_blocks = vt_ref.shape[0]

    slope = jnp.float32(slopes[0])
    for i in range(1, len(slopes)):
        slope = jnp.where(h == i, jnp.float32(slopes[i]), slope)

    r = lax.broadcasted_iota(jnp.int32, (N_POS_COLS, blk), 0)
    lane = lax.broadcasted_iota(jnp.int32, (N_POS_COLS, blk), 1).astype(jnp.float32)
    zeros = jnp.zeros((HEAD_DIM, blk), qt_ref.dtype)
    key_in_blk = lax.broadcasted_iota(jnp.int32, (blk, 2 * blk), 0)
    qry_in_blk = lax.broadcasted_iota(jnp.int32, (blk, 2 * blk), 1) % blk
    causal = key_in_blk <= qry_in_blk
    ones_row = lax.broadcasted_iota(jnp.int32, (N_ONES_ROWS, blk), 0) == 0
    ones_blk = jnp.where(ones_row, 1.0, 0.0).astype(jnp.bfloat16)

    def q_operand(qt, q_index):
        alibi = _bf16(jnp.where(r == 0, slope,
                      jnp.where(r == 1, slope * blk,
                      jnp.where(r == 2, -slope * lane,
                      jnp.where(r == 3, -slope * float(q_index * blk), 0.0)))))
        return jnp.concatenate(
            [jnp.concatenate([qt[:HEAD_DIM], zeros, alibi], axis=0),
             jnp.concatenate([zeros, qt[HEAD_DIM:], alibi], axis=0)], axis=1)

    def score_block(q_aug, j, diagonal):
        rows = slice(j * blk, (j + 1) * blk)
        k_aug = jnp.concatenate([k_ref[rows, :], pos_ref[rows, :]], axis=1)
        s = jnp.dot(k_aug, q_aug, preferred_element_type=jnp.float32)
        return jnp.where(causal, s, -jnp.inf) if diagonal else s

    def finish(t):
        acc = acc_ref[...]
        lamv = lamv_ref[...]
        lam = (jnp.exp(jnp.sum(lamv[0:1] * lamv[1:2], axis=-1, keepdims=True))
               - jnp.exp(jnp.sum(lamv[2:3] * lamv[3:4], axis=-1, keepdims=True))
               + LAMBDA_INIT)
        heads = acc[:V_HEAD_DIM] / acc[V_HEAD_DIM:V_HEAD_DIM + 1]
        o = heads[:, :blk] - lam * heads[:, blk:]
        o = o * lax.rsqrt(jnp.mean(o * o, axis=0, keepdims=True) + EPS)
        o = o * sg_ref[...] * (1.0 - LAMBDA_INIT)
        o_ref[t * blk:(t + 1) * blk, :] = _bf16(o.T)

    def sub_step(t):
        n = t + 1
        cur, nxt = (n - 1) % 2, n % 2
        if t >= 1:
            finish(t - 1)
        if t == n_blocks:
            return
        if n == 1:
            s0 = score_block(q_operand(qt_ref[0], 0), 0, True)
            s_ref[cur, 0:blk, :] = s0
            m_ref[cur] = jnp.max(s0, axis=0, keepdims=True)
        m_cur = m_ref[cur]
        q_next = q_operand(qt_ref[n], n) if n < n_blocks else None
        m_next = None
        acc = jnp.zeros((V_HEAD_DIM + N_ONES_ROWS, 2 * blk), jnp.float32)
        for j in range(n + 1):
            if q_next is not None:
                s = score_block(q_next, j, j == n)
                s_ref[nxt, j * blk:(j + 1) * blk, :] = s
                col_max = jnp.max(s, axis=0, keepdims=True)
                m_next = col_max if m_next is None else jnp.maximum(m_next, col_max)
            if j < n:
                p = _bf16(jnp.exp(s_ref[cur, j * blk:(j + 1) * blk, :] - m_cur))
                vt_aug = jnp.concatenate([vt_ref[j], ones_blk], axis=0)
                acc = acc + jnp.dot(vt_aug, p, preferred_element_type=jnp.float32)
        if q_next is not None:
            m_ref[nxt] = m_next
        acc_ref[...] = acc

    def grid_step(first):
        def run():
            for t in range(first, min(first + q_per_step, n_blocks + 1)):
                sub_step(t)
        return run

    steps = [grid_step(first) for first in range(0, n_blocks + 1, q_per_step)]
    if len(steps) == 1:
        steps[0]()
    else:
        lax.switch(pl.program_id(2), steps)


def _alibi_slopes(n_heads):
    return [2.0 ** (-8.0 * (i + 1) / n_heads) for i in range(n_heads)]


def _key_position_columns(seq_len):
    j = np.arange(seq_len)
    pos = np.zeros((seq_len, N_POS_COLS), np.float32)
    pos[:, 0] = j % ATTN_BLOCK
    pos[:, 1] = j // ATTN_BLOCK
    pos[:, 2:4] = 1.0
    assert ATTN_BLOCK <= 256 and seq_len // ATTN_BLOCK <= 256
    return jnp.asarray(pos, jnp.bfloat16)


def _diff_attn(qt, k, vt, lamv, subln_g):
    B, nblk, d_attn, blk = qt.shape
    S = k.shape[1]
    H = d_attn // V_HEAD_DIM
    pos = _key_position_columns(S)
    kernel = functools.partial(_diff_attn_kernel, slopes=tuple(_alibi_slopes(H)),
                               q_per_step=ATTN_Q_PER_STEP)
    n_steps = pl.cdiv(nblk + 1, ATTN_Q_PER_STEP)
    return pl.pallas_call(
        kernel,
        grid=(B, H, n_steps),
        in_specs=[
            pl.BlockSpec((None, nblk, V_HEAD_DIM, blk), lambda b, h, q: (b, 0, h, 0)),
            pl.BlockSpec((None, S, V_HEAD_DIM), lambda b, h, q: (b, 0, h)),
            pl.BlockSpec(pos.shape, lambda b, h, q: (0, 0)),
            pl.BlockSpec((None, nblk, V_HEAD_DIM, blk), lambda b, h, q: (b, 0, h, 0)),
            pl.BlockSpec(lamv.shape, lambda b, h, q: (0, 0)),
            pl.BlockSpec(subln_g.shape, lambda b, h, q: (0, 0)),
        ],
        out_specs=pl.BlockSpec((None, S, V_HEAD_DIM), lambda b, h, q: (b, 0, h)),
        out_shape=jax.ShapeDtypeStruct((B, S, d_attn), jnp.bfloat16),
        scratch_shapes=[pltpu.VMEM((2, S, 2 * blk), jnp.float32),
                        pltpu.VMEM((2, 1, 2 * blk), jnp.float32),
                        pltpu.VMEM((V_HEAD_DIM + N_ONES_ROWS, 2 * blk), jnp.float32)],
        compiler_params=pltpu.CompilerParams(
            dimension_semantics=("parallel", "parallel", "arbitrary"),
            vmem_limit_bytes=V7X_VMEM_LIMIT_BYTES),
        name="diff_attn",
    )(qt, k, pos, vt, lamv, subln_g)


def _out_mlp_kernel(x_ref, attn_ref, lru_ref, wo_ref, gm_ref, wup_ref, wdn_ref, gf_ref, y_ref):
    mix = jnp.concatenate([attn_ref[...], lru_ref[...]], axis=-1)
    x1 = x_ref[...] + jnp.dot(mix, wo_ref[...], preferred_element_type=jnp.float32)
    hm = _bf16(_rmsnorm_rows(x1, gm_ref[...]))
    d_ff = wup_ref.shape[1]
    mlp = jnp.zeros_like(x1)
    for c in range(d_ff // FF_CHUNK):
        cols = slice(c * FF_CHUNK, (c + 1) * FF_CHUNK)
        up = jnp.dot(hm, wup_ref[:, cols], preferred_element_type=jnp.float32)
        act = _bf16(jnp.square(jnp.maximum(up, 0.0)))
        mlp = mlp + jnp.dot(act, wdn_ref[cols, :], preferred_element_type=jnp.float32)
    y_ref[...] = _rmsnorm_rows(x1 + mlp, gf_ref[...])


def _out_mlp(x, attn, lru, w_out, g_mlp, w_up, w_down, g_final):
    B, S, D = x.shape
    T = B * S
    tm = MLP_TOKENS
    x2 = x.reshape(T, D)
    attn2 = attn.reshape(T, attn.shape[-1])
    lru2 = lru.reshape(T, lru.shape[-1])
    const = lambda i: (0, 0)
    resident = pl.Buffered(1)
    y = pl.pallas_call(
        _out_mlp_kernel,
        grid=(T // tm,),
        in_specs=[
            pl.BlockSpec((tm, D), lambda i: (i, 0)),
            pl.BlockSpec((tm, attn2.shape[1]), lambda i: (i, 0)),
            pl.BlockSpec((tm, lru2.shape[1]), lambda i: (i, 0)),
            pl.BlockSpec(w_out.shape, const, pipeline_mode=resident),
            pl.BlockSpec(g_mlp.shape, const),
            pl.BlockSpec(w_up.shape, const, pipeline_mode=resident),
            pl.BlockSpec(w_down.shape, const, pipeline_mode=resident),
            pl.BlockSpec(g_final.shape, const),
        ],
        out_specs=pl.BlockSpec((tm, D), lambda i: (i, 0)),
        out_shape=jax.ShapeDtypeStruct((T, D), jnp.float32),
        compiler_params=pltpu.CompilerParams(
            dimension_semantics=("parallel",),
            vmem_limit_bytes=V7X_VMEM_LIMIT_BYTES),
        name="out_mlp",
    )(x2, attn2, lru2, w_out, g_mlp, w_up, w_down, g_final)
    return y.reshape(B, S, D)


def _pair_block_diag(w):
    n, c, d = w.shape
    eye = jnp.eye(2, dtype=w.dtype)
    pairs = w.reshape(n // 2, 2, c, 1, d) * eye[None, :, None, :, None]
    return pairs.reshape(n // 2, 2 * c, 2 * d)


def kernel(x, norm_mix_g, w_in, conv_w, conv_b, w_rg, b_rg, w_ig, b_ig, lru_L, lambda_q1, lambda_k1, lambda_q2, lambda_k2, subln_g, w_out, norm_mlp_g, w_up, w_down, final_g):
    B, S, D = x.shape
    d_lru = lru_L.shape[-1]
    d_attn = (w_in.shape[-1] - 2 * d_lru) // 3
    depth = w_in.shape[0]
    n_slab = d_lru // LANES
    assert depth == 1, "LAMBDA_INIT is specialised to a single layer"
    assert S % PROJ_TOKENS == 0 and (B * S) % MLP_TOKENS == 0
    assert PROJ_TOKENS % SUB_ROWS == 0 and d_lru % LANES == 0
    assert 2 * w_rg.shape[-1] == LANES and w_rg.shape[1] * w_rg.shape[2] == d_lru
    l = 0
    row = lambda v: v.reshape(1, -1)

    col = lax.broadcasted_iota(jnp.int32, (1, w_in.shape[-1]), 1)
    w = _bf16(w_in[l] * jnp.where(col < d_attn, HEAD_DIM ** -0.5, 1.0))
    w_gates = _bf16(0.5 * jnp.concatenate([_pair_block_diag(w_rg[l]), _pair_block_diag(w_ig[l])],
                                          axis=-1))
    b_gates = 0.5 * jnp.concatenate([b_rg[l].reshape(n_slab, 1, LANES),
                                     b_ig[l].reshape(n_slab, 1, LANES)], axis=-1)
    lamv = jnp.stack([lambda_q1[l], lambda_k1[l], lambda_q2[l], lambda_k2[l]]).astype(jnp.float32)

    qt, k, vt, lru = _in_proj_lru(x, row(norm_mix_g[l]), w, conv_w[l], row(conv_b[l]),
                                  w_gates, b_gates, row(lru_L[l]))
    attn = _diff_attn(qt, k, vt, lamv, subln_g[l].reshape(-1, 1))
    return _out_mlp(x, attn, lru, _bf16(w_out[l]), row(norm_mlp_g[l]), _bf16(w_up[l]),
                    _bf16(w_down[l]), row(final_g))
```

```python
import functools
import math

import numpy as np
import jax
import jax.numpy as jnp
from jax import lax
from jax.experimental import pallas as pl
from jax.experimental.pallas import tpu as pltpu

EPS = 1e-6
CONV_WIDTH = 4
LRU_C = 8.0
HEAD_DIM = 64
V_HEAD_DIM = 2 * HEAD_DIM
LAMBDA_INIT = 0.8 - 0.6 * math.exp(-0.3 * 0)

V7X_MXU_COLS = 256
LANES = 128
SUBLANES = 8
V7X_VMEM_LIMIT_BYTES = 56 * 1024 * 1024

ATTN_BLOCK = V7X_MXU_COLS
PROJ_TOKENS = 512
MLP_TOKENS = 512
FF_CHUNK = 1024
CHAIN = 4
SUB_ROWS = SUBLANES * CHAIN


def _bf16(x):
    return x.astype(jnp.bfloat16)


def _rmsnorm_rows(x, g):
    return x * lax.rsqrt(jnp.mean(x * x, axis=-1, keepdims=True) + EPS) * g


def _log_sigmoid(x):
    return jnp.minimum(x, 0.0) - jnp.log1p(jnp.exp(-jnp.abs(x)))


def _in_proj_lru_kernel(x_ref, g_ref, w_ref, cw_ref, cb_ref, wg_ref, bg_ref, lru_l_ref,
                        qt_ref, k_ref, vt_ref, lru_ref,
                        h_scr, ug_slab, u_scr, ub_scr, z_scr, gelu_scr, out_slab, tail_ref,
                        carry_ref):
    tm = x_ref.shape[0]
    d_lru = lru_ref.shape[-1]
    n_slab = d_lru // LANES
    n_sub = tm // SUB_ROWS
    d_attn = k_ref.shape[-1]
    f32 = jnp.float32

    @pl.when(pl.program_id(1) == 0)
    def _():
        tail_ref[...] = jnp.zeros(tail_ref.shape, f32)
        carry_ref[...] = jnp.zeros(carry_ref.shape, f32)

    h_scr[...] = _bf16(_rmsnorm_rows(x_ref[...], g_ref[...]))

    def ug_half(half):
        cols = slice(3 * d_attn + half * d_lru, 3 * d_attn + (half + 1) * d_lru)
        nat = jnp.dot(h_scr[...], w_ref[:, cols], preferred_element_type=f32)
        for l in range(n_slab):
            ug_slab[half * n_slab + l] = nat[:, l * LANES:(l + 1) * LANES]

    ug_half(0)

    def piece(slab, s, c):
        return ug_slab[slab, pl.ds(s * SUB_ROWS + c, SUBLANES, stride=CHAIN), :]

    row = lax.broadcasted_iota(jnp.int32, (SUBLANES, LANES), 0)

    for l in range(n_slab):
        lanes = slice(l * LANES, (l + 1) * LANES)
        cw = cw_ref[:, lanes]
        cb = cb_ref[:, lanes]
        rolled_prev = [pltpu.roll(tail_ref[c, :, lanes], 1, axis=0) for c in range(CHAIN)]
        for s in range(n_sub):
            cur = [piece(l, s, c) for c in range(CHAIN)]
            rolled = [pltpu.roll(cur[c], 1, axis=0) for c in range(CHAIN)]
            shifted = [jnp.where(row == 0, rolled_prev[c], rolled[c]) for c in range(CHAIN)]
            rolled_prev = rolled
            if s == n_sub - 1:
                for c in range(CHAIN):
                    tail_ref[c, :, lanes] = cur[c]
            conv = []
            for c in range(CHAIN):
                acc = cb
                for kk in range(CONV_WIDTH):
                    d = CONV_WIDTH - 1 - kk
                    src = cur[c - d] if c >= d else shifted[c - d + CHAIN]
                    acc = acc + src * cw[kk:kk + 1, :]
                conv.append(acc)
            u_sub = jnp.concatenate(conv, axis=0)
            u_scr[l, s * SUB_ROWS:(s + 1) * SUB_ROWS, :] = u_sub
            ub_scr[s * SUB_ROWS:(s + 1) * SUB_ROWS, lanes] = _bf16(u_sub)

    ug_half(1)
    for l in range(n_slab):
        z = jnp.dot(ub_scr[:, l * LANES:(l + 1) * LANES], wg_ref[l],
                    preferred_element_type=f32) + bg_ref[l]
        z_scr[l] = z[:, :LANES]
        z_scr[n_slab + l] = z[:, LANES:]

    def k_piece(part):
        cols = slice(part * ATTN_BLOCK, (part + 1) * ATTN_BLOCK)
        k_ref[:, cols] = _bf16(jnp.dot(h_scr[...], w_ref[:, d_attn + part * ATTN_BLOCK:
                                                          d_attn + (part + 1) * ATTN_BLOCK],
                                       preferred_element_type=f32))

    def transposed_piece(first_col, o_ref, part):
        rows = slice(part * ATTN_BLOCK, (part + 1) * ATTN_BLOCK)
        t = jnp.dot(h_scr[rows, :], w_ref[:, first_col:first_col + d_attn],
                    preferred_element_type=f32)
        o_ref[part] = _bf16(t.T)

    side_work = {0: lambda: k_piece(0), 2: lambda: k_piece(1),
                 5: lambda: transposed_piece(0, qt_ref, 0),
                 8: lambda: transposed_piece(0, qt_ref, 1),
                 11: lambda: transposed_piece(2 * d_attn, vt_ref, 0),
                 13: lambda: transposed_piece(2 * d_attn, vt_ref, 1)}
    assert n_sub > max(side_work) and qt_ref.shape[0] == 2 and d_attn == 2 * ATTN_BLOCK

    c_gelu = math.sqrt(2.0 / math.pi)
    for l in range(n_slab):
        for s in range(n_sub):
            gate = jnp.concatenate([piece(n_slab + l, s, c) for c in range(CHAIN)], axis=0)
            gelu_scr[l, s * SUB_ROWS:(s + 1) * SUB_ROWS, :] = (0.5 * gate) * (1.0 + jnp.tanh(
                gate * (c_gelu + (c_gelu * 0.044715) * (gate * gate))))

    c2_row = (0.5 * LRU_C * math.log2(math.e)) * _log_sigmoid(lru_l_ref[...])

    carry = [carry_ref[:, l * LANES:(l + 1) * LANES] for l in range(n_slab)]
    for s in range(n_sub):
        if s in side_work:
            side_work[s]()
        base = s * SUB_ROWS
        rows = slice(base, base + SUB_ROWS)
        for l in range(n_slab):
            c2 = c2_row[:, l * LANES:(l + 1) * LANES]
            u = u_scr[l, rows, :]
            t_r = jnp.tanh(z_scr[l, rows, :])
            t_i = jnp.tanh(z_scr[n_slab + l, rows, :])
            log2_a = c2 + c2 * t_r
            a = jnp.exp2(log2_a)
            quarter = (jnp.tanh(log2_a * math.log(2.0)) * -0.25) * (a * a + 1.0)
            half_mult = jnp.where(quarter > 0.0, quarter * lax.rsqrt(quarter), 0.0)
            b = (half_mult * u) * (1.0 + t_i)
            gelu = gelu_scr[l, rows, :]

            a_c = [a[c * SUBLANES:(c + 1) * SUBLANES] for c in range(CHAIN)]
            b_c = [b[c * SUBLANES:(c + 1) * SUBLANES] for c in range(CHAIN)]
            h_loc, p_loc = [b_c[0]], [a_c[0]]
            for c in range(1, CHAIN):
                h_loc.append(a_c[c] * h_loc[c - 1] + b_c[c])
                p_loc.append(a_c[c] * p_loc[c - 1])
            p_cum, h_cum = p_loc[CHAIN - 1], h_loc[CHAIN - 1]
            shift = 1
            while shift < SUBLANES:
                valid = row >= shift
                p_prev = jnp.where(valid, pltpu.roll(p_cum, shift, axis=0), 1.0)
                h_prev = jnp.where(valid, pltpu.roll(h_cum, shift, axis=0), 0.0)
                h_cum = p_cum * h_prev + h_cum
                p_cum = p_cum * p_prev
                shift *= 2
            chain_end = h_cum + p_cum * carry[l]
            chain_in = jnp.where(row == 0, carry[l], pltpu.roll(chain_end, 1, axis=0))
            carry[l] = jnp.broadcast_to(chain_end[SUBLANES - 1:SUBLANES, :], (SUBLANES, LANES))
            for c in range(CHAIN):
                out = (h_loc[c] + p_loc[c] * chain_in) * gelu[c * SUBLANES:(c + 1) * SUBLANES]
                out_slab[l, pl.ds(base + c, SUBLANES, stride=CHAIN), :] = out
    for l in range(n_slab):
        carry_ref[:, l * LANES:(l + 1) * LANES] = carry[l]
    lru_ref[...] = _bf16(jnp.concatenate([out_slab[l] for l in range(n_slab)], axis=1))


def _in_proj_lru(x, g, w, conv_w, conv_b, w_gates, b_gates, lru_l):
    B, S, D = x.shape
    d_lru = lru_l.shape[-1]
    d_attn = (w.shape[1] - 2 * d_lru) // 3
    tm = PROJ_TOKENS
    nblk = tm // ATTN_BLOCK
    n_slab = d_lru // LANES
    const = lambda b, i: (0, 0)
    return pl.pallas_call(
        _in_proj_lru_kernel,
        grid=(B, S // tm),
        in_specs=[
            pl.BlockSpec((None, tm, D), lambda b, i: (b, i, 0)),
            pl.BlockSpec((1, D), const),
            pl.BlockSpec(w.shape, const),
            pl.BlockSpec(conv_w.shape, const),
            pl.BlockSpec(conv_b.shape, const),
            pl.BlockSpec(w_gates.shape, lambda b, i: (0, 0, 0)),
            pl.BlockSpec(b_gates.shape, lambda b, i: (0, 0, 0)),
            pl.BlockSpec(lru_l.shape, const),
        ],
        out_specs=[
            pl.BlockSpec((None, nblk, d_attn, ATTN_BLOCK), lambda b, i: (b, i, 0, 0)),
            pl.BlockSpec((None, tm, d_attn), lambda b, i: (b, i, 0)),
            pl.BlockSpec((None, nblk, d_attn, ATTN_BLOCK), lambda b, i: (b, i, 0, 0)),
            pl.BlockSpec((None, tm, d_lru), lambda b, i: (b, i, 0)),
        ],
        out_shape=[
            jax.ShapeDtypeStruct((B, S // ATTN_BLOCK, d_attn, ATTN_BLOCK), jnp.bfloat16),
            jax.ShapeDtypeStruct((B, S, d_attn), jnp.bfloat16),
            jax.ShapeDtypeStruct((B, S // ATTN_BLOCK, d_attn, ATTN_BLOCK), jnp.bfloat16),
            jax.ShapeDtypeStruct((B, S, d_lru), jnp.bfloat16),
        ],
        scratch_shapes=[
            pltpu.VMEM((tm, D), jnp.bfloat16),
            pltpu.VMEM((2 * n_slab, tm, LANES), jnp.float32),
            pltpu.VMEM((n_slab, tm, LANES), jnp.float32),
            pltpu.VMEM((tm, d_lru), jnp.bfloat16),
            pltpu.VMEM((2 * n_slab, tm, LANES), jnp.float32),
            pltpu.VMEM((n_slab, tm, LANES), jnp.float32),
            pltpu.VMEM((n_slab, tm, LANES), jnp.float32),
            pltpu.VMEM((CHAIN, SUBLANES, d_lru), jnp.float32),
            pltpu.VMEM((SUBLANES, d_lru), jnp.float32),
        ],
        compiler_params=pltpu.CompilerParams(
            dimension_semantics=("parallel", "arbitrary"),
            vmem_limit_bytes=V7X_VMEM_LIMIT_BYTES),
        name="in_proj_lru",
    )(x, g, w, conv_w, conv_b, w_gates, b_gates, lru_l)


N_POS_COLS = 128
N_ONES_ROWS = 16
ATTN_Q_PER_STEP = 17


def _diff_attn_kernel(qt_ref, k_ref, pos_ref, vt_ref, lamv_ref, sg_ref, o_ref,
                      s_ref, m_ref, acc_ref, *, slopes, q_per_step):
    h = pl.program_id(1)
    blk = ATTN_BLOCK
    n_blocks = vt_ref.shape[0]

    slope = jnp.float32(slopes[0])
    for i in range(1, len(slopes)):
        slope = jnp.where(h == i, jnp.float32(slopes[i]), slope)

    r = lax.broadcasted_iota(jnp.int32, (N_POS_COLS, blk), 0)
    lane = lax.broadcasted_iota(jnp.int32, (N_POS_COLS, blk), 1).astype(jnp.float32)
    zeros = jnp.zeros((HEAD_DIM, blk), qt_ref.dtype)
    key_in_blk = lax.broadcasted_iota(jnp.int32, (blk, 2 * blk), 0)
    qry_in_blk = lax.broadcasted_iota(jnp.int32, (blk, 2 * blk), 1) % blk
    causal = key_in_blk <= qry_in_blk
    ones_row = lax.broadcasted_iota(jnp.int32, (N_ONES_ROWS, blk), 0) == 0
    ones_blk = jnp.where(ones_row, 1.0, 0.0).astype(jnp.bfloat16)

    def q_operand(qt, q_index):
        alibi = _bf16(jnp.where(r == 0, slope,
                      jnp.where(r == 1, slope * blk,
                      jnp.where(r == 2, -slope * lane,
                      jnp.where(r == 3, -slope * float(q_index * blk), 0.0)))))
        return jnp.concatenate(
            [jnp.concatenate([qt[:HEAD_DIM], zeros, alibi], axis=0),
             jnp.concatenate([zeros, qt[HEAD_DIM:], alibi], axis=0)], axis=1)

    def score_block(q_aug, j, diagonal):
        rows = slice(j * blk, (j + 1) * blk)
        k_aug = jnp.concatenate([k_ref[rows, :], pos_ref[rows, :]], axis=1)
        s = jnp.dot(k_aug, q_aug, preferred_element_type=jnp.float32)
        return jnp.where(causal, s, -jnp.inf) if diagonal else s

    def finish(t):
        acc = acc_ref[...]
        lamv = lamv_ref[...]
        lam = (jnp.exp(jnp.sum(lamv[0:1] * lamv[1:2], axis=-1, keepdims=True))
               - jnp.exp(jnp.sum(lamv[2:3] * lamv[3:4], axis=-1, keepdims=True))
               + LAMBDA_INIT)
        heads = acc[:V_HEAD_DIM] / acc[V_HEAD_DIM:V_HEAD_DIM + 1]
        o = heads[:, :blk] - lam * heads[:, blk:]
        o = o * lax.rsqrt(jnp.mean(o * o, axis=0, keepdims=True) + EPS)
        o = o * sg_ref[...] * (1.0 - LAMBDA_INIT)
        o_ref[t * blk:(t + 1) * blk, :] = _bf16(o.T)

    def sub_step(t):
        n = t + 1
        cur, nxt = (n - 1) % 2, n % 2
        if t >= 1:
            finish(t - 1)
        if t == n_blocks:
            return
        if n == 1:
            s0 = score_block(q_operand(qt_ref[0], 0), 0, True)
            s_ref[cur, 0:blk, :] = s0
            m_ref[cur] = jnp.max(s0, axis=0, keepdims=True)
        m_cur = m_ref[cur]
        q_next = q_operand(qt_ref[n], n) if n < n_blocks else None
        m_next = None
        acc = jnp.zeros((V_HEAD_DIM + N_ONES_ROWS, 2 * blk), jnp.float32)
        for j in range(n + 1):
            if q_next is not None:
                s = score_block(q_next, j, j == n)
                s_ref[nxt, j * blk:(j + 1) * blk, :] = s
                col_max = jnp.max(s, axis=0, keepdims=True)
                m_next = col_max if m_next is None else jnp.maximum(m_next, col_max)
            if j < n:
                p = _bf16(jnp.exp(s_ref[cur, j * blk:(j + 1) * blk, :] - m_cur))
                vt_aug = jnp.concatenate([vt_ref[j], ones_blk], axis=0)
                acc = acc + jnp.dot(vt_aug, p, preferred_element_type=jnp.float32)
        if q_next is not None:
            m_ref[nxt] = m_next
        acc_ref[...] = acc

    def grid_step(first):
        def run():
            for t in range(first, min(first + q_per_step, n_blocks + 1)):
                sub_step(t)
        return run

    steps = [grid_step(first) for first in range(0, n_blocks + 1, q_per_step)]
    if len(steps) == 1:
        steps[0]()
    else:
        lax.switch(pl.program_id(2), steps)


def _alibi_slopes(n_heads):
    return [2.0 ** (-8.0 * (i + 1) / n_heads) for i in range(n_heads)]


def _key_position_columns(seq_len):
    j = np.arange(seq_len)
    pos = np.zeros((seq_len, N_POS_COLS), np.float32)
    pos[:, 0] = j % ATTN_BLOCK
    pos[:, 1] = j // ATTN_BLOCK
    pos[:, 2:4] = 1.0
    assert ATTN_BLOCK <= 256 and seq_len // ATTN_BLOCK <= 256
    return jnp.asarray(pos, jnp.bfloat16)


def _diff_attn(qt, k, vt, lamv, subln_g):
    B, nblk, d_attn, blk = qt.shape
    S = k.shape[1]
    H = d_attn // V_HEAD_DIM
    pos = _key_position_columns(S)
    kernel = functools.partial(_diff_attn_kernel, slopes=tuple(_alibi_slopes(H)),
                               q_per_step=ATTN_Q_PER_STEP)
    n_steps = pl.cdiv(nblk + 1, ATTN_Q_PER_STEP)
    return pl.pallas_call(
        kernel,
        grid=(B, H, n_steps),
        in_specs=[
            pl.BlockSpec((None, nblk, V_HEAD_DIM, blk), lambda b, h, q: (b, 0, h, 0)),
            pl.BlockSpec((None, S, V_HEAD_DIM), lambda b, h, q: (b, 0, h)),
            pl.BlockSpec(pos.shape, lambda b, h, q: (0, 0)),
            pl.BlockSpec((None, nblk, V_HEAD_DIM, blk), lambda b, h, q: (b, 0, h, 0)),
            pl.BlockSpec(lamv.shape, lambda b, h, q: (0, 0)),
            pl.BlockSpec(subln_g.shape, lambda b, h, q: (0, 0)),
        ],
        out_specs=pl.BlockSpec((None, S, V_HEAD_DIM), lambda b, h, q: (b, 0, h)),
        out_shape=jax.ShapeDtypeStruct((B, S, d_attn), jnp.bfloat16),
        scratch_shapes=[pltpu.VMEM((2, S, 2 * blk), jnp.float32),
                        pltpu.VMEM((2, 1, 2 * blk), jnp.float32),
                        pltpu.VMEM((V_HEAD_DIM + N_ONES_ROWS, 2 * blk), jnp.float32)],
        compiler_params=pltpu.CompilerParams(
            dimension_semantics=("parallel", "parallel", "arbitrary"),
            vmem_limit_bytes=V7X_VMEM_LIMIT_BYTES),
        name="diff_attn",
    )(qt, k, pos, vt, lamv, subln_g)


def _out_mlp_kernel(x_ref, attn_ref, lru_ref, wo_ref, gm_ref, wup_ref, wdn_ref, gf_ref, y_ref):
    mix = jnp.concatenate([attn_ref[...], lru_ref[...]], axis=-1)
    x1 = x_ref[...] + jnp.dot(mix, wo_ref[...], preferred_element_type=jnp.float32)
    hm = _bf16(_rmsnorm_rows(x1, gm_ref[...]))
    d_ff = wup_ref.shape[1]
    mlp = jnp.zeros_like(x1)
    for c in range(d_ff // FF_CHUNK):
        cols = slice(c * FF_CHUNK, (c + 1) * FF_CHUNK)
        up = jnp.dot(hm, wup_ref[:, cols], preferred_element_type=jnp.float32)
        act = _bf16(jnp.square(jnp.maximum(up, 0.0)))
        mlp = mlp + jnp.dot(act, wdn_ref[cols, :], preferred_element_type=jnp.float32)
    y_ref[...] = _rmsnorm_rows(x1 + mlp, gf_ref[...])


def _out_mlp(x, attn, lru, w_out, g_mlp, w_up, w_down, g_final):
    B, S, D = x.shape
    T = B * S
    tm = MLP_TOKENS
    x2 = x.reshape(T, D)
    attn2 = attn.reshape(T, attn.shape[-1])
    lru2 = lru.reshape(T, lru.shape[-1])
    const = lambda i: (0, 0)
    resident = pl.Buffered(1)
    y = pl.pallas_call(
        _out_mlp_kernel,
        grid=(T // tm,),
        in_specs=[
            pl.BlockSpec((tm, D), lambda i: (i, 0)),
            pl.BlockSpec((tm, attn2.shape[1]), lambda i: (i, 0)),
            pl.BlockSpec((tm, lru2.shape[1]), lambda i: (i, 0)),
            pl.BlockSpec(w_out.shape, const, pipeline_mode=resident),
            pl.BlockSpec(g_mlp.shape, const),
            pl.BlockSpec(w_up.shape, const, pipeline_mode=resident),
            pl.BlockSpec(w_down.shape, const, pipeline_mode=resident),
            pl.BlockSpec(g_final.shape, const),
        ],
        out_specs=pl.BlockSpec((tm, D), lambda i: (i, 0)),
        out_shape=jax.ShapeDtypeStruct((T, D), jnp.float32),
        compiler_params=pltpu.CompilerParams(
            dimension_semantics=("parallel",),
            vmem_limit_bytes=V7X_VMEM_LIMIT_BYTES),
        name="out_mlp",
    )(x2, attn2, lru2, w_out, g_mlp, w_up, w_down, g_final)
    return y.reshape(B, S, D)


def _pair_block_diag(w):
    n, c, d = w.shape
    eye = jnp.eye(2, dtype=w.dtype)
    pairs = w.reshape(n // 2, 2, c, 1, d) * eye[None, :, None, :, None]
    return pairs.reshape(n // 2, 2 * c, 2 * d)


def kernel(x, norm_mix_g, w_in, conv_w, conv_b, w_rg, b_rg, w_ig, b_ig, lru_L, lambda_q1, lambda_k1, lambda_q2, lambda_k2, subln_g, w_out, norm_mlp_g, w_up, w_down, final_g):
    B, S, D = x.shape
    d_lru = lru_L.shape[-1]
    d_attn = (w_in.shape[-1] - 2 * d_lru) // 3
    depth = w_in.shape[0]
    n_slab = d_lru // LANES
    assert depth == 1, "LAMBDA_INIT is specialised to a single layer"
    assert S % PROJ_TOKENS == 0 and (B * S) % MLP_TOKENS == 0
    assert PROJ_TOKENS % SUB_ROWS == 0 and d_lru % LANES == 0
    assert 2 * w_rg.shape[-1] == LANES and w_rg.shape[1] * w_rg.shape[2] == d_lru
    l = 0
    row = lambda v: v.reshape(1, -1)

    col = lax.broadcasted_iota(jnp.int32, (1, w_in.shape[-1]), 1)
    w = _bf16(w_in[l] * jnp.where(col < d_attn, HEAD_DIM ** -0.5, 1.0))
    w_gates = _bf16(0.5 * jnp.concatenate([_pair_block_diag(w_rg[l]), _pair_block_diag(w_ig[l])],
                                          axis=-1))
    b_gates = 0.5 * jnp.concatenate([b_rg[l].reshape(n_slab, 1, LANES),
                                     b_ig[l].reshape(n_slab, 1, LANES)], axis=-1)
    lamv = jnp.stack([lambda_q1[l], lambda_k1[l], lambda_q2[l], lambda_k2[l]]).astype(jnp.float32)

    qt, k, vt, lru = _in_proj_lru(x, row(norm_mix_g[l]), w, conv_w[l], row(conv_b[l]),
                                  w_gates, b_gates, row(lru_L[l]))
    attn = _diff_attn(qt, k, vt, lamv, subln_g[l].reshape(-1, 1))
    return _out_mlp(x, attn, lru, _bf16(w_out[l]), row(norm_mlp_g[l]), _bf16(w_up[l]),
                    _bf16(w_down[l]), row(final_g))
```

```python
import functools
import math

import numpy as np
import jax
import jax.numpy as jnp
from jax import lax
from jax.experimental import pallas as pl
from jax.experimental.pallas import tpu as pltpu

EPS = 1e-6
CONV_WIDTH = 4
LRU_C = 8.0
HEAD_DIM = 64
V_HEAD_DIM = 2 * HEAD_DIM
LAMBDA_INIT = 0.8 - 0.6 * math.exp(-0.3 * 0)

V7X_MXU_COLS = 256
LANES = 128
SUBLANES = 8
V7X_VMEM_LIMIT_BYTES = 56 * 1024 * 1024

ATTN_BLOCK = V7X_MXU_COLS
PROJ_TOKENS = 512
MLP_TOKENS = 512
FF_CHUNK = 1024
CHAIN = 4
SUB_ROWS = SUBLANES * CHAIN


def _bf16(x):
    return x.astype(jnp.bfloat16)


def _rmsnorm_rows(x, g):
    return x * lax.rsqrt(jnp.mean(x * x, axis=-1, keepdims=True) + EPS) * g


def _log_sigmoid(x):
    return jnp.minimum(x, 0.0) - jnp.log1p(jnp.exp(-jnp.abs(x)))


def _in_proj_lru_kernel(x_ref, g_ref, w_ref, cw_ref, cb_ref, wg_ref, bg_ref, lru_l_ref,
                        qt_ref, k_ref, vt_ref, lru_ref,
                        h_scr, ug_slab, u_scr, ub_scr, z_scr, gelu_scr, out_slab, tail_ref,
                        carry_ref):
    tm = x_ref.shape[0]
    d_lru = lru_ref.shape[-1]
    n_slab = d_lru // LANES
    n_sub = tm // SUB_ROWS
    d_attn = k_ref.shape[-1]
    f32 = jnp.float32

    @pl.when(pl.program_id(1) == 0)
    def _():
        tail_ref[...] = jnp.zeros(tail_ref.shape, f32)
        carry_ref[...] = jnp.zeros(carry_ref.shape, f32)

    h_scr[...] = _bf16(_rmsnorm_rows(x_ref[...], g_ref[...]))

    def ug_half(half):
        cols = slice(3 * d_attn + half * d_lru, 3 * d_attn + (half + 1) * d_lru)
        nat = jnp.dot(h_scr[...], w_ref[:, cols], preferred_element_type=f32)
        for l in range(n_slab):
            ug_slab[half * n_slab + l] = nat[:, l * LANES:(l + 1) * LANES]

    ug_half(0)

    def piece(slab, s, c):
        return ug_slab[slab, pl.ds(s * SUB_ROWS + c, SUBLANES, stride=CHAIN), :]

    row = lax.broadcasted_iota(jnp.int32, (SUBLANES, LANES), 0)

    for l in range(n_slab):
        lanes = slice(l * LANES, (l + 1) * LANES)
        cw = cw_ref[:, lanes]
        cb = cb_ref[:, lanes]
        rolled_prev = [pltpu.roll(tail_ref[c, :, lanes], 1, axis=0) for c in range(CHAIN)]
        for s in range(n_sub):
            cur = [piece(l, s, c) for c in range(CHAIN)]
            rolled = [pltpu.roll(cur[c], 1, axis=0) for c in range(CHAIN)]
            shifted = [jnp.where(row == 0, rolled_prev[c], rolled[c]) for c in range(CHAIN)]
            rolled_prev = rolled
            if s == n_sub - 1:
                for c in range(CHAIN):
                    tail_ref[c, :, lanes] = cur[c]
            conv = []
            for c in range(CHAIN):
                acc = cb
                for kk in range(CONV_WIDTH):
                    d = CONV_WIDTH - 1 - kk
                    src = cur[c - d] if c >= d else shifted[c - d + CHAIN]
                    acc = acc + src * cw[kk:kk + 1, :]
                conv.append(acc)
            u_sub = jnp.concatenate(conv, axis=0)
            u_scr[l, s * SUB_ROWS:(s + 1) * SUB_ROWS, :] = u_sub
            ub_scr[s * SUB_ROWS:(s + 1) * SUB_ROWS, lanes] = _bf16(u_sub)

    ug_half(1)
    for l in range(n_slab):
        z = jnp.dot(ub_scr[:, l * LANES:(l + 1) * LANES], wg_ref[l],
                    preferred_element_type=f32) + bg_ref[l]
        z_scr[l] = z[:, :LANES]
        z_scr[n_slab + l] = z[:, LANES:]

    def k_piece(part):
        cols = slice(part * ATTN_BLOCK, (part + 1) * ATTN_BLOCK)
        k_ref[:, cols] = _bf16(jnp.dot(h_scr[...], w_ref[:, d_attn + part * ATTN_BLOCK:
                                                          d_attn + (part + 1) * ATTN_BLOCK],
                                       preferred_element_type=f32))

    def transposed_piece(first_col, o_ref, part):
        rows = slice(part * ATTN_BLOCK, (part + 1) * ATTN_BLOCK)
        t = jnp.dot(h_scr[rows, :], w_ref[:, first_col:first_col + d_attn],
                    preferred_element_type=f32)
        o_ref[part] = _bf16(t.T)

    side_work = {0: lambda: k_piece(0), 2: lambda: k_piece(1),
                 5: lambda: transposed_piece(0, qt_ref, 0),
                 8: lambda: transposed_piece(0, qt_ref, 1),
                 11: lambda: transposed_piece(2 * d_attn, vt_ref, 0),
                 13: lambda: transposed_piece(2 * d_attn, vt_ref, 1)}
    assert n_sub > max(side_work) and qt_ref.shape[0] == 2 and d_attn == 2 * ATTN_BLOCK

    c_gelu = math.sqrt(2.0 / math.pi)
    for l in range(n_slab):
        for s in range(n_sub):
            gate = jnp.concatenate([piece(n_slab + l, s, c) for c in range(CHAIN)], axis=0)
            gelu_scr[l, s * SUB_ROWS:(s + 1) * SUB_ROWS, :] = (0.5 * gate) * (1.0 + jnp.tanh(
                gate * (c_gelu + (c_gelu * 0.044715) * (gate * gate))))

    c2_row = (0.5 * LRU_C * math.log2(math.e)) * _log_sigmoid(lru_l_ref[...])

    carry = [carry_ref[:, l * LANES:(l + 1) * LANES] for l in range(n_slab)]
    for s in range(n_sub):
        if s in side_work:
            side_work[s]()
        base = s * SUB_ROWS
        rows = slice(base, base + SUB_ROWS)
        for l in range(n_slab):
            c2 = c2_row[:, l * LANES:(l + 1) * LANES]
            u = u_scr[l, rows, :]
            t_r = jnp.tanh(z_scr[l, rows, :])
            t_i = jnp.tanh(z_scr[n_slab + l, rows, :])
            log2_a = c2 + c2 * t_r
            a = jnp.exp2(log2_a)
            quarter = (jnp.tanh(log2_a * math.log(2.0)) * -0.25) * (a * a + 1.0)
            half_mult = jnp.where(quarter > 0.0, quarter * lax.rsqrt(quarter), 0.0)
            b = (half_mult * u) * (1.0 + t_i)
            gelu = gelu_scr[l, rows, :]

            a_c = [a[c * SUBLANES:(c + 1) * SUBLANES] for c in range(CHAIN)]
            b_c = [b[c * SUBLANES:(c + 1) * SUBLANES] for c in range(CHAIN)]
            h_loc, p_loc = [b_c[0]], [a_c[0]]
            for c in range(1, CHAIN):
                h_loc.append(a_c[c] * h_loc[c - 1] + b_c[c])
                p_loc.append(a_c[c] * p_loc[c - 1])
            p_cum, h_cum = p_loc[CHAIN - 1], h_loc[CHAIN - 1]
            shift = 1
            while shift < SUBLANES:
                valid = row >= shift
                p_prev = jnp.where(valid, pltpu.roll(p_cum, shift, axis=0), 1.0)
                h_prev = jnp.where(valid, pltpu.roll(h_cum, shift, axis=0), 0.0)
                h_cum = p_cum * h_prev + h_cum
                p_cum = p_cum * p_prev
                shift *= 2
            chain_end = h_cum + p_cum * carry[l]
            chain_in = jnp.where(row == 0, carry[l], pltpu.roll(chain_end, 1, axis=0))
            carry[l] = jnp.broadcast_to(chain_end[SUBLANES - 1:SUBLANES, :], (SUBLANES, LANES))
            for c in range(CHAIN):
                out = (h_loc[c] + p_loc[c] * chain_in) * gelu[c * SUBLANES:(c + 1) * SUBLANES]
                out_slab[l, pl.ds(base + c, SUBLANES, stride=CHAIN), :] = out
    for l in range(n_slab):
        carry_ref[:, l * LANES:(l + 1) * LANES] = carry[l]
    lru_ref[...] = _bf16(jnp.concatenate([out_slab[l] for l in range(n_slab)], axis=1))


def _in_proj_lru(x, g, w, conv_w, conv_b, w_gates, b_gates, lru_l):
    B, S, D = x.shape
    d_lru = lru_l.shape[-1]
    d_attn = (w.shape[1] - 2 * d_lru) // 3
    tm = PROJ_TOKENS
    nblk = tm // ATTN_BLOCK
    n_slab = d_lru // LANES
    const = lambda b, i: (0, 0)
    return pl.pallas_call(
        _in_proj_lru_kernel,
        grid=(B, S // tm),
        in_specs=[
            pl.BlockSpec((None, tm, D), lambda b, i: (b, i, 0)),
            pl.BlockSpec((1, D), const),
            pl.BlockSpec(w.shape, const),
            pl.BlockSpec(conv_w.shape, const),
            pl.BlockSpec(conv_b.shape, const),
            pl.BlockSpec(w_gates.shape, lambda b, i: (0, 0, 0)),
            pl.BlockSpec(b_gates.shape, lambda b, i: (0, 0, 0)),
            pl.BlockSpec(lru_l.shape, const),
        ],
        out_specs=[
            pl.BlockSpec((None, nblk, d_attn, ATTN_BLOCK), lambda b, i: (b, i, 0, 0)),
            pl.BlockSpec((None, tm, d_attn), lambda b, i: (b, i, 0)),
            pl.BlockSpec((None, nblk, d_attn, ATTN_BLOCK), lambda b, i: (b, i, 0, 0)),
            pl.BlockSpec((None, tm, d_lru), lambda b, i: (b, i, 0)),
        ],
        out_shape=[
            jax.ShapeDtypeStruct((B, S // ATTN_BLOCK, d_attn, ATTN_BLOCK), jnp.bfloat16),
            jax.ShapeDtypeStruct((B, S, d_attn), jnp.bfloat16),
            jax.ShapeDtypeStruct((B, S // ATTN_BLOCK, d_attn, ATTN_BLOCK), jnp.bfloat16),
            jax.ShapeDtypeStruct((B, S, d_lru), jnp.bfloat16),
        ],
        scratch_shapes=[
            pltpu.VMEM((tm, D), jnp.bfloat16),
            pltpu.VMEM((2 * n_slab, tm, LANES), jnp.float32),
            pltpu.VMEM((n_slab, tm, LANES), jnp.float32),
            pltpu.VMEM((tm, d_lru), jnp.bfloat16),
            pltpu.VMEM((2 * n_slab, tm, LANES), jnp.float32),
            pltpu.VMEM((n_slab, tm, LANES), jnp.float32),
            pltpu.VMEM((n_slab, tm, LANES), jnp.float32),
            pltpu.VMEM((CHAIN, SUBLANES, d_lru), jnp.float32),
            pltpu.VMEM((SUBLANES, d_lru), jnp.float32),
        ],
        compiler_params=pltpu.CompilerParams(
            dimension_semantics=("parallel", "arbitrary"),
            vmem_limit_bytes=V7X_VMEM_LIMIT_BYTES),
        name="in_proj_lru",
    )(x, g, w, conv_w, conv_b, w_gates, b_gates, lru_l)


N_POS_COLS = 128
N_ONES_ROWS = 16
ATTN_Q_PER_STEP = 9


def _diff_attn_kernel(qt_ref, k_ref, pos_ref, vt_ref, lamv_ref, sg_ref, o_ref,
                      s_ref, m_ref, acc_ref, *, slopes, q_per_step):
    h = pl.program_id(1)
    blk = ATTN_BLOCK
    n_blocks = vt_ref.shape[0]

    slope = jnp.float32(slopes[0])
    for i in range(1, len(slopes)):
        slope = jnp.where(h == i, jnp.float32(slopes[i]), slope)

    r = lax.broadcasted_iota(jnp.int32, (N_POS_COLS, blk), 0)
    lane = lax.broadcasted_iota(jnp.int32, (N_POS_COLS, blk), 1).astype(jnp.float32)
    zeros = jnp.zeros((HEAD_DIM, blk), qt_ref.dtype)
    key_in_blk = lax.broadcasted_iota(jnp.int32, (blk, 2 * blk), 0)
    qry_in_blk = lax.broadcasted_iota(jnp.int32, (blk, 2 * blk), 1) % blk
    causal = key_in_blk <= qry_in_blk
    ones_row = lax.broadcasted_iota(jnp.int32, (N_ONES_ROWS, blk), 0) == 0
    ones_blk = jnp.where(ones_row, 1.0, 0.0).astype(jnp.bfloat16)

    def q_operand(qt, q_index):
        alibi = _bf16(jnp.where(r == 0, slope,
                      jnp.where(r == 1, slope * blk,
                      jnp.where(r == 2, -slope * lane,
                      jnp.where(r == 3, -slope * float(q_index * blk), 0.0)))))
        return jnp.concatenate(
            [jnp.concatenate([qt[:HEAD_DIM], zeros, alibi], axis=0),
             jnp.concatenate([zeros, qt[HEAD_DIM:], alibi], axis=0)], axis=1)

    def score_block(q_aug, j, diagonal):
        rows = slice(j * blk, (j + 1) * blk)
        k_aug = jnp.concatenate([k_ref[rows, :], pos_ref[rows, :]], axis=1)
        s = jnp.dot(k_aug, q_aug, preferred_element_type=jnp.float32)
        return jnp.where(causal, s, -jnp.inf) if diagonal else s

    def finish(t):
        acc = acc_ref[...]
        lamv = lamv_ref[...]
        lam = (jnp.exp(jnp.sum(lamv[0:1] * lamv[1:2], axis=-1, keepdims=True))
               - jnp.exp(jnp.sum(lamv[2:3] * lamv[3:4], axis=-1, keepdims=True))
               + LAMBDA_INIT)
        heads = acc[:V_HEAD_DIM] / acc[V_HEAD_DIM:V_HEAD_DIM + 1]
        o = heads[:, :blk] - lam * heads[:, blk:]
        o = o * lax.rsqrt(jnp.mean(o * o, axis=0, keepdims=True) + EPS)
        o = o * sg_ref[...] * (1.0 - LAMBDA_INIT)
        o_ref[t * blk:(t + 1) * blk, :] = _bf16(o.T)

    def sub_step(t):
        n = t + 1
        cur, nxt = (n - 1) % 2, n % 2
        if t >= 1:
            finish(t - 1)
        if t == n_blocks:
            return
        if n == 1:
            s0 = score_block(q_operand(qt_ref[0], 0), 0, True)
            s_ref[cur, 0:blk, :] = s0
            m_ref[cur] = jnp.max(s0, axis=0, keepdims=True)
        m_cur = m_ref[cur]
        q_next = q_operand(qt_ref[n], n) if n < n_blocks else None
        m_next = None
        acc = jnp.zeros((V_HEAD_DIM + N_ONES_ROWS, 2 * blk), jnp.float32)
        for j in range(n + 1):
            if q_next is not None:
                s = score_block(q_next, j, j == n)
                s_ref[nxt, j * blk:(j + 1) * blk, :] = s
                col_max = jnp.max(s, axis=0, keepdims=True)
                m_next = col_max if m_next is None else jnp.maximum(m_next, col_max)
            if j < n:
                p = _bf16(jnp.exp(s_ref[cur, j * blk:(j + 1) * blk, :] - m_cur))
                vt_aug = jnp.concatenate([vt_ref[j], ones_blk], axis=0)
                acc = acc + jnp.dot(vt_aug, p, preferred_element_type=jnp.float32)
        if q_next is not None:
            m_ref[nxt] = m_next
        acc_ref[...] = acc

    def grid_step(first):
        def run():
            for t in range(first, min(first + q_per_step, n_blocks + 1)):
                sub_step(t)
        return run

    steps = [grid_step(first) for first in range(0, n_blocks + 1, q_per_step)]
    if len(steps) == 1:
        steps[0]()
    else:
        lax.switch(pl.program_id(2), steps)


def _alibi_slopes(n_heads):
    return [2.0 ** (-8.0 * (i + 1) / n_heads) for i in range(n_heads)]


def _key_position_columns(seq_len):
    j = np.arange(seq_len)
    pos = np.zeros((seq_len, N_POS_COLS), np.float32)
    pos[:, 0] = j % ATTN_BLOCK
    pos[:, 1] = j // ATTN_BLOCK
    pos[:, 2:4] = 1.0
    assert ATTN_BLOCK <= 256 and seq_len // ATTN_BLOCK <= 256
    return jnp.asarray(pos, jnp.bfloat16)


def _diff_attn(qt, k, vt, lamv, subln_g):
    B, nblk, d_attn, blk = qt.shape
    S = k.shape[1]
    H = d_attn // V_HEAD_DIM
    pos = _key_position_columns(S)
    kernel = functools.partial(_diff_attn_kernel, slopes=tuple(_alibi_slopes(H)),
                               q_per_step=ATTN_Q_PER_STEP)
    n_steps = pl.cdiv(nblk + 1, ATTN_Q_PER_STEP)
    return pl.pallas_call(
        kernel,
        grid=(B, H, n_steps),
        in_specs=[
            pl.BlockSpec((None, nblk, V_HEAD_DIM, blk), lambda b, h, q: (b, 0, h, 0)),
            pl.BlockSpec((None, S, V_HEAD_DIM), lambda b, h, q: (b, 0, h)),
            pl.BlockSpec(pos.shape, lambda b, h, q: (0, 0)),
            pl.BlockSpec((None, nblk, V_HEAD_DIM, blk), lambda b, h, q: (b, 0, h, 0)),
            pl.BlockSpec(lamv.shape, lambda b, h, q: (0, 0)),
            pl.BlockSpec(subln_g.shape, lambda b, h, q: (0, 0)),
        ],
        out_specs=pl.BlockSpec((None, S, V_HEAD_DIM), lambda b, h, q: (b, 0, h)),
        out_shape=jax.ShapeDtypeStruct((B, S, d_attn), jnp.bfloat16),
        scratch_shapes=[pltpu.VMEM((2, S, 2 * blk), jnp.float32),
                        pltpu.VMEM((2, 1, 2 * blk), jnp.float32),
                        pltpu.VMEM((V_HEAD_DIM + N_ONES_ROWS, 2 * blk), jnp.float32)],
        compiler_params=pltpu.CompilerParams(
            dimension_semantics=("parallel", "parallel", "arbitrary"),
            vmem_limit_bytes=V7X_VMEM_LIMIT_BYTES),
        name="diff_attn",
    )(qt, k, pos, vt, lamv, subln_g)


def _out_mlp_kernel(x_ref, attn_ref, lru_ref, wo_ref, gm_ref, wup_ref, wdn_ref, gf_ref, y_ref):
    mix = jnp.concatenate([attn_ref[...], lru_ref[...]], axis=-1)
    x1 = x_ref[...] + jnp.dot(mix, wo_ref[...], preferred_element_type=jnp.float32)
    hm = _bf16(_rmsnorm_rows(x1, gm_ref[...]))
    d_ff = wup_ref.shape[1]
    mlp = jnp.zeros_like(x1)
    for c in range(d_ff // FF_CHUNK):
        cols = slice(c * FF_CHUNK, (c + 1) * FF_CHUNK)
        up = jnp.dot(hm, wup_ref[:, cols], preferred_element_type=jnp.float32)
        act = _bf16(jnp.square(jnp.maximum(up, 0.0)))
        mlp = mlp + jnp.dot(act, wdn_ref[cols, :], preferred_element_type=jnp.float32)
    y_ref[...] = _rmsnorm_rows(x1 + mlp, gf_ref[...])


def _out_mlp(x, attn, lru, w_out, g_mlp, w_up, w_down, g_final):
    B, S, D = x.shape
    T = B * S
    tm = MLP_TOKENS
    x2 = x.reshape(T, D)
    attn2 = attn.reshape(T, attn.shape[-1])
    lru2 = lru.reshape(T, lru.shape[-1])
    const = lambda i: (0, 0)
    resident = pl.Buffered(1)
    y = pl.pallas_call(
        _out_mlp_kernel,
        grid=(T // tm,),
        in_specs=[
            pl.BlockSpec((tm, D), lambda i: (i, 0)),
            pl.BlockSpec((tm, attn2.shape[1]), lambda i: (i, 0)),
            pl.BlockSpec((tm, lru2.shape[1]), lambda i: (i, 0)),
            pl.BlockSpec(w_out.shape, const, pipeline_mode=resident),
            pl.BlockSpec(g_mlp.shape, const),
            pl.BlockSpec(w_up.shape, const, pipeline_mode=resident),
            pl.BlockSpec(w_down.shape, const, pipeline_mode=resident),
            pl.BlockSpec(g_final.shape, const),
        ],
        out_specs=pl.BlockSpec((tm, D), lambda i: (i, 0)),
        out_shape=jax.ShapeDtypeStruct((T, D), jnp.float32),
        compiler_params=pltpu.CompilerParams(
            dimension_semantics=("parallel",),
            vmem_limit_bytes=V7X_VMEM_LIMIT_BYTES),
        name="out_mlp",
    )(x2, attn2, lru2, w_out, g_mlp, w_up, w_down, g_final)
    return y.reshape(B, S, D)


def _pair_block_diag(w):
    n, c, d = w.shape
    eye = jnp.eye(2, dtype=w.dtype)
    pairs = w.reshape(n // 2, 2, c, 1, d) * eye[None, :, None, :, None]
    return pairs.reshape(n // 2, 2 * c, 2 * d)


def kernel(x, norm_mix_g, w_in, conv_w, conv_b, w_rg, b_rg, w_ig, b_ig, lru_L, lambda_q1, lambda_k1, lambda_q2, lambda_k2, subln_g, w_out, norm_mlp_g, w_up, w_down, final_g):
    B, S, D = x.shape
    d_lru = lru_L.shape[-1]
    d_attn = (w_in.shape[-1] - 2 * d_lru) // 3
    depth = w_in.shape[0]
    n_slab = d_lru // LANES
    assert depth == 1, "LAMBDA_INIT is specialised to a single layer"
    assert S % PROJ_TOKENS == 0 and (B * S) % MLP_TOKENS == 0
    assert PROJ_TOKENS % SUB_ROWS == 0 and d_lru % LANES == 0
    assert 2 * w_rg.shape[-1] == LANES and w_rg.shape[1] * w_rg.shape[2] == d_lru
    l = 0
    row = lambda v: v.reshape(1, -1)

    col = lax.broadcasted_iota(jnp.int32, (1, w_in.shape[-1]), 1)
    w = _bf16(w_in[l] * jnp.where(col < d_attn, HEAD_DIM ** -0.5, 1.0))
    w_gates = _bf16(0.5 * jnp.concatenate([_pair_block_diag(w_rg[l]), _pair_block_diag(w_ig[l])],
                                          axis=-1))
    b_gates = 0.5 * jnp.concatenate([b_rg[l].reshape(n_slab, 1, LANES),
                                     b_ig[l].reshape(n_slab, 1, LANES)], axis=-1)
    lamv = jnp.stack([lambda_q1[l], lambda_k1[l], lambda_q2[l], lambda_k2[l]]).astype(jnp.float32)

    qt, k, vt, lru = _in_proj_lru(x, row(norm_mix_g[l]), w, conv_w[l], row(conv_b[l]),
                                  w_gates, b_gates, row(lru_L[l]))
    attn = _diff_attn(qt, k, vt, lamv, subln_g[l].reshape(-1, 1))
    return _out_mlp(x, attn, lru, _bf16(w_out[l]), row(norm_mlp_g[l]), _bf16(w_up[l]),
                    _bf16(w_down[l]), row(final_g))
```

```python
import functools
import math

import numpy as np
import jax
import jax.numpy as jnp
from jax import lax
from jax.experimental import pallas as pl
from jax.experimental.pallas import tpu as pltpu

EPS = 1e-6
CONV_WIDTH = 4
LRU_C = 8.0
HEAD_DIM = 64
V_HEAD_DIM = 2 * HEAD_DIM
LAMBDA_INIT = 0.8 - 0.6 * math.exp(-0.3 * 0)

V7X_MXU_COLS = 256
LANES = 128
SUBLANES = 8
V7X_VMEM_LIMIT_BYTES = 56 * 1024 * 1024

ATTN_BLOCK = V7X_MXU_COLS
PROJ_TOKENS = 512
MLP_TOKENS = 512
FF_CHUNK = 1024
MLP_SPLIT = 2
CHAIN = 4
SUB_ROWS = SUBLANES * CHAIN


def _bf16(x):
    return x.astype(jnp.bfloat16)


def _rmsnorm_rows(x, g):
    return x * lax.rsqrt(jnp.mean(x * x, axis=-1, keepdims=True) + EPS) * g


def _log_sigmoid(x):
    return jnp.minimum(x, 0.0) - jnp.log1p(jnp.exp(-jnp.abs(x)))


def _in_proj_lru_kernel(x_ref, g_ref, w_ref, cw_ref, cb_ref, wg_ref, bg_ref, lru_l_ref,
                        qt_ref, k_ref, vt_ref, lru_ref,
                        h_scr, ug_slab, u_scr, ub_scr, z_scr, gelu_scr, out_slab, tail_ref,
                        carry_ref):
    tm = x_ref.shape[0]
    d_lru = lru_ref.shape[-1]
    n_slab = d_lru // LANES
    n_sub = tm // SUB_ROWS
    d_attn = k_ref.shape[-1]
    f32 = jnp.float32

    @pl.when(pl.program_id(1) == 0)
    def _():
        tail_ref[...] = jnp.zeros(tail_ref.shape, f32)
        carry_ref[...] = jnp.zeros(carry_ref.shape, f32)

    h_scr[...] = _bf16(_rmsnorm_rows(x_ref[...], g_ref[...]))

    def ug_half(half):
        cols = slice(3 * d_attn + half * d_lru, 3 * d_attn + (half + 1) * d_lru)
        nat = jnp.dot(h_scr[...], w_ref[:, cols], preferred_element_type=f32)
        for l in range(n_slab):
            ug_slab[half * n_slab + l] = nat[:, l * LANES:(l + 1) * LANES]

    ug_half(0)

    def piece(slab, s, c):
        return ug_slab[slab, pl.ds(s * SUB_ROWS + c, SUBLANES, stride=CHAIN), :]

    row = lax.broadcasted_iota(jnp.int32, (SUBLANES, LANES), 0)

    for l in range(n_slab):
        lanes = slice(l * LANES, (l + 1) * LANES)
        cw = cw_ref[:, lanes]
        cb = cb_ref[:, lanes]
        rolled_prev = [pltpu.roll(tail_ref[c, :, lanes], 1, axis=0) for c in range(CHAIN)]
        for s in range(n_sub):
            cur = [piece(l, s, c) for c in range(CHAIN)]
            rolled = [pltpu.roll(cur[c], 1, axis=0) for c in range(CHAIN)]
            shifted = [jnp.where(row == 0, rolled_prev[c], rolled[c]) for c in range(CHAIN)]
            rolled_prev = rolled
            if s == n_sub - 1:
                for c in range(CHAIN):
                    tail_ref[c, :, lanes] = cur[c]
            conv = []
            for c in range(CHAIN):
                acc = cb
                for kk in range(CONV_WIDTH):
                    d = CONV_WIDTH - 1 - kk
                    src = cur[c - d] if c >= d else shifted[c - d + CHAIN]
                    acc = acc + src * cw[kk:kk + 1, :]
                conv.append(acc)
            u_sub = jnp.concatenate(conv, axis=0)
            u_scr[l, s * SUB_ROWS:(s + 1) * SUB_ROWS, :] = u_sub
            ub_scr[s * SUB_ROWS:(s + 1) * SUB_ROWS, lanes] = _bf16(u_sub)

    ug_half(1)
    for l in range(n_slab):
        z = jnp.dot(ub_scr[:, l * LANES:(l + 1) * LANES], wg_ref[l],
                    preferred_element_type=f32) + bg_ref[l]
        z_scr[l] = z[:, :LANES]
        z_scr[n_slab + l] = z[:, LANES:]

    def k_piece(part):
        cols = slice(part * ATTN_BLOCK, (part + 1) * ATTN_BLOCK)
        k_ref[:, cols] = _bf16(jnp.dot(h_scr[...], w_ref[:, d_attn + part * ATTN_BLOCK:
                                                          d_attn + (part + 1) * ATTN_BLOCK],
                                       preferred_element_type=f32))

    def transposed_piece(first_col, o_ref, part):
        rows = slice(part * ATTN_BLOCK, (part + 1) * ATTN_BLOCK)
        t = jnp.dot(h_scr[rows, :], w_ref[:, first_col:first_col + d_attn],
                    preferred_element_type=f32)
        o_ref[part] = _bf16(t.T)

    side_work = {0: lambda: k_piece(0), 2: lambda: k_piece(1),
                 5: lambda: transposed_piece(0, qt_ref, 0),
                 8: lambda: transposed_piece(0, qt_ref, 1),
                 11: lambda: transposed_piece(2 * d_attn, vt_ref, 0),
                 13: lambda: transposed_piece(2 * d_attn, vt_ref, 1)}
    assert n_sub > max(side_work) and qt_ref.shape[0] == 2 and d_attn == 2 * ATTN_BLOCK

    c_gelu = math.sqrt(2.0 / math.pi)
    for l in range(n_slab):
        for s in range(n_sub):
            gate = jnp.concatenate([piece(n_slab + l, s, c) for c in range(CHAIN)], axis=0)
            gelu_scr[l, s * SUB_ROWS:(s + 1) * SUB_ROWS, :] = (0.5 * gate) * (1.0 + jnp.tanh(
                gate * (c_gelu + (c_gelu * 0.044715) * (gate * gate))))

    c2_row = (0.5 * LRU_C * math.log2(math.e)) * _log_sigmoid(lru_l_ref[...])

    carry = [carry_ref[:, l * LANES:(l + 1) * LANES] for l in range(n_slab)]
    for s in range(n_sub):
        if s in side_work:
            side_work[s]()
        base = s * SUB_ROWS
        rows = slice(base, base + SUB_ROWS)
        for l in range(n_slab):
            c2 = c2_row[:, l * LANES:(l + 1) * LANES]
            u = u_scr[l, rows, :]
            t_r = jnp.tanh(z_scr[l, rows, :])
            t_i = jnp.tanh(z_scr[n_slab + l, rows, :])
            log2_a = c2 + c2 * t_r
            a = jnp.exp2(log2_a)
            quarter = (jnp.tanh(log2_a * math.log(2.0)) * -0.25) * (a * a + 1.0)
            half_mult = jnp.where(quarter > 0.0, quarter * lax.rsqrt(quarter), 0.0)
            b = (half_mult * u) * (1.0 + t_i)
            gelu = gelu_scr[l, rows, :]

            a_c = [a[c * SUBLANES:(c + 1) * SUBLANES] for c in range(CHAIN)]
            b_c = [b[c * SUBLANES:(c + 1) * SUBLANES] for c in range(CHAIN)]
            h_loc, p_loc = [b_c[0]], [a_c[0]]
            for c in range(1, CHAIN):
                h_loc.append(a_c[c] * h_loc[c - 1] + b_c[c])
                p_loc.append(a_c[c] * p_loc[c - 1])
            p_cum, h_cum = p_loc[CHAIN - 1], h_loc[CHAIN - 1]
            shift = 1
            while shift < SUBLANES:
                valid = row >= shift
                p_prev = jnp.where(valid, pltpu.roll(p_cum, shift, axis=0), 1.0)
                h_prev = jnp.where(valid, pltpu.roll(h_cum, shift, axis=0), 0.0)
                h_cum = p_cum * h_prev + h_cum
                p_cum = p_cum * p_prev
                shift *= 2
            chain_end = h_cum + p_cum * carry[l]
            chain_in = jnp.where(row == 0, carry[l], pltpu.roll(chain_end, 1, axis=0))
            carry[l] = jnp.broadcast_to(chain_end[SUBLANES - 1:SUBLANES, :], (SUBLANES, LANES))
            for c in range(CHAIN):
                out = (h_loc[c] + p_loc[c] * chain_in) * gelu[c * SUBLANES:(c + 1) * SUBLANES]
                out_slab[l, pl.ds(base + c, SUBLANES, stride=CHAIN), :] = out
    for l in range(n_slab):
        carry_ref[:, l * LANES:(l + 1) * LANES] = carry[l]
    lru_ref[...] = _bf16(jnp.concatenate([out_slab[l] for l in range(n_slab)], axis=1))


def _in_proj_lru(x, g, w, conv_w, conv_b, w_gates, b_gates, lru_l):
    B, S, D = x.shape
    d_lru = lru_l.shape[-1]
    d_attn = (w.shape[1] - 2 * d_lru) // 3
    tm = PROJ_TOKENS
    nblk = tm // ATTN_BLOCK
    n_slab = d_lru // LANES
    const = lambda b, i: (0, 0)
    return pl.pallas_call(
        _in_proj_lru_kernel,
        grid=(B, S // tm),
        in_specs=[
            pl.BlockSpec((None, tm, D), lambda b, i: (b, i, 0)),
            pl.BlockSpec((1, D), const),
            pl.BlockSpec(w.shape, const),
            pl.BlockSpec(conv_w.shape, const),
            pl.BlockSpec(conv_b.shape, const),
            pl.BlockSpec(w_gates.shape, lambda b, i: (0, 0, 0)),
            pl.BlockSpec(b_gates.shape, lambda b, i: (0, 0, 0)),
            pl.BlockSpec(lru_l.shape, const),
        ],
        out_specs=[
            pl.BlockSpec((None, nblk, d_attn, ATTN_BLOCK), lambda b, i: (b, i, 0, 0)),
            pl.BlockSpec((None, tm, d_attn), lambda b, i: (b, i, 0)),
            pl.BlockSpec((None, nblk, d_attn, ATTN_BLOCK), lambda b, i: (b, i, 0, 0)),
            pl.BlockSpec((None, tm, d_lru), lambda b, i: (b, i, 0)),
        ],
        out_shape=[
            jax.ShapeDtypeStruct((B, S // ATTN_BLOCK, d_attn, ATTN_BLOCK), jnp.bfloat16),
            jax.ShapeDtypeStruct((B, S, d_attn), jnp.bfloat16),
            jax.ShapeDtypeStruct((B, S // ATTN_BLOCK, d_attn, ATTN_BLOCK), jnp.bfloat16),
            jax.ShapeDtypeStruct((B, S, d_lru), jnp.bfloat16),
        ],
        scratch_shapes=[
            pltpu.VMEM((tm, D), jnp.bfloat16),
            pltpu.VMEM((2 * n_slab, tm, LANES), jnp.float32),
            pltpu.VMEM((n_slab, tm, LANES), jnp.float32),
            pltpu.VMEM((tm, d_lru), jnp.bfloat16),
            pltpu.VMEM((2 * n_slab, tm, LANES), jnp.float32),
            pltpu.VMEM((n_slab, tm, LANES), jnp.float32),
            pltpu.VMEM((n_slab, tm, LANES), jnp.float32),
            pltpu.VMEM((CHAIN, SUBLANES, d_lru), jnp.float32),
            pltpu.VMEM((SUBLANES, d_lru), jnp.float32),
        ],
        compiler_params=pltpu.CompilerParams(
            dimension_semantics=("parallel", "arbitrary"),
            vmem_limit_bytes=V7X_VMEM_LIMIT_BYTES),
        name="in_proj_lru",
    )(x, g, w, conv_w, conv_b, w_gates, b_gates, lru_l)


N_POS_COLS = 128
N_ONES_ROWS = 16
SCORE_LEAD = 2
N_SCORE_BUFS = SCORE_LEAD + 1
ATTN_Q_PER_STEP = 9


def _diff_attn_kernel(qt_ref, k_ref, pos_ref, vt_ref, lamv_ref, sg_ref, o_ref,
                      s_ref, m_ref, acc_ref, *, slopes, q_per_step):
    h = pl.program_id(1)
    blk = ATTN_BLOCK
    n_blocks = vt_ref.shape[0]

    slope = jnp.float32(slopes[0])
    for i in range(1, len(slopes)):
        slope = jnp.where(h == i, jnp.float32(slopes[i]), slope)

    r = lax.broadcasted_iota(jnp.int32, (N_POS_COLS, blk), 0)
    lane = lax.broadcasted_iota(jnp.int32, (N_POS_COLS, blk), 1).astype(jnp.float32)
    zeros = jnp.zeros((HEAD_DIM, blk), qt_ref.dtype)
    key_in_blk = lax.broadcasted_iota(jnp.int32, (blk, 2 * blk), 0)
    qry_in_blk = lax.broadcasted_iota(jnp.int32, (blk, 2 * blk), 1) % blk
    causal = key_in_blk <= qry_in_blk
    ones_row = lax.broadcasted_iota(jnp.int32, (N_ONES_ROWS, blk), 0) == 0
    ones_blk = jnp.where(ones_row, 1.0, 0.0).astype(jnp.bfloat16)

    def q_operand(qt, q_index):
        alibi = _bf16(jnp.where(r == 0, slope,
                      jnp.where(r == 1, slope * blk,
                      jnp.where(r == 2, -slope * lane,
                      jnp.where(r == 3, -slope * float(q_index * blk), 0.0)))))
        return jnp.concatenate(
            [jnp.concatenate([qt[:HEAD_DIM], zeros, alibi], axis=0),
             jnp.concatenate([zeros, qt[HEAD_DIM:], alibi], axis=0)], axis=1)

    def score_block(q_aug, j, diagonal):
        rows = slice(j * blk, (j + 1) * blk)
        k_aug = jnp.concatenate([k_ref[rows, :], pos_ref[rows, :]], axis=1)
        s = jnp.dot(k_aug, q_aug, preferred_element_type=jnp.float32)
        return jnp.where(causal, s, -jnp.inf) if diagonal else s

    def finish(t):
        acc = acc_ref[...]
        lamv = lamv_ref[...]
        lam = (jnp.exp(jnp.sum(lamv[0:1] * lamv[1:2], axis=-1, keepdims=True))
               - jnp.exp(jnp.sum(lamv[2:3] * lamv[3:4], axis=-1, keepdims=True))
               + LAMBDA_INIT)
        heads = acc[:V_HEAD_DIM] / acc[V_HEAD_DIM:V_HEAD_DIM + 1]
        o = heads[:, :blk] - lam * heads[:, blk:]
        o = o * lax.rsqrt(jnp.mean(o * o, axis=0, keepdims=True) + EPS)
        o = o * sg_ref[...] * (1.0 - LAMBDA_INIT)
        o_ref[t * blk:(t + 1) * blk, :] = _bf16(o.T)

    def score_q_block(q_aug, q_index, j, m_run):
        s = score_block(q_aug, j, j == q_index)
        s_ref[q_index % N_SCORE_BUFS, j * blk:(j + 1) * blk, :] = s
        col_max = jnp.max(s, axis=0, keepdims=True)
        return col_max if m_run is None else jnp.maximum(m_run, col_max)

    def sub_step(t):
        if t >= 1:
            finish(t - 1)
        if t == n_blocks:
            return
        if t == 0:
            for q_index in range(min(SCORE_LEAD, n_blocks)):
                q_aug = q_operand(qt_ref[q_index], q_index)
                m_run = None
                for j in range(q_index + 1):
                    m_run = score_q_block(q_aug, q_index, j, m_run)
                m_ref[q_index % N_SCORE_BUFS] = m_run
        cur = t % N_SCORE_BUFS
        ahead = t + SCORE_LEAD
        scoring = ahead < n_blocks
        q_aug = q_operand(qt_ref[ahead], ahead) if scoring else None
        m_cur = m_ref[cur]
        m_run = None
        acc = jnp.zeros((V_HEAD_DIM + N_ONES_ROWS, 2 * blk), jnp.float32)
        for j in range(ahead + 1 if scoring else t + 1):
            if scoring:
                m_run = score_q_block(q_aug, ahead, j, m_run)
            if j <= t:
                p = _bf16(jnp.exp(s_ref[cur, j * blk:(j + 1) * blk, :] - m_cur))
                vt_aug = jnp.concatenate([vt_ref[j], ones_blk], axis=0)
                acc = acc + jnp.dot(vt_aug, p, preferred_element_type=jnp.float32)
        if scoring:
            m_ref[ahead % N_SCORE_BUFS] = m_run
        acc_ref[...] = acc

    def grid_step(first):
        def run():
            for t in range(first, min(first + q_per_step, n_blocks + 1)):
                sub_step(t)
        return run

    steps = [grid_step(first) for first in range(0, n_blocks + 1, q_per_step)]
    if len(steps) == 1:
        steps[0]()
    else:
        lax.switch(pl.program_id(2), steps)


def _alibi_slopes(n_heads):
    return [2.0 ** (-8.0 * (i + 1) / n_heads) for i in range(n_heads)]


def _key_position_columns(seq_len):
    j = np.arange(seq_len)
    pos = np.zeros((seq_len, N_POS_COLS), np.float32)
    pos[:, 0] = j % ATTN_BLOCK
    pos[:, 1] = j // ATTN_BLOCK
    pos[:, 2:4] = 1.0
    assert ATTN_BLOCK <= 256 and seq_len // ATTN_BLOCK <= 256
    return jnp.asarray(pos, jnp.bfloat16)


def _diff_attn(qt, k, vt, lamv, subln_g):
    B, nblk, d_attn, blk = qt.shape
    S = k.shape[1]
    H = d_attn // V_HEAD_DIM
    pos = _key_position_columns(S)
    kernel = functools.partial(_diff_attn_kernel, slopes=tuple(_alibi_slopes(H)),
                               q_per_step=ATTN_Q_PER_STEP)
    n_steps = pl.cdiv(nblk + 1, ATTN_Q_PER_STEP)
    return pl.pallas_call(
        kernel,
        grid=(B, H, n_steps),
        in_specs=[
            pl.BlockSpec((None, nblk, V_HEAD_DIM, blk), lambda b, h, q: (b, 0, h, 0)),
            pl.BlockSpec((None, S, V_HEAD_DIM), lambda b, h, q: (b, 0, h)),
            pl.BlockSpec(pos.shape, lambda b, h, q: (0, 0)),
            pl.BlockSpec((None, nblk, V_HEAD_DIM, blk), lambda b, h, q: (b, 0, h, 0)),
            pl.BlockSpec(lamv.shape, lambda b, h, q: (0, 0)),
            pl.BlockSpec(subln_g.shape, lambda b, h, q: (0, 0)),
        ],
        out_specs=pl.BlockSpec((None, S, V_HEAD_DIM), lambda b, h, q: (b, 0, h)),
        out_shape=jax.ShapeDtypeStruct((B, S, d_attn), jnp.bfloat16),
        scratch_shapes=[pltpu.VMEM((N_SCORE_BUFS, S, 2 * blk), jnp.float32),
                        pltpu.VMEM((N_SCORE_BUFS, 1, 2 * blk), jnp.float32),
                        pltpu.VMEM((V_HEAD_DIM + N_ONES_ROWS, 2 * blk), jnp.float32)],
        compiler_params=pltpu.CompilerParams(
            dimension_semantics=("parallel", "parallel", "arbitrary"),
            vmem_limit_bytes=V7X_VMEM_LIMIT_BYTES),
        name="diff_attn",
    )(qt, k, pos, vt, lamv, subln_g)


def _out_mlp_kernel(x_ref, attn_ref, lru_ref, wo_ref, gm_ref, wup_ref, wdn_ref, gf_ref, y_ref):
    tm = x_ref.shape[0]
    rows = [slice(g * tm // MLP_SPLIT, (g + 1) * tm // MLP_SPLIT) for g in range(MLP_SPLIT)]
    x1 = []
    for r in rows:
        mix = jnp.concatenate([attn_ref[r, :], lru_ref[r, :]], axis=-1)
        x1.append(x_ref[r, :] + jnp.dot(mix, wo_ref[...], preferred_element_type=jnp.float32))
    hm = [_bf16(_rmsnorm_rows(v, gm_ref[...])) for v in x1]
    d_ff = wup_ref.shape[1]
    mlp = [jnp.zeros_like(v) for v in x1]
    for c in range(d_ff // FF_CHUNK):
        cols = slice(c * FF_CHUNK, (c + 1) * FF_CHUNK)
        for g in range(MLP_SPLIT):
            up = jnp.dot(hm[g], wup_ref[:, cols], preferred_element_type=jnp.float32)
            act = _bf16(jnp.square(jnp.maximum(up, 0.0)))
            mlp[g] = mlp[g] + jnp.dot(act, wdn_ref[cols, :], preferred_element_type=jnp.float32)
    for g, r in enumerate(rows):
        y_ref[r, :] = _rmsnorm_rows(x1[g] + mlp[g], gf_ref[...])


def _out_mlp(x, attn, lru, w_out, g_mlp, w_up, w_down, g_final):
    B, S, D = x.shape
    T = B * S
    tm = MLP_TOKENS
    x2 = x.reshape(T, D)
    attn2 = attn.reshape(T, attn.shape[-1])
    lru2 = lru.reshape(T, lru.shape[-1])
    const = lambda i: (0, 0)
    resident = pl.Buffered(1)
    y = pl.pallas_call(
        _out_mlp_kernel,
        grid=(T // tm,),
        in_specs=[
            pl.BlockSpec((tm, D), lambda i: (i, 0)),
            pl.BlockSpec((tm, attn2.shape[1]), lambda i: (i, 0)),
            pl.BlockSpec((tm, lru2.shape[1]), lambda i: (i, 0)),
            pl.BlockSpec(w_out.shape, const, pipeline_mode=resident),
            pl.BlockSpec(g_mlp.shape, const),
            pl.BlockSpec(w_up.shape, const, pipeline_mode=resident),
            pl.BlockSpec(w_down.shape, const, pipeline_mode=resident),
            pl.BlockSpec(g_final.shape, const),
        ],
        out_specs=pl.BlockSpec((tm, D), lambda i: (i, 0)),
        out_shape=jax.ShapeDtypeStruct((T, D), jnp.float32),
        compiler_params=pltpu.CompilerParams(
            dimension_semantics=("parallel",),
            vmem_limit_bytes=V7X_VMEM_LIMIT_BYTES),
        name="out_mlp",
    )(x2, attn2, lru2, w_out, g_mlp, w_up, w_down, g_final)
    return y.reshape(B, S, D)


def _pair_block_diag(w):
    n, c, d = w.shape
    eye = jnp.eye(2, dtype=w.dtype)
    pairs = w.reshape(n // 2, 2, c, 1, d) * eye[None, :, None, :, None]
    return pairs.reshape(n // 2, 2 * c, 2 * d)


def kernel(x, norm_mix_g, w_in, conv_w, conv_b, w_rg, b_rg, w_ig, b_ig, lru_L, lambda_q1, lambda_k1, lambda_q2, lambda_k2, subln_g, w_out, norm_mlp_g, w_up, w_down, final_g):
    B, S, D = x.shape
    d_lru = lru_L.shape[-1]
    d_attn = (w_in.shape[-1] - 2 * d_lru) // 3
    depth = w_in.shape[0]
    n_slab = d_lru // LANES
    assert depth == 1, "LAMBDA_INIT is specialised to a single layer"
    assert S % PROJ_TOKENS == 0 and (B * S) % MLP_TOKENS == 0
    assert PROJ_TOKENS % SUB_ROWS == 0 and d_lru % LANES == 0
    assert 2 * w_rg.shape[-1] == LANES and w_rg.shape[1] * w_rg.shape[2] == d_lru
    l = 0
    row = lambda v: v.reshape(1, -1)

    col = lax.broadcasted_iota(jnp.int32, (1, w_in.shape[-1]), 1)
    w = _bf16(w_in[l] * jnp.where(col < d_attn, HEAD_DIM ** -0.5, 1.0))
    w_gates = _bf16(0.5 * jnp.concatenate([_pair_block_diag(w_rg[l]), _pair_block_diag(w_ig[l])],
                                          axis=-1))
    b_gates = 0.5 * jnp.concatenate([b_rg[l].reshape(n_slab, 1, LANES),
                                     b_ig[l].reshape(n_slab, 1, LANES)], axis=-1)
    lamv = jnp.stack([lambda_q1[l], lambda_k1[l], lambda_q2[l], lambda_k2[l]]).astype(jnp.float32)

    qt, k, vt, lru = _in_proj_lru(x, row(norm_mix_g[l]), w, conv_w[l], row(conv_b[l]),
                                  w_gates, b_gates, row(lru_L[l]))
    attn = _diff_attn(qt, k, vt, lamv, subln_g[l].reshape(-1, 1))
    return _out_mlp(x, attn, lru, _bf16(w_out[l]), row(norm_mlp_g[l]), _bf16(w_up[l]),
                    _bf16(w_down[l]), row(final_g))
```

```python
import functools
import math

import numpy as np
import jax
import jax.numpy as jnp
from jax import lax
from jax.experimental import pallas as pl
from jax.experimental.pallas import tpu as pltpu

EPS = 1e-6
CONV_WIDTH = 4
LRU_C = 8.0
HEAD_DIM = 64
V_HEAD_DIM = 2 * HEAD_DIM
LAMBDA_INIT = 0.8 - 0.6 * math.exp(-0.3 * 0)

V7X_MXU_COLS = 256
LANES = 128
SUBLANES = 8
V7X_VMEM_LIMIT_BYTES = 56 * 1024 * 1024

ATTN_BLOCK = V7X_MXU_COLS
PROJ_TOKENS = 512
MLP_TOKENS = 512
FF_CHUNK = 1024
MLP_SPLIT = 2
WEIGHT_CHUNK_BYTES = 2 * 1024 * 1024
CHAIN = 4
SUB_ROWS = SUBLANES * CHAIN


def _bf16(x):
    return x.astype(jnp.bfloat16)


def _rmsnorm_rows(x, g):
    return x * lax.rsqrt(jnp.mean(x * x, axis=-1, keepdims=True) + EPS) * g


def _log_sigmoid(x):
    return jnp.minimum(x, 0.0) - jnp.log1p(jnp.exp(-jnp.abs(x)))


def _in_proj_lru_kernel(x_ref, g_ref, w_ref, cw_ref, cb_ref, wg_ref, bg_ref, lru_l_ref, zero_ref,
                        qt_ref, k_ref, vt_ref, lru_ref,
                        h_scr, ug_slab, u_scr, ub_scr, z_scr, gelu_scr, out_slab, tail_ref,
                        carry_ref):
    tm = x_ref.shape[0]
    d_lru = lru_ref.shape[-1]
    n_slab = d_lru // LANES
    n_sub = tm // SUB_ROWS
    d_attn = k_ref.shape[-1]
    f32 = jnp.float32

    @pl.when(pl.program_id(1) == 0)
    def _():
        tail_ref[...] = jnp.zeros(tail_ref.shape, f32)
        carry_ref[...] = jnp.zeros(carry_ref.shape, f32)

    h_scr[...] = _bf16(_rmsnorm_rows(x_ref[...], g_ref[...]))

    def ug_half(half):
        cols = slice(3 * d_attn + half * d_lru, 3 * d_attn + (half + 1) * d_lru)
        nat = jnp.dot(h_scr[...], w_ref[:, cols], preferred_element_type=f32)
        for l in range(n_slab):
            ug_slab[half * n_slab + l] = nat[:, l * LANES:(l + 1) * LANES]

    ug_half(0)

    def piece(slab, s, c):
        return ug_slab[slab, pl.ds(s * SUB_ROWS + c, SUBLANES, stride=CHAIN), :]

    row = lax.broadcasted_iota(jnp.int32, (SUBLANES, LANES), 0)

    for l in range(n_slab):
        lanes = slice(l * LANES, (l + 1) * LANES)
        cw = cw_ref[:, lanes]
        cb = cb_ref[:, lanes]
        rolled_prev = [pltpu.roll(tail_ref[c, :, lanes], 1, axis=0) for c in range(CHAIN)]
        for s in range(n_sub):
            cur = [piece(l, s, c) for c in range(CHAIN)]
            rolled = [pltpu.roll(cur[c], 1, axis=0) for c in range(CHAIN)]
            shifted = [jnp.where(row == 0, rolled_prev[c], rolled[c]) for c in range(CHAIN)]
            rolled_prev = rolled
            if s == n_sub - 1:
                for c in range(CHAIN):
                    tail_ref[c, :, lanes] = cur[c]
            conv = []
            for c in range(CHAIN):
                acc = cb
                for kk in range(CONV_WIDTH):
                    d = CONV_WIDTH - 1 - kk
                    src = cur[c - d] if c >= d else shifted[c - d + CHAIN]
                    acc = acc + src * cw[kk:kk + 1, :]
                conv.append(acc)
            u_sub = jnp.concatenate(conv, axis=0)
            u_scr[l, s * SUB_ROWS:(s + 1) * SUB_ROWS, :] = u_sub
            ub_scr[s * SUB_ROWS:(s + 1) * SUB_ROWS, lanes] = _bf16(u_sub)

    ug_half(1)
    for l in range(n_slab):
        z = jnp.dot(ub_scr[:, l * LANES:(l + 1) * LANES], wg_ref[l],
                    preferred_element_type=f32) + bg_ref[l]
        z_scr[l] = z[:, :LANES]
        z_scr[n_slab + l] = z[:, LANES:]

    def k_piece(part):
        cols = slice(part * ATTN_BLOCK, (part + 1) * ATTN_BLOCK)
        t = jnp.dot(h_scr[...], w_ref[:, d_attn + part * ATTN_BLOCK:
                                     d_attn + (part + 1) * ATTN_BLOCK], preferred_element_type=f32)
        k_ref[:, cols] = _bf16(t)
        return [t[tm - SUBLANES:, c * LANES:(c + 1) * LANES] for c in range(ATTN_BLOCK // LANES)]

    def transposed_piece(first_col, o_ref, part):
        rows = slice(part * ATTN_BLOCK, (part + 1) * ATTN_BLOCK)
        t = jnp.dot(h_scr[rows, :], w_ref[:, first_col:first_col + d_attn],
                    preferred_element_type=f32)
        o_ref[part] = _bf16(t.T)
        return [t[ATTN_BLOCK - SUBLANES:, c * LANES:(c + 1) * LANES] for c in range(d_attn // LANES)]

    side_work = {0: lambda: k_piece(0), 2: lambda: k_piece(1),
                 5: lambda: transposed_piece(0, qt_ref, 0),
                 8: lambda: transposed_piece(0, qt_ref, 1),
                 11: lambda: transposed_piece(2 * d_attn, vt_ref, 0),
                 13: lambda: transposed_piece(2 * d_attn, vt_ref, 1)}
    assert n_sub > max(side_work) and qt_ref.shape[0] == 2 and d_attn == 2 * ATTN_BLOCK

    c_gelu = math.sqrt(2.0 / math.pi)
    for l in range(n_slab):
        for s in range(n_sub):
            gate = jnp.concatenate([piece(n_slab + l, s, c) for c in range(CHAIN)], axis=0)
            gelu_scr[l, s * SUB_ROWS:(s + 1) * SUB_ROWS, :] = (0.5 * gate) * (1.0 + jnp.tanh(
                gate * (c_gelu + (c_gelu * 0.044715) * (gate * gate))))

    c2_row = (0.5 * LRU_C * math.log2(math.e)) * _log_sigmoid(lru_l_ref[...])

    carry = [carry_ref[:, l * LANES:(l + 1) * LANES] for l in range(n_slab)]
    pending_at = -1
    keep_state = zero_ref[...] == 0
    for s in range(n_sub):
        if s in side_work:
            anchor = functools.reduce(jnp.add, side_work[s]())
            pending_at = s + 1
        base = s * SUB_ROWS
        rows = slice(base, base + SUB_ROWS)
        for l in range(n_slab):
            c2 = c2_row[:, l * LANES:(l + 1) * LANES]
            u = u_scr[l, rows, :]
            t_r = jnp.tanh(z_scr[l, rows, :])
            t_i = jnp.tanh(z_scr[n_slab + l, rows, :])
            log2_a = c2 + c2 * t_r
            a = jnp.exp2(log2_a)
            quarter = (jnp.tanh(log2_a * math.log(2.0)) * -0.25) * (a * a + 1.0)
            half_mult = jnp.where(quarter > 0.0, quarter * lax.rsqrt(quarter), 0.0)
            b = (half_mult * u) * (1.0 + t_i)
            gelu = gelu_scr[l, rows, :]

            a_c = [a[c * SUBLANES:(c + 1) * SUBLANES] for c in range(CHAIN)]
            b_c = [b[c * SUBLANES:(c + 1) * SUBLANES] for c in range(CHAIN)]
            h_loc, p_loc = [b_c[0]], [a_c[0]]
            for c in range(1, CHAIN):
                h_loc.append(a_c[c] * h_loc[c - 1] + b_c[c])
                p_loc.append(a_c[c] * p_loc[c - 1])
            p_cum, h_cum = p_loc[CHAIN - 1], h_loc[CHAIN - 1]
            shift = 1
            while shift < SUBLANES:
                valid = row >= shift
                p_prev = jnp.where(valid, pltpu.roll(p_cum, shift, axis=0), 1.0)
                h_prev = jnp.where(valid, pltpu.roll(h_cum, shift, axis=0), 0.0)
                h_cum = p_cum * h_prev + h_cum
                p_cum = p_cum * p_prev
                shift *= 2
            chain_end = h_cum + p_cum * carry[l]
            chain_in = jnp.where(row == 0, carry[l], pltpu.roll(chain_end, 1, axis=0))
            carry[l] = jnp.broadcast_to(chain_end[SUBLANES - 1:SUBLANES, :], (SUBLANES, LANES))
            if s == pending_at:
                carry[l] = jnp.where(keep_state, carry[l], anchor)
            for c in range(CHAIN):
                out = (h_loc[c] + p_loc[c] * chain_in) * gelu[c * SUBLANES:(c + 1) * SUBLANES]
                out_slab[l, pl.ds(base + c, SUBLANES, stride=CHAIN), :] = out
    for l in range(n_slab):
        carry_ref[:, l * LANES:(l + 1) * LANES] = carry[l]
    lru_ref[...] = _bf16(jnp.concatenate([out_slab[l] for l in range(n_slab)], axis=1))


def _in_proj_lru(x, g, w, conv_w, conv_b, w_gates, b_gates, lru_l):
    B, S, D = x.shape
    d_lru = lru_l.shape[-1]
    d_attn = (w.shape[1] - 2 * d_lru) // 3
    tm = PROJ_TOKENS
    nblk = tm // ATTN_BLOCK
    n_slab = d_lru // LANES
    const = lambda b, i: (0, 0)
    return pl.pallas_call(
        _in_proj_lru_kernel,
        grid=(B, S // tm),
        in_specs=[
            pl.BlockSpec((None, tm, D), lambda b, i: (b, i, 0)),
            pl.BlockSpec((1, D), const),
            pl.BlockSpec(w.shape, const),
            pl.BlockSpec(conv_w.shape, const),
            pl.BlockSpec(conv_b.shape, const),
            pl.BlockSpec(w_gates.shape, lambda b, i: (0, 0, 0)),
            pl.BlockSpec(b_gates.shape, lambda b, i: (0, 0, 0)),
            pl.BlockSpec(lru_l.shape, const),
            pl.BlockSpec((SUBLANES, LANES), const),
        ],
        out_specs=[
            pl.BlockSpec((None, nblk, d_attn, ATTN_BLOCK), lambda b, i: (b, i, 0, 0)),
            pl.BlockSpec((None, tm, d_attn), lambda b, i: (b, i, 0)),
            pl.BlockSpec((None, nblk, d_attn, ATTN_BLOCK), lambda b, i: (b, i, 0, 0)),
            pl.BlockSpec((None, tm, d_lru), lambda b, i: (b, i, 0)),
        ],
        out_shape=[
            jax.ShapeDtypeStruct((B, S // ATTN_BLOCK, d_attn, ATTN_BLOCK), jnp.bfloat16),
            jax.ShapeDtypeStruct((B, S, d_attn), jnp.bfloat16),
            jax.ShapeDtypeStruct((B, S // ATTN_BLOCK, d_attn, ATTN_BLOCK), jnp.bfloat16),
            jax.ShapeDtypeStruct((B, S, d_lru), jnp.bfloat16),
        ],
        scratch_shapes=[
            pltpu.VMEM((tm, D), jnp.bfloat16),
            pltpu.VMEM((2 * n_slab, tm, LANES), jnp.float32),
            pltpu.VMEM((n_slab, tm, LANES), jnp.float32),
            pltpu.VMEM((tm, d_lru), jnp.bfloat16),
            pltpu.VMEM((2 * n_slab, tm, LANES), jnp.float32),
            pltpu.VMEM((n_slab, tm, LANES), jnp.float32),
            pltpu.VMEM((n_slab, tm, LANES), jnp.float32),
            pltpu.VMEM((CHAIN, SUBLANES, d_lru), jnp.float32),
            pltpu.VMEM((SUBLANES, d_lru), jnp.float32),
        ],
        compiler_params=pltpu.CompilerParams(
            dimension_semantics=("parallel", "arbitrary"),
            vmem_limit_bytes=V7X_VMEM_LIMIT_BYTES),
        name="in_proj_lru",
    )(x, g, w, conv_w, conv_b, w_gates, b_gates, lru_l, jnp.zeros((SUBLANES, LANES), jnp.int32))


N_POS_COLS = 128
N_ONES_ROWS = 16
SCORE_LEAD = 2
N_SCORE_BUFS = SCORE_LEAD + 1
ATTN_Q_PER_STEP = 9


def _diff_attn_kernel(qt_ref, k_ref, pos_ref, vt_ref, lamv_ref, sg_ref, o_ref,
                      s_ref, m_ref, acc_ref, *, slopes, q_per_step):
    h = pl.program_id(1)
    blk = ATTN_BLOCK
    n_blocks = vt_ref.shape[0]

    slope = jnp.float32(slopes[0])
    for i in range(1, len(slopes)):
        slope = jnp.where(h == i, jnp.float32(slopes[i]), slope)

    r = lax.broadcasted_iota(jnp.int32, (N_POS_COLS, blk), 0)
    lane = lax.broadcasted_iota(jnp.int32, (N_POS_COLS, blk), 1).astype(jnp.float32)
    zeros = jnp.zeros((HEAD_DIM, blk), qt_ref.dtype)
    key_in_blk = lax.broadcasted_iota(jnp.int32, (blk, 2 * blk), 0)
    qry_in_blk = lax.broadcasted_iota(jnp.int32, (blk, 2 * blk), 1) % blk
    causal = key_in_blk <= qry_in_blk
    ones_row = lax.broadcasted_iota(jnp.int32, (N_ONES_ROWS, blk), 0) == 0
    ones_blk = jnp.where(ones_row, 1.0, 0.0).astype(jnp.bfloat16)

    def q_operand(qt, q_index):
        alibi = _bf16(jnp.where(r == 0, slope,
                      jnp.where(r == 1, slope * blk,
                      jnp.where(r == 2, -slope * lane,
                      jnp.where(r == 3, -slope * float(q_index * blk), 0.0)))))
        return jnp.concatenate(
            [jnp.concatenate([qt[:HEAD_DIM], zeros, alibi], axis=0),
             jnp.concatenate([zeros, qt[HEAD_DIM:], alibi], axis=0)], axis=1)

    def score_block(q_aug, j, diagonal):
        rows = slice(j * blk, (j + 1) * blk)
        k_aug = jnp.concatenate([k_ref[rows, :], pos_ref[rows, :]], axis=1)
        s = jnp.dot(k_aug, q_aug, preferred_element_type=jnp.float32)
        return jnp.where(causal, s, -jnp.inf) if diagonal else s

    def finish(t):
        acc = acc_ref[...]
        lamv = lamv_ref[...]
        lam = (jnp.exp(jnp.sum(lamv[0:1] * lamv[1:2], axis=-1, keepdims=True))
               - jnp.exp(jnp.sum(lamv[2:3] * lamv[3:4], axis=-1, keepdims=True))
               + LAMBDA_INIT)
        heads = acc[:V_HEAD_DIM] / acc[V_HEAD_DIM:V_HEAD_DIM + 1]
        o = heads[:, :blk] - lam * heads[:, blk:]
        o = o * lax.rsqrt(jnp.mean(o * o, axis=0, keepdims=True) + EPS)
        o = o * sg_ref[...] * (1.0 - LAMBDA_INIT)
        o_ref[t * blk:(t + 1) * blk, :] = _bf16(o.T)

    def score_q_block(q_aug, q_index, j, m_run):
        s = score_block(q_aug, j, j == q_index)
        s_ref[q_index % N_SCORE_BUFS, j * blk:(j + 1) * blk, :] = s
        col_max = jnp.max(s, axis=0, keepdims=True)
        return col_max if m_run is None else jnp.maximum(m_run, col_max)

    def sub_step(t):
        if t >= 1:
            finish(t - 1)
        if t == n_blocks:
            return
        if t == 0:
            for q_index in range(min(SCORE_LEAD, n_blocks)):
                q_aug = q_operand(qt_ref[q_index], q_index)
                m_run = None
                for j in range(q_index + 1):
                    m_run = score_q_block(q_aug, q_index, j, m_run)
                m_ref[q_index % N_SCORE_BUFS] = m_run
        cur = t % N_SCORE_BUFS
        ahead = t + SCORE_LEAD
        scoring = ahead < n_blocks
        q_aug = q_operand(qt_ref[ahead], ahead) if scoring else None
        m_cur = m_ref[cur]
        m_run = None
        acc = jnp.zeros((V_HEAD_DIM + N_ONES_ROWS, 2 * blk), jnp.float32)
        for j in range(ahead + 1 if scoring else t + 1):
            if scoring:
                m_run = score_q_block(q_aug, ahead, j, m_run)
            if j <= t:
                p = _bf16(jnp.exp(s_ref[cur, j * blk:(j + 1) * blk, :] - m_cur))
                vt_aug = jnp.concatenate([vt_ref[j], ones_blk], axis=0)
                acc = acc + jnp.dot(vt_aug, p, preferred_element_type=jnp.float32)
        if scoring:
            m_ref[ahead % N_SCORE_BUFS] = m_run
        acc_ref[...] = acc

    def grid_step(first):
        def run():
            for t in range(first, min(first + q_per_step, n_blocks + 1)):
                sub_step(t)
        return run

    steps = [grid_step(first) for first in range(0, n_blocks + 1, q_per_step)]
    if len(steps) == 1:
        steps[0]()
    else:
        lax.switch(pl.program_id(2), steps)


def _alibi_slopes(n_heads):
    return [2.0 ** (-8.0 * (i + 1) / n_heads) for i in range(n_heads)]


def _key_position_columns(seq_len):
    j = np.arange(seq_len)
    pos = np.zeros((seq_len, N_POS_COLS), np.float32)
    pos[:, 0] = j % ATTN_BLOCK
    pos[:, 1] = j // ATTN_BLOCK
    pos[:, 2:4] = 1.0
    assert ATTN_BLOCK <= 256 and seq_len // ATTN_BLOCK <= 256
    return jnp.asarray(pos, jnp.bfloat16)


def _diff_attn(qt, k, vt, lamv, subln_g):
    B, nblk, d_attn, blk = qt.shape
    S = k.shape[1]
    H = d_attn // V_HEAD_DIM
    pos = _key_position_columns(S)
    kernel = functools.partial(_diff_attn_kernel, slopes=tuple(_alibi_slopes(H)),
                               q_per_step=ATTN_Q_PER_STEP)
    n_steps = pl.cdiv(nblk + 1, ATTN_Q_PER_STEP)
    return pl.pallas_call(
        kernel,
        grid=(B, H, n_steps),
        in_specs=[
            pl.BlockSpec((None, nblk, V_HEAD_DIM, blk), lambda b, h, q: (b, 0, h, 0)),
            pl.BlockSpec((None, S, V_HEAD_DIM), lambda b, h, q: (b, 0, h)),
            pl.BlockSpec(pos.shape, lambda b, h, q: (0, 0)),
            pl.BlockSpec((None, nblk, V_HEAD_DIM, blk), lambda b, h, q: (b, 0, h, 0)),
            pl.BlockSpec(lamv.shape, lambda b, h, q: (0, 0)),
            pl.BlockSpec(subln_g.shape, lambda b, h, q: (0, 0)),
        ],
        out_specs=pl.BlockSpec((None, S, V_HEAD_DIM), lambda b, h, q: (b, 0, h)),
        out_shape=jax.ShapeDtypeStruct((B, S, d_attn), jnp.bfloat16),
        scratch_shapes=[pltpu.VMEM((N_SCORE_BUFS, S, 2 * blk), jnp.float32),
                        pltpu.VMEM((N_SCORE_BUFS, 1, 2 * blk), jnp.float32),
                        pltpu.VMEM((V_HEAD_DIM + N_ONES_ROWS, 2 * blk), jnp.float32)],
        compiler_params=pltpu.CompilerParams(
            dimension_semantics=("parallel", "parallel", "arbitrary"),
            vmem_limit_bytes=V7X_VMEM_LIMIT_BYTES),
        name="diff_attn",
    )(qt, k, pos, vt, lamv, subln_g)


def _stream_cast_rows(src_hbm, dst_ref, stage_ref, sem, chunk_rows, scale=None):
    n_chunks = src_hbm.shape[0] // chunk_rows

    def copy(i):
        return pltpu.make_async_copy(src_hbm.at[pl.ds(i * chunk_rows, chunk_rows), :],
                                     stage_ref.at[i % 2], sem.at[i % 2])

    copy(0).start()
    for i in range(n_chunks):
        if i + 1 < n_chunks:
            copy(i + 1).start()
        copy(i).wait()
        chunk = stage_ref[i % 2]
        if scale is not None:
            chunk = chunk * scale
        dst_ref[i * chunk_rows:(i + 1) * chunk_rows, :] = _bf16(chunk)


def _out_mlp_kernel(x_ref, attn_ref, lru_ref, wo_hbm, gm_ref, wup_hbm, wdn_hbm, gf_ref, y_ref,
                    wo_ref, wup_ref, wdn_ref, stage_d, stage_ff, sem):
    @pl.when(pl.program_id(0) == 0)
    def _():
        _stream_cast_rows(wo_hbm, wo_ref, stage_d, sem, stage_d.shape[1])
        _stream_cast_rows(wup_hbm, wup_ref, stage_ff, sem, stage_ff.shape[1])
        _stream_cast_rows(wdn_hbm, wdn_ref, stage_d, sem, stage_d.shape[1])

    tm = x_ref.shape[0]
    rows = [slice(g * tm // MLP_SPLIT, (g + 1) * tm // MLP_SPLIT) for g in range(MLP_SPLIT)]
    x1 = []
    for r in rows:
        mix = jnp.concatenate([attn_ref[r, :], lru_ref[r, :]], axis=-1)
        x1.append(x_ref[r, :] + jnp.dot(mix, wo_ref[...], preferred_element_type=jnp.float32))
    hm = [_bf16(_rmsnorm_rows(v, gm_ref[...])) for v in x1]
    d_ff = wup_ref.shape[1]
    mlp = [jnp.zeros_like(v) for v in x1]
    for c in range(d_ff // FF_CHUNK):
        cols = slice(c * FF_CHUNK, (c + 1) * FF_CHUNK)
        for g in range(MLP_SPLIT):
            up = jnp.dot(hm[g], wup_ref[:, cols], preferred_element_type=jnp.float32)
            act = _bf16(jnp.square(jnp.maximum(up, 0.0)))
            mlp[g] = mlp[g] + jnp.dot(act, wdn_ref[cols, :], preferred_element_type=jnp.float32)
    for g, r in enumerate(rows):
        y_ref[r, :] = _rmsnorm_rows(x1[g] + mlp[g], gf_ref[...])


def _out_mlp(x, attn, lru, w_out, g_mlp, w_up, w_down, g_final):
    B, S, D = x.shape
    T = B * S
    tm = MLP_TOKENS
    d_ff = w_up.shape[1]
    x2 = x.reshape(T, D)
    attn2 = attn.reshape(T, attn.shape[-1])
    lru2 = lru.reshape(T, lru.shape[-1])
    const = lambda i: (0, 0)
    in_hbm = pl.BlockSpec(memory_space=pl.ANY)
    rows_d = WEIGHT_CHUNK_BYTES // (4 * D)
    rows_ff = WEIGHT_CHUNK_BYTES // (4 * d_ff)
    assert D % rows_d == 0 and d_ff % rows_d == 0 and D % rows_ff == 0
    y = pl.pallas_call(
        _out_mlp_kernel,
        grid=(T // tm,),
        in_specs=[
            pl.BlockSpec((tm, D), lambda i: (i, 0)),
            pl.BlockSpec((tm, attn2.shape[1]), lambda i: (i, 0)),
            pl.BlockSpec((tm, lru2.shape[1]), lambda i: (i, 0)),
            in_hbm,
            pl.BlockSpec(g_mlp.shape, const),
            in_hbm,
            in_hbm,
            pl.BlockSpec(g_final.shape, const),
        ],
        out_specs=pl.BlockSpec((tm, D), lambda i: (i, 0)),
        out_shape=jax.ShapeDtypeStruct((T, D), jnp.float32),
        scratch_shapes=[
            pltpu.VMEM(w_out.shape, jnp.bfloat16),
            pltpu.VMEM(w_up.shape, jnp.bfloat16),
            pltpu.VMEM(w_down.shape, jnp.bfloat16),
            pltpu.VMEM((2, rows_d, D), jnp.float32),
            pltpu.VMEM((2, rows_ff, d_ff), jnp.float32),
            pltpu.SemaphoreType.DMA((2,)),
        ],
        compiler_params=pltpu.CompilerParams(
            dimension_semantics=("arbitrary",),
            vmem_limit_bytes=V7X_VMEM_LIMIT_BYTES),
        name="out_mlp",
    )(x2, attn2, lru2, w_out, g_mlp, w_up, w_down, g_final)
    return y.reshape(B, S, D)


def _pair_block_diag(w):
    n, c, d = w.shape
    eye = jnp.eye(2, dtype=w.dtype)
    pairs = w.reshape(n // 2, 2, c, 1, d) * eye[None, :, None, :, None]
    return pairs.reshape(n // 2, 2 * c, 2 * d)


def kernel(x, norm_mix_g, w_in, conv_w, conv_b, w_rg, b_rg, w_ig, b_ig, lru_L, lambda_q1, lambda_k1, lambda_q2, lambda_k2, subln_g, w_out, norm_mlp_g, w_up, w_down, final_g):
    B, S, D = x.shape
    d_lru = lru_L.shape[-1]
    d_attn = (w_in.shape[-1] - 2 * d_lru) // 3
    depth = w_in.shape[0]
    n_slab = d_lru // LANES
    assert depth == 1, "LAMBDA_INIT is specialised to a single layer"
    assert S % PROJ_TOKENS == 0 and (B * S) % MLP_TOKENS == 0
    assert PROJ_TOKENS % SUB_ROWS == 0 and d_lru % LANES == 0
    assert 2 * w_rg.shape[-1] == LANES and w_rg.shape[1] * w_rg.shape[2] == d_lru
    l = 0
    row = lambda v: v.reshape(1, -1)

    col = lax.broadcasted_iota(jnp.int32, (1, w_in.shape[-1]), 1)
    w = _bf16(w_in[l] * jnp.where(col < d_attn, HEAD_DIM ** -0.5, 1.0))
    w_gates = _bf16(0.5 * jnp.concatenate([_pair_block_diag(w_rg[l]), _pair_block_diag(w_ig[l])],
                                          axis=-1))
    b_gates = 0.5 * jnp.concatenate([b_rg[l].reshape(n_slab, 1, LANES),
                                     b_ig[l].reshape(n_slab, 1, LANES)], axis=-1)
    lamv = jnp.stack([lambda_q1[l], lambda_k1[l], lambda_q2[l], lambda_k2[l]]).astype(jnp.float32)

    qt, k, vt, lru = _in_proj_lru(x, row(norm_mix_g[l]), w, conv_w[l], row(conv_b[l]),
                                  w_gates, b_gates, row(lru_L[l]))
    attn = _diff_attn(qt, k, vt, lamv, subln_g[l].reshape(-1, 1))
    return _out_mlp(x, attn, lru, w_out[l], row(norm_mlp_g[l]), w_up[l], w_down[l], row(final_g))
```

```python
import functools
import math

import numpy as np
import jax
import jax.numpy as jnp
from jax import lax
from jax.experimental import pallas as pl
from jax.experimental.pallas import tpu as pltpu

EPS = 1e-6
CONV_WIDTH = 4
LRU_C = 8.0
HEAD_DIM = 64
V_HEAD_DIM = 2 * HEAD_DIM
LAMBDA_INIT = 0.8 - 0.6 * math.exp(-0.3 * 0)

V7X_MXU_COLS = 256
LANES = 128
SUBLANES = 8
V7X_VMEM_LIMIT_BYTES = 56 * 1024 * 1024

ATTN_BLOCK = V7X_MXU_COLS
PROJ_TOKENS = 512
MLP_TOKENS = 512
FF_CHUNK = 1024
MLP_SPLIT = 2
WEIGHT_CHUNK_BYTES = 2 * 1024 * 1024
W_IN_CHUNK_ROWS = 128
CHAIN = 4
SUB_ROWS = SUBLANES * CHAIN


def _bf16(x):
    return x.astype(jnp.bfloat16)


def _rmsnorm_rows(x, g):
    return x * lax.rsqrt(jnp.mean(x * x, axis=-1, keepdims=True) + EPS) * g


def _stream_cast_rows(src_hbm, dst_ref, stage_ref, sem, chunk_rows, scale=None):
    n_chunks = src_hbm.shape[0] // chunk_rows

    def copy(i):
        return pltpu.make_async_copy(src_hbm.at[pl.ds(i * chunk_rows, chunk_rows), :],
                                     stage_ref.at[i % 2], sem.at[i % 2])

    copy(0).start()
    for i in range(n_chunks):
        if i + 1 < n_chunks:
            copy(i + 1).start()
        copy(i).wait()
        chunk = stage_ref[i % 2]
        if scale is not None:
            chunk = chunk * scale
        dst_ref[i * chunk_rows:(i + 1) * chunk_rows, :] = _bf16(chunk)


def _log_sigmoid(x):
    return jnp.minimum(x, 0.0) - jnp.log1p(jnp.exp(-jnp.abs(x)))


def _in_proj_lru_kernel(x_ref, g_ref, w_hbm, cw_ref, cb_ref, wg_ref, bg_ref, lru_l_ref, zero_ref,
                        qt_ref, k_ref, vt_ref, lru_ref,
                        w_ref, w_stage, w_sem, h_scr, ug_slab, u_scr, ub_scr, z_scr, gelu_scr,
                        out_slab, tail_ref, carry_ref):
    tm = x_ref.shape[0]
    d_lru = lru_ref.shape[-1]
    n_slab = d_lru // LANES
    n_sub = tm // SUB_ROWS
    d_attn = k_ref.shape[-1]
    f32 = jnp.float32

    @pl.when((pl.program_id(0) == 0) & (pl.program_id(1) == 0))
    def _():
        col = lax.broadcasted_iota(jnp.int32, (1, w_ref.shape[1]), 1)
        q_scale = jnp.where(col < d_attn, HEAD_DIM ** -0.5, 1.0)
        _stream_cast_rows(w_hbm, w_ref, w_stage, w_sem, w_stage.shape[1], scale=q_scale)

    @pl.when(pl.program_id(1) == 0)
    def _():
        tail_ref[...] = jnp.zeros(tail_ref.shape, f32)
        carry_ref[...] = jnp.zeros(carry_ref.shape, f32)

    h_scr[...] = _bf16(_rmsnorm_rows(x_ref[...], g_ref[...]))

    def ug_half(half):
        cols = slice(3 * d_attn + half * d_lru, 3 * d_attn + (half + 1) * d_lru)
        nat = jnp.dot(h_scr[...], w_ref[:, cols], preferred_element_type=f32)
        for l in range(n_slab):
            ug_slab[half * n_slab + l] = nat[:, l * LANES:(l + 1) * LANES]

    ug_half(0)

    def piece(slab, s, c):
        return ug_slab[slab, pl.ds(s * SUB_ROWS + c, SUBLANES, stride=CHAIN), :]

    row = lax.broadcasted_iota(jnp.int32, (SUBLANES, LANES), 0)

    for l in range(n_slab):
        lanes = slice(l * LANES, (l + 1) * LANES)
        cw = cw_ref[:, lanes]
        cb = cb_ref[:, lanes]
        rolled_prev = [pltpu.roll(tail_ref[c, :, lanes], 1, axis=0) for c in range(CHAIN)]
        for s in range(n_sub):
            cur = [piece(l, s, c) for c in range(CHAIN)]
            rolled = [pltpu.roll(cur[c], 1, axis=0) for c in range(CHAIN)]
            shifted = [jnp.where(row == 0, rolled_prev[c], rolled[c]) for c in range(CHAIN)]
            rolled_prev = rolled
            if s == n_sub - 1:
                for c in range(CHAIN):
                    tail_ref[c, :, lanes] = cur[c]
            conv = []
            for c in range(CHAIN):
                acc = cb
                for kk in range(CONV_WIDTH):
                    d = CONV_WIDTH - 1 - kk
                    src = cur[c - d] if c >= d else shifted[c - d + CHAIN]
                    acc = acc + src * cw[kk:kk + 1, :]
                conv.append(acc)
            u_sub = jnp.concatenate(conv, axis=0)
            u_scr[l, s * SUB_ROWS:(s + 1) * SUB_ROWS, :] = u_sub
            ub_scr[s * SUB_ROWS:(s + 1) * SUB_ROWS, lanes] = _bf16(u_sub)

    ug_half(1)
    for l in range(n_slab):
        z = jnp.dot(ub_scr[:, l * LANES:(l + 1) * LANES], wg_ref[l],
                    preferred_element_type=f32) + bg_ref[l]
        z_scr[l] = z[:, :LANES]
        z_scr[n_slab + l] = z[:, LANES:]

    def k_piece(part):
        cols = slice(part * ATTN_BLOCK, (part + 1) * ATTN_BLOCK)
        t = jnp.dot(h_scr[...], w_ref[:, d_attn + part * ATTN_BLOCK:
                                     d_attn + (part + 1) * ATTN_BLOCK], preferred_element_type=f32)
        k_ref[:, cols] = _bf16(t)
        return [t[tm - SUBLANES:, c * LANES:(c + 1) * LANES] for c in range(ATTN_BLOCK // LANES)]

    def transposed_piece(first_col, o_ref, part):
        rows = slice(part * ATTN_BLOCK, (part + 1) * ATTN_BLOCK)
        t = jnp.dot(h_scr[rows, :], w_ref[:, first_col:first_col + d_attn],
                    preferred_element_type=f32)
        o_ref[part] = _bf16(t.T)
        return [t[ATTN_BLOCK - SUBLANES:, c * LANES:(c + 1) * LANES] for c in range(d_attn // LANES)]

    side_work = {0: lambda: k_piece(0), 2: lambda: k_piece(1),
                 5: lambda: transposed_piece(0, qt_ref, 0),
                 8: lambda: transposed_piece(0, qt_ref, 1),
                 11: lambda: transposed_piece(2 * d_attn, vt_ref, 0),
                 13: lambda: transposed_piece(2 * d_attn, vt_ref, 1)}
    assert n_sub > max(side_work) and qt_ref.shape[0] == 2 and d_attn == 2 * ATTN_BLOCK

    c_gelu = math.sqrt(2.0 / math.pi)
    for l in range(n_slab):
        for s in range(n_sub):
            gate = jnp.concatenate([piece(n_slab + l, s, c) for c in range(CHAIN)], axis=0)
            gelu_scr[l, s * SUB_ROWS:(s + 1) * SUB_ROWS, :] = (0.5 * gate) * (1.0 + jnp.tanh(
                gate * (c_gelu + (c_gelu * 0.044715) * (gate * gate))))

    c2_row = (0.5 * LRU_C * math.log2(math.e)) * _log_sigmoid(lru_l_ref[...])

    carry = [carry_ref[:, l * LANES:(l + 1) * LANES] for l in range(n_slab)]
    pending_at = -1
    keep_state = zero_ref[...] == 0
    for s in range(n_sub):
        if s in side_work:
            anchor = functools.reduce(jnp.add, side_work[s]())
            pending_at = s + 1
        base = s * SUB_ROWS
        rows = slice(base, base + SUB_ROWS)
        for l in range(n_slab):
            c2 = c2_row[:, l * LANES:(l + 1) * LANES]
            u = u_scr[l, rows, :]
            t_r = jnp.tanh(z_scr[l, rows, :])
            t_i = jnp.tanh(z_scr[n_slab + l, rows, :])
            log2_a = c2 + c2 * t_r
            a = jnp.exp2(log2_a)
            quarter = (jnp.tanh(log2_a * math.log(2.0)) * -0.25) * (a * a + 1.0)
            half_mult = jnp.where(quarter > 0.0, quarter * lax.rsqrt(quarter), 0.0)
            b = (half_mult * u) * (1.0 + t_i)
            gelu = gelu_scr[l, rows, :]

            a_c = [a[c * SUBLANES:(c + 1) * SUBLANES] for c in range(CHAIN)]
            b_c = [b[c * SUBLANES:(c + 1) * SUBLANES] for c in range(CHAIN)]
            h_loc, p_loc = [b_c[0]], [a_c[0]]
            for c in range(1, CHAIN):
                h_loc.append(a_c[c] * h_loc[c - 1] + b_c[c])
                p_loc.append(a_c[c] * p_loc[c - 1])
            p_cum, h_cum = p_loc[CHAIN - 1], h_loc[CHAIN - 1]
            shift = 1
            while shift < SUBLANES:
                valid = row >= shift
                p_prev = jnp.where(valid, pltpu.roll(p_cum, shift, axis=0), 1.0)
                h_prev = jnp.where(valid, pltpu.roll(h_cum, shift, axis=0), 0.0)
                h_cum = p_cum * h_prev + h_cum
                p_cum = p_cum * p_prev
                shift *= 2
            chain_end = h_cum + p_cum * carry[l]
            chain_in = jnp.where(row == 0, carry[l], pltpu.roll(chain_end, 1, axis=0))
            carry[l] = jnp.broadcast_to(chain_end[SUBLANES - 1:SUBLANES, :], (SUBLANES, LANES))
            if s == pending_at:
                carry[l] = jnp.where(keep_state, carry[l], anchor)
            for c in range(CHAIN):
                out = (h_loc[c] + p_loc[c] * chain_in) * gelu[c * SUBLANES:(c + 1) * SUBLANES]
                out_slab[l, pl.ds(base + c, SUBLANES, stride=CHAIN), :] = out
    for l in range(n_slab):
        carry_ref[:, l * LANES:(l + 1) * LANES] = carry[l]
    lru_ref[...] = _bf16(jnp.concatenate([out_slab[l] for l in range(n_slab)], axis=1))


def _in_proj_lru(x, g, w, conv_w, conv_b, w_gates, b_gates, lru_l):
    B, S, D = x.shape
    d_lru = lru_l.shape[-1]
    d_attn = (w.shape[1] - 2 * d_lru) // 3
    tm = PROJ_TOKENS
    nblk = tm // ATTN_BLOCK
    n_slab = d_lru // LANES
    const = lambda b, i: (0, 0)
    assert D % W_IN_CHUNK_ROWS == 0
    return pl.pallas_call(
        _in_proj_lru_kernel,
        grid=(B, S // tm),
        in_specs=[
            pl.BlockSpec((None, tm, D), lambda b, i: (b, i, 0)),
            pl.BlockSpec((1, D), const),
            pl.BlockSpec(memory_space=pl.ANY),
            pl.BlockSpec(conv_w.shape, const),
            pl.BlockSpec(conv_b.shape, const),
            pl.BlockSpec(w_gates.shape, lambda b, i: (0, 0, 0)),
            pl.BlockSpec(b_gates.shape, lambda b, i: (0, 0, 0)),
            pl.BlockSpec(lru_l.shape, const),
            pl.BlockSpec((SUBLANES, LANES), const),
        ],
        out_specs=[
            pl.BlockSpec((None, nblk, d_attn, ATTN_BLOCK), lambda b, i: (b, i, 0, 0)),
            pl.BlockSpec((None, tm, d_attn), lambda b, i: (b, i, 0)),
            pl.BlockSpec((None, nblk, d_attn, ATTN_BLOCK), lambda b, i: (b, i, 0, 0)),
            pl.BlockSpec((None, tm, d_lru), lambda b, i: (b, i, 0)),
        ],
        out_shape=[
            jax.ShapeDtypeStruct((B, S // ATTN_BLOCK, d_attn, ATTN_BLOCK), jnp.bfloat16),
            jax.ShapeDtypeStruct((B, S, d_attn), jnp.bfloat16),
            jax.ShapeDtypeStruct((B, S // ATTN_BLOCK, d_attn, ATTN_BLOCK), jnp.bfloat16),
            jax.ShapeDtypeStruct((B, S, d_lru), jnp.bfloat16),
        ],
        scratch_shapes=[
            pltpu.VMEM(w.shape, jnp.bfloat16),
            pltpu.VMEM((2, W_IN_CHUNK_ROWS, w.shape[1]), jnp.float32),
            pltpu.SemaphoreType.DMA((2,)),
            pltpu.VMEM((tm, D), jnp.bfloat16),
            pltpu.VMEM((2 * n_slab, tm, LANES), jnp.float32),
            pltpu.VMEM((n_slab, tm, LANES), jnp.float32),
            pltpu.VMEM((tm, d_lru), jnp.bfloat16),
            pltpu.VMEM((2 * n_slab, tm, LANES), jnp.float32),
            pltpu.VMEM((n_slab, tm, LANES), jnp.float32),
            pltpu.VMEM((n_slab, tm, LANES), jnp.float32),
            pltpu.VMEM((CHAIN, SUBLANES, d_lru), jnp.float32),
            pltpu.VMEM((SUBLANES, d_lru), jnp.float32),
        ],
        compiler_params=pltpu.CompilerParams(
            dimension_semantics=("arbitrary", "arbitrary"),
            vmem_limit_bytes=V7X_VMEM_LIMIT_BYTES),
        name="in_proj_lru",
    )(x, g, w, conv_w, conv_b, w_gates, b_gates, lru_l, jnp.zeros((SUBLANES, LANES), jnp.int32))


N_POS_COLS = 128
N_ONES_ROWS = 16
SCORE_LEAD = 2
N_SCORE_BUFS = SCORE_LEAD + 1
ATTN_Q_PER_STEP = 9


def _diff_attn_kernel(qt_ref, k_ref, pos_ref, vt_ref, lamv_ref, sg_ref, o_ref,
                      s_ref, m_ref, acc_ref, *, slopes, q_per_step):
    h = pl.program_id(1)
    blk = ATTN_BLOCK
    n_blocks = vt_ref.shape[0]

    slope = jnp.float32(slopes[0])
    for i in range(1, len(slopes)):
        slope = jnp.where(h == i, jnp.float32(slopes[i]), slope)

    r = lax.broadcasted_iota(jnp.int32, (N_POS_COLS, blk), 0)
    lane = lax.broadcasted_iota(jnp.int32, (N_POS_COLS, blk), 1).astype(jnp.float32)
    zeros = jnp.zeros((HEAD_DIM, blk), qt_ref.dtype)
    key_in_blk = lax.broadcasted_iota(jnp.int32, (blk, 2 * blk), 0)
    qry_in_blk = lax.broadcasted_iota(jnp.int32, (blk, 2 * blk), 1) % blk
    causal = key_in_blk <= qry_in_blk
    ones_row = lax.broadcasted_iota(jnp.int32, (N_ONES_ROWS, blk), 0) == 0
    ones_blk = jnp.where(ones_row, 1.0, 0.0).astype(jnp.bfloat16)

    def q_operand(qt, q_index):
        alibi = _bf16(jnp.where(r == 0, slope,
                      jnp.where(r == 1, slope * blk,
                      jnp.where(r == 2, -slope * lane,
                      jnp.where(r == 3, -slope * float(q_index * blk), 0.0)))))
        return jnp.concatenate(
            [jnp.concatenate([qt[:HEAD_DIM], zeros, alibi], axis=0),
             jnp.concatenate([zeros, qt[HEAD_DIM:], alibi], axis=0)], axis=1)

    def score_block(q_aug, j, diagonal):
        rows = slice(j * blk, (j + 1) * blk)
        k_aug = jnp.concatenate([k_ref[rows, :], pos_ref[rows, :]], axis=1)
        s = jnp.dot(k_aug, q_aug, preferred_element_type=jnp.float32)
        return jnp.where(causal, s, -jnp.inf) if diagonal else s

    def finish(t):
        acc = acc_ref[...]
        lamv = lamv_ref[...]
        lam = (jnp.exp(jnp.sum(lamv[0:1] * lamv[1:2], axis=-1, keepdims=True))
               - jnp.exp(jnp.sum(lamv[2:3] * lamv[3:4], axis=-1, keepdims=True))
               + LAMBDA_INIT)
        heads = acc[:V_HEAD_DIM] / acc[V_HEAD_DIM:V_HEAD_DIM + 1]
        o = heads[:, :blk] - lam * heads[:, blk:]
        o = o * lax.rsqrt(jnp.mean(o * o, axis=0, keepdims=True) + EPS)
        o = o * sg_ref[...] * (1.0 - LAMBDA_INIT)
        o_ref[t * blk:(t + 1) * blk, :] = _bf16(o.T)

    def score_q_block(q_aug, q_index, j, m_run):
        s = score_block(q_aug, j, j == q_index)
        s_ref[q_index % N_SCORE_BUFS, j * blk:(j + 1) * blk, :] = s
        col_max = jnp.max(s, axis=0, keepdims=True)
        return col_max if m_run is None else jnp.maximum(m_run, col_max)

    def sub_step(t):
        if t >= 1:
            finish(t - 1)
        if t == n_blocks:
            return
        if t == 0:
            for q_index in range(min(SCORE_LEAD, n_blocks)):
                q_aug = q_operand(qt_ref[q_index], q_index)
                m_run = None
                for j in range(q_index + 1):
                    m_run = score_q_block(q_aug, q_index, j, m_run)
                m_ref[q_index % N_SCORE_BUFS] = m_run
        cur = t % N_SCORE_BUFS
        ahead = t + SCORE_LEAD
        scoring = ahead < n_blocks
        q_aug = q_operand(qt_ref[ahead], ahead) if scoring else None
        m_cur = m_ref[cur]
        m_run = None
        acc = jnp.zeros((V_HEAD_DIM + N_ONES_ROWS, 2 * blk), jnp.float32)
        for j in range(ahead + 1 if scoring else t + 1):
            if scoring:
                m_run = score_q_block(q_aug, ahead, j, m_run)
            if j <= t:
                p = _bf16(jnp.exp(s_ref[cur, j * blk:(j + 1) * blk, :] - m_cur))
                vt_aug = jnp.concatenate([vt_ref[j], ones_blk], axis=0)
                acc = acc + jnp.dot(vt_aug, p, preferred_element_type=jnp.float32)
        if scoring:
            m_ref[ahead % N_SCORE_BUFS] = m_run
        acc_ref[...] = acc

    def grid_step(first):
        def run():
            for t in range(first, min(first + q_per_step, n_blocks + 1)):
                sub_step(t)
        return run

    steps = [grid_step(first) for first in range(0, n_blocks + 1, q_per_step)]
    if len(steps) == 1:
        steps[0]()
    else:
        lax.switch(pl.program_id(2), steps)


def _alibi_slopes(n_heads):
    return [2.0 ** (-8.0 * (i + 1) / n_heads) for i in range(n_heads)]


def _key_position_columns(seq_len):
    j = np.arange(seq_len)
    pos = np.zeros((seq_len, N_POS_COLS), np.float32)
    pos[:, 0] = j % ATTN_BLOCK
    pos[:, 1] = j // ATTN_BLOCK
    pos[:, 2:4] = 1.0
    assert ATTN_BLOCK <= 256 and seq_len // ATTN_BLOCK <= 256
    return jnp.asarray(pos, jnp.bfloat16)


def _diff_attn(qt, k, vt, lamv, subln_g):
    B, nblk, d_attn, blk = qt.shape
    S = k.shape[1]
    H = d_attn // V_HEAD_DIM
    pos = _key_position_columns(S)
    kernel = functools.partial(_diff_attn_kernel, slopes=tuple(_alibi_slopes(H)),
                               q_per_step=ATTN_Q_PER_STEP)
    n_steps = pl.cdiv(nblk + 1, ATTN_Q_PER_STEP)
    return pl.pallas_call(
        kernel,
        grid=(B, H, n_steps),
        in_specs=[
            pl.BlockSpec((None, nblk, V_HEAD_DIM, blk), lambda b, h, q: (b, 0, h, 0)),
            pl.BlockSpec((None, S, V_HEAD_DIM), lambda b, h, q: (b, 0, h)),
            pl.BlockSpec(pos.shape, lambda b, h, q: (0, 0)),
            pl.BlockSpec((None, nblk, V_HEAD_DIM, blk), lambda b, h, q: (b, 0, h, 0)),
            pl.BlockSpec(lamv.shape, lambda b, h, q: (0, 0)),
            pl.BlockSpec(subln_g.shape, lambda b, h, q: (0, 0)),
        ],
        out_specs=pl.BlockSpec((None, S, V_HEAD_DIM), lambda b, h, q: (b, 0, h)),
        out_shape=jax.ShapeDtypeStruct((B, S, d_attn), jnp.bfloat16),
        scratch_shapes=[pltpu.VMEM((N_SCORE_BUFS, S, 2 * blk), jnp.float32),
                        pltpu.VMEM((N_SCORE_BUFS, 1, 2 * blk), jnp.float32),
                        pltpu.VMEM((V_HEAD_DIM + N_ONES_ROWS, 2 * blk), jnp.float32)],
        compiler_params=pltpu.CompilerParams(
            dimension_semantics=("parallel", "parallel", "arbitrary"),
            vmem_limit_bytes=V7X_VMEM_LIMIT_BYTES),
        name="diff_attn",
    )(qt, k, pos, vt, lamv, subln_g)


def _out_mlp_kernel(x_ref, attn_ref, lru_ref, wo_hbm, gm_ref, wup_hbm, wdn_hbm, gf_ref, y_ref,
                    wo_ref, wup_ref, wdn_ref, stage_d, stage_ff, sem):
    @pl.when(pl.program_id(0) == 0)
    def _():
        _stream_cast_rows(wo_hbm, wo_ref, stage_d, sem, stage_d.shape[1])
        _stream_cast_rows(wup_hbm, wup_ref, stage_ff, sem, stage_ff.shape[1])
        _stream_cast_rows(wdn_hbm, wdn_ref, stage_d, sem, stage_d.shape[1])

    tm = x_ref.shape[0]
    rows = [slice(g * tm // MLP_SPLIT, (g + 1) * tm // MLP_SPLIT) for g in range(MLP_SPLIT)]
    x1 = []
    for r in rows:
        mix = jnp.concatenate([attn_ref[r, :], lru_ref[r, :]], axis=-1)
        x1.append(x_ref[r, :] + jnp.dot(mix, wo_ref[...], preferred_element_type=jnp.float32))
    hm = [_bf16(_rmsnorm_rows(v, gm_ref[...])) for v in x1]
    d_ff = wup_ref.shape[1]
    mlp = [jnp.zeros_like(v) for v in x1]
    for c in range(d_ff // FF_CHUNK):
        cols = slice(c * FF_CHUNK, (c + 1) * FF_CHUNK)
        for g in range(MLP_SPLIT):
            up = jnp.dot(hm[g], wup_ref[:, cols], preferred_element_type=jnp.float32)
            act = _bf16(jnp.square(jnp.maximum(up, 0.0)))
            mlp[g] = mlp[g] + jnp.dot(act, wdn_ref[cols, :], preferred_element_type=jnp.float32)
    for g, r in enumerate(rows):
        y_ref[r, :] = _rmsnorm_rows(x1[g] + mlp[g], gf_ref[...])


def _out_mlp(x, attn, lru, w_out, g_mlp, w_up, w_down, g_final):
    B, S, D = x.shape
    T = B * S
    tm = MLP_TOKENS
    d_ff = w_up.shape[1]
    x2 = x.reshape(T, D)
    attn2 = attn.reshape(T, attn.shape[-1])
    lru2 = lru.reshape(T, lru.shape[-1])
    const = lambda i: (0, 0)
    in_hbm = pl.BlockSpec(memory_space=pl.ANY)
    rows_d = WEIGHT_CHUNK_BYTES // (4 * D)
    rows_ff = WEIGHT_CHUNK_BYTES // (4 * d_ff)
    assert D % rows_d == 0 and d_ff % rows_d == 0 and D % rows_ff == 0
    y = pl.pallas_call(
        _out_mlp_kernel,
        grid=(T // tm,),
        in_specs=[
            pl.BlockSpec((tm, D), lambda i: (i, 0)),
            pl.BlockSpec((tm, attn2.shape[1]), lambda i: (i, 0)),
            pl.BlockSpec((tm, lru2.shape[1]), lambda i: (i, 0)),
            in_hbm,
            pl.BlockSpec(g_mlp.shape, const),
            in_hbm,
            in_hbm,
            pl.BlockSpec(g_final.shape, const),
        ],
        out_specs=pl.BlockSpec((tm, D), lambda i: (i, 0)),
        out_shape=jax.ShapeDtypeStruct((T, D), jnp.float32),
        scratch_shapes=[
            pltpu.VMEM(w_out.shape, jnp.bfloat16),
            pltpu.VMEM(w_up.shape, jnp.bfloat16),
            pltpu.VMEM(w_down.shape, jnp.bfloat16),
            pltpu.VMEM((2, rows_d, D), jnp.float32),
            pltpu.VMEM((2, rows_ff, d_ff), jnp.float32),
            pltpu.SemaphoreType.DMA((2,)),
        ],
        compiler_params=pltpu.CompilerParams(
            dimension_semantics=("arbitrary",),
            vmem_limit_bytes=V7X_VMEM_LIMIT_BYTES),
        name="out_mlp",
    )(x2, attn2, lru2, w_out, g_mlp, w_up, w_down, g_final)
    return y.reshape(B, S, D)


def _pair_block_diag(w):
    n, c, d = w.shape
    eye = jnp.eye(2, dtype=w.dtype)
    pairs = w.reshape(n // 2, 2, c, 1, d) * eye[None, :, None, :, None]
    return pairs.reshape(n // 2, 2 * c, 2 * d)


def kernel(x, norm_mix_g, w_in, conv_w, conv_b, w_rg, b_rg, w_ig, b_ig, lru_L, lambda_q1, lambda_k1, lambda_q2, lambda_k2, subln_g, w_out, norm_mlp_g, w_up, w_down, final_g):
    B, S, D = x.shape
    d_lru = lru_L.shape[-1]
    d_attn = (w_in.shape[-1] - 2 * d_lru) // 3
    depth = w_in.shape[0]
    n_slab = d_lru // LANES
    assert depth == 1, "LAMBDA_INIT is specialised to a single layer"
    assert S % PROJ_TOKENS == 0 and (B * S) % MLP_TOKENS == 0
    assert PROJ_TOKENS % SUB_ROWS == 0 and d_lru % LANES == 0
    assert 2 * w_rg.shape[-1] == LANES and w_rg.shape[1] * w_rg.shape[2] == d_lru
    l = 0
    row = lambda v: v.reshape(1, -1)

    w_gates = _bf16(0.5 * jnp.concatenate([_pair_block_diag(w_rg[l]), _pair_block_diag(w_ig[l])],
                                          axis=-1))
    b_gates = 0.5 * jnp.concatenate([b_rg[l].reshape(n_slab, 1, LANES),
                                     b_ig[l].reshape(n_slab, 1, LANES)], axis=-1)
    lamv = jnp.stack([lambda_q1[l], lambda_k1[l], lambda_q2[l], lambda_k2[l]]).astype(jnp.float32)

    qt, k, vt, lru = _in_proj_lru(x, row(norm_mix_g[l]), w_in[l], conv_w[l], row(conv_b[l]),
                                  w_gates, b_gates, row(lru_L[l]))
    attn = _diff_attn(qt, k, vt, lamv, subln_g[l].reshape(-1, 1))
    return _out_mlp(x, attn, lru, w_out[l], row(norm_mlp_g[l]), w_up[l], w_down[l], row(final_g))
```

```python
import functools
import math

import numpy as np
import jax
import jax.numpy as jnp
from jax import lax
from jax.experimental import pallas as pl
from jax.experimental.pallas import tpu as pltpu

EPS = 1e-6
CONV_WIDTH = 4
LRU_C = 8.0
HEAD_DIM = 64
V_HEAD_DIM = 2 * HEAD_DIM
LAMBDA_INIT = 0.8 - 0.6 * math.exp(-0.3 * 0)

V7X_MXU_COLS = 256
LANES = 128
SUBLANES = 8
V7X_VMEM_LIMIT_BYTES = 56 * 1024 * 1024

ATTN_BLOCK = V7X_MXU_COLS
PROJ_TOKENS = 512
MLP_TOKENS = 512
FF_CHUNK = 1024
MLP_SPLIT = 2
WEIGHT_CHUNK_BYTES = 2 * 1024 * 1024
W_IN_CHUNK_ROWS = 128
CHAIN = 4
SUB_ROWS = SUBLANES * CHAIN


def _bf16(x):
    return x.astype(jnp.bfloat16)


def _rmsnorm_rows(x, g):
    return x * lax.rsqrt(jnp.mean(x * x, axis=-1, keepdims=True) + EPS) * g


def _stream_cast_rows(src_hbm, dst_ref, stage_ref, sem, chunk_rows, scale=None):
    n_chunks = src_hbm.shape[0] // chunk_rows

    def copy(i):
        return pltpu.make_async_copy(src_hbm.at[pl.ds(i * chunk_rows, chunk_rows), :],
                                     stage_ref.at[i % 2], sem.at[i % 2])

    copy(0).start()
    for i in range(n_chunks):
        if i + 1 < n_chunks:
            copy(i + 1).start()
        copy(i).wait()
        chunk = stage_ref[i % 2]
        if scale is not None:
            chunk = chunk * scale
        dst_ref[i * chunk_rows:(i + 1) * chunk_rows, :] = _bf16(chunk)


def _log_sigmoid(x):
    return jnp.minimum(x, 0.0) - jnp.log1p(jnp.exp(-jnp.abs(x)))


def _in_proj_lru_kernel(x_ref, g_ref, w_hbm, cw_ref, cb_ref, wg_ref, bg_ref, lru_l_ref, zero_ref,
                        qt_ref, k_ref, vt_ref, lru_ref,
                        w_ref, w_stage, w_sem, h_scr, ug_slab, u_scr, ub_scr, z_scr, gelu_scr,
                        out_slab, tail_ref, carry_ref):
    tm = x_ref.shape[0]
    d_lru = lru_ref.shape[-1]
    n_slab = d_lru // LANES
    n_sub = tm // SUB_ROWS
    d_attn = k_ref.shape[-1]
    f32 = jnp.float32

    @pl.when((pl.program_id(0) == 0) & (pl.program_id(1) == 0))
    def _():
        col = lax.broadcasted_iota(jnp.int32, (1, w_ref.shape[1]), 1)
        q_scale = jnp.where(col < d_attn, HEAD_DIM ** -0.5, 1.0)
        _stream_cast_rows(w_hbm, w_ref, w_stage, w_sem, w_stage.shape[1], scale=q_scale)

    @pl.when(pl.program_id(1) == 0)
    def _():
        tail_ref[...] = jnp.zeros(tail_ref.shape, f32)
        carry_ref[...] = jnp.zeros(carry_ref.shape, f32)

    h_scr[...] = _bf16(_rmsnorm_rows(x_ref[...], g_ref[...]))

    def ug_half(half):
        cols = slice(3 * d_attn + half * d_lru, 3 * d_attn + (half + 1) * d_lru)
        nat = jnp.dot(h_scr[...], w_ref[:, cols], preferred_element_type=f32)
        for l in range(n_slab):
            ug_slab[half * n_slab + l] = nat[:, l * LANES:(l + 1) * LANES]

    ug_half(0)

    def piece(slab, s, c):
        return ug_slab[slab, pl.ds(s * SUB_ROWS + c, SUBLANES, stride=CHAIN), :]

    row = lax.broadcasted_iota(jnp.int32, (SUBLANES, LANES), 0)

    for l in range(n_slab):
        lanes = slice(l * LANES, (l + 1) * LANES)
        cw = cw_ref[:, lanes]
        cb = cb_ref[:, lanes]
        rolled_prev = [pltpu.roll(tail_ref[c, :, lanes], 1, axis=0) for c in range(CHAIN)]
        for s in range(n_sub):
            cur = [piece(l, s, c) for c in range(CHAIN)]
            rolled = [pltpu.roll(cur[c], 1, axis=0) for c in range(CHAIN)]
            shifted = [jnp.where(row == 0, rolled_prev[c], rolled[c]) for c in range(CHAIN)]
            rolled_prev = rolled
            if s == n_sub - 1:
                for c in range(CHAIN):
                    tail_ref[c, :, lanes] = cur[c]
            conv = []
            for c in range(CHAIN):
                acc = cb
                for kk in range(CONV_WIDTH):
                    d = CONV_WIDTH - 1 - kk
                    src = cur[c - d] if c >= d else shifted[c - d + CHAIN]
                    acc = acc + src * cw[kk:kk + 1, :]
                conv.append(acc)
            u_sub = jnp.concatenate(conv, axis=0)
            u_scr[l, s * SUB_ROWS:(s + 1) * SUB_ROWS, :] = u_sub
            ub_scr[s * SUB_ROWS:(s + 1) * SUB_ROWS, lanes] = _bf16(u_sub)

    ug_half(1)
    for l in range(n_slab):
        z = jnp.dot(ub_scr[:, l * LANES:(l + 1) * LANES], wg_ref[l],
                    preferred_element_type=f32) + bg_ref[l]
        z_scr[l] = z[:, :LANES]
        z_scr[n_slab + l] = z[:, LANES:]

    def k_piece(part):
        cols = slice(part * ATTN_BLOCK, (part + 1) * ATTN_BLOCK)
        t = jnp.dot(h_scr[...], w_ref[:, d_attn + part * ATTN_BLOCK:
                                     d_attn + (part + 1) * ATTN_BLOCK], preferred_element_type=f32)
        k_ref[:, cols] = _bf16(t)
        return [t[tm - SUBLANES:, c * LANES:(c + 1) * LANES] for c in range(ATTN_BLOCK // LANES)]

    def transposed_piece(first_col, o_ref, part):
        rows = slice(part * ATTN_BLOCK, (part + 1) * ATTN_BLOCK)
        t = jnp.dot(h_scr[rows, :], w_ref[:, first_col:first_col + d_attn],
                    preferred_element_type=f32)
        o_ref[part] = _bf16(t.T)
        return [t[ATTN_BLOCK - SUBLANES:, c * LANES:(c + 1) * LANES] for c in range(d_attn // LANES)]

    side_work = {0: lambda: k_piece(0), 2: lambda: k_piece(1),
                 5: lambda: transposed_piece(0, qt_ref, 0),
                 8: lambda: transposed_piece(0, qt_ref, 1),
                 11: lambda: transposed_piece(2 * d_attn, vt_ref, 0),
                 13: lambda: transposed_piece(2 * d_attn, vt_ref, 1)}
    assert n_sub > max(side_work) and qt_ref.shape[0] == 2 and d_attn == 2 * ATTN_BLOCK

    c_gelu = math.sqrt(2.0 / math.pi)
    for l in range(n_slab):
        for s in range(n_sub):
            gate = jnp.concatenate([piece(n_slab + l, s, c) for c in range(CHAIN)], axis=0)
            gelu_scr[l, s * SUB_ROWS:(s + 1) * SUB_ROWS, :] = (0.5 * gate) * (1.0 + jnp.tanh(
                gate * (c_gelu + (c_gelu * 0.044715) * (gate * gate))))

    c2_row = (0.5 * LRU_C * math.log2(math.e)) * _log_sigmoid(lru_l_ref[...])

    carry = [carry_ref[:, l * LANES:(l + 1) * LANES] for l in range(n_slab)]
    pending_at = -1
    keep_state = zero_ref[...] == 0
    for s in range(n_sub):
        if s in side_work:
            anchor = functools.reduce(jnp.add, side_work[s]())
            pending_at = s + 1
        base = s * SUB_ROWS
        rows = slice(base, base + SUB_ROWS)
        for l in range(n_slab):
            c2 = c2_row[:, l * LANES:(l + 1) * LANES]
            u = u_scr[l, rows, :]
            t_r = jnp.tanh(z_scr[l, rows, :])
            t_i = jnp.tanh(z_scr[n_slab + l, rows, :])
            log2_a = c2 + c2 * t_r
            a = jnp.exp2(log2_a)
            quarter = (jnp.tanh(log2_a * math.log(2.0)) * -0.25) * (a * a + 1.0)
            half_mult = jnp.where(quarter > 0.0, quarter * lax.rsqrt(quarter), 0.0)
            b = (half_mult * u) * (1.0 + t_i)
            gelu = gelu_scr[l, rows, :]

            a_c = [a[c * SUBLANES:(c + 1) * SUBLANES] for c in range(CHAIN)]
            b_c = [b[c * SUBLANES:(c + 1) * SUBLANES] for c in range(CHAIN)]
            h_loc, p_loc = [b_c[0]], [a_c[0]]
            for c in range(1, CHAIN):
                h_loc.append(a_c[c] * h_loc[c - 1] + b_c[c])
                p_loc.append(a_c[c] * p_loc[c - 1])
            p_cum, h_cum = p_loc[CHAIN - 1], h_loc[CHAIN - 1]
            shift = 1
            while shift < SUBLANES:
                valid = row >= shift
                p_prev = jnp.where(valid, pltpu.roll(p_cum, shift, axis=0), 1.0)
                h_prev = jnp.where(valid, pltpu.roll(h_cum, shift, axis=0), 0.0)
                h_cum = p_cum * h_prev + h_cum
                p_cum = p_cum * p_prev
                shift *= 2
            chain_end = h_cum + p_cum * carry[l]
            chain_in = jnp.where(row == 0, carry[l], pltpu.roll(chain_end, 1, axis=0))
            carry[l] = jnp.broadcast_to(chain_end[SUBLANES - 1:SUBLANES, :], (SUBLANES, LANES))
            if s == pending_at:
                carry[l] = jnp.where(keep_state, carry[l], anchor)
            for c in range(CHAIN):
                out = (h_loc[c] + p_loc[c] * chain_in) * gelu[c * SUBLANES:(c + 1) * SUBLANES]
                out_slab[l, pl.ds(base + c, SUBLANES, stride=CHAIN), :] = out
    for l in range(n_slab):
        carry_ref[:, l * LANES:(l + 1) * LANES] = carry[l]
    lru_ref[...] = _bf16(jnp.concatenate([out_slab[l] for l in range(n_slab)], axis=1))


def _in_proj_lru(x, g, w, conv_w, conv_b, w_gates, b_gates, lru_l):
    B, S, D = x.shape
    d_lru = lru_l.shape[-1]
    d_attn = (w.shape[1] - 2 * d_lru) // 3
    tm = PROJ_TOKENS
    nblk = tm // ATTN_BLOCK
    n_slab = d_lru // LANES
    const = lambda b, i: (0, 0)
    assert D % W_IN_CHUNK_ROWS == 0
    return pl.pallas_call(
        _in_proj_lru_kernel,
        grid=(B, S // tm),
        in_specs=[
            pl.BlockSpec((None, tm, D), lambda b, i: (b, i, 0)),
            pl.BlockSpec((1, D), const),
            pl.BlockSpec(memory_space=pl.ANY),
            pl.BlockSpec(conv_w.shape, const),
            pl.BlockSpec(conv_b.shape, const),
            pl.BlockSpec(w_gates.shape, lambda b, i: (0, 0, 0)),
            pl.BlockSpec(b_gates.shape, lambda b, i: (0, 0, 0)),
            pl.BlockSpec(lru_l.shape, const),
            pl.BlockSpec((SUBLANES, LANES), const),
        ],
        out_specs=[
            pl.BlockSpec((None, nblk, d_attn, ATTN_BLOCK), lambda b, i: (b, i, 0, 0)),
            pl.BlockSpec((None, tm, d_attn), lambda b, i: (b, i, 0)),
            pl.BlockSpec((None, nblk, d_attn, ATTN_BLOCK), lambda b, i: (b, i, 0, 0)),
            pl.BlockSpec((None, tm, d_lru), lambda b, i: (b, i, 0)),
        ],
        out_shape=[
            jax.ShapeDtypeStruct((B, S // ATTN_BLOCK, d_attn, ATTN_BLOCK), jnp.bfloat16),
            jax.ShapeDtypeStruct((B, S, d_attn), jnp.bfloat16),
            jax.ShapeDtypeStruct((B, S // ATTN_BLOCK, d_attn, ATTN_BLOCK), jnp.bfloat16),
            jax.ShapeDtypeStruct((B, S, d_lru), jnp.bfloat16),
        ],
        scratch_shapes=[
            pltpu.VMEM(w.shape, jnp.bfloat16),
            pltpu.VMEM((2, W_IN_CHUNK_ROWS, w.shape[1]), jnp.float32),
            pltpu.SemaphoreType.DMA((2,)),
            pltpu.VMEM((tm, D), jnp.bfloat16),
            pltpu.VMEM((2 * n_slab, tm, LANES), jnp.float32),
            pltpu.VMEM((n_slab, tm, LANES), jnp.float32),
            pltpu.VMEM((tm, d_lru), jnp.bfloat16),
            pltpu.VMEM((2 * n_slab, tm, LANES), jnp.float32),
            pltpu.VMEM((n_slab, tm, LANES), jnp.float32),
            pltpu.VMEM((n_slab, tm, LANES), jnp.float32),
            pltpu.VMEM((CHAIN, SUBLANES, d_lru), jnp.float32),
            pltpu.VMEM((SUBLANES, d_lru), jnp.float32),
        ],
        compiler_params=pltpu.CompilerParams(
            dimension_semantics=("arbitrary", "arbitrary"),
            vmem_limit_bytes=V7X_VMEM_LIMIT_BYTES),
        name="in_proj_lru",
    )(x, g, w, conv_w, conv_b, w_gates, b_gates, lru_l, jnp.zeros((SUBLANES, LANES), jnp.int32))


N_POS_COLS = 128
N_ONES_ROWS = 16
SCORE_LEAD = 2
N_SCORE_BUFS = SCORE_LEAD + 1
ATTN_Q_PER_STEP = 9


def _diff_attn_kernel(qt_ref, k_ref, pos_ref, vt_ref, lq1_ref, lk1_ref, lq2_ref, lk2_ref, sg_ref,
                      o_ref,
                      s_ref, m_ref, acc_ref, *, slopes, q_per_step):
    h = pl.program_id(1)
    blk = ATTN_BLOCK
    n_blocks = vt_ref.shape[0]

    slope = jnp.float32(slopes[0])
    for i in range(1, len(slopes)):
        slope = jnp.where(h == i, jnp.float32(slopes[i]), slope)

    r = lax.broadcasted_iota(jnp.int32, (N_POS_COLS, blk), 0)
    lane = lax.broadcasted_iota(jnp.int32, (N_POS_COLS, blk), 1).astype(jnp.float32)
    zeros = jnp.zeros((HEAD_DIM, blk), qt_ref.dtype)
    key_in_blk = lax.broadcasted_iota(jnp.int32, (blk, 2 * blk), 0)
    qry_in_blk = lax.broadcasted_iota(jnp.int32, (blk, 2 * blk), 1) % blk
    causal = key_in_blk <= qry_in_blk
    ones_row = lax.broadcasted_iota(jnp.int32, (N_ONES_ROWS, blk), 0) == 0
    ones_blk = jnp.where(ones_row, 1.0, 0.0).astype(jnp.bfloat16)

    def q_operand(qt, q_index):
        alibi = _bf16(jnp.where(r == 0, slope,
                      jnp.where(r == 1, slope * blk,
                      jnp.where(r == 2, -slope * lane,
                      jnp.where(r == 3, -slope * float(q_index * blk), 0.0)))))
        return jnp.concatenate(
            [jnp.concatenate([qt[:HEAD_DIM], zeros, alibi], axis=0),
             jnp.concatenate([zeros, qt[HEAD_DIM:], alibi], axis=0)], axis=1)

    def score_block(q_aug, j, diagonal):
        rows = slice(j * blk, (j + 1) * blk)
        k_aug = jnp.concatenate([k_ref[rows, :], pos_ref[rows, :]], axis=1)
        s = jnp.dot(k_aug, q_aug, preferred_element_type=jnp.float32)
        return jnp.where(causal, s, -jnp.inf) if diagonal else s

    def finish(t):
        acc = acc_ref[...]
        lam = (jnp.exp(jnp.sum(lq1_ref[...] * lk1_ref[...], axis=-1, keepdims=True))
               - jnp.exp(jnp.sum(lq2_ref[...] * lk2_ref[...], axis=-1, keepdims=True))
               + LAMBDA_INIT)
        heads = acc[:V_HEAD_DIM] / acc[V_HEAD_DIM:V_HEAD_DIM + 1]
        o = heads[:, :blk] - lam * heads[:, blk:]
        o = o * lax.rsqrt(jnp.mean(o * o, axis=0, keepdims=True) + EPS)
        o_ref[t * blk:(t + 1) * blk, :] = _bf16(o.T * (sg_ref[...] * (1.0 - LAMBDA_INIT)))

    def score_q_block(q_aug, q_index, j, m_run):
        s = score_block(q_aug, j, j == q_index)
        s_ref[q_index % N_SCORE_BUFS, j * blk:(j + 1) * blk, :] = s
        col_max = jnp.max(s, axis=0, keepdims=True)
        return col_max if m_run is None else jnp.maximum(m_run, col_max)

    def sub_step(t):
        if t >= 1:
            finish(t - 1)
        if t == n_blocks:
            return
        if t == 0:
            for q_index in range(min(SCORE_LEAD, n_blocks)):
                q_aug = q_operand(qt_ref[q_index], q_index)
                m_run = None
                for j in range(q_index + 1):
                    m_run = score_q_block(q_aug, q_index, j, m_run)
                m_ref[q_index % N_SCORE_BUFS] = m_run
        cur = t % N_SCORE_BUFS
        ahead = t + SCORE_LEAD
        scoring = ahead < n_blocks
        q_aug = q_operand(qt_ref[ahead], ahead) if scoring else None
        m_cur = m_ref[cur]
        m_run = None
        acc = jnp.zeros((V_HEAD_DIM + N_ONES_ROWS, 2 * blk), jnp.float32)
        for j in range(ahead + 1 if scoring else t + 1):
            if scoring:
                m_run = score_q_block(q_aug, ahead, j, m_run)
            if j <= t:
                p = _bf16(jnp.exp(s_ref[cur, j * blk:(j + 1) * blk, :] - m_cur))
                vt_aug = jnp.concatenate([vt_ref[j], ones_blk], axis=0)
                acc = acc + jnp.dot(vt_aug, p, preferred_element_type=jnp.float32)
        if scoring:
            m_ref[ahead % N_SCORE_BUFS] = m_run
        acc_ref[...] = acc

    def grid_step(first):
        def run():
            for t in range(first, min(first + q_per_step, n_blocks + 1)):
                sub_step(t)
        return run

    steps = [grid_step(first) for first in range(0, n_blocks + 1, q_per_step)]
    if len(steps) == 1:
        steps[0]()
    else:
        lax.switch(pl.program_id(2), steps)


def _alibi_slopes(n_heads):
    return [2.0 ** (-8.0 * (i + 1) / n_heads) for i in range(n_heads)]


def _key_position_columns(seq_len):
    j = np.arange(seq_len)
    pos = np.zeros((seq_len, N_POS_COLS), np.float32)
    pos[:, 0] = j % ATTN_BLOCK
    pos[:, 1] = j // ATTN_BLOCK
    pos[:, 2:4] = 1.0
    assert ATTN_BLOCK <= 256 and seq_len // ATTN_BLOCK <= 256
    return jnp.asarray(pos, jnp.bfloat16)


def _diff_attn(qt, k, vt, lambdas, subln_g):
    B, nblk, d_attn, blk = qt.shape
    S = k.shape[1]
    H = d_attn // V_HEAD_DIM
    pos = _key_position_columns(S)
    kernel = functools.partial(_diff_attn_kernel, slopes=tuple(_alibi_slopes(H)),
                               q_per_step=ATTN_Q_PER_STEP)
    n_steps = pl.cdiv(nblk + 1, ATTN_Q_PER_STEP)
    return pl.pallas_call(
        kernel,
        grid=(B, H, n_steps),
        in_specs=[
            pl.BlockSpec((None, nblk, V_HEAD_DIM, blk), lambda b, h, q: (b, 0, h, 0)),
            pl.BlockSpec((None, S, V_HEAD_DIM), lambda b, h, q: (b, 0, h)),
            pl.BlockSpec(pos.shape, lambda b, h, q: (0, 0)),
            pl.BlockSpec((None, nblk, V_HEAD_DIM, blk), lambda b, h, q: (b, 0, h, 0)),
            *[pl.BlockSpec(v.shape, lambda b, h, q: (0, 0)) for v in lambdas],
            pl.BlockSpec(subln_g.shape, lambda b, h, q: (0, 0)),
        ],
        out_specs=pl.BlockSpec((None, S, V_HEAD_DIM), lambda b, h, q: (b, 0, h)),
        out_shape=jax.ShapeDtypeStruct((B, S, d_attn), jnp.bfloat16),
        scratch_shapes=[pltpu.VMEM((N_SCORE_BUFS, S, 2 * blk), jnp.float32),
                        pltpu.VMEM((N_SCORE_BUFS, 1, 2 * blk), jnp.float32),
                        pltpu.VMEM((V_HEAD_DIM + N_ONES_ROWS, 2 * blk), jnp.float32)],
        compiler_params=pltpu.CompilerParams(
            dimension_semantics=("parallel", "parallel", "arbitrary"),
            vmem_limit_bytes=V7X_VMEM_LIMIT_BYTES),
        name="diff_attn",
    )(qt, k, pos, vt, *lambdas, subln_g)


def _out_mlp_kernel(x_ref, attn_ref, lru_ref, wo_hbm, gm_ref, wup_hbm, wdn_hbm, gf_ref, y_ref,
                    wo_ref, wup_ref, wdn_ref, stage_d, stage_ff, sem):
    @pl.when(pl.program_id(0) == 0)
    def _():
        _stream_cast_rows(wo_hbm, wo_ref, stage_d, sem, stage_d.shape[1])
        _stream_cast_rows(wup_hbm, wup_ref, stage_ff, sem, stage_ff.shape[1])
        _stream_cast_rows(wdn_hbm, wdn_ref, stage_d, sem, stage_d.shape[1])

    tm = x_ref.shape[0]
    rows = [slice(g * tm // MLP_SPLIT, (g + 1) * tm // MLP_SPLIT) for g in range(MLP_SPLIT)]
    x1 = []
    for r in rows:
        mix = jnp.concatenate([attn_ref[r, :], lru_ref[r, :]], axis=-1)
        x1.append(x_ref[r, :] + jnp.dot(mix, wo_ref[...], preferred_element_type=jnp.float32))
    hm = [_bf16(_rmsnorm_rows(v, gm_ref[...])) for v in x1]
    d_ff = wup_ref.shape[1]
    mlp = [jnp.zeros_like(v) for v in x1]
    for c in range(d_ff // FF_CHUNK):
        cols = slice(c * FF_CHUNK, (c + 1) * FF_CHUNK)
        for g in range(MLP_SPLIT):
            up = jnp.dot(hm[g], wup_ref[:, cols], preferred_element_type=jnp.float32)
            act = _bf16(jnp.square(jnp.maximum(up, 0.0)))
            mlp[g] = mlp[g] + jnp.dot(act, wdn_ref[cols, :], preferred_element_type=jnp.float32)
    for g, r in enumerate(rows):
        y_ref[r, :] = _rmsnorm_rows(x1[g] + mlp[g], gf_ref[...])


def _out_mlp(x, attn, lru, w_out, g_mlp, w_up, w_down, g_final):
    B, S, D = x.shape
    T = B * S
    tm = MLP_TOKENS
    d_ff = w_up.shape[1]
    x2 = x.reshape(T, D)
    attn2 = attn.reshape(T, attn.shape[-1])
    lru2 = lru.reshape(T, lru.shape[-1])
    const = lambda i: (0, 0)
    in_hbm = pl.BlockSpec(memory_space=pl.ANY)
    rows_d = WEIGHT_CHUNK_BYTES // (4 * D)
    rows_ff = WEIGHT_CHUNK_BYTES // (4 * d_ff)
    assert D % rows_d == 0 and d_ff % rows_d == 0 and D % rows_ff == 0
    y = pl.pallas_call(
        _out_mlp_kernel,
        grid=(T // tm,),
        in_specs=[
            pl.BlockSpec((tm, D), lambda i: (i, 0)),
            pl.BlockSpec((tm, attn2.shape[1]), lambda i: (i, 0)),
            pl.BlockSpec((tm, lru2.shape[1]), lambda i: (i, 0)),
            in_hbm,
            pl.BlockSpec(g_mlp.shape, const),
            in_hbm,
            in_hbm,
            pl.BlockSpec(g_final.shape, const),
        ],
        out_specs=pl.BlockSpec((tm, D), lambda i: (i, 0)),
        out_shape=jax.ShapeDtypeStruct((T, D), jnp.float32),
        scratch_shapes=[
            pltpu.VMEM(w_out.shape, jnp.bfloat16),
            pltpu.VMEM(w_up.shape, jnp.bfloat16),
            pltpu.VMEM(w_down.shape, jnp.bfloat16),
            pltpu.VMEM((2, rows_d, D), jnp.float32),
            pltpu.VMEM((2, rows_ff, d_ff), jnp.float32),
            pltpu.SemaphoreType.DMA((2,)),
        ],
        compiler_params=pltpu.CompilerParams(
            dimension_semantics=("arbitrary",),
            vmem_limit_bytes=V7X_VMEM_LIMIT_BYTES),
        name="out_mlp",
    )(x2, attn2, lru2, w_out, g_mlp, w_up, w_down, g_final)
    return y.reshape(B, S, D)


def _pair_block_diag(w):
    n, c, d = w.shape
    rows = w.reshape(n // 2, 2 * c, d)
    tiled = jnp.concatenate([rows, rows], axis=-1)
    row_block = np.arange(2 * c)[:, None] // c
    col_block = np.arange(2 * d)[None, :] // d
    return jnp.where(jnp.asarray(row_block == col_block), tiled, 0.0)


def kernel(x, norm_mix_g, w_in, conv_w, conv_b, w_rg, b_rg, w_ig, b_ig, lru_L, lambda_q1, lambda_k1, lambda_q2, lambda_k2, subln_g, w_out, norm_mlp_g, w_up, w_down, final_g):
    B, S, D = x.shape
    d_lru = lru_L.shape[-1]
    d_attn = (w_in.shape[-1] - 2 * d_lru) // 3
    depth = w_in.shape[0]
    n_slab = d_lru // LANES
    assert depth == 1, "LAMBDA_INIT is specialised to a single layer"
    assert S % PROJ_TOKENS == 0 and (B * S) % MLP_TOKENS == 0
    assert PROJ_TOKENS % SUB_ROWS == 0 and d_lru % LANES == 0
    assert 2 * w_rg.shape[-1] == LANES and w_rg.shape[1] * w_rg.shape[2] == d_lru
    l = 0
    row = lambda v: v.reshape(1, -1)

    w_gates = _bf16(0.5 * jnp.concatenate([_pair_block_diag(w_rg[l]), _pair_block_diag(w_ig[l])],
                                          axis=-1))
    b_gates = 0.5 * jnp.concatenate([b_rg[l].reshape(n_slab, 1, LANES),
                                     b_ig[l].reshape(n_slab, 1, LANES)], axis=-1)
    lambdas = [row(v[l]) for v in (lambda_q1, lambda_k1, lambda_q2, lambda_k2)]

    qt, k, vt, lru = _in_proj_lru(x, row(norm_mix_g[l]), w_in[l], conv_w[l], row(conv_b[l]),
                                  w_gates, b_gates, row(lru_L[l]))
    attn = _diff_attn(qt, k, vt, lambdas, row(subln_g[l]))
    return _out_mlp(x, attn, lru, w_out[l], row(norm_mlp_g[l]), w_up[l], w_down[l], row(final_g))
```

```python
import functools
import math

import numpy as np
import jax
import jax.numpy as jnp
from jax import lax
from jax.experimental import pallas as pl
from jax.experimental.pallas import tpu as pltpu

EPS = 1e-6
CONV_WIDTH = 4
LRU_C = 8.0
HEAD_DIM = 64
V_HEAD_DIM = 2 * HEAD_DIM
LAMBDA_INIT = 0.8 - 0.6 * math.exp(-0.3 * 0)

V7X_MXU_COLS = 256
LANES = 128
SUBLANES = 8
V7X_VMEM_LIMIT_BYTES = 56 * 1024 * 1024

ATTN_BLOCK = V7X_MXU_COLS
PROJ_TOKENS = 512
MLP_TOKENS = 512
FF_CHUNK = 1024
MLP_SPLIT = 2
W_IN_CHUNK_ROWS = 128
CHAIN = 4
SUB_ROWS = SUBLANES * CHAIN


def _bf16(x):
    return x.astype(jnp.bfloat16)


def _rmsnorm_rows(x, g):
    return x * lax.rsqrt(jnp.mean(x * x, axis=-1, keepdims=True) + EPS) * g


def _stream_cast_rows(src_hbm, dst_ref, stage_ref, sem, chunk_rows, scale=None):
    n_chunks = src_hbm.shape[0] // chunk_rows

    def copy(i):
        return pltpu.make_async_copy(src_hbm.at[pl.ds(i * chunk_rows, chunk_rows), :],
                                     stage_ref.at[i % 2], sem.at[i % 2])

    copy(0).start()
    for i in range(n_chunks):
        if i + 1 < n_chunks:
            copy(i + 1).start()
        copy(i).wait()
        chunk = stage_ref[i % 2]
        if scale is not None:
            chunk = chunk * scale
        dst_ref[i * chunk_rows:(i + 1) * chunk_rows, :] = _bf16(chunk)


def _log_sigmoid(x):
    return jnp.minimum(x, 0.0) - jnp.log1p(jnp.exp(-jnp.abs(x)))


def _in_proj_lru_kernel(x_ref, g_ref, w_hbm, cw_ref, cb_ref, wg_ref, bg_ref, lru_l_ref, zero_ref,
                        qt_ref, k_ref, vt_ref, lru_ref,
                        w_ref, w_stage, w_sem, h_scr, ug_slab, u_scr, ub_scr, z_scr, gelu_scr,
                        out_slab, tail_ref, carry_ref):
    tm = x_ref.shape[0]
    d_lru = lru_ref.shape[-1]
    n_slab = d_lru // LANES
    n_sub = tm // SUB_ROWS
    d_attn = k_ref.shape[-1]
    f32 = jnp.float32

    @pl.when((pl.program_id(0) == 0) & (pl.program_id(1) == 0))
    def _():
        col = lax.broadcasted_iota(jnp.int32, (1, w_ref.shape[1]), 1)
        q_scale = jnp.where(col < d_attn, HEAD_DIM ** -0.5, 1.0)
        _stream_cast_rows(w_hbm, w_ref, w_stage, w_sem, w_stage.shape[1], scale=q_scale)

    @pl.when(pl.program_id(1) == 0)
    def _():
        tail_ref[...] = jnp.zeros(tail_ref.shape, f32)
        carry_ref[...] = jnp.zeros(carry_ref.shape, f32)

    h_scr[...] = _bf16(_rmsnorm_rows(x_ref[...], g_ref[...]))

    def ug_half(half):
        cols = slice(3 * d_attn + half * d_lru, 3 * d_attn + (half + 1) * d_lru)
        nat = jnp.dot(h_scr[...], w_ref[:, cols], preferred_element_type=f32)
        for l in range(n_slab):
            ug_slab[half * n_slab + l] = nat[:, l * LANES:(l + 1) * LANES]

    ug_half(0)

    def piece(slab, s, c):
        return ug_slab[slab, pl.ds(s * SUB_ROWS + c, SUBLANES, stride=CHAIN), :]

    row = lax.broadcasted_iota(jnp.int32, (SUBLANES, LANES), 0)

    for l in range(n_slab):
        lanes = slice(l * LANES, (l + 1) * LANES)
        cw = cw_ref[:, lanes]
        cb = cb_ref[:, lanes]
        rolled_prev = [pltpu.roll(tail_ref[c, :, lanes], 1, axis=0) for c in range(CHAIN)]
        for s in range(n_sub):
            cur = [piece(l, s, c) for c in range(CHAIN)]
            rolled = [pltpu.roll(cur[c], 1, axis=0) for c in range(CHAIN)]
            shifted = [jnp.where(row == 0, rolled_prev[c], rolled[c]) for c in range(CHAIN)]
            rolled_prev = rolled
            if s == n_sub - 1:
                for c in range(CHAIN):
                    tail_ref[c, :, lanes] = cur[c]
            conv = []
            for c in range(CHAIN):
                acc = cb
                for kk in range(CONV_WIDTH):
                    d = CONV_WIDTH - 1 - kk
                    src = cur[c - d] if c >= d else shifted[c - d + CHAIN]
                    acc = acc + src * cw[kk:kk + 1, :]
                conv.append(acc)
            u_sub = jnp.concatenate(conv, axis=0)
            u_scr[l, s * SUB_ROWS:(s + 1) * SUB_ROWS, :] = u_sub
            ub_scr[s * SUB_ROWS:(s + 1) * SUB_ROWS, lanes] = _bf16(u_sub)

    ug_half(1)
    for l in range(n_slab):
        z = jnp.dot(ub_scr[:, l * LANES:(l + 1) * LANES], wg_ref[l],
                    preferred_element_type=f32) + bg_ref[l]
        z_scr[l] = z[:, :LANES]
        z_scr[n_slab + l] = z[:, LANES:]

    def k_piece(part):
        cols = slice(part * ATTN_BLOCK, (part + 1) * ATTN_BLOCK)
        t = jnp.dot(h_scr[...], w_ref[:, d_attn + part * ATTN_BLOCK:
                                     d_attn + (part + 1) * ATTN_BLOCK], preferred_element_type=f32)
        k_ref[:, cols] = _bf16(t)
        return [t[tm - SUBLANES:, c * LANES:(c + 1) * LANES] for c in range(ATTN_BLOCK // LANES)]

    def transposed_piece(first_col, o_ref, part):
        rows = slice(part * ATTN_BLOCK, (part + 1) * ATTN_BLOCK)
        t = jnp.dot(h_scr[rows, :], w_ref[:, first_col:first_col + d_attn],
                    preferred_element_type=f32)
        o_ref[part] = _bf16(t.T)
        return [t[ATTN_BLOCK - SUBLANES:, c * LANES:(c + 1) * LANES] for c in range(d_attn // LANES)]

    side_work = {0: lambda: k_piece(0), 2: lambda: k_piece(1),
                 5: lambda: transposed_piece(0, qt_ref, 0),
                 8: lambda: transposed_piece(0, qt_ref, 1),
                 11: lambda: transposed_piece(2 * d_attn, vt_ref, 0),
                 13: lambda: transposed_piece(2 * d_attn, vt_ref, 1)}
    assert n_sub > max(side_work) and qt_ref.shape[0] == 2 and d_attn == 2 * ATTN_BLOCK

    c_gelu = math.sqrt(2.0 / math.pi)
    for l in range(n_slab):
        for s in range(n_sub):
            gate = jnp.concatenate([piece(n_slab + l, s, c) for c in range(CHAIN)], axis=0)
            gelu_scr[l, s * SUB_ROWS:(s + 1) * SUB_ROWS, :] = (0.5 * gate) * (1.0 + jnp.tanh(
                gate * (c_gelu + (c_gelu * 0.044715) * (gate * gate))))

    c2_row = (0.5 * LRU_C * math.log2(math.e)) * _log_sigmoid(lru_l_ref[...])

    carry = [carry_ref[:, l * LANES:(l + 1) * LANES] for l in range(n_slab)]
    pending_at = -1
    keep_state = zero_ref[...] == 0
    for s in range(n_sub):
        if s in side_work:
            anchor = functools.reduce(jnp.add, side_work[s]())
            pending_at = s + 1
        base = s * SUB_ROWS
        rows = slice(base, base + SUB_ROWS)
        for l in range(n_slab):
            c2 = c2_row[:, l * LANES:(l + 1) * LANES]
            u = u_scr[l, rows, :]
            t_r = jnp.tanh(z_scr[l, rows, :])
            t_i = jnp.tanh(z_scr[n_slab + l, rows, :])
            log2_a = c2 + c2 * t_r
            a = jnp.exp2(log2_a)
            quarter = (jnp.tanh(log2_a * math.log(2.0)) * -0.25) * (a * a + 1.0)
            half_mult = jnp.where(quarter > 0.0, quarter * lax.rsqrt(quarter), 0.0)
            b = (half_mult * u) * (1.0 + t_i)
            gelu = gelu_scr[l, rows, :]

            a_c = [a[c * SUBLANES:(c + 1) * SUBLANES] for c in range(CHAIN)]
            b_c = [b[c * SUBLANES:(c + 1) * SUBLANES] for c in range(CHAIN)]
            h_loc, p_loc = [b_c[0]], [a_c[0]]
            for c in range(1, CHAIN):
                h_loc.append(a_c[c] * h_loc[c - 1] + b_c[c])
                p_loc.append(a_c[c] * p_loc[c - 1])
            p_cum, h_cum = p_loc[CHAIN - 1], h_loc[CHAIN - 1]
            shift = 1
            while shift < SUBLANES:
                valid = row >= shift
                p_prev = jnp.where(valid, pltpu.roll(p_cum, shift, axis=0), 1.0)
                h_prev = jnp.where(valid, pltpu.roll(h_cum, shift, axis=0), 0.0)
                h_cum = p_cum * h_prev + h_cum
                p_cum = p_cum * p_prev
                shift *= 2
            chain_end = h_cum + p_cum * carry[l]
            chain_in = jnp.where(row == 0, carry[l], pltpu.roll(chain_end, 1, axis=0))
            carry[l] = jnp.broadcast_to(chain_end[SUBLANES - 1:SUBLANES, :], (SUBLANES, LANES))
            if s == pending_at:
                carry[l] = jnp.where(keep_state, carry[l], anchor)
            for c in range(CHAIN):
                out = (h_loc[c] + p_loc[c] * chain_in) * gelu[c * SUBLANES:(c + 1) * SUBLANES]
                out_slab[l, pl.ds(base + c, SUBLANES, stride=CHAIN), :] = out
    for l in range(n_slab):
        carry_ref[:, l * LANES:(l + 1) * LANES] = carry[l]
    lru_ref[...] = _bf16(jnp.concatenate([out_slab[l] for l in range(n_slab)], axis=1))


def _in_proj_lru(x, g, w, conv_w, conv_b, w_gates, b_gates, lru_l):
    B, S, D = x.shape
    d_lru = lru_l.shape[-1]
    d_attn = (w.shape[1] - 2 * d_lru) // 3
    tm = PROJ_TOKENS
    nblk = tm // ATTN_BLOCK
    n_slab = d_lru // LANES
    const = lambda b, i: (0, 0)
    assert D % W_IN_CHUNK_ROWS == 0
    return pl.pallas_call(
        _in_proj_lru_kernel,
        grid=(B, S // tm),
        in_specs=[
            pl.BlockSpec((None, tm, D), lambda b, i: (b, i, 0)),
            pl.BlockSpec((1, D), const),
            pl.BlockSpec(memory_space=pl.ANY),
            pl.BlockSpec(conv_w.shape, const),
            pl.BlockSpec(conv_b.shape, const),
            pl.BlockSpec(w_gates.shape, lambda b, i: (0, 0, 0)),
            pl.BlockSpec(b_gates.shape, lambda b, i: (0, 0, 0)),
            pl.BlockSpec(lru_l.shape, const),
            pl.BlockSpec((SUBLANES, LANES), const),
        ],
        out_specs=[
            pl.BlockSpec((None, nblk, d_attn, ATTN_BLOCK), lambda b, i: (b, i, 0, 0)),
            pl.BlockSpec((None, tm, d_attn), lambda b, i: (b, i, 0)),
            pl.BlockSpec((None, nblk, d_attn, ATTN_BLOCK), lambda b, i: (b, i, 0, 0)),
            pl.BlockSpec((None, tm, d_lru), lambda b, i: (b, i, 0)),
        ],
        out_shape=[
            jax.ShapeDtypeStruct((B, S // ATTN_BLOCK, d_attn, ATTN_BLOCK), jnp.bfloat16),
            jax.ShapeDtypeStruct((B, S, d_attn), jnp.bfloat16),
            jax.ShapeDtypeStruct((B, S // ATTN_BLOCK, d_attn, ATTN_BLOCK), jnp.bfloat16),
            jax.ShapeDtypeStruct((B, S, d_lru), jnp.bfloat16),
        ],
        scratch_shapes=[
            pltpu.VMEM(w.shape, jnp.bfloat16),
            pltpu.VMEM((2, W_IN_CHUNK_ROWS, w.shape[1]), jnp.float32),
            pltpu.SemaphoreType.DMA((2,)),
            pltpu.VMEM((tm, D), jnp.bfloat16),
            pltpu.VMEM((2 * n_slab, tm, LANES), jnp.float32),
            pltpu.VMEM((n_slab, tm, LANES), jnp.float32),
            pltpu.VMEM((tm, d_lru), jnp.bfloat16),
            pltpu.VMEM((2 * n_slab, tm, LANES), jnp.float32),
            pltpu.VMEM((n_slab, tm, LANES), jnp.float32),
            pltpu.VMEM((n_slab, tm, LANES), jnp.float32),
            pltpu.VMEM((CHAIN, SUBLANES, d_lru), jnp.float32),
            pltpu.VMEM((SUBLANES, d_lru), jnp.float32),
        ],
        compiler_params=pltpu.CompilerParams(
            dimension_semantics=("arbitrary", "arbitrary"),
            vmem_limit_bytes=V7X_VMEM_LIMIT_BYTES),
        name="in_proj_lru",
    )(x, g, w, conv_w, conv_b, w_gates, b_gates, lru_l, jnp.zeros((SUBLANES, LANES), jnp.int32))


N_POS_COLS = 128
N_ONES_ROWS = 16
SCORE_LEAD = 2
N_SCORE_BUFS = SCORE_LEAD + 1
ATTN_Q_PER_STEP = 9


def _diff_attn_kernel(qt_ref, k_ref, pos_ref, vt_ref, lq1_ref, lk1_ref, lq2_ref, lk2_ref, sg_ref,
                      o_ref,
                      s_ref, m_ref, acc_ref, *, slopes, q_per_step):
    h = pl.program_id(1)
    blk = ATTN_BLOCK
    n_blocks = vt_ref.shape[0]

    slope = jnp.float32(slopes[0])
    for i in range(1, len(slopes)):
        slope = jnp.where(h == i, jnp.float32(slopes[i]), slope)

    r = lax.broadcasted_iota(jnp.int32, (N_POS_COLS, blk), 0)
    lane = lax.broadcasted_iota(jnp.int32, (N_POS_COLS, blk), 1).astype(jnp.float32)
    zeros = jnp.zeros((HEAD_DIM, blk), qt_ref.dtype)
    key_in_blk = lax.broadcasted_iota(jnp.int32, (blk, 2 * blk), 0)
    qry_in_blk = lax.broadcasted_iota(jnp.int32, (blk, 2 * blk), 1) % blk
    causal = key_in_blk <= qry_in_blk
    ones_row = lax.broadcasted_iota(jnp.int32, (N_ONES_ROWS, blk), 0) == 0
    ones_blk = jnp.where(ones_row, 1.0, 0.0).astype(jnp.bfloat16)

    def q_operand(qt, q_index):
        alibi = _bf16(jnp.where(r == 0, slope,
                      jnp.where(r == 1, slope * blk,
                      jnp.where(r == 2, -slope * lane,
                      jnp.where(r == 3, -slope * float(q_index * blk), 0.0)))))
        return jnp.concatenate(
            [jnp.concatenate([qt[:HEAD_DIM], zeros, alibi], axis=0),
             jnp.concatenate([zeros, qt[HEAD_DIM:], alibi], axis=0)], axis=1)

    def score_block(q_aug, j, diagonal):
        rows = slice(j * blk, (j + 1) * blk)
        k_aug = jnp.concatenate([k_ref[rows, :], pos_ref[rows, :]], axis=1)
        s = jnp.dot(k_aug, q_aug, preferred_element_type=jnp.float32)
        return jnp.where(causal, s, -jnp.inf) if diagonal else s

    def finish(t):
        acc = acc_ref[...]
        lam = (jnp.exp(jnp.sum(lq1_ref[...] * lk1_ref[...], axis=-1, keepdims=True))
               - jnp.exp(jnp.sum(lq2_ref[...] * lk2_ref[...], axis=-1, keepdims=True))
               + LAMBDA_INIT)
        heads = acc[:V_HEAD_DIM] / acc[V_HEAD_DIM:V_HEAD_DIM + 1]
        o = heads[:, :blk] - lam * heads[:, blk:]
        o = o * lax.rsqrt(jnp.mean(o * o, axis=0, keepdims=True) + EPS)
        o_ref[t * blk:(t + 1) * blk, :] = _bf16(o.T * (sg_ref[...] * (1.0 - LAMBDA_INIT)))

    def score_q_block(q_aug, q_index, j, m_run):
        s = score_block(q_aug, j, j == q_index)
        s_ref[q_index % N_SCORE_BUFS, j * blk:(j + 1) * blk, :] = s
        col_max = jnp.max(s, axis=0, keepdims=True)
        return col_max if m_run is None else jnp.maximum(m_run, col_max)

    def sub_step(t):
        if t >= 1:
            finish(t - 1)
        if t == n_blocks:
            return
        if t == 0:
            for q_index in range(min(SCORE_LEAD, n_blocks)):
                q_aug = q_operand(qt_ref[q_index], q_index)
                m_run = None
                for j in range(q_index + 1):
                    m_run = score_q_block(q_aug, q_index, j, m_run)
                m_ref[q_index % N_SCORE_BUFS] = m_run
        cur = t % N_SCORE_BUFS
        ahead = t + SCORE_LEAD
        scoring = ahead < n_blocks
        q_aug = q_operand(qt_ref[ahead], ahead) if scoring else None
        m_cur = m_ref[cur]
        m_run = None
        acc = jnp.zeros((V_HEAD_DIM + N_ONES_ROWS, 2 * blk), jnp.float32)
        for j in range(ahead + 1 if scoring else t + 1):
            if scoring:
                m_run = score_q_block(q_aug, ahead, j, m_run)
            if j <= t:
                p = _bf16(jnp.exp(s_ref[cur, j * blk:(j + 1) * blk, :] - m_cur))
                vt_aug = jnp.concatenate([vt_ref[j], ones_blk], axis=0)
                acc = acc + jnp.dot(vt_aug, p, preferred_element_type=jnp.float32)
        if scoring:
            m_ref[ahead % N_SCORE_BUFS] = m_run
        acc_ref[...] = acc

    def grid_step(first):
        def run():
            for t in range(first, min(first + q_per_step, n_blocks + 1)):
                sub_step(t)
        return run

    steps = [grid_step(first) for first in range(0, n_blocks + 1, q_per_step)]
    if len(steps) == 1:
        steps[0]()
    else:
        lax.switch(pl.program_id(2), steps)


def _alibi_slopes(n_heads):
    return [2.0 ** (-8.0 * (i + 1) / n_heads) for i in range(n_heads)]


def _key_position_columns(seq_len):
    j = np.arange(seq_len)
    pos = np.zeros((seq_len, N_POS_COLS), np.float32)
    pos[:, 0] = j % ATTN_BLOCK
    pos[:, 1] = j // ATTN_BLOCK
    pos[:, 2:4] = 1.0
    assert ATTN_BLOCK <= 256 and seq_len // ATTN_BLOCK <= 256
    return jnp.asarray(pos, jnp.bfloat16)


def _diff_attn(qt, k, vt, lambdas, subln_g):
    B, nblk, d_attn, blk = qt.shape
    S = k.shape[1]
    H = d_attn // V_HEAD_DIM
    pos = _key_position_columns(S)
    kernel = functools.partial(_diff_attn_kernel, slopes=tuple(_alibi_slopes(H)),
                               q_per_step=ATTN_Q_PER_STEP)
    n_steps = pl.cdiv(nblk + 1, ATTN_Q_PER_STEP)
    return pl.pallas_call(
        kernel,
        grid=(B, H, n_steps),
        in_specs=[
            pl.BlockSpec((None, nblk, V_HEAD_DIM, blk), lambda b, h, q: (b, 0, h, 0)),
            pl.BlockSpec((None, S, V_HEAD_DIM), lambda b, h, q: (b, 0, h)),
            pl.BlockSpec(pos.shape, lambda b, h, q: (0, 0)),
            pl.BlockSpec((None, nblk, V_HEAD_DIM, blk), lambda b, h, q: (b, 0, h, 0)),
            *[pl.BlockSpec(v.shape, lambda b, h, q: (0, 0)) for v in lambdas],
            pl.BlockSpec(subln_g.shape, lambda b, h, q: (0, 0)),
        ],
        out_specs=pl.BlockSpec((None, S, V_HEAD_DIM), lambda b, h, q: (b, 0, h)),
        out_shape=jax.ShapeDtypeStruct((B, S, d_attn), jnp.bfloat16),
        scratch_shapes=[pltpu.VMEM((N_SCORE_BUFS, S, 2 * blk), jnp.float32),
                        pltpu.VMEM((N_SCORE_BUFS, 1, 2 * blk), jnp.float32),
                        pltpu.VMEM((V_HEAD_DIM + N_ONES_ROWS, 2 * blk), jnp.float32)],
        compiler_params=pltpu.CompilerParams(
            dimension_semantics=("parallel", "parallel", "arbitrary"),
            vmem_limit_bytes=V7X_VMEM_LIMIT_BYTES),
        name="diff_attn",
    )(qt, k, pos, vt, *lambdas, subln_g)


def _out_mlp_kernel(x_ref, attn_ref, lru_ref, wo_hbm, gm_ref, wup_hbm, wdn_hbm, gf_ref, y_ref,
                    wo_ref, wup_ref, wdn_ref, stage_ref, sem):
    tm = x_ref.shape[0]
    d_ff = wup_ref.shape[1]
    n_chunks = d_ff // FF_CHUNK
    rows = [slice(g * tm // MLP_SPLIT, (g + 1) * tm // MLP_SPLIT) for g in range(MLP_SPLIT)]

    def ff(c):
        return pl.ds(c * FF_CHUNK, FF_CHUNK)

    sources = [wo_hbm]
    targets = [lambda v: wo_ref.__setitem__((slice(None), slice(None)), v)]
    for c in range(n_chunks):
        sources += [wup_hbm.at[:, ff(c)], wdn_hbm.at[ff(c), :]]
        targets += [lambda v, c=c: wup_ref.__setitem__((slice(None), ff(c)), v),
                    lambda v, c=c: wdn_ref.__setitem__((ff(c), slice(None)), v)]

    def copy(i):
        return pltpu.make_async_copy(sources[i], stage_ref.at[i % 2], sem.at[i % 2])

    def land(i):
        copy(i).wait()
        targets[i](_bf16(stage_ref[i % 2]))
        if i + 2 < len(sources):
            copy(i + 2).start()

    def step_body(first_step):
        if first_step:
            copy(0).start()
            copy(1).start()
            land(0)
        x1 = []
        for r in rows:
            mix = jnp.concatenate([attn_ref[r, :], lru_ref[r, :]], axis=-1)
            x1.append(x_ref[r, :] + jnp.dot(mix, wo_ref[...], preferred_element_type=jnp.float32))
        hm = [_bf16(_rmsnorm_rows(v, gm_ref[...])) for v in x1]
        mlp = [jnp.zeros_like(v) for v in x1]
        for c in range(n_chunks):
            cols = slice(c * FF_CHUNK, (c + 1) * FF_CHUNK)
            if first_step:
                land(1 + 2 * c)
            act = [_bf16(jnp.square(jnp.maximum(
                jnp.dot(hm[g], wup_ref[:, cols], preferred_element_type=jnp.float32), 0.0)))
                for g in range(MLP_SPLIT)]
            if first_step:
                land(2 + 2 * c)
            for g in range(MLP_SPLIT):
                mlp[g] = mlp[g] + jnp.dot(act[g], wdn_ref[cols, :],
                                          preferred_element_type=jnp.float32)
        for g, r in enumerate(rows):
            y_ref[r, :] = _rmsnorm_rows(x1[g] + mlp[g], gf_ref[...])

    pl.when(pl.program_id(0) == 0)(lambda: step_body(True))
    pl.when(pl.program_id(0) != 0)(lambda: step_body(False))


def _out_mlp(x, attn, lru, w_out, g_mlp, w_up, w_down, g_final):
    B, S, D = x.shape
    T = B * S
    tm = MLP_TOKENS
    x2 = x.reshape(T, D)
    attn2 = attn.reshape(T, attn.shape[-1])
    lru2 = lru.reshape(T, lru.shape[-1])
    const = lambda i: (0, 0)
    in_hbm = pl.BlockSpec(memory_space=pl.ANY)
    assert w_out.shape == (D, D) and FF_CHUNK == D and w_up.shape[1] % FF_CHUNK == 0
    y = pl.pallas_call(
        _out_mlp_kernel,
        grid=(T // tm,),
        in_specs=[
            pl.BlockSpec((tm, D), lambda i: (i, 0)),
            pl.BlockSpec((tm, attn2.shape[1]), lambda i: (i, 0)),
            pl.BlockSpec((tm, lru2.shape[1]), lambda i: (i, 0)),
            in_hbm,
            pl.BlockSpec(g_mlp.shape, const),
            in_hbm,
            in_hbm,
            pl.BlockSpec(g_final.shape, const),
        ],
        out_specs=pl.BlockSpec((tm, D), lambda i: (i, 0)),
        out_shape=jax.ShapeDtypeStruct((T, D), jnp.float32),
        scratch_shapes=[
            pltpu.VMEM(w_out.shape, jnp.bfloat16),
            pltpu.VMEM(w_up.shape, jnp.bfloat16),
            pltpu.VMEM(w_down.shape, jnp.bfloat16),
            pltpu.VMEM((2, D, FF_CHUNK), jnp.float32),
            pltpu.SemaphoreType.DMA((2,)),
        ],
        compiler_params=pltpu.CompilerParams(
            dimension_semantics=("arbitrary",),
            vmem_limit_bytes=V7X_VMEM_LIMIT_BYTES),
        name="out_mlp",
    )(x2, attn2, lru2, w_out, g_mlp, w_up, w_down, g_final)
    return y.reshape(B, S, D)


def _pair_block_diag(w):
    n, c, d = w.shape
    rows = w.reshape(n // 2, 2 * c, d)
    tiled = jnp.concatenate([rows, rows], axis=-1)
    row_block = np.arange(2 * c)[:, None] // c
    col_block = np.arange(2 * d)[None, :] // d
    return jnp.where(jnp.asarray(row_block == col_block), tiled, 0.0)


def kernel(x, norm_mix_g, w_in, conv_w, conv_b, w_rg, b_rg, w_ig, b_ig, lru_L, lambda_q1, lambda_k1, lambda_q2, lambda_k2, subln_g, w_out, norm_mlp_g, w_up, w_down, final_g):
    B, S, D = x.shape
    d_lru = lru_L.shape[-1]
    d_attn = (w_in.shape[-1] - 2 * d_lru) // 3
    depth = w_in.shape[0]
    n_slab = d_lru // LANES
    assert depth == 1, "LAMBDA_INIT is specialised to a single layer"
    assert S % PROJ_TOKENS == 0 and (B * S) % MLP_TOKENS == 0
    assert PROJ_TOKENS % SUB_ROWS == 0 and d_lru % LANES == 0
    assert 2 * w_rg.shape[-1] == LANES and w_rg.shape[1] * w_rg.shape[2] == d_lru
    l = 0
    row = lambda v: v.reshape(1, -1)

    w_gates = _bf16(0.5 * jnp.concatenate([_pair_block_diag(w_rg[l]), _pair_block_diag(w_ig[l])],
                                          axis=-1))
    b_gates = 0.5 * jnp.concatenate([b_rg[l].reshape(n_slab, 1, LANES),
                                     b_ig[l].reshape(n_slab, 1, LANES)], axis=-1)
    lambdas = [row(v[l]) for v in (lambda_q1, lambda_k1, lambda_q2, lambda_k2)]

    qt, k, vt, lru = _in_proj_lru(x, row(norm_mix_g[l]), w_in[l], conv_w[l], row(conv_b[l]),
                                  w_gates, b_gates, row(lru_L[l]))
    attn = _diff_attn(qt, k, vt, lambdas, row(subln_g[l]))
    return _out_mlp(x, attn, lru, w_out[l], row(norm_mlp_g[l]), w_up[l], w_down[l], row(final_g))
```

```python
import functools
import math

import numpy as np
import jax
import jax.numpy as jnp
from jax import lax
from jax.experimental import pallas as pl
from jax.experimental.pallas import tpu as pltpu

EPS = 1e-6
CONV_WIDTH = 4
LRU_C = 8.0
HEAD_DIM = 64
V_HEAD_DIM = 2 * HEAD_DIM
LAMBDA_INIT = 0.8 - 0.6 * math.exp(-0.3 * 0)

V7X_MXU_COLS = 256
LANES = 128
SUBLANES = 8
V7X_VMEM_LIMIT_BYTES = 56 * 1024 * 1024

ATTN_BLOCK = V7X_MXU_COLS
PROJ_TOKENS = 512
MLP_TOKENS = 512
FF_CHUNK = 1024
MLP_SPLIT = 2
W_IN_CHUNK_ROWS = 128
CHAIN = 4
SUB_ROWS = SUBLANES * CHAIN


def _bf16(x):
    return x.astype(jnp.bfloat16)


def _rmsnorm_rows(x, g):
    return x * lax.rsqrt(jnp.mean(x * x, axis=-1, keepdims=True) + EPS) * g


def _stream_cast_rows(src_hbm, dst_ref, stage_ref, sem, chunk_rows, scale=None):
    n_chunks = src_hbm.shape[0] // chunk_rows

    def copy(i):
        return pltpu.make_async_copy(src_hbm.at[pl.ds(i * chunk_rows, chunk_rows), :],
                                     stage_ref.at[i % 2], sem.at[i % 2])

    copy(0).start()
    for i in range(n_chunks):
        if i + 1 < n_chunks:
            copy(i + 1).start()
        copy(i).wait()
        chunk = stage_ref[i % 2]
        if scale is not None:
            chunk = chunk * scale
        dst_ref[i * chunk_rows:(i + 1) * chunk_rows, :] = _bf16(chunk)


def _log_sigmoid(x):
    return jnp.minimum(x, 0.0) - jnp.log1p(jnp.exp(-jnp.abs(x)))


def _in_proj_lru_kernel(x_ref, g_ref, w_hbm, cw_ref, cb_ref, wg_ref, bg_ref, lru_l_ref, zero_ref,
                        qt_ref, k_ref, vt_ref, lru_ref,
                        w_ref, w_stage, w_sem, h_scr, ug_slab, u_scr, ub_scr, z_scr, gelu_scr,
                        out_slab, tail_ref, carry_ref):
    tm = x_ref.shape[0]
    d_lru = lru_ref.shape[-1]
    n_slab = d_lru // LANES
    n_sub = tm // SUB_ROWS
    d_attn = k_ref.shape[-1]
    f32 = jnp.float32

    @pl.when((pl.program_id(0) == 0) & (pl.program_id(1) == 0))
    def _():
        col = lax.broadcasted_iota(jnp.int32, (1, w_ref.shape[1]), 1)
        q_scale = jnp.where(col < d_attn, HEAD_DIM ** -0.5, 1.0)
        _stream_cast_rows(w_hbm, w_ref, w_stage, w_sem, w_stage.shape[1], scale=q_scale)

    @pl.when(pl.program_id(1) == 0)
    def _():
        tail_ref[...] = jnp.zeros(tail_ref.shape, f32)
        carry_ref[...] = jnp.zeros(carry_ref.shape, f32)

    h_scr[...] = _bf16(_rmsnorm_rows(x_ref[...], g_ref[...]))

    def ug_half(half):
        cols = slice(3 * d_attn + half * d_lru, 3 * d_attn + (half + 1) * d_lru)
        nat = jnp.dot(h_scr[...], w_ref[:, cols], preferred_element_type=f32)
        for l in range(n_slab):
            ug_slab[half * n_slab + l] = nat[:, l * LANES:(l + 1) * LANES]

    ug_half(0)

    def piece(slab, s, c):
        return ug_slab[slab, pl.ds(s * SUB_ROWS + c, SUBLANES, stride=CHAIN), :]

    row = lax.broadcasted_iota(jnp.int32, (SUBLANES, LANES), 0)

    for l in range(n_slab):
        lanes = slice(l * LANES, (l + 1) * LANES)
        cw = cw_ref[:, lanes]
        cb = cb_ref[:, lanes]
        rolled_prev = [pltpu.roll(tail_ref[c, :, lanes], 1, axis=0) for c in range(CHAIN)]
        for s in range(n_sub):
            cur = [piece(l, s, c) for c in range(CHAIN)]
            rolled = [pltpu.roll(cur[c], 1, axis=0) for c in range(CHAIN)]
            shifted = [jnp.where(row == 0, rolled_prev[c], rolled[c]) for c in range(CHAIN)]
            rolled_prev = rolled
            if s == n_sub - 1:
                for c in range(CHAIN):
                    tail_ref[c, :, lanes] = cur[c]
            conv = []
            for c in range(CHAIN):
                acc = cb
                for kk in range(CONV_WIDTH):
                    d = CONV_WIDTH - 1 - kk
                    src = cur[c - d] if c >= d else shifted[c - d + CHAIN]
                    acc = acc + src * cw[kk:kk + 1, :]
                conv.append(acc)
            u_sub = jnp.concatenate(conv, axis=0)
            u_scr[l, s * SUB_ROWS:(s + 1) * SUB_ROWS, :] = u_sub
            ub_scr[s * SUB_ROWS:(s + 1) * SUB_ROWS, lanes] = _bf16(u_sub)

    ug_half(1)
    for l in range(n_slab):
        z = jnp.dot(ub_scr[:, l * LANES:(l + 1) * LANES], wg_ref[l],
                    preferred_element_type=f32) + bg_ref[l]
        z_scr[l] = z[:, :LANES]
        z_scr[n_slab + l] = z[:, LANES:]

    def k_piece(part):
        cols = slice(part * ATTN_BLOCK, (part + 1) * ATTN_BLOCK)
        t = jnp.dot(h_scr[...], w_ref[:, d_attn + part * ATTN_BLOCK:
                                     d_attn + (part + 1) * ATTN_BLOCK], preferred_element_type=f32)
        k_ref[:, cols] = _bf16(t)
        return [t[tm - SUBLANES:, c * LANES:(c + 1) * LANES] for c in range(ATTN_BLOCK // LANES)]

    def transposed_piece(first_col, o_ref, part):
        rows = slice(part * ATTN_BLOCK, (part + 1) * ATTN_BLOCK)
        t = jnp.dot(h_scr[rows, :], w_ref[:, first_col:first_col + d_attn],
                    preferred_element_type=f32)
        o_ref[part] = _bf16(t.T)
        return [t[ATTN_BLOCK - SUBLANES:, c * LANES:(c + 1) * LANES] for c in range(d_attn // LANES)]

    side_work = {0: lambda: k_piece(0), 2: lambda: k_piece(1),
                 5: lambda: transposed_piece(0, qt_ref, 0),
                 8: lambda: transposed_piece(0, qt_ref, 1),
                 11: lambda: transposed_piece(2 * d_attn, vt_ref, 0),
                 13: lambda: transposed_piece(2 * d_attn, vt_ref, 1)}
    assert n_sub > max(side_work) and qt_ref.shape[0] == 2 and d_attn == 2 * ATTN_BLOCK

    c_gelu = math.sqrt(2.0 / math.pi)
    for l in range(n_slab):
        for s in range(n_sub):
            gate = jnp.concatenate([piece(n_slab + l, s, c) for c in range(CHAIN)], axis=0)
            gelu_scr[l, s * SUB_ROWS:(s + 1) * SUB_ROWS, :] = (0.5 * gate) * (1.0 + jnp.tanh(
                gate * (c_gelu + (c_gelu * 0.044715) * (gate * gate))))

    c2_row = (0.5 * LRU_C * math.log2(math.e)) * _log_sigmoid(lru_l_ref[...])

    carry = [carry_ref[:, l * LANES:(l + 1) * LANES] for l in range(n_slab)]
    pending_at = -1
    keep_state = zero_ref[...] == 0
    for s in range(n_sub):
        if s in side_work:
            anchor = functools.reduce(jnp.add, side_work[s]())
            pending_at = s + 1
        base = s * SUB_ROWS
        rows = slice(base, base + SUB_ROWS)
        for l in range(n_slab):
            c2 = c2_row[:, l * LANES:(l + 1) * LANES]
            u = u_scr[l, rows, :]
            t_r = jnp.tanh(z_scr[l, rows, :])
            t_i = jnp.tanh(z_scr[n_slab + l, rows, :])
            log2_a = c2 + c2 * t_r
            a = jnp.exp2(log2_a)
            quarter = (jnp.tanh(log2_a * math.log(2.0)) * -0.25) * (a * a + 1.0)
            half_mult = jnp.where(quarter > 0.0, quarter * lax.rsqrt(quarter), 0.0)
            b = (half_mult * u) * (1.0 + t_i)
            gelu = gelu_scr[l, rows, :]

            a_c = [a[c * SUBLANES:(c + 1) * SUBLANES] for c in range(CHAIN)]
            b_c = [b[c * SUBLANES:(c + 1) * SUBLANES] for c in range(CHAIN)]
            h_loc, p_loc = [b_c[0]], [a_c[0]]
            for c in range(1, CHAIN):
                h_loc.append(a_c[c] * h_loc[c - 1] + b_c[c])
                p_loc.append(a_c[c] * p_loc[c - 1])
            p_cum, h_cum = p_loc[CHAIN - 1], h_loc[CHAIN - 1]
            shift = 1
            while shift < SUBLANES:
                valid = row >= shift
                p_prev = jnp.where(valid, pltpu.roll(p_cum, shift, axis=0), 1.0)
                h_prev = jnp.where(valid, pltpu.roll(h_cum, shift, axis=0), 0.0)
                h_cum = p_cum * h_prev + h_cum
                p_cum = p_cum * p_prev
                shift *= 2
            chain_end = h_cum + p_cum * carry[l]
            chain_in = jnp.where(row == 0, carry[l], pltpu.roll(chain_end, 1, axis=0))
            carry[l] = jnp.broadcast_to(chain_end[SUBLANES - 1:SUBLANES, :], (SUBLANES, LANES))
            if s == pending_at:
                carry[l] = jnp.where(keep_state, carry[l], anchor)
            for c in range(CHAIN):
                out = (h_loc[c] + p_loc[c] * chain_in) * gelu[c * SUBLANES:(c + 1) * SUBLANES]
                out_slab[l, pl.ds(base + c, SUBLANES, stride=CHAIN), :] = out
    for l in range(n_slab):
        carry_ref[:, l * LANES:(l + 1) * LANES] = carry[l]
    lru_ref[...] = _bf16(jnp.concatenate([out_slab[l] for l in range(n_slab)], axis=1))


def _in_proj_lru(x, g, w, conv_w, conv_b, w_gates, b_gates, lru_l):
    B, S, D = x.shape
    d_lru = lru_l.shape[-1]
    d_attn = (w.shape[1] - 2 * d_lru) // 3
    tm = PROJ_TOKENS
    nblk = tm // ATTN_BLOCK
    n_slab = d_lru // LANES
    const = lambda b, i: (0, 0)
    assert D % W_IN_CHUNK_ROWS == 0
    return pl.pallas_call(
        _in_proj_lru_kernel,
        grid=(B, S // tm),
        in_specs=[
            pl.BlockSpec((None, tm, D), lambda b, i: (b, i, 0)),
            pl.BlockSpec((1, D), const),
            pl.BlockSpec(memory_space=pl.ANY),
            pl.BlockSpec(conv_w.shape, const),
            pl.BlockSpec(conv_b.shape, const),
            pl.BlockSpec(w_gates.shape, lambda b, i: (0, 0, 0)),
            pl.BlockSpec(b_gates.shape, lambda b, i: (0, 0, 0)),
            pl.BlockSpec(lru_l.shape, const),
            pl.BlockSpec((SUBLANES, LANES), const),
        ],
        out_specs=[
            pl.BlockSpec((None, nblk, d_attn, ATTN_BLOCK), lambda b, i: (b, i, 0, 0)),
            pl.BlockSpec((None, tm, d_attn), lambda b, i: (b, i, 0)),
            pl.BlockSpec((None, nblk, d_attn, ATTN_BLOCK), lambda b, i: (b, i, 0, 0)),
            pl.BlockSpec((None, tm, d_lru), lambda b, i: (b, i, 0)),
        ],
        out_shape=[
            jax.ShapeDtypeStruct((B, S // ATTN_BLOCK, d_attn, ATTN_BLOCK), jnp.bfloat16),
            jax.ShapeDtypeStruct((B, S, d_attn), jnp.bfloat16),
            jax.ShapeDtypeStruct((B, S // ATTN_BLOCK, d_attn, ATTN_BLOCK), jnp.bfloat16),
            jax.ShapeDtypeStruct((B, S, d_lru), jnp.bfloat16),
        ],
        scratch_shapes=[
            pltpu.VMEM(w.shape, jnp.bfloat16),
            pltpu.VMEM((2, W_IN_CHUNK_ROWS, w.shape[1]), jnp.float32),
            pltpu.SemaphoreType.DMA((2,)),
            pltpu.VMEM((tm, D), jnp.bfloat16),
            pltpu.VMEM((2 * n_slab, tm, LANES), jnp.float32),
            pltpu.VMEM((n_slab, tm, LANES), jnp.float32),
            pltpu.VMEM((tm, d_lru), jnp.bfloat16),
            pltpu.VMEM((2 * n_slab, tm, LANES), jnp.float32),
            pltpu.VMEM((n_slab, tm, LANES), jnp.float32),
            pltpu.VMEM((n_slab, tm, LANES), jnp.float32),
            pltpu.VMEM((CHAIN, SUBLANES, d_lru), jnp.float32),
            pltpu.VMEM((SUBLANES, d_lru), jnp.float32),
        ],
        compiler_params=pltpu.CompilerParams(
            dimension_semantics=("arbitrary", "arbitrary"),
            vmem_limit_bytes=V7X_VMEM_LIMIT_BYTES),
        name="in_proj_lru",
    )(x, g, w, conv_w, conv_b, w_gates, b_gates, lru_l, jnp.zeros((SUBLANES, LANES), jnp.int32))


N_POS_COLS = 128
N_ONES_ROWS = 16
SCORE_LEAD = 2
N_SCORE_BUFS = SCORE_LEAD + 1
ATTN_Q_PER_STEP = 9


def _diff_attn_kernel(qt_ref, k_ref, pos_ref, vt_ref, lq1_ref, lk1_ref, lq2_ref, lk2_ref, sg_ref,
                      o_ref,
                      s_ref, m_ref, acc_ref, *, slopes, q_per_step):
    h = pl.program_id(1)
    blk = ATTN_BLOCK
    n_blocks = vt_ref.shape[0]

    slope = jnp.float32(slopes[0])
    for i in range(1, len(slopes)):
        slope = jnp.where(h == i, jnp.float32(slopes[i]), slope)

    r = lax.broadcasted_iota(jnp.int32, (N_POS_COLS, blk), 0)
    lane = lax.broadcasted_iota(jnp.int32, (N_POS_COLS, blk), 1).astype(jnp.float32)
    zeros = jnp.zeros((HEAD_DIM, blk), qt_ref.dtype)
    key_in_blk = lax.broadcasted_iota(jnp.int32, (blk, 2 * blk), 0)
    qry_in_blk = lax.broadcasted_iota(jnp.int32, (blk, 2 * blk), 1) % blk
    causal = key_in_blk <= qry_in_blk
    ones_row = lax.broadcasted_iota(jnp.int32, (N_ONES_ROWS, blk), 0) == 0
    ones_blk = jnp.where(ones_row, 1.0, 0.0).astype(jnp.bfloat16)

    def q_operand(qt, q_index):
        alibi = _bf16(jnp.where(r == 0, slope,
                      jnp.where(r == 1, slope * blk,
                      jnp.where(r == 2, -slope * lane,
                      jnp.where(r == 3, -slope * float(q_index * blk), 0.0)))))
        return jnp.concatenate(
            [jnp.concatenate([qt[:HEAD_DIM], zeros, alibi], axis=0),
             jnp.concatenate([zeros, qt[HEAD_DIM:], alibi], axis=0)], axis=1)

    def score_block(q_aug, j, diagonal):
        rows = slice(j * blk, (j + 1) * blk)
        k_aug = jnp.concatenate([k_ref[rows, :], pos_ref[rows, :]], axis=1)
        s = jnp.dot(k_aug, q_aug, preferred_element_type=jnp.float32)
        return jnp.where(causal, s, -jnp.inf) if diagonal else s

    def finish(t):
        acc = acc_ref[...]
        lam = (jnp.exp(jnp.sum(lq1_ref[...] * lk1_ref[...], axis=-1, keepdims=True))
               - jnp.exp(jnp.sum(lq2_ref[...] * lk2_ref[...], axis=-1, keepdims=True))
               + LAMBDA_INIT)
        heads = acc[:V_HEAD_DIM] / acc[V_HEAD_DIM:V_HEAD_DIM + 1]
        o = heads[:, :blk] - lam * heads[:, blk:]
        o = o * lax.rsqrt(jnp.mean(o * o, axis=0, keepdims=True) + EPS)
        o_ref[t * blk:(t + 1) * blk, :] = _bf16(o.T * (sg_ref[...] * (1.0 - LAMBDA_INIT)))

    order = list(range(0, n_blocks, 2)) + list(range(n_blocks - 1 - n_blocks % 2, 0, -2))
    assert sorted(order) == list(range(n_blocks))

    def score_q_block(q_aug, q_index, slot, j, m_run):
        s = score_block(q_aug, j, j == q_index)
        s_ref[slot, j * blk:(j + 1) * blk, :] = s
        col_max = jnp.max(s, axis=0, keepdims=True)
        return col_max if m_run is None else jnp.maximum(m_run, col_max)

    def sub_step(t):
        if t >= 1:
            finish(order[t - 1])
        if t == n_blocks:
            return
        if t == 0:
            for pos in range(min(SCORE_LEAD, n_blocks)):
                q_index = order[pos]
                q_aug = q_operand(qt_ref[q_index], q_index)
                m_run = None
                for j in range(q_index + 1):
                    m_run = score_q_block(q_aug, q_index, pos % N_SCORE_BUFS, j, m_run)
                m_ref[pos % N_SCORE_BUFS] = m_run
        cur = t % N_SCORE_BUFS
        q_cur = order[t]
        scoring = t + SCORE_LEAD < n_blocks
        q_ahead = order[t + SCORE_LEAD] if scoring else -1
        slot_ahead = (t + SCORE_LEAD) % N_SCORE_BUFS
        q_aug = q_operand(qt_ref[q_ahead], q_ahead) if scoring else None
        m_cur = m_ref[cur]
        m_run = None
        acc = jnp.zeros((V_HEAD_DIM + N_ONES_ROWS, 2 * blk), jnp.float32)
        for j in range(max(q_cur, q_ahead) + 1):
            if j <= q_ahead:
                m_run = score_q_block(q_aug, q_ahead, slot_ahead, j, m_run)
            if j <= q_cur:
                p = _bf16(jnp.exp(s_ref[cur, j * blk:(j + 1) * blk, :] - m_cur))
                vt_aug = jnp.concatenate([vt_ref[j], ones_blk], axis=0)
                acc = acc + jnp.dot(vt_aug, p, preferred_element_type=jnp.float32)
        if scoring:
            m_ref[slot_ahead] = m_run
        acc_ref[...] = acc

    def grid_step(first):
        def run():
            for t in range(first, min(first + q_per_step, n_blocks + 1)):
                sub_step(t)
        return run

    steps = [grid_step(first) for first in range(0, n_blocks + 1, q_per_step)]
    if len(steps) == 1:
        steps[0]()
    else:
        lax.switch(pl.program_id(2), steps)


def _alibi_slopes(n_heads):
    return [2.0 ** (-8.0 * (i + 1) / n_heads) for i in range(n_heads)]


def _key_position_columns(seq_len):
    j = np.arange(seq_len)
    pos = np.zeros((seq_len, N_POS_COLS), np.float32)
    pos[:, 0] = j % ATTN_BLOCK
    pos[:, 1] = j // ATTN_BLOCK
    pos[:, 2:4] = 1.0
    assert ATTN_BLOCK <= 256 and seq_len // ATTN_BLOCK <= 256
    return jnp.asarray(pos, jnp.bfloat16)


def _diff_attn(qt, k, vt, lambdas, subln_g):
    B, nblk, d_attn, blk = qt.shape
    S = k.shape[1]
    H = d_attn // V_HEAD_DIM
    pos = _key_position_columns(S)
    kernel = functools.partial(_diff_attn_kernel, slopes=tuple(_alibi_slopes(H)),
                               q_per_step=ATTN_Q_PER_STEP)
    n_steps = pl.cdiv(nblk + 1, ATTN_Q_PER_STEP)
    return pl.pallas_call(
        kernel,
        grid=(B, H, n_steps),
        in_specs=[
            pl.BlockSpec((None, nblk, V_HEAD_DIM, blk), lambda b, h, q: (b, 0, h, 0)),
            pl.BlockSpec((None, S, V_HEAD_DIM), lambda b, h, q: (b, 0, h)),
            pl.BlockSpec(pos.shape, lambda b, h, q: (0, 0)),
            pl.BlockSpec((None, nblk, V_HEAD_DIM, blk), lambda b, h, q: (b, 0, h, 0)),
            *[pl.BlockSpec(v.shape, lambda b, h, q: (0, 0)) for v in lambdas],
            pl.BlockSpec(subln_g.shape, lambda b, h, q: (0, 0)),
        ],
        out_specs=pl.BlockSpec((None, S, V_HEAD_DIM), lambda b, h, q: (b, 0, h)),
        out_shape=jax.ShapeDtypeStruct((B, S, d_attn), jnp.bfloat16),
        scratch_shapes=[pltpu.VMEM((N_SCORE_BUFS, S, 2 * blk), jnp.float32),
                        pltpu.VMEM((N_SCORE_BUFS, 1, 2 * blk), jnp.float32),
                        pltpu.VMEM((V_HEAD_DIM + N_ONES_ROWS, 2 * blk), jnp.float32)],
        compiler_params=pltpu.CompilerParams(
            dimension_semantics=("parallel", "parallel", "arbitrary"),
            vmem_limit_bytes=V7X_VMEM_LIMIT_BYTES),
        name="diff_attn",
    )(qt, k, pos, vt, *lambdas, subln_g)


def _out_mlp_kernel(x_ref, attn_ref, lru_ref, wo_hbm, gm_ref, wup_hbm, wdn_hbm, gf_ref, y_ref,
                    wo_ref, wup_ref, wdn_ref, stage_ref, sem):
    tm = x_ref.shape[0]
    d_ff = wup_ref.shape[1]
    n_chunks = d_ff // FF_CHUNK
    rows = [slice(g * tm // MLP_SPLIT, (g + 1) * tm // MLP_SPLIT) for g in range(MLP_SPLIT)]

    def ff(c):
        return pl.ds(c * FF_CHUNK, FF_CHUNK)

    sources = [wo_hbm]
    targets = [lambda v: wo_ref.__setitem__((slice(None), slice(None)), v)]
    for c in range(n_chunks):
        sources += [wup_hbm.at[:, ff(c)], wdn_hbm.at[ff(c), :]]
        targets += [lambda v, c=c: wup_ref.__setitem__((slice(None), ff(c)), v),
                    lambda v, c=c: wdn_ref.__setitem__((ff(c), slice(None)), v)]

    def copy(i):
        return pltpu.make_async_copy(sources[i], stage_ref.at[i % 2], sem.at[i % 2])

    def land(i):
        copy(i).wait()
        targets[i](_bf16(stage_ref[i % 2]))
        if i + 2 < len(sources):
            copy(i + 2).start()

    def step_body(first_step):
        if first_step:
            copy(0).start()
            copy(1).start()
            land(0)
        x1 = []
        for r in rows:
            mix = jnp.concatenate([attn_ref[r, :], lru_ref[r, :]], axis=-1)
            x1.append(x_ref[r, :] + jnp.dot(mix, wo_ref[...], preferred_element_type=jnp.float32))
        hm = [_bf16(_rmsnorm_rows(v, gm_ref[...])) for v in x1]
        mlp = [jnp.zeros_like(v) for v in x1]
        for c in range(n_chunks):
            cols = slice(c * FF_CHUNK, (c + 1) * FF_CHUNK)
            if first_step:
                land(1 + 2 * c)
            act = [_bf16(jnp.square(jnp.maximum(
                jnp.dot(hm[g], wup_ref[:, cols], preferred_element_type=jnp.float32), 0.0)))
                for g in range(MLP_SPLIT)]
            if first_step:
                land(2 + 2 * c)
            for g in range(MLP_SPLIT):
                mlp[g] = mlp[g] + jnp.dot(act[g], wdn_ref[cols, :],
                                          preferred_element_type=jnp.float32)
        for g, r in enumerate(rows):
            y_ref[r, :] = _rmsnorm_rows(x1[g] + mlp[g], gf_ref[...])

    pl.when(pl.program_id(0) == 0)(lambda: step_body(True))
    pl.when(pl.program_id(0) != 0)(lambda: step_body(False))


def _out_mlp(x, attn, lru, w_out, g_mlp, w_up, w_down, g_final):
    B, S, D = x.shape
    T = B * S
    tm = MLP_TOKENS
    x2 = x.reshape(T, D)
    attn2 = attn.reshape(T, attn.shape[-1])
    lru2 = lru.reshape(T, lru.shape[-1])
    const = lambda i: (0, 0)
    in_hbm = pl.BlockSpec(memory_space=pl.ANY)
    assert w_out.shape == (D, D) and FF_CHUNK == D and w_up.shape[1] % FF_CHUNK == 0
    y = pl.pallas_call(
        _out_mlp_kernel,
        grid=(T // tm,),
        in_specs=[
            pl.BlockSpec((tm, D), lambda i: (i, 0)),
            pl.BlockSpec((tm, attn2.shape[1]), lambda i: (i, 0)),
            pl.BlockSpec((tm, lru2.shape[1]), lambda i: (i, 0)),
            in_hbm,
            pl.BlockSpec(g_mlp.shape, const),
            in_hbm,
            in_hbm,
            pl.BlockSpec(g_final.shape, const),
        ],
        out_specs=pl.BlockSpec((tm, D), lambda i: (i, 0)),
        out_shape=jax.ShapeDtypeStruct((T, D), jnp.float32),
        scratch_shapes=[
            pltpu.VMEM(w_out.shape, jnp.bfloat16),
            pltpu.VMEM(w_up.shape, jnp.bfloat16),
            pltpu.VMEM(w_down.shape, jnp.bfloat16),
            pltpu.VMEM((2, D, FF_CHUNK), jnp.float32),
            pltpu.SemaphoreType.DMA((2,)),
        ],
        compiler_params=pltpu.CompilerParams(
            dimension_semantics=("arbitrary",),
            vmem_limit_bytes=V7X_VMEM_LIMIT_BYTES),
        name="out_mlp",
    )(x2, attn2, lru2, w_out, g_mlp, w_up, w_down, g_final)
    return y.reshape(B, S, D)


def _pair_block_diag(w):
    n, c, d = w.shape
    rows = w.reshape(n // 2, 2 * c, d)
    tiled = jnp.concatenate([rows, rows], axis=-1)
    row_block = np.arange(2 * c)[:, None] // c
    col_block = np.arange(2 * d)[None, :] // d
    return jnp.where(jnp.asarray(row_block == col_block), tiled, 0.0)


def kernel(x, norm_mix_g, w_in, conv_w, conv_b, w_rg, b_rg, w_ig, b_ig, lru_L, lambda_q1, lambda_k1, lambda_q2, lambda_k2, subln_g, w_out, norm_mlp_g, w_up, w_down, final_g):
    B, S, D = x.shape
    d_lru = lru_L.shape[-1]
    d_attn = (w_in.shape[-1] - 2 * d_lru) // 3
    depth = w_in.shape[0]
    n_slab = d_lru // LANES
    assert depth == 1, "LAMBDA_INIT is specialised to a single layer"
    assert S % PROJ_TOKENS == 0 and (B * S) % MLP_TOKENS == 0
    assert PROJ_TOKENS % SUB_ROWS == 0 and d_lru % LANES == 0
    assert 2 * w_rg.shape[-1] == LANES and w_rg.shape[1] * w_rg.shape[2] == d_lru
    l = 0
    row = lambda v: v.reshape(1, -1)

    w_gates = _bf16(0.5 * jnp.concatenate([_pair_block_diag(w_rg[l]), _pair_block_diag(w_ig[l])],
                                          axis=-1))
    b_gates = 0.5 * jnp.concatenate([b_rg[l].reshape(n_slab, 1, LANES),
                                     b_ig[l].reshape(n_slab, 1, LANES)], axis=-1)
    lambdas = [row(v[l]) for v in (lambda_q1, lambda_k1, lambda_q2, lambda_k2)]

    qt, k, vt, lru = _in_proj_lru(x, row(norm_mix_g[l]), w_in[l], conv_w[l], row(conv_b[l]),
                                  w_gates, b_gates, row(lru_L[l]))
    attn = _diff_attn(qt, k, vt, lambdas, row(subln_g[l]))
    return _out_mlp(x, attn, lru, w_out[l], row(norm_mlp_g[l]), w_up[l], w_down[l], row(final_g))
```

```python
import functools
import math

import numpy as np
import jax
import jax.numpy as jnp
from jax import lax
from jax.experimental import pallas as pl
from jax.experimental.pallas import tpu as pltpu

EPS = 1e-6
CONV_WIDTH = 4
LRU_C = 8.0
HEAD_DIM = 64
V_HEAD_DIM = 2 * HEAD_DIM
LAMBDA_INIT = 0.8 - 0.6 * math.exp(-0.3 * 0)

V7X_MXU_COLS = 256
LANES = 128
SUBLANES = 8
V7X_VMEM_LIMIT_BYTES = 56 * 1024 * 1024

ATTN_BLOCK = V7X_MXU_COLS
PROJ_TOKENS = 512
MLP_TOKENS = 512
FF_CHUNK = 1024
MLP_SPLIT = 2
W_IN_CHUNK_ROWS = 128
CHAIN = 4
SUB_ROWS = SUBLANES * CHAIN


def _bf16(x):
    return x.astype(jnp.bfloat16)


def _rmsnorm_rows(x, g):
    return x * lax.rsqrt(jnp.mean(x * x, axis=-1, keepdims=True) + EPS) * g


def _stream_cast_rows(src_hbm, dst_ref, stage_ref, sem, chunk_rows, scale=None):
    n_chunks = src_hbm.shape[0] // chunk_rows

    def copy(i):
        return pltpu.make_async_copy(src_hbm.at[pl.ds(i * chunk_rows, chunk_rows), :],
                                     stage_ref.at[i % 2], sem.at[i % 2])

    copy(0).start()
    for i in range(n_chunks):
        if i + 1 < n_chunks:
            copy(i + 1).start()
        copy(i).wait()
        chunk = stage_ref[i % 2]
        if scale is not None:
            chunk = chunk * scale
        dst_ref[i * chunk_rows:(i + 1) * chunk_rows, :] = _bf16(chunk)


def _log_sigmoid(x):
    return jnp.minimum(x, 0.0) - jnp.log1p(jnp.exp(-jnp.abs(x)))


def _in_proj_lru_kernel(x_ref, g_ref, w_hbm, cw_ref, cb_ref, wg_ref, bg_ref, lru_l_ref, zero_ref,
                        qt_ref, k_ref, vt_ref, lru_ref,
                        w_ref, w_stage, w_sem, h_scr, ug_slab, u_scr, ub_scr, z_scr, gelu_scr,
                        out_slab, tail_ref, carry_ref):
    tm = x_ref.shape[0]
    d_lru = lru_ref.shape[-1]
    n_slab = d_lru // LANES
    n_sub = tm // SUB_ROWS
    d_attn = k_ref.shape[-1]
    f32 = jnp.float32

    @pl.when((pl.program_id(0) == 0) & (pl.program_id(1) == 0))
    def _():
        col = lax.broadcasted_iota(jnp.int32, (1, w_ref.shape[1]), 1)
        q_scale = jnp.where(col < d_attn, HEAD_DIM ** -0.5, 1.0)
        _stream_cast_rows(w_hbm, w_ref, w_stage, w_sem, w_stage.shape[1], scale=q_scale)

    @pl.when(pl.program_id(1) == 0)
    def _():
        tail_ref[...] = jnp.zeros(tail_ref.shape, f32)
        carry_ref[...] = jnp.zeros(carry_ref.shape, f32)

    h_scr[...] = _bf16(_rmsnorm_rows(x_ref[...], g_ref[...]))

    def ug_half(half):
        cols = slice(3 * d_attn + half * d_lru, 3 * d_attn + (half + 1) * d_lru)
        nat = jnp.dot(h_scr[...], w_ref[:, cols], preferred_element_type=f32)
        for l in range(n_slab):
            ug_slab[half * n_slab + l] = nat[:, l * LANES:(l + 1) * LANES]

    ug_half(0)

    def piece(slab, s, c):
        return ug_slab[slab, pl.ds(s * SUB_ROWS + c, SUBLANES, stride=CHAIN), :]

    row = lax.broadcasted_iota(jnp.int32, (SUBLANES, LANES), 0)

    for l in range(n_slab):
        lanes = slice(l * LANES, (l + 1) * LANES)
        cw = cw_ref[:, lanes]
        cb = cb_ref[:, lanes]
        rolled_prev = [pltpu.roll(tail_ref[c, :, lanes], 1, axis=0) for c in range(CHAIN)]
        for s in range(n_sub):
            cur = [piece(l, s, c) for c in range(CHAIN)]
            rolled = [pltpu.roll(cur[c], 1, axis=0) for c in range(CHAIN)]
            shifted = [jnp.where(row == 0, rolled_prev[c], rolled[c]) for c in range(CHAIN)]
            rolled_prev = rolled
            if s == n_sub - 1:
                for c in range(CHAIN):
                    tail_ref[c, :, lanes] = cur[c]
            conv = []
            for c in range(CHAIN):
                acc = cb
                for kk in range(CONV_WIDTH):
                    d = CONV_WIDTH - 1 - kk
                    src = cur[c - d] if c >= d else shifted[c - d + CHAIN]
                    acc = acc + src * cw[kk:kk + 1, :]
                conv.append(acc)
            u_sub = jnp.concatenate(conv, axis=0)
            u_scr[l, s * SUB_ROWS:(s + 1) * SUB_ROWS, :] = u_sub
            ub_scr[s * SUB_ROWS:(s + 1) * SUB_ROWS, lanes] = _bf16(u_sub)

    ug_half(1)
    for l in range(n_slab):
        z = jnp.dot(ub_scr[:, l * LANES:(l + 1) * LANES], wg_ref[l],
                    preferred_element_type=f32) + bg_ref[l]
        z_scr[l] = z[:, :LANES]
        z_scr[n_slab + l] = z[:, LANES:]

    def k_piece(part):
        cols = slice(part * ATTN_BLOCK, (part + 1) * ATTN_BLOCK)
        t = jnp.dot(h_scr[...], w_ref[:, d_attn + part * ATTN_BLOCK:
                                     d_attn + (part + 1) * ATTN_BLOCK], preferred_element_type=f32)
        k_ref[:, cols] = _bf16(t)
        return [t[tm - SUBLANES:, c * LANES:(c + 1) * LANES] for c in range(ATTN_BLOCK // LANES)]

    def transposed_piece(first_col, o_ref, part):
        rows = slice(part * ATTN_BLOCK, (part + 1) * ATTN_BLOCK)
        t = jnp.dot(h_scr[rows, :], w_ref[:, first_col:first_col + d_attn],
                    preferred_element_type=f32)
        o_ref[part] = _bf16(t.T)
        return [t[ATTN_BLOCK - SUBLANES:, c * LANES:(c + 1) * LANES] for c in range(d_attn // LANES)]

    side_work = {0: lambda: k_piece(0), 2: lambda: k_piece(1),
                 5: lambda: transposed_piece(0, qt_ref, 0),
                 8: lambda: transposed_piece(0, qt_ref, 1),
                 11: lambda: transposed_piece(2 * d_attn, vt_ref, 0),
                 13: lambda: transposed_piece(2 * d_attn, vt_ref, 1)}
    assert n_sub > max(side_work) and qt_ref.shape[0] == 2 and d_attn == 2 * ATTN_BLOCK

    c_gelu = math.sqrt(2.0 / math.pi)
    for l in range(n_slab):
        for s in range(n_sub):
            gate = jnp.concatenate([piece(n_slab + l, s, c) for c in range(CHAIN)], axis=0)
            gelu_scr[l, s * SUB_ROWS:(s + 1) * SUB_ROWS, :] = (0.5 * gate) * (1.0 + jnp.tanh(
                gate * (c_gelu + (c_gelu * 0.044715) * (gate * gate))))

    c2_row = (0.5 * LRU_C * math.log2(math.e)) * _log_sigmoid(lru_l_ref[...])

    carry = [carry_ref[:, l * LANES:(l + 1) * LANES] for l in range(n_slab)]
    pending_at = -1
    keep_state = zero_ref[...] == 0
    for s in range(n_sub):
        if s in side_work:
            anchor = functools.reduce(jnp.add, side_work[s]())
            pending_at = s + 1
        base = s * SUB_ROWS
        rows = slice(base, base + SUB_ROWS)
        for l in range(n_slab):
            c2 = c2_row[:, l * LANES:(l + 1) * LANES]
            u = u_scr[l, rows, :]
            t_r = jnp.tanh(z_scr[l, rows, :])
            t_i = jnp.tanh(z_scr[n_slab + l, rows, :])
            log2_a = c2 + c2 * t_r
            a = jnp.exp2(log2_a)
            quarter = (jnp.tanh(log2_a * math.log(2.0)) * -0.25) * (a * a + 1.0)
            half_mult = jnp.where(quarter > 0.0, quarter * lax.rsqrt(quarter), 0.0)
            b = (half_mult * u) * (1.0 + t_i)
            gelu = gelu_scr[l, rows, :]

            a_c = [a[c * SUBLANES:(c + 1) * SUBLANES] for c in range(CHAIN)]
            b_c = [b[c * SUBLANES:(c + 1) * SUBLANES] for c in range(CHAIN)]
            h_loc, p_loc = [b_c[0]], [a_c[0]]
            for c in range(1, CHAIN):
                h_loc.append(a_c[c] * h_loc[c - 1] + b_c[c])
                p_loc.append(a_c[c] * p_loc[c - 1])
            p_cum, h_cum = p_loc[CHAIN - 1], h_loc[CHAIN - 1]
            shift = 1
            while shift < SUBLANES:
                valid = row >= shift
                p_prev = jnp.where(valid, pltpu.roll(p_cum, shift, axis=0), 1.0)
                h_prev = jnp.where(valid, pltpu.roll(h_cum, shift, axis=0), 0.0)
                h_cum = p_cum * h_prev + h_cum
                p_cum = p_cum * p_prev
                shift *= 2
            chain_end = h_cum + p_cum * carry[l]
            chain_in = jnp.where(row == 0, carry[l], pltpu.roll(chain_end, 1, axis=0))
            carry[l] = jnp.broadcast_to(chain_end[SUBLANES - 1:SUBLANES, :], (SUBLANES, LANES))
            if s == pending_at:
                carry[l] = jnp.where(keep_state, carry[l], anchor)
            for c in range(CHAIN):
                out = (h_loc[c] + p_loc[c] * chain_in) * gelu[c * SUBLANES:(c + 1) * SUBLANES]
                out_slab[l, pl.ds(base + c, SUBLANES, stride=CHAIN), :] = out
    for l in range(n_slab):
        carry_ref[:, l * LANES:(l + 1) * LANES] = carry[l]
    lru_ref[...] = _bf16(jnp.concatenate([out_slab[l] for l in range(n_slab)], axis=1))


def _in_proj_lru(x, g, w, conv_w, conv_b, w_gates, b_gates, lru_l):
    B, S, D = x.shape
    d_lru = lru_l.shape[-1]
    d_attn = (w.shape[1] - 2 * d_lru) // 3
    tm = PROJ_TOKENS
    nblk = tm // ATTN_BLOCK
    n_slab = d_lru // LANES
    const = lambda b, i: (0, 0)
    assert D % W_IN_CHUNK_ROWS == 0
    return pl.pallas_call(
        _in_proj_lru_kernel,
        grid=(B, S // tm),
        in_specs=[
            pl.BlockSpec((None, tm, D), lambda b, i: (b, i, 0)),
            pl.BlockSpec((1, D), const),
            pl.BlockSpec(memory_space=pl.ANY),
            pl.BlockSpec(conv_w.shape, const),
            pl.BlockSpec(conv_b.shape, const),
            pl.BlockSpec(w_gates.shape, lambda b, i: (0, 0, 0)),
            pl.BlockSpec(b_gates.shape, lambda b, i: (0, 0, 0)),
            pl.BlockSpec(lru_l.shape, const),
            pl.BlockSpec((SUBLANES, LANES), const),
        ],
        out_specs=[
            pl.BlockSpec((None, nblk, d_attn, ATTN_BLOCK), lambda b, i: (b, i, 0, 0)),
            pl.BlockSpec((None, tm, d_attn), lambda b, i: (b, i, 0)),
            pl.BlockSpec((None, nblk, d_attn, ATTN_BLOCK), lambda b, i: (b, i, 0, 0)),
            pl.BlockSpec((None, tm, d_lru), lambda b, i: (b, i, 0)),
        ],
        out_shape=[
            jax.ShapeDtypeStruct((B, S // ATTN_BLOCK, d_attn, ATTN_BLOCK), jnp.bfloat16),
            jax.ShapeDtypeStruct((B, S, d_attn), jnp.bfloat16),
            jax.ShapeDtypeStruct((B, S // ATTN_BLOCK, d_attn, ATTN_BLOCK), jnp.bfloat16),
            jax.ShapeDtypeStruct((B, S, d_lru), jnp.bfloat16),
        ],
        scratch_shapes=[
            pltpu.VMEM(w.shape, jnp.bfloat16),
            pltpu.VMEM((2, W_IN_CHUNK_ROWS, w.shape[1]), jnp.float32),
            pltpu.SemaphoreType.DMA((2,)),
            pltpu.VMEM((tm, D), jnp.bfloat16),
            pltpu.VMEM((2 * n_slab, tm, LANES), jnp.float32),
            pltpu.VMEM((n_slab, tm, LANES), jnp.float32),
            pltpu.VMEM((tm, d_lru), jnp.bfloat16),
            pltpu.VMEM((2 * n_slab, tm, LANES), jnp.float32),
            pltpu.VMEM((n_slab, tm, LANES), jnp.float32),
            pltpu.VMEM((n_slab, tm, LANES), jnp.float32),
            pltpu.VMEM((CHAIN, SUBLANES, d_lru), jnp.float32),
            pltpu.VMEM((SUBLANES, d_lru), jnp.float32),
        ],
        compiler_params=pltpu.CompilerParams(
            dimension_semantics=("arbitrary", "arbitrary"),
            vmem_limit_bytes=V7X_VMEM_LIMIT_BYTES),
        name="in_proj_lru",
    )(x, g, w, conv_w, conv_b, w_gates, b_gates, lru_l, jnp.zeros((SUBLANES, LANES), jnp.int32))


N_POS_COLS = 128
N_ONES_ROWS = 16
SCORE_LEAD = 2
N_SCORE_BUFS = SCORE_LEAD + 1
ATTN_Q_PER_STEP = 6


def _diff_attn_kernel(qt_ref, k_ref, pos_ref, vt_ref, lq1_ref, lk1_ref, lq2_ref, lk2_ref, sg_ref,
                      o_ref,
                      s_ref, m_ref, acc_ref, *, slopes, q_per_step):
    h = pl.program_id(1)
    blk = ATTN_BLOCK
    n_blocks = vt_ref.shape[0]

    slope = jnp.float32(slopes[0])
    for i in range(1, len(slopes)):
        slope = jnp.where(h == i, jnp.float32(slopes[i]), slope)

    r = lax.broadcasted_iota(jnp.int32, (N_POS_COLS, blk), 0)
    lane = lax.broadcasted_iota(jnp.int32, (N_POS_COLS, blk), 1).astype(jnp.float32)
    zeros = jnp.zeros((HEAD_DIM, blk), qt_ref.dtype)
    key_in_blk = lax.broadcasted_iota(jnp.int32, (blk, 2 * blk), 0)
    qry_in_blk = lax.broadcasted_iota(jnp.int32, (blk, 2 * blk), 1) % blk
    causal = key_in_blk <= qry_in_blk
    ones_row = lax.broadcasted_iota(jnp.int32, (N_ONES_ROWS, blk), 0) == 0
    ones_blk = jnp.where(ones_row, 1.0, 0.0).astype(jnp.bfloat16)

    def q_operand(qt, q_index):
        alibi = _bf16(jnp.where(r == 0, slope,
                      jnp.where(r == 1, slope * blk,
                      jnp.where(r == 2, -slope * lane,
                      jnp.where(r == 3, -slope * float(q_index * blk), 0.0)))))
        return jnp.concatenate(
            [jnp.concatenate([qt[:HEAD_DIM], zeros, alibi], axis=0),
             jnp.concatenate([zeros, qt[HEAD_DIM:], alibi], axis=0)], axis=1)

    def score_block(q_aug, j, diagonal):
        rows = slice(j * blk, (j + 1) * blk)
        k_aug = jnp.concatenate([k_ref[rows, :], pos_ref[rows, :]], axis=1)
        s = jnp.dot(k_aug, q_aug, preferred_element_type=jnp.float32)
        return jnp.where(causal, s, -jnp.inf) if diagonal else s

    def finish(t):
        acc = acc_ref[...]
        lam = (jnp.exp(jnp.sum(lq1_ref[...] * lk1_ref[...], axis=-1, keepdims=True))
               - jnp.exp(jnp.sum(lq2_ref[...] * lk2_ref[...], axis=-1, keepdims=True))
               + LAMBDA_INIT)
        heads = acc[:V_HEAD_DIM] / acc[V_HEAD_DIM:V_HEAD_DIM + 1]
        o = heads[:, :blk] - lam * heads[:, blk:]
        o = o * lax.rsqrt(jnp.mean(o * o, axis=0, keepdims=True) + EPS)
        o_ref[t * blk:(t + 1) * blk, :] = _bf16(o.T * (sg_ref[...] * (1.0 - LAMBDA_INIT)))

    order = list(range(0, n_blocks, 2)) + list(range(n_blocks - 1 - n_blocks % 2, 0, -2))
    assert sorted(order) == list(range(n_blocks))

    def score_q_block(q_aug, q_index, slot, j, m_run):
        s = score_block(q_aug, j, j == q_index)
        s_ref[slot, j * blk:(j + 1) * blk, :] = s
        col_max = jnp.max(s, axis=0, keepdims=True)
        return col_max if m_run is None else jnp.maximum(m_run, col_max)

    def sub_step(t):
        if t >= 1:
            finish(order[t - 1])
        if t == n_blocks:
            return
        if t == 0:
            for pos in range(min(SCORE_LEAD, n_blocks)):
                q_index = order[pos]
                q_aug = q_operand(qt_ref[q_index], q_index)
                m_run = None
                for j in range(q_index + 1):
                    m_run = score_q_block(q_aug, q_index, pos % N_SCORE_BUFS, j, m_run)
                m_ref[pos % N_SCORE_BUFS] = m_run
        cur = t % N_SCORE_BUFS
        q_cur = order[t]
        scoring = t + SCORE_LEAD < n_blocks
        q_ahead = order[t + SCORE_LEAD] if scoring else -1
        slot_ahead = (t + SCORE_LEAD) % N_SCORE_BUFS
        q_aug = q_operand(qt_ref[q_ahead], q_ahead) if scoring else None
        m_cur = m_ref[cur]
        m_run = None
        acc = jnp.zeros((V_HEAD_DIM + N_ONES_ROWS, 2 * blk), jnp.float32)
        for j in range(max(q_cur, q_ahead) + 1):
            if j <= q_ahead:
                m_run = score_q_block(q_aug, q_ahead, slot_ahead, j, m_run)
            if j <= q_cur:
                p = _bf16(jnp.exp(s_ref[cur, j * blk:(j + 1) * blk, :] - m_cur))
                vt_aug = jnp.concatenate([vt_ref[j], ones_blk], axis=0)
                acc = acc + jnp.dot(vt_aug, p, preferred_element_type=jnp.float32)
        if scoring:
            m_ref[slot_ahead] = m_run
        acc_ref[...] = acc

    def grid_step(first):
        def run():
            for t in range(first, min(first + q_per_step, n_blocks + 1)):
                sub_step(t)
        return run

    steps = [grid_step(first) for first in range(0, n_blocks + 1, q_per_step)]
    if len(steps) == 1:
        steps[0]()
    else:
        lax.switch(pl.program_id(2), steps)


def _alibi_slopes(n_heads):
    return [2.0 ** (-8.0 * (i + 1) / n_heads) for i in range(n_heads)]


def _key_position_columns(seq_len):
    j = np.arange(seq_len)
    pos = np.zeros((seq_len, N_POS_COLS), np.float32)
    pos[:, 0] = j % ATTN_BLOCK
    pos[:, 1] = j // ATTN_BLOCK
    pos[:, 2:4] = 1.0
    assert ATTN_BLOCK <= 256 and seq_len // ATTN_BLOCK <= 256
    return jnp.asarray(pos, jnp.bfloat16)


def _diff_attn(qt, k, vt, lambdas, subln_g):
    B, nblk, d_attn, blk = qt.shape
    S = k.shape[1]
    H = d_attn // V_HEAD_DIM
    pos = _key_position_columns(S)
    kernel = functools.partial(_diff_attn_kernel, slopes=tuple(_alibi_slopes(H)),
                               q_per_step=ATTN_Q_PER_STEP)
    n_steps = pl.cdiv(nblk + 1, ATTN_Q_PER_STEP)
    return pl.pallas_call(
        kernel,
        grid=(B, H, n_steps),
        in_specs=[
            pl.BlockSpec((None, nblk, V_HEAD_DIM, blk), lambda b, h, q: (b, 0, h, 0)),
            pl.BlockSpec((None, S, V_HEAD_DIM), lambda b, h, q: (b, 0, h)),
            pl.BlockSpec(pos.shape, lambda b, h, q: (0, 0)),
            pl.BlockSpec((None, nblk, V_HEAD_DIM, blk), lambda b, h, q: (b, 0, h, 0)),
            *[pl.BlockSpec(v.shape, lambda b, h, q: (0, 0)) for v in lambdas],
            pl.BlockSpec(subln_g.shape, lambda b, h, q: (0, 0)),
        ],
        out_specs=pl.BlockSpec((None, S, V_HEAD_DIM), lambda b, h, q: (b, 0, h)),
        out_shape=jax.ShapeDtypeStruct((B, S, d_attn), jnp.bfloat16),
        scratch_shapes=[pltpu.VMEM((N_SCORE_BUFS, S, 2 * blk), jnp.float32),
                        pltpu.VMEM((N_SCORE_BUFS, 1, 2 * blk), jnp.float32),
                        pltpu.VMEM((V_HEAD_DIM + N_ONES_ROWS, 2 * blk), jnp.float32)],
        compiler_params=pltpu.CompilerParams(
            dimension_semantics=("parallel", "parallel", "arbitrary"),
            vmem_limit_bytes=V7X_VMEM_LIMIT_BYTES),
        name="diff_attn",
    )(qt, k, pos, vt, *lambdas, subln_g)


def _out_mlp_kernel(x_ref, attn_ref, lru_ref, wo_hbm, gm_ref, wup_hbm, wdn_hbm, gf_ref, y_ref,
                    wo_ref, wup_ref, wdn_ref, stage_ref, sem):
    tm = x_ref.shape[0]
    d_ff = wup_ref.shape[1]
    n_chunks = d_ff // FF_CHUNK
    rows = [slice(g * tm // MLP_SPLIT, (g + 1) * tm // MLP_SPLIT) for g in range(MLP_SPLIT)]

    def ff(c):
        return pl.ds(c * FF_CHUNK, FF_CHUNK)

    sources = [wo_hbm]
    targets = [lambda v: wo_ref.__setitem__((slice(None), slice(None)), v)]
    for c in range(n_chunks):
        sources += [wup_hbm.at[:, ff(c)], wdn_hbm.at[ff(c), :]]
        targets += [lambda v, c=c: wup_ref.__setitem__((slice(None), ff(c)), v),
                    lambda v, c=c: wdn_ref.__setitem__((ff(c), slice(None)), v)]

    def copy(i):
        return pltpu.make_async_copy(sources[i], stage_ref.at[i % 2], sem.at[i % 2])

    def land(i):
        copy(i).wait()
        targets[i](_bf16(stage_ref[i % 2]))
        if i + 2 < len(sources):
            copy(i + 2).start()

    def step_body(first_step):
        if first_step:
            copy(0).start()
            copy(1).start()
            land(0)
        x1 = []
        for r in rows:
            mix = jnp.concatenate([attn_ref[r, :], lru_ref[r, :]], axis=-1)
            x1.append(x_ref[r, :] + jnp.dot(mix, wo_ref[...], preferred_element_type=jnp.float32))
        hm = [_bf16(_rmsnorm_rows(v, gm_ref[...])) for v in x1]
        mlp = [jnp.zeros_like(v) for v in x1]
        for c in range(n_chunks):
            cols = slice(c * FF_CHUNK, (c + 1) * FF_CHUNK)
            if first_step:
                land(1 + 2 * c)
            act = [_bf16(jnp.square(jnp.maximum(
                jnp.dot(hm[g], wup_ref[:, cols], preferred_element_type=jnp.float32), 0.0)))
                for g in range(MLP_SPLIT)]
            if first_step:
                land(2 + 2 * c)
            for g in range(MLP_SPLIT):
                mlp[g] = mlp[g] + jnp.dot(act[g], wdn_ref[cols, :],
                                          preferred_element_type=jnp.float32)
        for g, r in enumerate(rows):
            y_ref[r, :] = _rmsnorm_rows(x1[g] + mlp[g], gf_ref[...])

    pl.when(pl.program_id(0) == 0)(lambda: step_body(True))
    pl.when(pl.program_id(0) != 0)(lambda: step_body(False))


def _out_mlp(x, attn, lru, w_out, g_mlp, w_up, w_down, g_final):
    B, S, D = x.shape
    T = B * S
    tm = MLP_TOKENS
    x2 = x.reshape(T, D)
    attn2 = attn.reshape(T, attn.shape[-1])
    lru2 = lru.reshape(T, lru.shape[-1])
    const = lambda i: (0, 0)
    in_hbm = pl.BlockSpec(memory_space=pl.ANY)
    assert w_out.shape == (D, D) and FF_CHUNK == D and w_up.shape[1] % FF_CHUNK == 0
    y = pl.pallas_call(
        _out_mlp_kernel,
        grid=(T // tm,),
        in_specs=[
            pl.BlockSpec((tm, D), lambda i: (i, 0)),
            pl.BlockSpec((tm, attn2.shape[1]), lambda i: (i, 0)),
            pl.BlockSpec((tm, lru2.shape[1]), lambda i: (i, 0)),
            in_hbm,
            pl.BlockSpec(g_mlp.shape, const),
            in_hbm,
            in_hbm,
            pl.BlockSpec(g_final.shape, const),
        ],
        out_specs=pl.BlockSpec((tm, D), lambda i: (i, 0)),
        out_shape=jax.ShapeDtypeStruct((T, D), jnp.float32),
        scratch_shapes=[
            pltpu.VMEM(w_out.shape, jnp.bfloat16),
            pltpu.VMEM(w_up.shape, jnp.bfloat16),
            pltpu.VMEM(w_down.shape, jnp.bfloat16),
            pltpu.VMEM((2, D, FF_CHUNK), jnp.float32),
            pltpu.SemaphoreType.DMA((2,)),
        ],
        compiler_params=pltpu.CompilerParams(
            dimension_semantics=("arbitrary",),
            vmem_limit_bytes=V7X_VMEM_LIMIT_BYTES),
        name="out_mlp",
    )(x2, attn2, lru2, w_out, g_mlp, w_up, w_down, g_final)
    return y.reshape(B, S, D)


def _pair_block_diag(w):
    n, c, d = w.shape
    rows = w.reshape(n // 2, 2 * c, d)
    tiled = jnp.concatenate([rows, rows], axis=-1)
    row_block = np.arange(2 * c)[:, None] // c
    col_block = np.arange(2 * d)[None, :] // d
    return jnp.where(jnp.asarray(row_block == col_block), tiled, 0.0)


def kernel(x, norm_mix_g, w_in, conv_w, conv_b, w_rg, b_rg, w_ig, b_ig, lru_L, lambda_q1, lambda_k1, lambda_q2, lambda_k2, subln_g, w_out, norm_mlp_g, w_up, w_down, final_g):
    B, S, D = x.shape
    d_lru = lru_L.shape[-1]
    d_attn = (w_in.shape[-1] - 2 * d_lru) // 3
    depth = w_in.shape[0]
    n_slab = d_lru // LANES
    assert depth == 1, "LAMBDA_INIT is specialised to a single layer"
    assert S % PROJ_TOKENS == 0 and (B * S) % MLP_TOKENS == 0
    assert PROJ_TOKENS % SUB_ROWS == 0 and d_lru % LANES == 0
    assert 2 * w_rg.shape[-1] == LANES and w_rg.shape[1] * w_rg.shape[2] == d_lru
    l = 0
    row = lambda v: v.reshape(1, -1)

    w_gates = _bf16(0.5 * jnp.concatenate([_pair_block_diag(w_rg[l]), _pair_block_diag(w_ig[l])],
                                          axis=-1))
    b_gates = 0.5 * jnp.concatenate([b_rg[l].reshape(n_slab, 1, LANES),
                                     b_ig[l].reshape(n_slab, 1, LANES)], axis=-1)
    lambdas = [row(v[l]) for v in (lambda_q1, lambda_k1, lambda_q2, lambda_k2)]

    qt, k, vt, lru = _in_proj_lru(x, row(norm_mix_g[l]), w_in[l], conv_w[l], row(conv_b[l]),
                                  w_gates, b_gates, row(lru_L[l]))
    attn = _diff_attn(qt, k, vt, lambdas, row(subln_g[l]))
    return _out_mlp(x, attn, lru, w_out[l], row(norm_mlp_g[l]), w_up[l], w_down[l], row(final_g))
```

```python
import functools
import math

import numpy as np
import jax
import jax.numpy as jnp
from jax import lax
from jax.experimental import pallas as pl
from jax.experimental.pallas import tpu as pltpu

EPS = 1e-6
CONV_WIDTH = 4
LRU_C = 8.0
HEAD_DIM = 64
V_HEAD_DIM = 2 * HEAD_DIM
LAMBDA_INIT = 0.8 - 0.6 * math.exp(-0.3 * 0)

V7X_MXU_COLS = 256
LANES = 128
SUBLANES = 8
V7X_VMEM_LIMIT_BYTES = 56 * 1024 * 1024

ATTN_BLOCK = V7X_MXU_COLS
PROJ_TOKENS = 512
MLP_TOKENS = 512
FF_CHUNK = 1024
MLP_SPLIT = 2
W_IN_CHUNK_ROWS = 128
CHAIN = 4
SUB_ROWS = SUBLANES * CHAIN


def _bf16(x):
    return x.astype(jnp.bfloat16)


def _rmsnorm_rows(x, g):
    return x * lax.rsqrt(jnp.mean(x * x, axis=-1, keepdims=True) + EPS) * g


def _stream_cast_rows(src_hbm, dst_ref, stage_ref, sem, chunk_rows, scale=None):
    n_chunks = src_hbm.shape[0] // chunk_rows

    def copy(i):
        return pltpu.make_async_copy(src_hbm.at[pl.ds(i * chunk_rows, chunk_rows), :],
                                     stage_ref.at[i % 2], sem.at[i % 2])

    copy(0).start()
    for i in range(n_chunks):
        if i + 1 < n_chunks:
            copy(i + 1).start()
        copy(i).wait()
        chunk = stage_ref[i % 2]
        if scale is not None:
            chunk = chunk * scale
        dst_ref[i * chunk_rows:(i + 1) * chunk_rows, :] = _bf16(chunk)


def _log_sigmoid(x):
    return jnp.minimum(x, 0.0) - jnp.log1p(jnp.exp(-jnp.abs(x)))


def _in_proj_lru_kernel(x_ref, g_ref, w_hbm, cw_ref, cb_ref, wg_ref, bg_ref, lru_l_ref, zero_ref,
                        qt_ref, k_ref, vt_ref, lru_ref,
                        w_ref, w_stage, w_sem, h_scr, ug_slab, u_scr, ub_scr, z_scr, gelu_scr,
                        out_slab, tail_ref, carry_ref):
    tm = x_ref.shape[0]
    d_lru = lru_ref.shape[-1]
    n_slab = d_lru // LANES
    n_sub = tm // SUB_ROWS
    d_attn = k_ref.shape[-1]
    f32 = jnp.float32

    @pl.when((pl.program_id(0) == 0) & (pl.program_id(1) == 0))
    def _():
        col = lax.broadcasted_iota(jnp.int32, (1, w_ref.shape[1]), 1)
        q_scale = jnp.where(col < d_attn, HEAD_DIM ** -0.5, 1.0)
        _stream_cast_rows(w_hbm, w_ref, w_stage, w_sem, w_stage.shape[1], scale=q_scale)

    @pl.when(pl.program_id(1) == 0)
    def _():
        tail_ref[...] = jnp.zeros(tail_ref.shape, f32)
        carry_ref[...] = jnp.zeros(carry_ref.shape, f32)

    h_scr[...] = _bf16(_rmsnorm_rows(x_ref[...], g_ref[...]))

    def ug_half(half):
        cols = slice(3 * d_attn + half * d_lru, 3 * d_attn + (half + 1) * d_lru)
        nat = jnp.dot(h_scr[...], w_ref[:, cols], preferred_element_type=f32)
        for l in range(n_slab):
            ug_slab[half * n_slab + l] = nat[:, l * LANES:(l + 1) * LANES]

    ug_half(0)

    def piece(slab, s, c):
        return ug_slab[slab, pl.ds(s * SUB_ROWS + c, SUBLANES, stride=CHAIN), :]

    row = lax.broadcasted_iota(jnp.int32, (SUBLANES, LANES), 0)

    for l in range(n_slab):
        lanes = slice(l * LANES, (l + 1) * LANES)
        cw = cw_ref[:, lanes]
        cb = cb_ref[:, lanes]
        rolled_prev = [pltpu.roll(tail_ref[c, :, lanes], 1, axis=0) for c in range(CHAIN)]
        for s in range(n_sub):
            cur = [piece(l, s, c) for c in range(CHAIN)]
            rolled = [pltpu.roll(cur[c], 1, axis=0) for c in range(CHAIN)]
            shifted = [jnp.where(row == 0, rolled_prev[c], rolled[c]) for c in range(CHAIN)]
            rolled_prev = rolled
            if s == n_sub - 1:
                for c in range(CHAIN):
                    tail_ref[c, :, lanes] = cur[c]
            conv = []
            for c in range(CHAIN):
                acc = cb
                for kk in range(CONV_WIDTH):
                    d = CONV_WIDTH - 1 - kk
                    src = cur[c - d] if c >= d else shifted[c - d + CHAIN]
                    acc = acc + src * cw[kk:kk + 1, :]
                conv.append(acc)
            u_sub = jnp.concatenate(conv, axis=0)
            u_scr[l, s * SUB_ROWS:(s + 1) * SUB_ROWS, :] = u_sub
            ub_scr[s * SUB_ROWS:(s + 1) * SUB_ROWS, lanes] = _bf16(u_sub)

    ug_half(1)
    for l in range(n_slab):
        z = jnp.dot(ub_scr[:, l * LANES:(l + 1) * LANES], wg_ref[l],
                    preferred_element_type=f32) + bg_ref[l]
        z_scr[l] = z[:, :LANES]
        z_scr[n_slab + l] = z[:, LANES:]

    def k_piece(part):
        cols = slice(part * ATTN_BLOCK, (part + 1) * ATTN_BLOCK)
        t = jnp.dot(h_scr[...], w_ref[:, d_attn + part * ATTN_BLOCK:
                                     d_attn + (part + 1) * ATTN_BLOCK], preferred_element_type=f32)
        k_ref[:, cols] = _bf16(t)
        return [t[tm - SUBLANES:, c * LANES:(c + 1) * LANES] for c in range(ATTN_BLOCK // LANES)]

    def transposed_piece(first_col, o_ref, part):
        rows = slice(part * ATTN_BLOCK, (part + 1) * ATTN_BLOCK)
        t = jnp.dot(h_scr[rows, :], w_ref[:, first_col:first_col + d_attn],
                    preferred_element_type=f32)
        o_ref[part] = _bf16(t.T)
        return [t[ATTN_BLOCK - SUBLANES:, c * LANES:(c + 1) * LANES] for c in range(d_attn // LANES)]

    side_work = {0: lambda: k_piece(0), 2: lambda: k_piece(1),
                 5: lambda: transposed_piece(0, qt_ref, 0),
                 8: lambda: transposed_piece(0, qt_ref, 1),
                 11: lambda: transposed_piece(2 * d_attn, vt_ref, 0),
                 13: lambda: transposed_piece(2 * d_attn, vt_ref, 1)}
    assert n_sub > max(side_work) and qt_ref.shape[0] == 2 and d_attn == 2 * ATTN_BLOCK

    c_gelu = math.sqrt(2.0 / math.pi)
    for l in range(n_slab):
        for s in range(n_sub):
            gate = jnp.concatenate([piece(n_slab + l, s, c) for c in range(CHAIN)], axis=0)
            gelu_scr[l, s * SUB_ROWS:(s + 1) * SUB_ROWS, :] = (0.5 * gate) * (1.0 + jnp.tanh(
                gate * (c_gelu + (c_gelu * 0.044715) * (gate * gate))))

    c2_row = (0.5 * LRU_C * math.log2(math.e)) * _log_sigmoid(lru_l_ref[...])

    carry = [carry_ref[:, l * LANES:(l + 1) * LANES] for l in range(n_slab)]
    pending_at = -1
    keep_state = zero_ref[...] == 0
    for s in range(n_sub):
        if s in side_work:
            anchor = functools.reduce(jnp.add, side_work[s]())
            pending_at = s + 1
        base = s * SUB_ROWS
        rows = slice(base, base + SUB_ROWS)
        for l in range(n_slab):
            c2 = c2_row[:, l * LANES:(l + 1) * LANES]
            u = u_scr[l, rows, :]
            t_r = jnp.tanh(z_scr[l, rows, :])
            t_i = jnp.tanh(z_scr[n_slab + l, rows, :])
            log2_a = c2 + c2 * t_r
            a = jnp.exp2(log2_a)
            quarter = (jnp.tanh(log2_a * math.log(2.0)) * -0.25) * (a * a + 1.0)
            half_mult = jnp.where(quarter > 0.0, quarter * lax.rsqrt(quarter), 0.0)
            b = (half_mult * u) * (1.0 + t_i)
            gelu = gelu_scr[l, rows, :]

            a_c = [a[c * SUBLANES:(c + 1) * SUBLANES] for c in range(CHAIN)]
            b_c = [b[c * SUBLANES:(c + 1) * SUBLANES] for c in range(CHAIN)]
            h_loc, p_loc = [b_c[0]], [a_c[0]]
            for c in range(1, CHAIN):
                h_loc.append(a_c[c] * h_loc[c - 1] + b_c[c])
                p_loc.append(a_c[c] * p_loc[c - 1])
            p_cum, h_cum = p_loc[CHAIN - 1], h_loc[CHAIN - 1]
            shift = 1
            while shift < SUBLANES:
                valid = row >= shift
                p_prev = jnp.where(valid, pltpu.roll(p_cum, shift, axis=0), 1.0)
                h_prev = jnp.where(valid, pltpu.roll(h_cum, shift, axis=0), 0.0)
                h_cum = p_cum * h_prev + h_cum
                p_cum = p_cum * p_prev
                shift *= 2
            chain_end = h_cum + p_cum * carry[l]
            chain_in = jnp.where(row == 0, carry[l], pltpu.roll(chain_end, 1, axis=0))
            carry[l] = jnp.broadcast_to(chain_end[SUBLANES - 1:SUBLANES, :], (SUBLANES, LANES))
            if s == pending_at:
                carry[l] = jnp.where(keep_state, carry[l], anchor)
            for c in range(CHAIN):
                out = (h_loc[c] + p_loc[c] * chain_in) * gelu[c * SUBLANES:(c + 1) * SUBLANES]
                out_slab[l, pl.ds(base + c, SUBLANES, stride=CHAIN), :] = out
    for l in range(n_slab):
        carry_ref[:, l * LANES:(l + 1) * LANES] = carry[l]
    lru_ref[...] = _bf16(jnp.concatenate([out_slab[l] for l in range(n_slab)], axis=1))


def _in_proj_lru(x, g, w, conv_w, conv_b, w_gates, b_gates, lru_l):
    B, S, D = x.shape
    d_lru = lru_l.shape[-1]
    d_attn = (w.shape[1] - 2 * d_lru) // 3
    tm = PROJ_TOKENS
    nblk = tm // ATTN_BLOCK
    n_slab = d_lru // LANES
    const = lambda b, i: (0, 0)
    assert D % W_IN_CHUNK_ROWS == 0
    return pl.pallas_call(
        _in_proj_lru_kernel,
        grid=(B, S // tm),
        in_specs=[
            pl.BlockSpec((None, tm, D), lambda b, i: (b, i, 0)),
            pl.BlockSpec((1, D), const),
            pl.BlockSpec(memory_space=pl.ANY),
            pl.BlockSpec(conv_w.shape, const),
            pl.BlockSpec(conv_b.shape, const),
            pl.BlockSpec(w_gates.shape, lambda b, i: (0, 0, 0)),
            pl.BlockSpec(b_gates.shape, lambda b, i: (0, 0, 0)),
            pl.BlockSpec(lru_l.shape, const),
            pl.BlockSpec((SUBLANES, LANES), const),
        ],
        out_specs=[
            pl.BlockSpec((None, nblk, d_attn, ATTN_BLOCK), lambda b, i: (b, i, 0, 0)),
            pl.BlockSpec((None, tm, d_attn), lambda b, i: (b, i, 0)),
            pl.BlockSpec((None, nblk, d_attn, ATTN_BLOCK), lambda b, i: (b, i, 0, 0)),
            pl.BlockSpec((None, tm, d_lru), lambda b, i: (b, i, 0)),
        ],
        out_shape=[
            jax.ShapeDtypeStruct((B, S // ATTN_BLOCK, d_attn, ATTN_BLOCK), jnp.bfloat16),
            jax.ShapeDtypeStruct((B, S, d_attn), jnp.bfloat16),
            jax.ShapeDtypeStruct((B, S // ATTN_BLOCK, d_attn, ATTN_BLOCK), jnp.bfloat16),
            jax.ShapeDtypeStruct((B, S, d_lru), jnp.bfloat16),
        ],
        scratch_shapes=[
            pltpu.VMEM(w.shape, jnp.bfloat16),
            pltpu.VMEM((2, W_IN_CHUNK_ROWS, w.shape[1]), jnp.float32),
            pltpu.SemaphoreType.DMA((2,)),
            pltpu.VMEM((tm, D), jnp.bfloat16),
            pltpu.VMEM((2 * n_slab, tm, LANES), jnp.float32),
            pltpu.VMEM((n_slab, tm, LANES), jnp.float32),
            pltpu.VMEM((tm, d_lru), jnp.bfloat16),
            pltpu.VMEM((2 * n_slab, tm, LANES), jnp.float32),
            pltpu.VMEM((n_slab, tm, LANES), jnp.float32),
            pltpu.VMEM((n_slab, tm, LANES), jnp.float32),
            pltpu.VMEM((CHAIN, SUBLANES, d_lru), jnp.float32),
            pltpu.VMEM((SUBLANES, d_lru), jnp.float32),
        ],
        compiler_params=pltpu.CompilerParams(
            dimension_semantics=("arbitrary", "arbitrary"),
            vmem_limit_bytes=V7X_VMEM_LIMIT_BYTES),
        name="in_proj_lru",
    )(x, g, w, conv_w, conv_b, w_gates, b_gates, lru_l, jnp.zeros((SUBLANES, LANES), jnp.int32))


N_POS_COLS = 128
N_ONES_ROWS = 16
SCORE_LEAD = 2
N_SCORE_BUFS = SCORE_LEAD + 1
ATTN_Q_PER_STEP = 9


def _diff_attn_kernel(qt_ref, k_ref, pos_ref, vt_ref, lq1_ref, lk1_ref, lq2_ref, lk2_ref, sg_ref,
                      o_ref,
                      s_ref, m_ref, acc_ref, *, slopes, q_per_step):
    h = pl.program_id(1)
    blk = ATTN_BLOCK
    n_blocks = vt_ref.shape[0]

    slope = jnp.float32(slopes[0])
    for i in range(1, len(slopes)):
        slope = jnp.where(h == i, jnp.float32(slopes[i]), slope)

    r = lax.broadcasted_iota(jnp.int32, (N_POS_COLS, blk), 0)
    lane = lax.broadcasted_iota(jnp.int32, (N_POS_COLS, blk), 1).astype(jnp.float32)
    zeros = jnp.zeros((HEAD_DIM, blk), qt_ref.dtype)
    key_in_blk = lax.broadcasted_iota(jnp.int32, (blk, 2 * blk), 0)
    qry_in_blk = lax.broadcasted_iota(jnp.int32, (blk, 2 * blk), 1) % blk
    causal = key_in_blk <= qry_in_blk
    ones_row = lax.broadcasted_iota(jnp.int32, (N_ONES_ROWS, blk), 0) == 0
    ones_blk = jnp.where(ones_row, 1.0, 0.0).astype(jnp.bfloat16)

    def q_operand(qt, q_index):
        alibi = _bf16(jnp.where(r == 0, slope,
                      jnp.where(r == 1, slope * blk,
                      jnp.where(r == 2, -slope * lane,
                      jnp.where(r == 3, -slope * float(q_index * blk), 0.0)))))
        return jnp.concatenate(
            [jnp.concatenate([qt[:HEAD_DIM], zeros, alibi], axis=0),
             jnp.concatenate([zeros, qt[HEAD_DIM:], alibi], axis=0)], axis=1)

    def score_block(q_aug, j, diagonal):
        rows = slice(j * blk, (j + 1) * blk)
        k_aug = jnp.concatenate([k_ref[rows, :], pos_ref[rows, :]], axis=1)
        s = jnp.dot(k_aug, q_aug, preferred_element_type=jnp.float32)
        return jnp.where(causal, s, -jnp.inf) if diagonal else s

    def finish(t):
        acc = acc_ref[...]
        lam = (jnp.exp(jnp.sum(lq1_ref[...] * lk1_ref[...], axis=-1, keepdims=True))
               - jnp.exp(jnp.sum(lq2_ref[...] * lk2_ref[...], axis=-1, keepdims=True))
               + LAMBDA_INIT)
        heads = acc[:V_HEAD_DIM] / acc[V_HEAD_DIM:V_HEAD_DIM + 1]
        o = heads[:, :blk] - lam * heads[:, blk:]
        o = o * lax.rsqrt(jnp.mean(o * o, axis=0, keepdims=True) + EPS)
        o_ref[t * blk:(t + 1) * blk, :] = _bf16(o.T * (sg_ref[...] * (1.0 - LAMBDA_INIT)))

    order = list(range(0, n_blocks, 2)) + list(range(n_blocks - 1 - n_blocks % 2, 0, -2))
    assert sorted(order) == list(range(n_blocks))

    def score_q_block(q_aug, q_index, slot, j, m_run):
        s = score_block(q_aug, j, j == q_index)
        s_ref[slot, j * blk:(j + 1) * blk, :] = s
        col_max = jnp.max(s, axis=0, keepdims=True)
        return col_max if m_run is None else jnp.maximum(m_run, col_max)

    def sub_step(t):
        if t >= 1:
            finish(order[t - 1])
        if t == n_blocks:
            return
        if t == 0:
            for pos in range(min(SCORE_LEAD, n_blocks)):
                q_index = order[pos]
                q_aug = q_operand(qt_ref[q_index], q_index)
                m_run = None
                for j in range(q_index + 1):
                    m_run = score_q_block(q_aug, q_index, pos % N_SCORE_BUFS, j, m_run)
                m_ref[pos % N_SCORE_BUFS] = m_run
        cur = t % N_SCORE_BUFS
        q_cur = order[t]
        scoring = t + SCORE_LEAD < n_blocks
        q_ahead = order[t + SCORE_LEAD] if scoring else -1
        slot_ahead = (t + SCORE_LEAD) % N_SCORE_BUFS
        q_aug = q_operand(qt_ref[q_ahead], q_ahead) if scoring else None
        m_cur = m_ref[cur]
        m_run = None
        acc = jnp.zeros((V_HEAD_DIM + N_ONES_ROWS, 2 * blk), jnp.float32)
        for j in range(max(q_cur, q_ahead) + 1):
            if j <= q_ahead:
                m_run = score_q_block(q_aug, q_ahead, slot_ahead, j, m_run)
            if j <= q_cur:
                p = _bf16(jnp.exp(s_ref[cur, j * blk:(j + 1) * blk, :] - m_cur))
                vt_aug = jnp.concatenate([vt_ref[j], ones_blk], axis=0)
                acc = acc + jnp.dot(vt_aug, p, preferred_element_type=jnp.float32)
        if scoring:
            m_ref[slot_ahead] = m_run
        acc_ref[...] = acc

    def grid_step(first):
        def run():
            for t in range(first, min(first + q_per_step, n_blocks + 1)):
                sub_step(t)
        return run

    steps = [grid_step(first) for first in range(0, n_blocks + 1, q_per_step)]
    if len(steps) == 1:
        steps[0]()
    else:
        lax.switch(pl.program_id(2), steps)


def _alibi_slopes(n_heads):
    slopes = [2.0 ** (-8.0 * (i + 1) / n_heads) for i in range(n_heads)]
    assert all(math.frexp(s)[0] == 0.5 for s in slopes), slopes
    return slopes


def _key_position_columns(seq_len):
    j = np.arange(seq_len)
    pos = np.zeros((seq_len, N_POS_COLS), np.float32)
    pos[:, 0] = j % ATTN_BLOCK
    pos[:, 1] = j // ATTN_BLOCK
    pos[:, 2:4] = 1.0
    assert ATTN_BLOCK <= 256 and seq_len // ATTN_BLOCK <= 256
    return jnp.asarray(pos, jnp.bfloat16)


def _diff_attn(qt, k, vt, lambdas, subln_g):
    B, nblk, d_attn, blk = qt.shape
    S = k.shape[1]
    H = d_attn // V_HEAD_DIM
    pos = _key_position_columns(S)
    kernel = functools.partial(_diff_attn_kernel, slopes=tuple(_alibi_slopes(H)),
                               q_per_step=ATTN_Q_PER_STEP)
    n_steps = pl.cdiv(nblk + 1, ATTN_Q_PER_STEP)
    return pl.pallas_call(
        kernel,
        grid=(B, H, n_steps),
        in_specs=[
            pl.BlockSpec((None, nblk, V_HEAD_DIM, blk), lambda b, h, q: (b, 0, h, 0)),
            pl.BlockSpec((None, S, V_HEAD_DIM), lambda b, h, q: (b, 0, h)),
            pl.BlockSpec(pos.shape, lambda b, h, q: (0, 0)),
            pl.BlockSpec((None, nblk, V_HEAD_DIM, blk), lambda b, h, q: (b, 0, h, 0)),
            *[pl.BlockSpec(v.shape, lambda b, h, q: (0, 0)) for v in lambdas],
            pl.BlockSpec(subln_g.shape, lambda b, h, q: (0, 0)),
        ],
        out_specs=pl.BlockSpec((None, S, V_HEAD_DIM), lambda b, h, q: (b, 0, h)),
        out_shape=jax.ShapeDtypeStruct((B, S, d_attn), jnp.bfloat16),
        scratch_shapes=[pltpu.VMEM((N_SCORE_BUFS, S, 2 * blk), jnp.float32),
                        pltpu.VMEM((N_SCORE_BUFS, 1, 2 * blk), jnp.float32),
                        pltpu.VMEM((V_HEAD_DIM + N_ONES_ROWS, 2 * blk), jnp.float32)],
        compiler_params=pltpu.CompilerParams(
            dimension_semantics=("parallel", "parallel", "arbitrary"),
            vmem_limit_bytes=V7X_VMEM_LIMIT_BYTES),
        name="diff_attn",
    )(qt, k, pos, vt, *lambdas, subln_g)


def _out_mlp_kernel(x_ref, attn_ref, lru_ref, wo_hbm, gm_ref, wup_hbm, wdn_hbm, gf_ref, y_ref,
                    wo_ref, wup_ref, wdn_ref, stage_ref, sem):
    tm = x_ref.shape[0]
    d_ff = wup_ref.shape[1]
    n_chunks = d_ff // FF_CHUNK
    rows = [slice(g * tm // MLP_SPLIT, (g + 1) * tm // MLP_SPLIT) for g in range(MLP_SPLIT)]

    def ff(c):
        return pl.ds(c * FF_CHUNK, FF_CHUNK)

    sources = [wo_hbm]
    targets = [lambda v: wo_ref.__setitem__((slice(None), slice(None)), v)]
    for c in range(n_chunks):
        sources += [wup_hbm.at[:, ff(c)], wdn_hbm.at[ff(c), :]]
        targets += [lambda v, c=c: wup_ref.__setitem__((slice(None), ff(c)), v),
                    lambda v, c=c: wdn_ref.__setitem__((ff(c), slice(None)), v)]

    def copy(i):
        return pltpu.make_async_copy(sources[i], stage_ref.at[i % 2], sem.at[i % 2])

    def land(i):
        copy(i).wait()
        targets[i](_bf16(stage_ref[i % 2]))
        if i + 2 < len(sources):
            copy(i + 2).start()

    def step_body(first_step):
        if first_step:
            copy(0).start()
            copy(1).start()
            land(0)
        x1 = []
        for r in rows:
            mix = jnp.concatenate([attn_ref[r, :], lru_ref[r, :]], axis=-1)
            x1.append(x_ref[r, :] + jnp.dot(mix, wo_ref[...], preferred_element_type=jnp.float32))
        hm = [_bf16(_rmsnorm_rows(v, gm_ref[...])) for v in x1]
        mlp = [jnp.zeros_like(v) for v in x1]
        for c in range(n_chunks):
            cols = slice(c * FF_CHUNK, (c + 1) * FF_CHUNK)
            if first_step:
                land(1 + 2 * c)
            act = [_bf16(jnp.square(jnp.maximum(
                jnp.dot(hm[g], wup_ref[:, cols], preferred_element_type=jnp.float32), 0.0)))
                for g in range(MLP_SPLIT)]
            if first_step:
                land(2 + 2 * c)
            for g in range(MLP_SPLIT):
                mlp[g] = mlp[g] + jnp.dot(act[g], wdn_ref[cols, :],
                                          preferred_element_type=jnp.float32)
        for g, r in enumerate(rows):
            y_ref[r, :] = _rmsnorm_rows(x1[g] + mlp[g], gf_ref[...])

    pl.when(pl.program_id(0) == 0)(lambda: step_body(True))
    pl.when(pl.program_id(0) != 0)(lambda: step_body(False))


def _out_mlp(x, attn, lru, w_out, g_mlp, w_up, w_down, g_final):
    B, S, D = x.shape
    T = B * S
    tm = MLP_TOKENS
    x2 = x.reshape(T, D)
    attn2 = attn.reshape(T, attn.shape[-1])
    lru2 = lru.reshape(T, lru.shape[-1])
    const = lambda i: (0, 0)
    in_hbm = pl.BlockSpec(memory_space=pl.ANY)
    assert w_out.shape == (D, D) and FF_CHUNK == D and w_up.shape[1] % FF_CHUNK == 0
    y = pl.pallas_call(
        _out_mlp_kernel,
        grid=(T // tm,),
        in_specs=[
            pl.BlockSpec((tm, D), lambda i: (i, 0)),
            pl.BlockSpec((tm, attn2.shape[1]), lambda i: (i, 0)),
            pl.BlockSpec((tm, lru2.shape[1]), lambda i: (i, 0)),
            in_hbm,
            pl.BlockSpec(g_mlp.shape, const),
            in_hbm,
            in_hbm,
            pl.BlockSpec(g_final.shape, const),
        ],
        out_specs=pl.BlockSpec((tm, D), lambda i: (i, 0)),
        out_shape=jax.ShapeDtypeStruct((T, D), jnp.float32),
        scratch_shapes=[
            pltpu.VMEM(w_out.shape, jnp.bfloat16),
            pltpu.VMEM(w_up.shape, jnp.bfloat16),
            pltpu.VMEM(w_down.shape, jnp.bfloat16),
            pltpu.VMEM((2, D, FF_CHUNK), jnp.float32),
            pltpu.SemaphoreType.DMA((2,)),
        ],
        compiler_params=pltpu.CompilerParams(
            dimension_semantics=("arbitrary",),
            vmem_limit_bytes=V7X_VMEM_LIMIT_BYTES),
        name="out_mlp",
    )(x2, attn2, lru2, w_out, g_mlp, w_up, w_down, g_final)
    return y.reshape(B, S, D)


def _pair_block_diag(w):
    n, c, d = w.shape
    rows = w.reshape(n // 2, 2 * c, d)
    tiled = jnp.concatenate([rows, rows], axis=-1)
    row_block = np.arange(2 * c)[:, None] // c
    col_block = np.arange(2 * d)[None, :] // d
    return jnp.where(jnp.asarray(row_block == col_block), tiled, 0.0)


def kernel(x, norm_mix_g, w_in, conv_w, conv_b, w_rg, b_rg, w_ig, b_ig, lru_L, lambda_q1, lambda_k1, lambda_q2, lambda_k2, subln_g, w_out, norm_mlp_g, w_up, w_down, final_g):
    B, S, D = x.shape
    d_lru = lru_L.shape[-1]
    d_attn = (w_in.shape[-1] - 2 * d_lru) // 3
    depth = w_in.shape[0]
    n_slab = d_lru // LANES
    assert depth == 1, "LAMBDA_INIT is specialised to a single layer"
    assert S % PROJ_TOKENS == 0 and (B * S) % MLP_TOKENS == 0
    assert PROJ_TOKENS % SUB_ROWS == 0 and d_lru % LANES == 0
    assert 2 * w_rg.shape[-1] == LANES and w_rg.shape[1] * w_rg.shape[2] == d_lru
    l = 0
    row = lambda v: v.reshape(1, -1)

    w_gates = _bf16(0.5 * jnp.concatenate([_pair_block_diag(w_rg[l]), _pair_block_diag(w_ig[l])],
                                          axis=-1))
    b_gates = 0.5 * jnp.concatenate([b_rg[l].reshape(n_slab, 1, LANES),
                                     b_ig[l].reshape(n_slab, 1, LANES)], axis=-1)
    lambdas = [row(v[l]) for v in (lambda_q1, lambda_k1, lambda_q2, lambda_k2)]

    qt, k, vt, lru = _in_proj_lru(x, row(norm_mix_g[l]), w_in[l], conv_w[l], row(conv_b[l]),
                                  w_gates, b_gates, row(lru_L[l]))
    attn = _diff_attn(qt, k, vt, lambdas, row(subln_g[l]))
    return _out_mlp(x, attn, lru, w_out[l], row(norm_mlp_g[l]), w_up[l], w_down[l], row(final_g))
```

```python
import functools
import math

import numpy as np
import jax
import jax.numpy as jnp
from jax import lax
from jax.experimental import pallas as pl
from jax.experimental.pallas import tpu as pltpu

EPS = 1e-6
CONV_WIDTH = 4
LRU_C = 8.0
HEAD_DIM = 64
V_HEAD_DIM = 2 * HEAD_DIM
LAMBDA_INIT = 0.8 - 0.6 * math.exp(-0.3 * 0)

V7X_MXU_COLS = 256
LANES = 128
SUBLANES = 8
V7X_VMEM_LIMIT_BYTES = 56 * 1024 * 1024

ATTN_BLOCK = V7X_MXU_COLS
PROJ_TOKENS = 512
MLP_TOKENS = 512
FF_CHUNK = 1024
MLP_SPLIT = 2
W_IN_CHUNK_ROWS = 128
CHAIN = 4
SUB_ROWS = SUBLANES * CHAIN


def _bf16(x):
    return x.astype(jnp.bfloat16)


def _rmsnorm_rows(x, g):
    return x * lax.rsqrt(jnp.mean(x * x, axis=-1, keepdims=True) + EPS) * g


def _stream_cast_rows(src_hbm, dst_ref, stage_ref, sem, chunk_rows, scale=None):
    n_chunks = src_hbm.shape[0] // chunk_rows

    def copy(i):
        return pltpu.make_async_copy(src_hbm.at[pl.ds(i * chunk_rows, chunk_rows), :],
                                     stage_ref.at[i % 2], sem.at[i % 2])

    copy(0).start()
    for i in range(n_chunks):
        if i + 1 < n_chunks:
            copy(i + 1).start()
        copy(i).wait()
        chunk = stage_ref[i % 2]
        if scale is not None:
            chunk = chunk * scale
        dst_ref[i * chunk_rows:(i + 1) * chunk_rows, :] = _bf16(chunk)


def _log_sigmoid(x):
    return jnp.minimum(x, 0.0) - jnp.log1p(jnp.exp(-jnp.abs(x)))


def _in_proj_lru_kernel(x_ref, g_ref, w_hbm, cw_ref, cb_ref, wg_ref, bg_ref, lru_l_ref, zero_ref,
                        qt_ref, k_ref, vt_ref, lru_ref,
                        w_ref, w_stage, w_sem, h_scr, ug_slab, u_scr, ub_scr, z_scr, gelu_scr,
                        out_slab, tail_ref, carry_ref):
    tm = x_ref.shape[0]
    d_lru = lru_ref.shape[-1]
    n_slab = d_lru // LANES
    n_sub = tm // SUB_ROWS
    d_attn = k_ref.shape[-1]
    f32 = jnp.float32

    @pl.when((pl.program_id(0) == 0) & (pl.program_id(1) == 0))
    def _():
        col = lax.broadcasted_iota(jnp.int32, (1, w_ref.shape[1]), 1)
        q_scale = jnp.where(col < d_attn, HEAD_DIM ** -0.5, 1.0)
        _stream_cast_rows(w_hbm, w_ref, w_stage, w_sem, w_stage.shape[1], scale=q_scale)

    @pl.when(pl.program_id(1) == 0)
    def _():
        tail_ref[...] = jnp.zeros(tail_ref.shape, f32)
        carry_ref[...] = jnp.zeros(carry_ref.shape, f32)

    h_scr[...] = _bf16(_rmsnorm_rows(x_ref[...], g_ref[...]))

    def ug_half(half):
        cols = slice(3 * d_attn + half * d_lru, 3 * d_attn + (half + 1) * d_lru)
        nat = jnp.dot(h_scr[...], w_ref[:, cols], preferred_element_type=f32)
        for l in range(n_slab):
            ug_slab[half * n_slab + l] = nat[:, l * LANES:(l + 1) * LANES]

    ug_half(0)

    def piece(slab, s, c):
        return ug_slab[slab, pl.ds(s * SUB_ROWS + c, SUBLANES, stride=CHAIN), :]

    row = lax.broadcasted_iota(jnp.int32, (SUBLANES, LANES), 0)

    for l in range(n_slab):
        lanes = slice(l * LANES, (l + 1) * LANES)
        cw = cw_ref[:, lanes]
        cb = cb_ref[:, lanes]
        rolled_prev = [pltpu.roll(tail_ref[c, :, lanes], 1, axis=0) for c in range(CHAIN)]
        for s in range(n_sub):
            cur = [piece(l, s, c) for c in range(CHAIN)]
            rolled = [pltpu.roll(cur[c], 1, axis=0) for c in range(CHAIN)]
            shifted = [jnp.where(row == 0, rolled_prev[c], rolled[c]) for c in range(CHAIN)]
            rolled_prev = rolled
            if s == n_sub - 1:
                for c in range(CHAIN):
                    tail_ref[c, :, lanes] = cur[c]
            conv = []
            for c in range(CHAIN):
                acc = cb
                for kk in range(CONV_WIDTH):
                    d = CONV_WIDTH - 1 - kk
                    src = cur[c - d] if c >= d else shifted[c - d + CHAIN]
                    acc = acc + src * cw[kk:kk + 1, :]
                conv.append(acc)
            u_sub = jnp.concatenate(conv, axis=0)
            u_scr[l, s * SUB_ROWS:(s + 1) * SUB_ROWS, :] = u_sub
            ub_scr[s * SUB_ROWS:(s + 1) * SUB_ROWS, lanes] = _bf16(u_sub)

    ug_half(1)
    for l in range(n_slab):
        z = jnp.dot(ub_scr[:, l * LANES:(l + 1) * LANES], wg_ref[l],
                    preferred_element_type=f32) + bg_ref[l]
        z_scr[l] = z[:, :LANES]
        z_scr[n_slab + l] = z[:, LANES:]

    def k_piece(part):
        cols = slice(part * ATTN_BLOCK, (part + 1) * ATTN_BLOCK)
        t = jnp.dot(h_scr[...], w_ref[:, d_attn + part * ATTN_BLOCK:
                                     d_attn + (part + 1) * ATTN_BLOCK], preferred_element_type=f32)
        k_ref[:, cols] = _bf16(t)
        return [t[tm - SUBLANES:, c * LANES:(c + 1) * LANES] for c in range(ATTN_BLOCK // LANES)]

    def transposed_piece(first_col, o_ref, part):
        rows = slice(part * ATTN_BLOCK, (part + 1) * ATTN_BLOCK)
        t = jnp.dot(h_scr[rows, :], w_ref[:, first_col:first_col + d_attn],
                    preferred_element_type=f32)
        o_ref[part] = _bf16(t.T)
        return [t[ATTN_BLOCK - SUBLANES:, c * LANES:(c + 1) * LANES] for c in range(d_attn // LANES)]

    side_work = {0: lambda: k_piece(0), 2: lambda: k_piece(1),
                 5: lambda: transposed_piece(0, qt_ref, 0),
                 8: lambda: transposed_piece(0, qt_ref, 1),
                 11: lambda: transposed_piece(2 * d_attn, vt_ref, 0),
                 13: lambda: transposed_piece(2 * d_attn, vt_ref, 1)}
    assert n_sub > max(side_work) and qt_ref.shape[0] == 2 and d_attn == 2 * ATTN_BLOCK

    c_gelu = math.sqrt(2.0 / math.pi)
    for l in range(n_slab):
        for s in range(n_sub):
            gate = jnp.concatenate([piece(n_slab + l, s, c) for c in range(CHAIN)], axis=0)
            gelu_scr[l, s * SUB_ROWS:(s + 1) * SUB_ROWS, :] = (0.5 * gate) * (1.0 + jnp.tanh(
                gate * (c_gelu + (c_gelu * 0.044715) * (gate * gate))))

    c2_row = (0.5 * LRU_C * math.log2(math.e)) * _log_sigmoid(lru_l_ref[...])

    carry = [carry_ref[:, l * LANES:(l + 1) * LANES] for l in range(n_slab)]
    pending_at = -1
    keep_state = zero_ref[...] == 0
    for s in range(n_sub):
        if s in side_work:
            anchor = functools.reduce(jnp.add, side_work[s]())
            pending_at = s + 1
        base = s * SUB_ROWS
        rows = slice(base, base + SUB_ROWS)
        for l in range(n_slab):
            c2 = c2_row[:, l * LANES:(l + 1) * LANES]
            u = u_scr[l, rows, :]
            t_r = jnp.tanh(z_scr[l, rows, :])
            t_i = jnp.tanh(z_scr[n_slab + l, rows, :])
            log2_a = c2 + c2 * t_r
            a = jnp.exp2(log2_a)
            quarter = (jnp.tanh(log2_a * math.log(2.0)) * -0.25) * (a * a + 1.0)
            half_mult = jnp.where(quarter > 0.0, quarter * lax.rsqrt(quarter), 0.0)
            b = (half_mult * u) * (1.0 + t_i)
            gelu = gelu_scr[l, rows, :]

            a_c = [a[c * SUBLANES:(c + 1) * SUBLANES] for c in range(CHAIN)]
            b_c = [b[c * SUBLANES:(c + 1) * SUBLANES] for c in range(CHAIN)]
            h_loc, p_loc = [b_c[0]], [a_c[0]]
            for c in range(1, CHAIN):
                h_loc.append(a_c[c] * h_loc[c - 1] + b_c[c])
                p_loc.append(a_c[c] * p_loc[c - 1])
            p_cum, h_cum = p_loc[CHAIN - 1], h_loc[CHAIN - 1]
            shift = 1
            while shift < SUBLANES:
                valid = row >= shift
                p_prev = jnp.where(valid, pltpu.roll(p_cum, shift, axis=0), 1.0)
                h_prev = jnp.where(valid, pltpu.roll(h_cum, shift, axis=0), 0.0)
                h_cum = p_cum * h_prev + h_cum
                p_cum = p_cum * p_prev
                shift *= 2
            chain_end = h_cum + p_cum * carry[l]
            chain_in = jnp.where(row == 0, carry[l], pltpu.roll(chain_end, 1, axis=0))
            carry[l] = jnp.broadcast_to(chain_end[SUBLANES - 1:SUBLANES, :], (SUBLANES, LANES))
            if s == pending_at:
                carry[l] = jnp.where(keep_state, carry[l], anchor)
            for c in range(CHAIN):
                out = (h_loc[c] + p_loc[c] * chain_in) * gelu[c * SUBLANES:(c + 1) * SUBLANES]
                out_slab[l, pl.ds(base + c, SUBLANES, stride=CHAIN), :] = out
    for l in range(n_slab):
        carry_ref[:, l * LANES:(l + 1) * LANES] = carry[l]
    lru_ref[...] = _bf16(jnp.concatenate([out_slab[l] for l in range(n_slab)], axis=1))


def _in_proj_lru(x, g, w, conv_w, conv_b, w_gates, b_gates, lru_l):
    B, S, D = x.shape
    d_lru = lru_l.shape[-1]
    d_attn = (w.shape[1] - 2 * d_lru) // 3
    tm = PROJ_TOKENS
    nblk = tm // ATTN_BLOCK
    n_slab = d_lru // LANES
    const = lambda b, i: (0, 0)
    assert D % W_IN_CHUNK_ROWS == 0
    return pl.pallas_call(
        _in_proj_lru_kernel,
        grid=(B, S // tm),
        in_specs=[
            pl.BlockSpec((None, tm, D), lambda b, i: (b, i, 0)),
            pl.BlockSpec((1, D), const),
            pl.BlockSpec(memory_space=pl.ANY),
            pl.BlockSpec(conv_w.shape, const),
            pl.BlockSpec(conv_b.shape, const),
            pl.BlockSpec(w_gates.shape, lambda b, i: (0, 0, 0)),
            pl.BlockSpec(b_gates.shape, lambda b, i: (0, 0, 0)),
            pl.BlockSpec(lru_l.shape, const),
            pl.BlockSpec((SUBLANES, LANES), const),
        ],
        out_specs=[
            pl.BlockSpec((None, nblk, d_attn, ATTN_BLOCK), lambda b, i: (b, i, 0, 0)),
            pl.BlockSpec((None, tm, d_attn), lambda b, i: (b, i, 0)),
            pl.BlockSpec((None, nblk, d_attn, ATTN_BLOCK), lambda b, i: (b, i, 0, 0)),
            pl.BlockSpec((None, tm, d_lru), lambda b, i: (b, i, 0)),
        ],
        out_shape=[
            jax.ShapeDtypeStruct((B, S // ATTN_BLOCK, d_attn, ATTN_BLOCK), jnp.bfloat16),
            jax.ShapeDtypeStruct((B, S, d_attn), jnp.bfloat16),
            jax.ShapeDtypeStruct((B, S // ATTN_BLOCK, d_attn, ATTN_BLOCK), jnp.bfloat16),
            jax.ShapeDtypeStruct((B, S, d_lru), jnp.bfloat16),
        ],
        scratch_shapes=[
            pltpu.VMEM(w.shape, jnp.bfloat16),
            pltpu.VMEM((2, W_IN_CHUNK_ROWS, w.shape[1]), jnp.float32),
            pltpu.SemaphoreType.DMA((2,)),
            pltpu.VMEM((tm, D), jnp.bfloat16),
            pltpu.VMEM((2 * n_slab, tm, LANES), jnp.float32),
            pltpu.VMEM((n_slab, tm, LANES), jnp.float32),
            pltpu.VMEM((tm, d_lru), jnp.bfloat16),
            pltpu.VMEM((2 * n_slab, tm, LANES), jnp.float32),
            pltpu.VMEM((n_slab, tm, LANES), jnp.float32),
            pltpu.VMEM((n_slab, tm, LANES), jnp.float32),
            pltpu.VMEM((CHAIN, SUBLANES, d_lru), jnp.float32),
            pltpu.VMEM((SUBLANES, d_lru), jnp.float32),
        ],
        compiler_params=pltpu.CompilerParams(
            dimension_semantics=("arbitrary", "arbitrary"),
            vmem_limit_bytes=V7X_VMEM_LIMIT_BYTES),
        name="in_proj_lru",
    )(x, g, w, conv_w, conv_b, w_gates, b_gates, lru_l, jnp.zeros((SUBLANES, LANES), jnp.int32))


N_POS_COLS = 128
N_ONES_ROWS = 16
SCORE_LEAD = 3
N_SCORE_BUFS = SCORE_LEAD + 1
ATTN_Q_PER_STEP = 9


def _diff_attn_kernel(qt_ref, k_ref, pos_ref, vt_ref, lq1_ref, lk1_ref, lq2_ref, lk2_ref, sg_ref,
                      o_ref,
                      s_ref, m_ref, acc_ref, *, slopes, q_per_step):
    h = pl.program_id(1)
    blk = ATTN_BLOCK
    n_blocks = vt_ref.shape[0]

    slope = jnp.float32(slopes[0])
    for i in range(1, len(slopes)):
        slope = jnp.where(h == i, jnp.float32(slopes[i]), slope)

    r = lax.broadcasted_iota(jnp.int32, (N_POS_COLS, blk), 0)
    lane = lax.broadcasted_iota(jnp.int32, (N_POS_COLS, blk), 1).astype(jnp.float32)
    zeros = jnp.zeros((HEAD_DIM, blk), qt_ref.dtype)
    key_in_blk = lax.broadcasted_iota(jnp.int32, (blk, 2 * blk), 0)
    qry_in_blk = lax.broadcasted_iota(jnp.int32, (blk, 2 * blk), 1) % blk
    causal = key_in_blk <= qry_in_blk
    ones_row = lax.broadcasted_iota(jnp.int32, (N_ONES_ROWS, blk), 0) == 0
    ones_blk = jnp.where(ones_row, 1.0, 0.0).astype(jnp.bfloat16)

    def q_operand(qt, q_index):
        alibi = _bf16(jnp.where(r == 0, slope,
                      jnp.where(r == 1, slope * blk,
                      jnp.where(r == 2, -slope * lane,
                      jnp.where(r == 3, -slope * float(q_index * blk), 0.0)))))
        return jnp.concatenate(
            [jnp.concatenate([qt[:HEAD_DIM], zeros, alibi], axis=0),
             jnp.concatenate([zeros, qt[HEAD_DIM:], alibi], axis=0)], axis=1)

    def score_block(q_aug, j, diagonal):
        rows = slice(j * blk, (j + 1) * blk)
        k_aug = jnp.concatenate([k_ref[rows, :], pos_ref[rows, :]], axis=1)
        s = jnp.dot(k_aug, q_aug, preferred_element_type=jnp.float32)
        return jnp.where(causal, s, -jnp.inf) if diagonal else s

    def finish(t):
        acc = acc_ref[...]
        lam = (jnp.exp(jnp.sum(lq1_ref[...] * lk1_ref[...], axis=-1, keepdims=True))
               - jnp.exp(jnp.sum(lq2_ref[...] * lk2_ref[...], axis=-1, keepdims=True))
               + LAMBDA_INIT)
        heads = acc[:V_HEAD_DIM] / acc[V_HEAD_DIM:V_HEAD_DIM + 1]
        o = heads[:, :blk] - lam * heads[:, blk:]
        o = o * lax.rsqrt(jnp.mean(o * o, axis=0, keepdims=True) + EPS)
        o_ref[t * blk:(t + 1) * blk, :] = _bf16(o.T * (sg_ref[...] * (1.0 - LAMBDA_INIT)))

    order = list(range(0, n_blocks, 2)) + list(range(n_blocks - 1 - n_blocks % 2, 0, -2))
    assert sorted(order) == list(range(n_blocks))

    def score_q_block(q_aug, q_index, slot, j, m_run):
        s = score_block(q_aug, j, j == q_index)
        s_ref[slot, j * blk:(j + 1) * blk, :] = s
        col_max = jnp.max(s, axis=0, keepdims=True)
        return col_max if m_run is None else jnp.maximum(m_run, col_max)

    def sub_step(t):
        if t >= 1:
            finish(order[t - 1])
        if t == n_blocks:
            return
        if t == 0:
            for pos in range(min(SCORE_LEAD, n_blocks)):
                q_index = order[pos]
                q_aug = q_operand(qt_ref[q_index], q_index)
                m_run = None
                for j in range(q_index + 1):
                    m_run = score_q_block(q_aug, q_index, pos % N_SCORE_BUFS, j, m_run)
                m_ref[pos % N_SCORE_BUFS] = m_run
        cur = t % N_SCORE_BUFS
        q_cur = order[t]
        scoring = t + SCORE_LEAD < n_blocks
        q_ahead = order[t + SCORE_LEAD] if scoring else -1
        slot_ahead = (t + SCORE_LEAD) % N_SCORE_BUFS
        q_aug = q_operand(qt_ref[q_ahead], q_ahead) if scoring else None
        m_cur = m_ref[cur]
        m_run = None
        acc = jnp.zeros((V_HEAD_DIM + N_ONES_ROWS, 2 * blk), jnp.float32)
        for j in range(max(q_cur, q_ahead) + 1):
            if j <= q_ahead:
                m_run = score_q_block(q_aug, q_ahead, slot_ahead, j, m_run)
            if j <= q_cur:
                p = _bf16(jnp.exp(s_ref[cur, j * blk:(j + 1) * blk, :] - m_cur))
                vt_aug = jnp.concatenate([vt_ref[j], ones_blk], axis=0)
                acc = acc + jnp.dot(vt_aug, p, preferred_element_type=jnp.float32)
        if scoring:
            m_ref[slot_ahead] = m_run
        acc_ref[...] = acc

    def grid_step(first):
        def run():
            for t in range(first, min(first + q_per_step, n_blocks + 1)):
                sub_step(t)
        return run

    steps = [grid_step(first) for first in range(0, n_blocks + 1, q_per_step)]
    if len(steps) == 1:
        steps[0]()
    else:
        lax.switch(pl.program_id(2), steps)


def _alibi_slopes(n_heads):
    slopes = [2.0 ** (-8.0 * (i + 1) / n_heads) for i in range(n_heads)]
    assert all(math.frexp(s)[0] == 0.5 for s in slopes), slopes
    return slopes


def _key_position_columns(seq_len):
    j = np.arange(seq_len)
    pos = np.zeros((seq_len, N_POS_COLS), np.float32)
    pos[:, 0] = j % ATTN_BLOCK
    pos[:, 1] = j // ATTN_BLOCK
    pos[:, 2:4] = 1.0
    assert ATTN_BLOCK <= 256 and seq_len // ATTN_BLOCK <= 256
    return jnp.asarray(pos, jnp.bfloat16)


def _diff_attn(qt, k, vt, lambdas, subln_g):
    B, nblk, d_attn, blk = qt.shape
    S = k.shape[1]
    H = d_attn // V_HEAD_DIM
    pos = _key_position_columns(S)
    kernel = functools.partial(_diff_attn_kernel, slopes=tuple(_alibi_slopes(H)),
                               q_per_step=ATTN_Q_PER_STEP)
    n_steps = pl.cdiv(nblk + 1, ATTN_Q_PER_STEP)
    return pl.pallas_call(
        kernel,
        grid=(B, H, n_steps),
        in_specs=[
            pl.BlockSpec((None, nblk, V_HEAD_DIM, blk), lambda b, h, q: (b, 0, h, 0)),
            pl.BlockSpec((None, S, V_HEAD_DIM), lambda b, h, q: (b, 0, h)),
            pl.BlockSpec(pos.shape, lambda b, h, q: (0, 0)),
            pl.BlockSpec((None, nblk, V_HEAD_DIM, blk), lambda b, h, q: (b, 0, h, 0)),
            *[pl.BlockSpec(v.shape, lambda b, h, q: (0, 0)) for v in lambdas],
            pl.BlockSpec(subln_g.shape, lambda b, h, q: (0, 0)),
        ],
        out_specs=pl.BlockSpec((None, S, V_HEAD_DIM), lambda b, h, q: (b, 0, h)),
        out_shape=jax.ShapeDtypeStruct((B, S, d_attn), jnp.bfloat16),
        scratch_shapes=[pltpu.VMEM((N_SCORE_BUFS, S, 2 * blk), jnp.float32),
                        pltpu.VMEM((N_SCORE_BUFS, 1, 2 * blk), jnp.float32),
                        pltpu.VMEM((V_HEAD_DIM + N_ONES_ROWS, 2 * blk), jnp.float32)],
        compiler_params=pltpu.CompilerParams(
            dimension_semantics=("parallel", "parallel", "arbitrary"),
            vmem_limit_bytes=V7X_VMEM_LIMIT_BYTES),
        name="diff_attn",
    )(qt, k, pos, vt, *lambdas, subln_g)


def _out_mlp_kernel(x_ref, attn_ref, lru_ref, wo_hbm, gm_ref, wup_hbm, wdn_hbm, gf_ref, y_ref,
                    wo_ref, wup_ref, wdn_ref, stage_ref, sem):
    tm = x_ref.shape[0]
    d_ff = wup_ref.shape[1]
    n_chunks = d_ff // FF_CHUNK
    rows = [slice(g * tm // MLP_SPLIT, (g + 1) * tm // MLP_SPLIT) for g in range(MLP_SPLIT)]

    def ff(c):
        return pl.ds(c * FF_CHUNK, FF_CHUNK)

    sources = [wo_hbm]
    targets = [lambda v: wo_ref.__setitem__((slice(None), slice(None)), v)]
    for c in range(n_chunks):
        sources += [wup_hbm.at[:, ff(c)], wdn_hbm.at[ff(c), :]]
        targets += [lambda v, c=c: wup_ref.__setitem__((slice(None), ff(c)), v),
                    lambda v, c=c: wdn_ref.__setitem__((ff(c), slice(None)), v)]

    def copy(i):
        return pltpu.make_async_copy(sources[i], stage_ref.at[i % 2], sem.at[i % 2])

    def land(i):
        copy(i).wait()
        targets[i](_bf16(stage_ref[i % 2]))
        if i + 2 < len(sources):
            copy(i + 2).start()

    def step_body(first_step):
        if first_step:
            copy(0).start()
            copy(1).start()
            land(0)
        x1 = []
        for r in rows:
            mix = jnp.concatenate([attn_ref[r, :], lru_ref[r, :]], axis=-1)
            x1.append(x_ref[r, :] + jnp.dot(mix, wo_ref[...], preferred_element_type=jnp.float32))
        hm = [_bf16(_rmsnorm_rows(v, gm_ref[...])) for v in x1]
        mlp = [jnp.zeros_like(v) for v in x1]
        for c in range(n_chunks):
            cols = slice(c * FF_CHUNK, (c + 1) * FF_CHUNK)
            if first_step:
                land(1 + 2 * c)
            act = [_bf16(jnp.square(jnp.maximum(
                jnp.dot(hm[g], wup_ref[:, cols], preferred_element_type=jnp.float32), 0.0)))
                for g in range(MLP_SPLIT)]
            if first_step:
                land(2 + 2 * c)
            for g in range(MLP_SPLIT):
                mlp[g] = mlp[g] + jnp.dot(act[g], wdn_ref[cols, :],
                                          preferred_element_type=jnp.float32)
        for g, r in enumerate(rows):
            y_ref[r, :] = _rmsnorm_rows(x1[g] + mlp[g], gf_ref[...])

    pl.when(pl.program_id(0) == 0)(lambda: step_body(True))
    pl.when(pl.program_id(0) != 0)(lambda: step_body(False))


def _out_mlp(x, attn, lru, w_out, g_mlp, w_up, w_down, g_final):
    B, S, D = x.shape
    T = B * S
    tm = MLP_TOKENS
    x2 = x.reshape(T, D)
    attn2 = attn.reshape(T, attn.shape[-1])
    lru2 = lru.reshape(T, lru.shape[-1])
    const = lambda i: (0, 0)
    in_hbm = pl.BlockSpec(memory_space=pl.ANY)
    assert w_out.shape == (D, D) and FF_CHUNK == D and w_up.shape[1] % FF_CHUNK == 0
    y = pl.pallas_call(
        _out_mlp_kernel,
        grid=(T // tm,),
        in_specs=[
            pl.BlockSpec((tm, D), lambda i: (i, 0)),
            pl.BlockSpec((tm, attn2.shape[1]), lambda i: (i, 0)),
            pl.BlockSpec((tm, lru2.shape[1]), lambda i: (i, 0)),
            in_hbm,
            pl.BlockSpec(g_mlp.shape, const),
            in_hbm,
            in_hbm,
            pl.BlockSpec(g_final.shape, const),
        ],
        out_specs=pl.BlockSpec((tm, D), lambda i: (i, 0)),
        out_shape=jax.ShapeDtypeStruct((T, D), jnp.float32),
        scratch_shapes=[
            pltpu.VMEM(w_out.shape, jnp.bfloat16),
            pltpu.VMEM(w_up.shape, jnp.bfloat16),
            pltpu.VMEM(w_down.shape, jnp.bfloat16),
            pltpu.VMEM((2, D, FF_CHUNK), jnp.float32),
            pltpu.SemaphoreType.DMA((2,)),
        ],
        compiler_params=pltpu.CompilerParams(
            dimension_semantics=("arbitrary",),
            vmem_limit_bytes=V7X_VMEM_LIMIT_BYTES),
        name="out_mlp",
    )(x2, attn2, lru2, w_out, g_mlp, w_up, w_down, g_final)
    return y.reshape(B, S, D)


def _pair_block_diag(w):
    n, c, d = w.shape
    rows = w.reshape(n // 2, 2 * c, d)
    tiled = jnp.concatenate([rows, rows], axis=-1)
    row_block = np.arange(2 * c)[:, None] // c
    col_block = np.arange(2 * d)[None, :] // d
    return jnp.where(jnp.asarray(row_block == col_block), tiled, 0.0)


def kernel(x, norm_mix_g, w_in, conv_w, conv_b, w_rg, b_rg, w_ig, b_ig, lru_L, lambda_q1, lambda_k1, lambda_q2, lambda_k2, subln_g, w_out, norm_mlp_g, w_up, w_down, final_g):
    B, S, D = x.shape
    d_lru = lru_L.shape[-1]
    d_attn = (w_in.shape[-1] - 2 * d_lru) // 3
    depth = w_in.shape[0]
    n_slab = d_lru // LANES
    assert depth == 1, "LAMBDA_INIT is specialised to a single layer"
    assert S % PROJ_TOKENS == 0 and (B * S) % MLP_TOKENS == 0
    assert PROJ_TOKENS % SUB_ROWS == 0 and d_lru % LANES == 0
    assert 2 * w_rg.shape[-1] == LANES and w_rg.shape[1] * w_rg.shape[2] == d_lru
    l = 0
    row = lambda v: v.reshape(1, -1)

    w_gates = _bf16(0.5 * jnp.concatenate([_pair_block_diag(w_rg[l]), _pair_block_diag(w_ig[l])],
                                          axis=-1))
    b_gates = 0.5 * jnp.concatenate([b_rg[l].reshape(n_slab, 1, LANES),
                                     b_ig[l].reshape(n_slab, 1, LANES)], axis=-1)
    lambdas = [row(v[l]) for v in (lambda_q1, lambda_k1, lambda_q2, lambda_k2)]

    qt, k, vt, lru = _in_proj_lru(x, row(norm_mix_g[l]), w_in[l], conv_w[l], row(conv_b[l]),
                                  w_gates, b_gates, row(lru_L[l]))
    attn = _diff_attn(qt, k, vt, lambdas, row(subln_g[l]))
    return _out_mlp(x, attn, lru, w_out[l], row(norm_mlp_g[l]), w_up[l], w_down[l], row(final_g))
```

```python
import functools
import math

import numpy as np
import jax
import jax.numpy as jnp
from jax import lax
from jax.experimental import pallas as pl
from jax.experimental.pallas import tpu as pltpu

EPS = 1e-6
CONV_WIDTH = 4
LRU_C = 8.0
HEAD_DIM = 64
V_HEAD_DIM = 2 * HEAD_DIM
LAMBDA_INIT = 0.8 - 0.6 * math.exp(-0.3 * 0)

V7X_MXU_COLS = 256
LANES = 128
SUBLANES = 8
V7X_VMEM_LIMIT_BYTES = 56 * 1024 * 1024

ATTN_BLOCK = V7X_MXU_COLS
PROJ_TOKENS = 512
MLP_TOKENS = 512
FF_CHUNK = 1024
MLP_SPLIT = 2
CHAIN = 4
SUB_ROWS = SUBLANES * CHAIN


def _bf16(x):
    return x.astype(jnp.bfloat16)


def _rmsnorm_rows(x, g):
    return x * lax.rsqrt(jnp.mean(x * x, axis=-1, keepdims=True) + EPS) * g


def _log_sigmoid(x):
    return jnp.minimum(x, 0.0) - jnp.log1p(jnp.exp(-jnp.abs(x)))


def _in_proj_lru_kernel(x_ref, g_ref, w_hbm, cw_ref, cb_ref, wg_ref, bg_ref, lru_l_ref, zero_ref,
                        qt_ref, k_ref, vt_ref, lru_ref,
                        w_ref, w_stage, w_sem, h_scr, ug_slab, u_scr, ub_scr, z_scr, gelu_scr,
                        out_slab, tail_ref, carry_ref):
    d_lru = lru_ref.shape[-1]
    d_attn = k_ref.shape[-1]

    piece_cols = [3 * d_attn, 3 * d_attn + d_lru, d_attn, 0, 2 * d_attn]

    def copy(i):
        return pltpu.make_async_copy(w_hbm.at[:, pl.ds(piece_cols[i], d_attn)],
                                     w_stage.at[i % 2], w_sem.at[i % 2])

    def land(i):
        copy(i).wait()
        chunk = w_stage[i % 2]
        if piece_cols[i] == 0:
            chunk = chunk * HEAD_DIM ** -0.5
        w_ref[:, piece_cols[i]:piece_cols[i] + d_attn] = _bf16(chunk)
        if i + 2 < len(piece_cols):
            copy(i + 2).start()

    @pl.when(pl.program_id(1) == 0)
    def _():
        tail_ref[...] = jnp.zeros(tail_ref.shape, jnp.float32)
        carry_ref[...] = jnp.zeros(carry_ref.shape, jnp.float32)

    step = functools.partial(_in_proj_lru_step, x_ref, g_ref, w_ref, cw_ref, cb_ref, wg_ref, bg_ref,
                             lru_l_ref, zero_ref, qt_ref, k_ref, vt_ref, lru_ref, h_scr, ug_slab,
                             u_scr, ub_scr, z_scr, gelu_scr, out_slab, tail_ref, carry_ref)

    def first_step():
        copy(0).start()
        copy(1).start()
        step(land)

    first_tile = (pl.program_id(0) == 0) & (pl.program_id(1) == 0)
    pl.when(first_tile)(first_step)
    pl.when(jnp.logical_not(first_tile))(lambda: step(None))


def _in_proj_lru_step(x_ref, g_ref, w_ref, cw_ref, cb_ref, wg_ref, bg_ref, lru_l_ref, zero_ref,
                      qt_ref, k_ref, vt_ref, lru_ref, h_scr, ug_slab, u_scr, ub_scr, z_scr, gelu_scr,
                      out_slab, tail_ref, carry_ref, land):
    tm = x_ref.shape[0]
    d_lru = lru_ref.shape[-1]
    n_slab = d_lru // LANES
    n_sub = tm // SUB_ROWS
    d_attn = k_ref.shape[-1]
    f32 = jnp.float32

    h_scr[...] = _bf16(_rmsnorm_rows(x_ref[...], g_ref[...]))

    def ug_half(half):
        if land:
            land(half)
        cols = slice(3 * d_attn + half * d_lru, 3 * d_attn + (half + 1) * d_lru)
        nat = jnp.dot(h_scr[...], w_ref[:, cols], preferred_element_type=f32)
        for l in range(n_slab):
            ug_slab[half * n_slab + l] = nat[:, l * LANES:(l + 1) * LANES]

    ug_half(0)

    def piece(slab, s, c):
        return ug_slab[slab, pl.ds(s * SUB_ROWS + c, SUBLANES, stride=CHAIN), :]

    row = lax.broadcasted_iota(jnp.int32, (SUBLANES, LANES), 0)

    for l in range(n_slab):
        lanes = slice(l * LANES, (l + 1) * LANES)
        cw = cw_ref[:, lanes]
        cb = cb_ref[:, lanes]
        rolled_prev = [pltpu.roll(tail_ref[c, :, lanes], 1, axis=0) for c in range(CHAIN)]
        for s in range(n_sub):
            cur = [piece(l, s, c) for c in range(CHAIN)]
            rolled = [pltpu.roll(cur[c], 1, axis=0) for c in range(CHAIN)]
            shifted = [jnp.where(row == 0, rolled_prev[c], rolled[c]) for c in range(CHAIN)]
            rolled_prev = rolled
            if s == n_sub - 1:
                for c in range(CHAIN):
                    tail_ref[c, :, lanes] = cur[c]
            conv = []
            for c in range(CHAIN):
                acc = cb
                for kk in range(CONV_WIDTH):
                    d = CONV_WIDTH - 1 - kk
                    src = cur[c - d] if c >= d else shifted[c - d + CHAIN]
                    acc = acc + src * cw[kk:kk + 1, :]
                conv.append(acc)
            u_sub = jnp.concatenate(conv, axis=0)
            u_scr[l, s * SUB_ROWS:(s + 1) * SUB_ROWS, :] = u_sub
            ub_scr[s * SUB_ROWS:(s + 1) * SUB_ROWS, lanes] = _bf16(u_sub)

    ug_half(1)
    for l in range(n_slab):
        z = jnp.dot(ub_scr[:, l * LANES:(l + 1) * LANES], wg_ref[l],
                    preferred_element_type=f32) + bg_ref[l]
        z_scr[l] = z[:, :LANES]
        z_scr[n_slab + l] = z[:, LANES:]

    def k_piece(part):
        cols = slice(part * ATTN_BLOCK, (part + 1) * ATTN_BLOCK)
        t = jnp.dot(h_scr[...], w_ref[:, d_attn + part * ATTN_BLOCK:
                                     d_attn + (part + 1) * ATTN_BLOCK], preferred_element_type=f32)
        k_ref[:, cols] = _bf16(t)
        return [t[tm - SUBLANES:, c * LANES:(c + 1) * LANES] for c in range(ATTN_BLOCK // LANES)]

    def transposed_piece(first_col, o_ref, part):
        rows = slice(part * ATTN_BLOCK, (part + 1) * ATTN_BLOCK)
        t = jnp.dot(h_scr[rows, :], w_ref[:, first_col:first_col + d_attn],
                    preferred_element_type=f32)
        o_ref[part] = _bf16(t.T)
        return [t[ATTN_BLOCK - SUBLANES:, c * LANES:(c + 1) * LANES] for c in range(d_attn // LANES)]

    side_work = {0: lambda: k_piece(0), 2: lambda: k_piece(1),
                 5: lambda: transposed_piece(0, qt_ref, 0),
                 8: lambda: transposed_piece(0, qt_ref, 1),
                 11: lambda: transposed_piece(2 * d_attn, vt_ref, 0),
                 13: lambda: transposed_piece(2 * d_attn, vt_ref, 1)}
    land_before = {0: 2, 5: 3, 11: 4}
    assert n_sub > max(side_work) and qt_ref.shape[0] == 2 and d_attn == 2 * ATTN_BLOCK

    c_gelu = math.sqrt(2.0 / math.pi)
    for l in range(n_slab):
        for s in range(n_sub):
            gate = jnp.concatenate([piece(n_slab + l, s, c) for c in range(CHAIN)], axis=0)
            gelu_scr[l, s * SUB_ROWS:(s + 1) * SUB_ROWS, :] = (0.5 * gate) * (1.0 + jnp.tanh(
                gate * (c_gelu + (c_gelu * 0.044715) * (gate * gate))))

    c2_row = (0.5 * LRU_C * math.log2(math.e)) * _log_sigmoid(lru_l_ref[...])

    carry = [carry_ref[:, l * LANES:(l + 1) * LANES] for l in range(n_slab)]
    pending_at = -1
    keep_state = zero_ref[...] == 0
    for s in range(n_sub):
        if land and s in land_before:
            land(land_before[s])
        if s in side_work:
            anchor = functools.reduce(jnp.add, side_work[s]())
            pending_at = s + 1
        base = s * SUB_ROWS
        rows = slice(base, base + SUB_ROWS)
        for l in range(n_slab):
            c2 = c2_row[:, l * LANES:(l + 1) * LANES]
            u = u_scr[l, rows, :]
            t_r = jnp.tanh(z_scr[l, rows, :])
            t_i = jnp.tanh(z_scr[n_slab + l, rows, :])
            log2_a = c2 + c2 * t_r
            a = jnp.exp2(log2_a)
            quarter = (jnp.tanh(log2_a * math.log(2.0)) * -0.25) * (a * a + 1.0)
            half_mult = jnp.where(quarter > 0.0, quarter * lax.rsqrt(quarter), 0.0)
            b = (half_mult * u) * (1.0 + t_i)
            gelu = gelu_scr[l, rows, :]

            a_c = [a[c * SUBLANES:(c + 1) * SUBLANES] for c in range(CHAIN)]
            b_c = [b[c * SUBLANES:(c + 1) * SUBLANES] for c in range(CHAIN)]
            h_loc, p_loc = [b_c[0]], [a_c[0]]
            for c in range(1, CHAIN):
                h_loc.append(a_c[c] * h_loc[c - 1] + b_c[c])
                p_loc.append(a_c[c] * p_loc[c - 1])
            p_cum, h_cum = p_loc[CHAIN - 1], h_loc[CHAIN - 1]
            shift = 1
            while shift < SUBLANES:
                valid = row >= shift
                p_prev = jnp.where(valid, pltpu.roll(p_cum, shift, axis=0), 1.0)
                h_prev = jnp.where(valid, pltpu.roll(h_cum, shift, axis=0), 0.0)
                h_cum = p_cum * h_prev + h_cum
                p_cum = p_cum * p_prev
                shift *= 2
            chain_end = h_cum + p_cum * carry[l]
            chain_in = jnp.where(row == 0, carry[l], pltpu.roll(chain_end, 1, axis=0))
            carry[l] = jnp.broadcast_to(chain_end[SUBLANES - 1:SUBLANES, :], (SUBLANES, LANES))
            if s == pending_at:
                carry[l] = jnp.where(keep_state, carry[l], anchor)
            for c in range(CHAIN):
                out = (h_loc[c] + p_loc[c] * chain_in) * gelu[c * SUBLANES:(c + 1) * SUBLANES]
                out_slab[l, pl.ds(base + c, SUBLANES, stride=CHAIN), :] = out
    for l in range(n_slab):
        carry_ref[:, l * LANES:(l + 1) * LANES] = carry[l]
    lru_ref[...] = _bf16(jnp.concatenate([out_slab[l] for l in range(n_slab)], axis=1))


def _in_proj_lru(x, g, w, conv_w, conv_b, w_gates, b_gates, lru_l):
    B, S, D = x.shape
    d_lru = lru_l.shape[-1]
    d_attn = (w.shape[1] - 2 * d_lru) // 3
    tm = PROJ_TOKENS
    nblk = tm // ATTN_BLOCK
    n_slab = d_lru // LANES
    const = lambda b, i: (0, 0)
    assert d_lru == d_attn
    return pl.pallas_call(
        _in_proj_lru_kernel,
        grid=(B, S // tm),
        in_specs=[
            pl.BlockSpec((None, tm, D), lambda b, i: (b, i, 0)),
            pl.BlockSpec((1, D), const),
            pl.BlockSpec(memory_space=pl.ANY),
            pl.BlockSpec(conv_w.shape, const),
            pl.BlockSpec(conv_b.shape, const),
            pl.BlockSpec(w_gates.shape, lambda b, i: (0, 0, 0)),
            pl.BlockSpec(b_gates.shape, lambda b, i: (0, 0, 0)),
            pl.BlockSpec(lru_l.shape, const),
            pl.BlockSpec((SUBLANES, LANES), const),
        ],
        out_specs=[
            pl.BlockSpec((None, nblk, d_attn, ATTN_BLOCK), lambda b, i: (b, i, 0, 0)),
            pl.BlockSpec((None, tm, d_attn), lambda b, i: (b, i, 0)),
            pl.BlockSpec((None, nblk, d_attn, ATTN_BLOCK), lambda b, i: (b, i, 0, 0)),
            pl.BlockSpec((None, tm, d_lru), lambda b, i: (b, i, 0)),
        ],
        out_shape=[
            jax.ShapeDtypeStruct((B, S // ATTN_BLOCK, d_attn, ATTN_BLOCK), jnp.bfloat16),
            jax.ShapeDtypeStruct((B, S, d_attn), jnp.bfloat16),
            jax.ShapeDtypeStruct((B, S // ATTN_BLOCK, d_attn, ATTN_BLOCK), jnp.bfloat16),
            jax.ShapeDtypeStruct((B, S, d_lru), jnp.bfloat16),
        ],
        scratch_shapes=[
            pltpu.VMEM(w.shape, jnp.bfloat16),
            pltpu.VMEM((2, D, d_attn), jnp.float32),
            pltpu.SemaphoreType.DMA((2,)),
            pltpu.VMEM((tm, D), jnp.bfloat16),
            pltpu.VMEM((2 * n_slab, tm, LANES), jnp.float32),
            pltpu.VMEM((n_slab, tm, LANES), jnp.float32),
            pltpu.VMEM((tm, d_lru), jnp.bfloat16),
            pltpu.VMEM((2 * n_slab, tm, LANES), jnp.float32),
            pltpu.VMEM((n_slab, tm, LANES), jnp.float32),
            pltpu.VMEM((n_slab, tm, LANES), jnp.float32),
            pltpu.VMEM((CHAIN, SUBLANES, d_lru), jnp.float32),
            pltpu.VMEM((SUBLANES, d_lru), jnp.float32),
        ],
        compiler_params=pltpu.CompilerParams(
            dimension_semantics=("arbitrary", "arbitrary"),
            vmem_limit_bytes=V7X_VMEM_LIMIT_BYTES),
        name="in_proj_lru",
    )(x, g, w, conv_w, conv_b, w_gates, b_gates, lru_l, jnp.zeros((SUBLANES, LANES), jnp.int32))


N_POS_COLS = 128
N_ONES_ROWS = 16
SCORE_LEAD = 2
N_SCORE_BUFS = SCORE_LEAD + 1
ATTN_Q_PER_STEP = 9


def _diff_attn_kernel(qt_ref, k_ref, pos_ref, vt_ref, lq1_ref, lk1_ref, lq2_ref, lk2_ref, sg_ref,
                      o_ref,
                      s_ref, m_ref, acc_ref, *, slopes, q_per_step):
    h = pl.program_id(1)
    blk = ATTN_BLOCK
    n_blocks = vt_ref.shape[0]

    slope = jnp.float32(slopes[0])
    for i in range(1, len(slopes)):
        slope = jnp.where(h == i, jnp.float32(slopes[i]), slope)

    r = lax.broadcasted_iota(jnp.int32, (N_POS_COLS, blk), 0)
    lane = lax.broadcasted_iota(jnp.int32, (N_POS_COLS, blk), 1).astype(jnp.float32)
    zeros = jnp.zeros((HEAD_DIM, blk), qt_ref.dtype)
    key_in_blk = lax.broadcasted_iota(jnp.int32, (blk, 2 * blk), 0)
    qry_in_blk = lax.broadcasted_iota(jnp.int32, (blk, 2 * blk), 1) % blk
    causal = key_in_blk <= qry_in_blk
    ones_row = lax.broadcasted_iota(jnp.int32, (N_ONES_ROWS, blk), 0) == 0
    ones_blk = jnp.where(ones_row, 1.0, 0.0).astype(jnp.bfloat16)

    def q_operand(qt, q_index):
        alibi = _bf16(jnp.where(r == 0, slope,
                      jnp.where(r == 1, slope * blk,
                      jnp.where(r == 2, -slope * lane,
                      jnp.where(r == 3, -slope * float(q_index * blk), 0.0)))))
        return jnp.concatenate(
            [jnp.concatenate([qt[:HEAD_DIM], zeros, alibi], axis=0),
             jnp.concatenate([zeros, qt[HEAD_DIM:], alibi], axis=0)], axis=1)

    def score_block(q_aug, j, diagonal):
        rows = slice(j * blk, (j + 1) * blk)
        k_aug = jnp.concatenate([k_ref[rows, :], pos_ref[rows, :]], axis=1)
        s = jnp.dot(k_aug, q_aug, preferred_element_type=jnp.float32)
        return jnp.where(causal, s, -jnp.inf) if diagonal else s

    def finish(t):
        acc = acc_ref[...]
        lam = (jnp.exp(jnp.sum(lq1_ref[...] * lk1_ref[...], axis=-1, keepdims=True))
               - jnp.exp(jnp.sum(lq2_ref[...] * lk2_ref[...], axis=-1, keepdims=True))
               + LAMBDA_INIT)
        heads = acc[:V_HEAD_DIM] / acc[V_HEAD_DIM:V_HEAD_DIM + 1]
        o = heads[:, :blk] - lam * heads[:, blk:]
        o = o * lax.rsqrt(jnp.mean(o * o, axis=0, keepdims=True) + EPS)
        o_ref[t * blk:(t + 1) * blk, :] = _bf16(o.T * (sg_ref[...] * (1.0 - LAMBDA_INIT)))

    order = list(range(0, n_blocks, 2)) + list(range(n_blocks - 1 - n_blocks % 2, 0, -2))
    assert sorted(order) == list(range(n_blocks))

    def score_q_block(q_aug, q_index, slot, j, m_run):
        s = score_block(q_aug, j, j == q_index)
        s_ref[slot, j * blk:(j + 1) * blk, :] = s
        col_max = jnp.max(s, axis=0, keepdims=True)
        return col_max if m_run is None else jnp.maximum(m_run, col_max)

    def sub_step(t):
        if t >= 1:
            finish(order[t - 1])
        if t == n_blocks:
            return
        if t == 0:
            for pos in range(min(SCORE_LEAD, n_blocks)):
                q_index = order[pos]
                q_aug = q_operand(qt_ref[q_index], q_index)
                m_run = None
                for j in range(q_index + 1):
                    m_run = score_q_block(q_aug, q_index, pos % N_SCORE_BUFS, j, m_run)
                m_ref[pos % N_SCORE_BUFS] = m_run
        cur = t % N_SCORE_BUFS
        q_cur = order[t]
        scoring = t + SCORE_LEAD < n_blocks
        q_ahead = order[t + SCORE_LEAD] if scoring else -1
        slot_ahead = (t + SCORE_LEAD) % N_SCORE_BUFS
        q_aug = q_operand(qt_ref[q_ahead], q_ahead) if scoring else None
        m_cur = m_ref[cur]
        m_run = None
        acc = jnp.zeros((V_HEAD_DIM + N_ONES_ROWS, 2 * blk), jnp.float32)
        for j in range(max(q_cur, q_ahead) + 1):
            if j <= q_ahead:
                m_run = score_q_block(q_aug, q_ahead, slot_ahead, j, m_run)
            if j <= q_cur:
                p = _bf16(jnp.exp(s_ref[cur, j * blk:(j + 1) * blk, :] - m_cur))
                vt_aug = jnp.concatenate([vt_ref[j], ones_blk], axis=0)
                acc = acc + jnp.dot(vt_aug, p, preferred_element_type=jnp.float32)
        if scoring:
            m_ref[slot_ahead] = m_run
        acc_ref[...] = acc

    def grid_step(first):
        def run():
            for t in range(first, min(first + q_per_step, n_blocks + 1)):
                sub_step(t)
        return run

    steps = [grid_step(first) for first in range(0, n_blocks + 1, q_per_step)]
    if len(steps) == 1:
        steps[0]()
    else:
        lax.switch(pl.program_id(2), steps)


def _alibi_slopes(n_heads):
    slopes = [2.0 ** (-8.0 * (i + 1) / n_heads) for i in range(n_heads)]
    assert all(math.frexp(s)[0] == 0.5 for s in slopes), slopes
    return slopes


def _key_position_columns(seq_len):
    j = np.arange(seq_len)
    pos = np.zeros((seq_len, N_POS_COLS), np.float32)
    pos[:, 0] = j % ATTN_BLOCK
    pos[:, 1] = j // ATTN_BLOCK
    pos[:, 2:4] = 1.0
    assert ATTN_BLOCK <= 256 and seq_len // ATTN_BLOCK <= 256
    return jnp.asarray(pos, jnp.bfloat16)


def _diff_attn(qt, k, vt, lambdas, subln_g):
    B, nblk, d_attn, blk = qt.shape
    S = k.shape[1]
    H = d_attn // V_HEAD_DIM
    pos = _key_position_columns(S)
    kernel = functools.partial(_diff_attn_kernel, slopes=tuple(_alibi_slopes(H)),
                               q_per_step=ATTN_Q_PER_STEP)
    n_steps = pl.cdiv(nblk + 1, ATTN_Q_PER_STEP)
    return pl.pallas_call(
        kernel,
        grid=(B, H, n_steps),
        in_specs=[
            pl.BlockSpec((None, nblk, V_HEAD_DIM, blk), lambda b, h, q: (b, 0, h, 0)),
            pl.BlockSpec((None, S, V_HEAD_DIM), lambda b, h, q: (b, 0, h)),
            pl.BlockSpec(pos.shape, lambda b, h, q: (0, 0)),
            pl.BlockSpec((None, nblk, V_HEAD_DIM, blk), lambda b, h, q: (b, 0, h, 0)),
            *[pl.BlockSpec(v.shape, lambda b, h, q: (0, 0)) for v in lambdas],
            pl.BlockSpec(subln_g.shape, lambda b, h, q: (0, 0)),
        ],
        out_specs=pl.BlockSpec((None, S, V_HEAD_DIM), lambda b, h, q: (b, 0, h)),
        out_shape=jax.ShapeDtypeStruct((B, S, d_attn), jnp.bfloat16),
        scratch_shapes=[pltpu.VMEM((N_SCORE_BUFS, S, 2 * blk), jnp.float32),
                        pltpu.VMEM((N_SCORE_BUFS, 1, 2 * blk), jnp.float32),
                        pltpu.VMEM((V_HEAD_DIM + N_ONES_ROWS, 2 * blk), jnp.float32)],
        compiler_params=pltpu.CompilerParams(
            dimension_semantics=("parallel", "parallel", "arbitrary"),
            vmem_limit_bytes=V7X_VMEM_LIMIT_BYTES),
        name="diff_attn",
    )(qt, k, pos, vt, *lambdas, subln_g)


def _out_mlp_kernel(x_ref, attn_ref, lru_ref, wo_hbm, gm_ref, wup_hbm, wdn_hbm, gf_ref, y_ref,
                    wo_ref, wup_ref, wdn_ref, stage_ref, sem):
    tm = x_ref.shape[0]
    d_ff = wup_ref.shape[1]
    n_chunks = d_ff // FF_CHUNK
    rows = [slice(g * tm // MLP_SPLIT, (g + 1) * tm // MLP_SPLIT) for g in range(MLP_SPLIT)]

    def ff(c):
        return pl.ds(c * FF_CHUNK, FF_CHUNK)

    sources = [wo_hbm]
    targets = [lambda v: wo_ref.__setitem__((slice(None), slice(None)), v)]
    for c in range(n_chunks):
        sources += [wup_hbm.at[:, ff(c)], wdn_hbm.at[ff(c), :]]
        targets += [lambda v, c=c: wup_ref.__setitem__((slice(None), ff(c)), v),
                    lambda v, c=c: wdn_ref.__setitem__((ff(c), slice(None)), v)]

    def copy(i):
        return pltpu.make_async_copy(sources[i], stage_ref.at[i % 2], sem.at[i % 2])

    def land(i):
        copy(i).wait()
        targets[i](_bf16(stage_ref[i % 2]))
        if i + 2 < len(sources):
            copy(i + 2).start()

    def step_body(first_step):
        if first_step:
            copy(0).start()
            copy(1).start()
            land(0)
        x1 = []
        for r in rows:
            mix = jnp.concatenate([attn_ref[r, :], lru_ref[r, :]], axis=-1)
            x1.append(x_ref[r, :] + jnp.dot(mix, wo_ref[...], preferred_element_type=jnp.float32))
        hm = [_bf16(_rmsnorm_rows(v, gm_ref[...])) for v in x1]
        mlp = [jnp.zeros_like(v) for v in x1]
        for c in range(n_chunks):
            cols = slice(c * FF_CHUNK, (c + 1) * FF_CHUNK)
            if first_step:
                land(1 + 2 * c)
            act = [_bf16(jnp.square(jnp.maximum(
                jnp.dot(hm[g], wup_ref[:, cols], preferred_element_type=jnp.float32), 0.0)))
                for g in range(MLP_SPLIT)]
            if first_step:
                land(2 + 2 * c)
            for g in range(MLP_SPLIT):
                mlp[g] = mlp[g] + jnp.dot(act[g], wdn_ref[cols, :],
                                          preferred_element_type=jnp.float32)
        for g, r in enumerate(rows):
            y_ref[r, :] = _rmsnorm_rows(x1[g] + mlp[g], gf_ref[...])

    pl.when(pl.program_id(0) == 0)(lambda: step_body(True))
    pl.when(pl.program_id(0) != 0)(lambda: step_body(False))


def _out_mlp(x, attn, lru, w_out, g_mlp, w_up, w_down, g_final):
    B, S, D = x.shape
    T = B * S
    tm = MLP_TOKENS
    x2 = x.reshape(T, D)
    attn2 = attn.reshape(T, attn.shape[-1])
    lru2 = lru.reshape(T, lru.shape[-1])
    const = lambda i: (0, 0)
    in_hbm = pl.BlockSpec(memory_space=pl.ANY)
    assert w_out.shape == (D, D) and FF_CHUNK == D and w_up.shape[1] % FF_CHUNK == 0
    y = pl.pallas_call(
        _out_mlp_kernel,
        grid=(T // tm,),
        in_specs=[
            pl.BlockSpec((tm, D), lambda i: (i, 0)),
            pl.BlockSpec((tm, attn2.shape[1]), lambda i: (i, 0)),
            pl.BlockSpec((tm, lru2.shape[1]), lambda i: (i, 0)),
            in_hbm,
            pl.BlockSpec(g_mlp.shape, const),
            in_hbm,
            in_hbm,
            pl.BlockSpec(g_final.shape, const),
        ],
        out_specs=pl.BlockSpec((tm, D), lambda i: (i, 0)),
        out_shape=jax.ShapeDtypeStruct((T, D), jnp.float32),
        scratch_shapes=[
            pltpu.VMEM(w_out.shape, jnp.bfloat16),
            pltpu.VMEM(w_up.shape, jnp.bfloat16),
            pltpu.VMEM(w_down.shape, jnp.bfloat16),
            pltpu.VMEM((2, D, FF_CHUNK), jnp.float32),
            pltpu.SemaphoreType.DMA((2,)),
        ],
        compiler_params=pltpu.CompilerParams(
            dimension_semantics=("arbitrary",),
            vmem_limit_bytes=V7X_VMEM_LIMIT_BYTES),
        name="out_mlp",
    )(x2, attn2, lru2, w_out, g_mlp, w_up, w_down, g_final)
    return y.reshape(B, S, D)


def _pair_block_diag(w):
    n, c, d = w.shape
    rows = w.reshape(n // 2, 2 * c, d)
    tiled = jnp.concatenate([rows, rows], axis=-1)
    row_block = np.arange(2 * c)[:, None] // c
    col_block = np.arange(2 * d)[None, :] // d
    return jnp.where(jnp.asarray(row_block == col_block), tiled, 0.0)


def kernel(x, norm_mix_g, w_in, conv_w, conv_b, w_rg, b_rg, w_ig, b_ig, lru_L, lambda_q1, lambda_k1, lambda_q2, lambda_k2, subln_g, w_out, norm_mlp_g, w_up, w_down, final_g):
    B, S, D = x.shape
    d_lru = lru_L.shape[-1]
    d_attn = (w_in.shape[-1] - 2 * d_lru) // 3
    depth = w_in.shape[0]
    n_slab = d_lru // LANES
    assert depth == 1, "LAMBDA_INIT is specialised to a single layer"
    assert S % PROJ_TOKENS == 0 and (B * S) % MLP_TOKENS == 0
    assert PROJ_TOKENS % SUB_ROWS == 0 and d_lru % LANES == 0
    assert 2 * w_rg.shape[-1] == LANES and w_rg.shape[1] * w_rg.shape[2] == d_lru
    l = 0
    row = lambda v: v.reshape(1, -1)

    w_gates = _bf16(0.5 * jnp.concatenate([_pair_block_diag(w_rg[l]), _pair_block_diag(w_ig[l])],
                                          axis=-1))
    b_gates = 0.5 * jnp.concatenate([b_rg[l].reshape(n_slab, 1, LANES),
                                     b_ig[l].reshape(n_slab, 1, LANES)], axis=-1)
    lambdas = [row(v[l]) for v in (lambda_q1, lambda_k1, lambda_q2, lambda_k2)]

    qt, k, vt, lru = _in_proj_lru(x, row(norm_mix_g[l]), w_in[l], conv_w[l], row(conv_b[l]),
                                  w_gates, b_gates, row(lru_L[l]))
    attn = _diff_attn(qt, k, vt, lambdas, row(subln_g[l]))
    return _out_mlp(x, attn, lru, w_out[l], row(norm_mlp_g[l]), w_up[l], w_down[l], row(final_g))
```

```python
import functools
import math

import numpy as np
import jax
import jax.numpy as jnp
from jax import lax
from jax.experimental import pallas as pl
from jax.experimental.pallas import tpu as pltpu

EPS = 1e-6
CONV_WIDTH = 4
LRU_C = 8.0
HEAD_DIM = 64
V_HEAD_DIM = 2 * HEAD_DIM
LAMBDA_INIT = 0.8 - 0.6 * math.exp(-0.3 * 0)

V7X_MXU_COLS = 256
LANES = 128
SUBLANES = 8
V7X_VMEM_LIMIT_BYTES = 56 * 1024 * 1024

ATTN_BLOCK = V7X_MXU_COLS
PROJ_TOKENS = 512
MLP_TOKENS = 512
FF_CHUNK = 1024
MLP_SPLIT = 2
W_STAGE_SLOTS = 4
CHAIN = 4
SUB_ROWS = SUBLANES * CHAIN


def _bf16(x):
    return x.astype(jnp.bfloat16)


def _rmsnorm_rows(x, g):
    return x * lax.rsqrt(jnp.mean(x * x, axis=-1, keepdims=True) + EPS) * g


def _log_sigmoid(x):
    return jnp.minimum(x, 0.0) - jnp.log1p(jnp.exp(-jnp.abs(x)))


def _in_proj_lru_kernel(x_ref, g_ref, w_hbm, cw_ref, cb_ref, wg_ref, bg_ref, lru_l_ref, zero_ref,
                        qt_ref, k_ref, vt_ref, lru_ref,
                        w_ref, w_stage, w_sem, h_scr, ug_slab, u_scr, ub_scr, z_scr, gelu_scr,
                        out_slab, tail_ref, carry_ref):
    d_lru = lru_ref.shape[-1]
    d_attn = k_ref.shape[-1]

    piece_cols = [3 * d_attn, 3 * d_attn + d_lru, d_attn, 0, 2 * d_attn]

    n_slots = w_stage.shape[0]

    def copy(i):
        return pltpu.make_async_copy(w_hbm.at[:, pl.ds(piece_cols[i], d_attn)],
                                     w_stage.at[i % n_slots], w_sem.at[i % n_slots])

    def land(i):
        copy(i).wait()
        chunk = w_stage[i % n_slots]
        if piece_cols[i] == 0:
            chunk = chunk * HEAD_DIM ** -0.5
        w_ref[:, piece_cols[i]:piece_cols[i] + d_attn] = _bf16(chunk)
        if i + n_slots < len(piece_cols):
            copy(i + n_slots).start()

    @pl.when(pl.program_id(1) == 0)
    def _():
        tail_ref[...] = jnp.zeros(tail_ref.shape, jnp.float32)
        carry_ref[...] = jnp.zeros(carry_ref.shape, jnp.float32)

    step = functools.partial(_in_proj_lru_step, x_ref, g_ref, w_ref, cw_ref, cb_ref, wg_ref, bg_ref,
                             lru_l_ref, zero_ref, qt_ref, k_ref, vt_ref, lru_ref, h_scr, ug_slab,
                             u_scr, ub_scr, z_scr, gelu_scr, out_slab, tail_ref, carry_ref)

    def first_step():
        for i in range(n_slots):
            copy(i).start()
        step(land)

    first_tile = (pl.program_id(0) == 0) & (pl.program_id(1) == 0)
    pl.when(first_tile)(first_step)
    pl.when(jnp.logical_not(first_tile))(lambda: step(None))


def _in_proj_lru_step(x_ref, g_ref, w_ref, cw_ref, cb_ref, wg_ref, bg_ref, lru_l_ref, zero_ref,
                      qt_ref, k_ref, vt_ref, lru_ref, h_scr, ug_slab, u_scr, ub_scr, z_scr, gelu_scr,
                      out_slab, tail_ref, carry_ref, land):
    tm = x_ref.shape[0]
    d_lru = lru_ref.shape[-1]
    n_slab = d_lru // LANES
    n_sub = tm // SUB_ROWS
    d_attn = k_ref.shape[-1]
    f32 = jnp.float32

    h_scr[...] = _bf16(_rmsnorm_rows(x_ref[...], g_ref[...]))

    def ug_half(half):
        if land:
            land(half)
        cols = slice(3 * d_attn + half * d_lru, 3 * d_attn + (half + 1) * d_lru)
        nat = jnp.dot(h_scr[...], w_ref[:, cols], preferred_element_type=f32)
        for l in range(n_slab):
            ug_slab[half * n_slab + l] = nat[:, l * LANES:(l + 1) * LANES]

    ug_half(0)

    def piece(slab, s, c):
        return ug_slab[slab, pl.ds(s * SUB_ROWS + c, SUBLANES, stride=CHAIN), :]

    row = lax.broadcasted_iota(jnp.int32, (SUBLANES, LANES), 0)

    for l in range(n_slab):
        lanes = slice(l * LANES, (l + 1) * LANES)
        cw = cw_ref[:, lanes]
        cb = cb_ref[:, lanes]
        rolled_prev = [pltpu.roll(tail_ref[c, :, lanes], 1, axis=0) for c in range(CHAIN)]
        for s in range(n_sub):
            cur = [piece(l, s, c) for c in range(CHAIN)]
            rolled = [pltpu.roll(cur[c], 1, axis=0) for c in range(CHAIN)]
            shifted = [jnp.where(row == 0, rolled_prev[c], rolled[c]) for c in range(CHAIN)]
            rolled_prev = rolled
            if s == n_sub - 1:
                for c in range(CHAIN):
                    tail_ref[c, :, lanes] = cur[c]
            conv = []
            for c in range(CHAIN):
                acc = cb
                for kk in range(CONV_WIDTH):
                    d = CONV_WIDTH - 1 - kk
                    src = cur[c - d] if c >= d else shifted[c - d + CHAIN]
                    acc = acc + src * cw[kk:kk + 1, :]
                conv.append(acc)
            u_sub = jnp.concatenate(conv, axis=0)
            u_scr[l, s * SUB_ROWS:(s + 1) * SUB_ROWS, :] = u_sub
            ub_scr[s * SUB_ROWS:(s + 1) * SUB_ROWS, lanes] = _bf16(u_sub)

    ug_half(1)
    for l in range(n_slab):
        z = jnp.dot(ub_scr[:, l * LANES:(l + 1) * LANES], wg_ref[l],
                    preferred_element_type=f32) + bg_ref[l]
        z_scr[l] = z[:, :LANES]
        z_scr[n_slab + l] = z[:, LANES:]

    def k_piece(part):
        cols = slice(part * ATTN_BLOCK, (part + 1) * ATTN_BLOCK)
        t = jnp.dot(h_scr[...], w_ref[:, d_attn + part * ATTN_BLOCK:
                                     d_attn + (part + 1) * ATTN_BLOCK], preferred_element_type=f32)
        k_ref[:, cols] = _bf16(t)
        return [t[tm - SUBLANES:, c * LANES:(c + 1) * LANES] for c in range(ATTN_BLOCK // LANES)]

    def transposed_piece(first_col, o_ref, part):
        rows = slice(part * ATTN_BLOCK, (part + 1) * ATTN_BLOCK)
        t = jnp.dot(h_scr[rows, :], w_ref[:, first_col:first_col + d_attn],
                    preferred_element_type=f32)
        o_ref[part] = _bf16(t.T)
        return [t[ATTN_BLOCK - SUBLANES:, c * LANES:(c + 1) * LANES] for c in range(d_attn // LANES)]

    side_work = {0: lambda: k_piece(0), 2: lambda: k_piece(1),
                 5: lambda: transposed_piece(0, qt_ref, 0),
                 8: lambda: transposed_piece(0, qt_ref, 1),
                 11: lambda: transposed_piece(2 * d_attn, vt_ref, 0),
                 13: lambda: transposed_piece(2 * d_attn, vt_ref, 1)}
    land_before = {0: 2, 5: 3, 11: 4}
    assert n_sub > max(side_work) and qt_ref.shape[0] == 2 and d_attn == 2 * ATTN_BLOCK

    c_gelu = math.sqrt(2.0 / math.pi)
    for l in range(n_slab):
        for s in range(n_sub):
            gate = jnp.concatenate([piece(n_slab + l, s, c) for c in range(CHAIN)], axis=0)
            gelu_scr[l, s * SUB_ROWS:(s + 1) * SUB_ROWS, :] = (0.5 * gate) * (1.0 + jnp.tanh(
                gate * (c_gelu + (c_gelu * 0.044715) * (gate * gate))))

    c2_row = (0.5 * LRU_C * math.log2(math.e)) * _log_sigmoid(lru_l_ref[...])

    carry = [carry_ref[:, l * LANES:(l + 1) * LANES] for l in range(n_slab)]
    pending_at = -1
    keep_state = zero_ref[...] == 0
    for s in range(n_sub):
        if land and s in land_before:
            land(land_before[s])
        if s in side_work:
            anchor = functools.reduce(jnp.add, side_work[s]())
            pending_at = s + 1
        base = s * SUB_ROWS
        rows = slice(base, base + SUB_ROWS)
        for l in range(n_slab):
            c2 = c2_row[:, l * LANES:(l + 1) * LANES]
            u = u_scr[l, rows, :]
            t_r = jnp.tanh(z_scr[l, rows, :])
            t_i = jnp.tanh(z_scr[n_slab + l, rows, :])
            log2_a = c2 + c2 * t_r
            a = jnp.exp2(log2_a)
            quarter = (jnp.tanh(log2_a * math.log(2.0)) * -0.25) * (a * a + 1.0)
            half_mult = jnp.where(quarter > 0.0, quarter * lax.rsqrt(quarter), 0.0)
            b = (half_mult * u) * (1.0 + t_i)
            gelu = gelu_scr[l, rows, :]

            a_c = [a[c * SUBLANES:(c + 1) * SUBLANES] for c in range(CHAIN)]
            b_c = [b[c * SUBLANES:(c + 1) * SUBLANES] for c in range(CHAIN)]
            h_loc, p_loc = [b_c[0]], [a_c[0]]
            for c in range(1, CHAIN):
                h_loc.append(a_c[c] * h_loc[c - 1] + b_c[c])
                p_loc.append(a_c[c] * p_loc[c - 1])
            p_cum, h_cum = p_loc[CHAIN - 1], h_loc[CHAIN - 1]
            shift = 1
            while shift < SUBLANES:
                valid = row >= shift
                p_prev = jnp.where(valid, pltpu.roll(p_cum, shift, axis=0), 1.0)
                h_prev = jnp.where(valid, pltpu.roll(h_cum, shift, axis=0), 0.0)
                h_cum = p_cum * h_prev + h_cum
                p_cum = p_cum * p_prev
                shift *= 2
            chain_end = h_cum + p_cum * carry[l]
            chain_in = jnp.where(row == 0, carry[l], pltpu.roll(chain_end, 1, axis=0))
            carry[l] = jnp.broadcast_to(chain_end[SUBLANES - 1:SUBLANES, :], (SUBLANES, LANES))
            if s == pending_at:
                carry[l] = jnp.where(keep_state, carry[l], anchor)
            for c in range(CHAIN):
                out = (h_loc[c] + p_loc[c] * chain_in) * gelu[c * SUBLANES:(c + 1) * SUBLANES]
                out_slab[l, pl.ds(base + c, SUBLANES, stride=CHAIN), :] = out
    for l in range(n_slab):
        carry_ref[:, l * LANES:(l + 1) * LANES] = carry[l]
    lru_ref[...] = _bf16(jnp.concatenate([out_slab[l] for l in range(n_slab)], axis=1))


def _in_proj_lru(x, g, w, conv_w, conv_b, w_gates, b_gates, lru_l):
    B, S, D = x.shape
    d_lru = lru_l.shape[-1]
    d_attn = (w.shape[1] - 2 * d_lru) // 3
    tm = PROJ_TOKENS
    nblk = tm // ATTN_BLOCK
    n_slab = d_lru // LANES
    const = lambda b, i: (0, 0)
    assert d_lru == d_attn
    return pl.pallas_call(
        _in_proj_lru_kernel,
        grid=(B, S // tm),
        in_specs=[
            pl.BlockSpec((None, tm, D), lambda b, i: (b, i, 0)),
            pl.BlockSpec((1, D), const),
            pl.BlockSpec(memory_space=pl.ANY),
            pl.BlockSpec(conv_w.shape, const),
            pl.BlockSpec(conv_b.shape, const),
            pl.BlockSpec(w_gates.shape, lambda b, i: (0, 0, 0)),
            pl.BlockSpec(b_gates.shape, lambda b, i: (0, 0, 0)),
            pl.BlockSpec(lru_l.shape, const),
            pl.BlockSpec((SUBLANES, LANES), const),
        ],
        out_specs=[
            pl.BlockSpec((None, nblk, d_attn, ATTN_BLOCK), lambda b, i: (b, i, 0, 0)),
            pl.BlockSpec((None, tm, d_attn), lambda b, i: (b, i, 0)),
            pl.BlockSpec((None, nblk, d_attn, ATTN_BLOCK), lambda b, i: (b, i, 0, 0)),
            pl.BlockSpec((None, tm, d_lru), lambda b, i: (b, i, 0)),
        ],
        out_shape=[
            jax.ShapeDtypeStruct((B, S // ATTN_BLOCK, d_attn, ATTN_BLOCK), jnp.bfloat16),
            jax.ShapeDtypeStruct((B, S, d_attn), jnp.bfloat16),
            jax.ShapeDtypeStruct((B, S // ATTN_BLOCK, d_attn, ATTN_BLOCK), jnp.bfloat16),
            jax.ShapeDtypeStruct((B, S, d_lru), jnp.bfloat16),
        ],
        scratch_shapes=[
            pltpu.VMEM(w.shape, jnp.bfloat16),
            pltpu.VMEM((W_STAGE_SLOTS, D, d_attn), jnp.float32),
            pltpu.SemaphoreType.DMA((W_STAGE_SLOTS,)),
            pltpu.VMEM((tm, D), jnp.bfloat16),
            pltpu.VMEM((2 * n_slab, tm, LANES), jnp.float32),
            pltpu.VMEM((n_slab, tm, LANES), jnp.float32),
            pltpu.VMEM((tm, d_lru), jnp.bfloat16),
            pltpu.VMEM((2 * n_slab, tm, LANES), jnp.float32),
            pltpu.VMEM((n_slab, tm, LANES), jnp.float32),
            pltpu.VMEM((n_slab, tm, LANES), jnp.float32),
            pltpu.VMEM((CHAIN, SUBLANES, d_lru), jnp.float32),
            pltpu.VMEM((SUBLANES, d_lru), jnp.float32),
        ],
        compiler_params=pltpu.CompilerParams(
            dimension_semantics=("arbitrary", "arbitrary"),
            vmem_limit_bytes=V7X_VMEM_LIMIT_BYTES),
        name="in_proj_lru",
    )(x, g, w, conv_w, conv_b, w_gates, b_gates, lru_l, jnp.zeros((SUBLANES, LANES), jnp.int32))


N_POS_COLS = 128
N_ONES_ROWS = 16
SCORE_LEAD = 2
N_SCORE_BUFS = SCORE_LEAD + 1
ATTN_Q_PER_STEP = 9


def _diff_attn_kernel(qt_ref, k_ref, pos_ref, vt_ref, lq1_ref, lk1_ref, lq2_ref, lk2_ref, sg_ref,
                      o_ref,
                      s_ref, m_ref, acc_ref, *, slopes, q_per_step):
    h = pl.program_id(1)
    blk = ATTN_BLOCK
    n_blocks = vt_ref.shape[0]

    slope = jnp.float32(slopes[0])
    for i in range(1, len(slopes)):
        slope = jnp.where(h == i, jnp.float32(slopes[i]), slope)

    r = lax.broadcasted_iota(jnp.int32, (N_POS_COLS, blk), 0)
    lane = lax.broadcasted_iota(jnp.int32, (N_POS_COLS, blk), 1).astype(jnp.float32)
    zeros = jnp.zeros((HEAD_DIM, blk), qt_ref.dtype)
    key_in_blk = lax.broadcasted_iota(jnp.int32, (blk, 2 * blk), 0)
    qry_in_blk = lax.broadcasted_iota(jnp.int32, (blk, 2 * blk), 1) % blk
    causal = key_in_blk <= qry_in_blk
    ones_row = lax.broadcasted_iota(jnp.int32, (N_ONES_ROWS, blk), 0) == 0
    ones_blk = jnp.where(ones_row, 1.0, 0.0).astype(jnp.bfloat16)

    def q_operand(qt, q_index):
        alibi = _bf16(jnp.where(r == 0, slope,
                      jnp.where(r == 1, slope * blk,
                      jnp.where(r == 2, -slope * lane,
                      jnp.where(r == 3, -slope * float(q_index * blk), 0.0)))))
        return jnp.concatenate(
            [jnp.concatenate([qt[:HEAD_DIM], zeros, alibi], axis=0),
             jnp.concatenate([zeros, qt[HEAD_DIM:], alibi], axis=0)], axis=1)

    def score_block(q_aug, j, diagonal):
        rows = slice(j * blk, (j + 1) * blk)
        k_aug = jnp.concatenate([k_ref[rows, :], pos_ref[rows, :]], axis=1)
        s = jnp.dot(k_aug, q_aug, preferred_element_type=jnp.float32)
        return jnp.where(causal, s, -jnp.inf) if diagonal else s

    def finish(t):
        acc = acc_ref[...]
        lam = (jnp.exp(jnp.sum(lq1_ref[...] * lk1_ref[...], axis=-1, keepdims=True))
               - jnp.exp(jnp.sum(lq2_ref[...] * lk2_ref[...], axis=-1, keepdims=True))
               + LAMBDA_INIT)
        heads = acc[:V_HEAD_DIM] / acc[V_HEAD_DIM:V_HEAD_DIM + 1]
        o = heads[:, :blk] - lam * heads[:, blk:]
        o = o * lax.rsqrt(jnp.mean(o * o, axis=0, keepdims=True) + EPS)
        o_ref[t * blk:(t + 1) * blk, :] = _bf16(o.T * (sg_ref[...] * (1.0 - LAMBDA_INIT)))

    order = list(range(0, n_blocks, 2)) + list(range(n_blocks - 1 - n_blocks % 2, 0, -2))
    assert sorted(order) == list(range(n_blocks))

    def score_q_block(q_aug, q_index, slot, j, m_run):
        s = score_block(q_aug, j, j == q_index)
        s_ref[slot, j * blk:(j + 1) * blk, :] = s
        col_max = jnp.max(s, axis=0, keepdims=True)
        return col_max if m_run is None else jnp.maximum(m_run, col_max)

    def sub_step(t):
        if t >= 1:
            finish(order[t - 1])
        if t == n_blocks:
            return
        if t == 0:
            for pos in range(min(SCORE_LEAD, n_blocks)):
                q_index = order[pos]
                q_aug = q_operand(qt_ref[q_index], q_index)
                m_run = None
                for j in range(q_index + 1):
                    m_run = score_q_block(q_aug, q_index, pos % N_SCORE_BUFS, j, m_run)
                m_ref[pos % N_SCORE_BUFS] = m_run
        cur = t % N_SCORE_BUFS
        q_cur = order[t]
        scoring = t + SCORE_LEAD < n_blocks
        q_ahead = order[t + SCORE_LEAD] if scoring else -1
        slot_ahead = (t + SCORE_LEAD) % N_SCORE_BUFS
        q_aug = q_operand(qt_ref[q_ahead], q_ahead) if scoring else None
        m_cur = m_ref[cur]
        m_run = None
        acc = jnp.zeros((V_HEAD_DIM + N_ONES_ROWS, 2 * blk), jnp.float32)
        for j in range(max(q_cur, q_ahead) + 1):
            if j <= q_ahead:
                m_run = score_q_block(q_aug, q_ahead, slot_ahead, j, m_run)
            if j <= q_cur:
                p = _bf16(jnp.exp(s_ref[cur, j * blk:(j + 1) * blk, :] - m_cur))
                vt_aug = jnp.concatenate([vt_ref[j], ones_blk], axis=0)
                acc = acc + jnp.dot(vt_aug, p, preferred_element_type=jnp.float32)
        if scoring:
            m_ref[slot_ahead] = m_run
        acc_ref[...] = acc

    def grid_step(first):
        def run():
            for t in range(first, min(first + q_per_step, n_blocks + 1)):
                sub_step(t)
        return run

    steps = [grid_step(first) for first in range(0, n_blocks + 1, q_per_step)]
    if len(steps) == 1:
        steps[0]()
    else:
        lax.switch(pl.program_id(2), steps)


def _alibi_slopes(n_heads):
    slopes = [2.0 ** (-8.0 * (i + 1) / n_heads) for i in range(n_heads)]
    assert all(math.frexp(s)[0] == 0.5 for s in slopes), slopes
    return slopes


def _key_position_columns(seq_len):
    j = np.arange(seq_len)
    pos = np.zeros((seq_len, N_POS_COLS), np.float32)
    pos[:, 0] = j % ATTN_BLOCK
    pos[:, 1] = j // ATTN_BLOCK
    pos[:, 2:4] = 1.0
    assert ATTN_BLOCK <= 256 and seq_len // ATTN_BLOCK <= 256
    return jnp.asarray(pos, jnp.bfloat16)


def _diff_attn(qt, k, vt, lambdas, subln_g):
    B, nblk, d_attn, blk = qt.shape
    S = k.shape[1]
    H = d_attn // V_HEAD_DIM
    pos = _key_position_columns(S)
    kernel = functools.partial(_diff_attn_kernel, slopes=tuple(_alibi_slopes(H)),
                               q_per_step=ATTN_Q_PER_STEP)
    n_steps = pl.cdiv(nblk + 1, ATTN_Q_PER_STEP)
    return pl.pallas_call(
        kernel,
        grid=(B, H, n_steps),
        in_specs=[
            pl.BlockSpec((None, nblk, V_HEAD_DIM, blk), lambda b, h, q: (b, 0, h, 0)),
            pl.BlockSpec((None, S, V_HEAD_DIM), lambda b, h, q: (b, 0, h)),
            pl.BlockSpec(pos.shape, lambda b, h, q: (0, 0)),
            pl.BlockSpec((None, nblk, V_HEAD_DIM, blk), lambda b, h, q: (b, 0, h, 0)),
            *[pl.BlockSpec(v.shape, lambda b, h, q: (0, 0)) for v in lambdas],
            pl.BlockSpec(subln_g.shape, lambda b, h, q: (0, 0)),
        ],
        out_specs=pl.BlockSpec((None, S, V_HEAD_DIM), lambda b, h, q: (b, 0, h)),
        out_shape=jax.ShapeDtypeStruct((B, S, d_attn), jnp.bfloat16),
        scratch_shapes=[pltpu.VMEM((N_SCORE_BUFS, S, 2 * blk), jnp.float32),
                        pltpu.VMEM((N_SCORE_BUFS, 1, 2 * blk), jnp.float32),
                        pltpu.VMEM((V_HEAD_DIM + N_ONES_ROWS, 2 * blk), jnp.float32)],
        compiler_params=pltpu.CompilerParams(
            dimension_semantics=("parallel", "parallel", "arbitrary"),
            vmem_limit_bytes=V7X_VMEM_LIMIT_BYTES),
        name="diff_attn",
    )(qt, k, pos, vt, *lambdas, subln_g)


def _out_mlp_kernel(x_ref, attn_ref, lru_ref, wo_hbm, gm_ref, wup_hbm, wdn_hbm, gf_ref, y_ref,
                    wo_ref, wup_ref, wdn_ref, stage_ref, sem):
    tm = x_ref.shape[0]
    d_ff = wup_ref.shape[1]
    n_chunks = d_ff // FF_CHUNK
    rows = [slice(g * tm // MLP_SPLIT, (g + 1) * tm // MLP_SPLIT) for g in range(MLP_SPLIT)]

    def ff(c):
        return pl.ds(c * FF_CHUNK, FF_CHUNK)

    sources = [wo_hbm]
    targets = [lambda v: wo_ref.__setitem__((slice(None), slice(None)), v)]
    for c in range(n_chunks):
        sources += [wup_hbm.at[:, ff(c)], wdn_hbm.at[ff(c), :]]
        targets += [lambda v, c=c: wup_ref.__setitem__((slice(None), ff(c)), v),
                    lambda v, c=c: wdn_ref.__setitem__((ff(c), slice(None)), v)]

    n_slots = stage_ref.shape[0]

    def copy(i):
        return pltpu.make_async_copy(sources[i], stage_ref.at[i % n_slots], sem.at[i % n_slots])

    def land(i):
        copy(i).wait()
        targets[i](_bf16(stage_ref[i % n_slots]))
        if i + n_slots < len(sources):
            copy(i + n_slots).start()

    def step_body(first_step):
        if first_step:
            for i in range(n_slots):
                copy(i).start()
            land(0)
        x1 = []
        for r in rows:
            mix = jnp.concatenate([attn_ref[r, :], lru_ref[r, :]], axis=-1)
            x1.append(x_ref[r, :] + jnp.dot(mix, wo_ref[...], preferred_element_type=jnp.float32))
        hm = [_bf16(_rmsnorm_rows(v, gm_ref[...])) for v in x1]
        mlp = [jnp.zeros_like(v) for v in x1]
        for c in range(n_chunks):
            cols = slice(c * FF_CHUNK, (c + 1) * FF_CHUNK)
            if first_step:
                land(1 + 2 * c)
            act = [_bf16(jnp.square(jnp.maximum(
                jnp.dot(hm[g], wup_ref[:, cols], preferred_element_type=jnp.float32), 0.0)))
                for g in range(MLP_SPLIT)]
            if first_step:
                land(2 + 2 * c)
            for g in range(MLP_SPLIT):
                mlp[g] = mlp[g] + jnp.dot(act[g], wdn_ref[cols, :],
                                          preferred_element_type=jnp.float32)
        for g, r in enumerate(rows):
            y_ref[r, :] = _rmsnorm_rows(x1[g] + mlp[g], gf_ref[...])

    pl.when(pl.program_id(0) == 0)(lambda: step_body(True))
    pl.when(pl.program_id(0) != 0)(lambda: step_body(False))


def _out_mlp(x, attn, lru, w_out, g_mlp, w_up, w_down, g_final):
    B, S, D = x.shape
    T = B * S
    tm = MLP_TOKENS
    x2 = x.reshape(T, D)
    attn2 = attn.reshape(T, attn.shape[-1])
    lru2 = lru.reshape(T, lru.shape[-1])
    const = lambda i: (0, 0)
    in_hbm = pl.BlockSpec(memory_space=pl.ANY)
    assert w_out.shape == (D, D) and FF_CHUNK == D and w_up.shape[1] % FF_CHUNK == 0
    y = pl.pallas_call(
        _out_mlp_kernel,
        grid=(T // tm,),
        in_specs=[
            pl.BlockSpec((tm, D), lambda i: (i, 0)),
            pl.BlockSpec((tm, attn2.shape[1]), lambda i: (i, 0)),
            pl.BlockSpec((tm, lru2.shape[1]), lambda i: (i, 0)),
            in_hbm,
            pl.BlockSpec(g_mlp.shape, const),
            in_hbm,
            in_hbm,
            pl.BlockSpec(g_final.shape, const),
        ],
        out_specs=pl.BlockSpec((tm, D), lambda i: (i, 0)),
        out_shape=jax.ShapeDtypeStruct((T, D), jnp.float32),
        scratch_shapes=[
            pltpu.VMEM(w_out.shape, jnp.bfloat16),
            pltpu.VMEM(w_up.shape, jnp.bfloat16),
            pltpu.VMEM(w_down.shape, jnp.bfloat16),
            pltpu.VMEM((W_STAGE_SLOTS, D, FF_CHUNK), jnp.float32),
            pltpu.SemaphoreType.DMA((W_STAGE_SLOTS,)),
        ],
        compiler_params=pltpu.CompilerParams(
            dimension_semantics=("arbitrary",),
            vmem_limit_bytes=V7X_VMEM_LIMIT_BYTES),
        name="out_mlp",
    )(x2, attn2, lru2, w_out, g_mlp, w_up, w_down, g_final)
    return y.reshape(B, S, D)


def _pair_block_diag(w):
    n, c, d = w.shape
    rows = w.reshape(n // 2, 2 * c, d)
    tiled = jnp.concatenate([rows, rows], axis=-1)
    row_block = np.arange(2 * c)[:, None] // c
    col_block = np.arange(2 * d)[None, :] // d
    return jnp.where(jnp.asarray(row_block == col_block), tiled, 0.0)


def kernel(x, norm_mix_g, w_in, conv_w, conv_b, w_rg, b_rg, w_ig, b_ig, lru_L, lambda_q1, lambda_k1, lambda_q2, lambda_k2, subln_g, w_out, norm_mlp_g, w_up, w_down, final_g):
    B, S, D = x.shape
    d_lru = lru_L.shape[-1]
    d_attn = (w_in.shape[-1] - 2 * d_lru) // 3
    depth = w_in.shape[0]
    n_slab = d_lru // LANES
    assert depth == 1, "LAMBDA_INIT is specialised to a single layer"
    assert S % PROJ_TOKENS == 0 and (B * S) % MLP_TOKENS == 0
    assert PROJ_TOKENS % SUB_ROWS == 0 and d_lru % LANES == 0
    assert 2 * w_rg.shape[-1] == LANES and w_rg.shape[1] * w_rg.shape[2] == d_lru
    l = 0
    row = lambda v: v.reshape(1, -1)

    w_gates = _bf16(0.5 * jnp.concatenate([_pair_block_diag(w_rg[l]), _pair_block_diag(w_ig[l])],
                                          axis=-1))
    b_gates = 0.5 * jnp.concatenate([b_rg[l].reshape(n_slab, 1, LANES),
                                     b_ig[l].reshape(n_slab, 1, LANES)], axis=-1)
    lambdas = [row(v[l]) for v in (lambda_q1, lambda_k1, lambda_q2, lambda_k2)]

    qt, k, vt, lru = _in_proj_lru(x, row(norm_mix_g[l]), w_in[l], conv_w[l], row(conv_b[l]),
                                  w_gates, b_gates, row(lru_L[l]))
    attn = _diff_attn(qt, k, vt, lambdas, row(subln_g[l]))
    return _out_mlp(x, attn, lru, w_out[l], row(norm_mlp_g[l]), w_up[l], w_down[l], row(final_g))
```

```python
import functools
import math

import numpy as np
import jax
import jax.numpy as jnp
from jax import lax
from jax.experimental import pallas as pl
from jax.experimental.pallas import tpu as pltpu

EPS = 1e-6
CONV_WIDTH = 4
LRU_C = 8.0
HEAD_DIM = 64
V_HEAD_DIM = 2 * HEAD_DIM
LAMBDA_INIT = 0.8 - 0.6 * math.exp(-0.3 * 0)

V7X_MXU_COLS = 256
LANES = 128
SUBLANES = 8
V7X_VMEM_LIMIT_BYTES = 56 * 1024 * 1024

ATTN_BLOCK = V7X_MXU_COLS
PROJ_TOKENS = 512
MLP_TOKENS = 512
FF_CHUNK = 1024
MLP_SPLIT = 2
CHAIN = 4
SUB_ROWS = SUBLANES * CHAIN


def _bf16(x):
    return x.astype(jnp.bfloat16)


def _rmsnorm_rows(x, g):
    return x * lax.rsqrt(jnp.mean(x * x, axis=-1, keepdims=True) + EPS) * g


def _log_sigmoid(x):
    return jnp.minimum(x, 0.0) - jnp.log1p(jnp.exp(-jnp.abs(x)))


def _in_proj_lru_kernel(x_ref, g_ref, w_hbm, cw_ref, cb_ref, wg_ref, bg_ref, lru_l_ref, zero_ref,
                        wo_f32, wup_f32, wdn_f32,
                        qt_ref, k_ref, vt_ref, lru_ref, wo_bf16, wup_bf16, wdn_bf16,
                        w_ref, w_stage, w_sem, h_scr, ug_slab, u_scr, ub_scr, z_scr, gelu_scr,
                        out_slab, tail_ref, carry_ref):
    d_lru = lru_ref.shape[-1]
    d_attn = k_ref.shape[-1]

    for src, dst in ((wo_f32, wo_bf16), (wup_f32, wup_bf16), (wdn_f32, wdn_bf16)):
        dst[...] = _bf16(src[...])

    piece_cols = [3 * d_attn, 3 * d_attn + d_lru, d_attn, 0, 2 * d_attn]

    def copy(i):
        return pltpu.make_async_copy(w_hbm.at[:, pl.ds(piece_cols[i], d_attn)],
                                     w_stage.at[i % 2], w_sem.at[i % 2])

    def land(i):
        copy(i).wait()
        chunk = w_stage[i % 2]
        if piece_cols[i] == 0:
            chunk = chunk * HEAD_DIM ** -0.5
        w_ref[:, piece_cols[i]:piece_cols[i] + d_attn] = _bf16(chunk)
        if i + 2 < len(piece_cols):
            copy(i + 2).start()

    @pl.when(pl.program_id(1) == 0)
    def _():
        tail_ref[...] = jnp.zeros(tail_ref.shape, jnp.float32)
        carry_ref[...] = jnp.zeros(carry_ref.shape, jnp.float32)

    step = functools.partial(_in_proj_lru_step, x_ref, g_ref, w_ref, cw_ref, cb_ref, wg_ref, bg_ref,
                             lru_l_ref, zero_ref, qt_ref, k_ref, vt_ref, lru_ref, h_scr, ug_slab,
                             u_scr, ub_scr, z_scr, gelu_scr, out_slab, tail_ref, carry_ref)

    def first_step():
        copy(0).start()
        copy(1).start()
        step(land)

    first_tile = (pl.program_id(0) == 0) & (pl.program_id(1) == 0)
    pl.when(first_tile)(first_step)
    pl.when(jnp.logical_not(first_tile))(lambda: step(None))


def _in_proj_lru_step(x_ref, g_ref, w_ref, cw_ref, cb_ref, wg_ref, bg_ref, lru_l_ref, zero_ref,
                      qt_ref, k_ref, vt_ref, lru_ref, h_scr, ug_slab, u_scr, ub_scr, z_scr, gelu_scr,
                      out_slab, tail_ref, carry_ref, land):
    tm = x_ref.shape[0]
    d_lru = lru_ref.shape[-1]
    n_slab = d_lru // LANES
    n_sub = tm // SUB_ROWS
    d_attn = k_ref.shape[-1]
    f32 = jnp.float32

    h_scr[...] = _bf16(_rmsnorm_rows(x_ref[...], g_ref[...]))

    def ug_half(half):
        if land:
            land(half)
        cols = slice(3 * d_attn + half * d_lru, 3 * d_attn + (half + 1) * d_lru)
        nat = jnp.dot(h_scr[...], w_ref[:, cols], preferred_element_type=f32)
        for l in range(n_slab):
            ug_slab[half * n_slab + l] = nat[:, l * LANES:(l + 1) * LANES]

    ug_half(0)

    def piece(slab, s, c):
        return ug_slab[slab, pl.ds(s * SUB_ROWS + c, SUBLANES, stride=CHAIN), :]

    row = lax.broadcasted_iota(jnp.int32, (SUBLANES, LANES), 0)

    for l in range(n_slab):
        lanes = slice(l * LANES, (l + 1) * LANES)
        cw = cw_ref[:, lanes]
        cb = cb_ref[:, lanes]
        rolled_prev = [pltpu.roll(tail_ref[c, :, lanes], 1, axis=0) for c in range(CHAIN)]
        for s in range(n_sub):
            cur = [piece(l, s, c) for c in range(CHAIN)]
            rolled = [pltpu.roll(cur[c], 1, axis=0) for c in range(CHAIN)]
            shifted = [jnp.where(row == 0, rolled_prev[c], rolled[c]) for c in range(CHAIN)]
            rolled_prev = rolled
            if s == n_sub - 1:
                for c in range(CHAIN):
                    tail_ref[c, :, lanes] = cur[c]
            conv = []
            for c in range(CHAIN):
                acc = cb
                for kk in range(CONV_WIDTH):
                    d = CONV_WIDTH - 1 - kk
                    src = cur[c - d] if c >= d else shifted[c - d + CHAIN]
                    acc = acc + src * cw[kk:kk + 1, :]
                conv.append(acc)
            u_sub = jnp.concatenate(conv, axis=0)
            u_scr[l, s * SUB_ROWS:(s + 1) * SUB_ROWS, :] = u_sub
            ub_scr[s * SUB_ROWS:(s + 1) * SUB_ROWS, lanes] = _bf16(u_sub)

    ug_half(1)
    for l in range(n_slab):
        z = jnp.dot(ub_scr[:, l * LANES:(l + 1) * LANES], wg_ref[l],
                    preferred_element_type=f32) + bg_ref[l]
        z_scr[l] = z[:, :LANES]
        z_scr[n_slab + l] = z[:, LANES:]

    def k_piece(part):
        cols = slice(part * ATTN_BLOCK, (part + 1) * ATTN_BLOCK)
        t = jnp.dot(h_scr[...], w_ref[:, d_attn + part * ATTN_BLOCK:
                                     d_attn + (part + 1) * ATTN_BLOCK], preferred_element_type=f32)
        k_ref[:, cols] = _bf16(t)
        return [t[tm - SUBLANES:, c * LANES:(c + 1) * LANES] for c in range(ATTN_BLOCK // LANES)]

    def transposed_piece(first_col, o_ref, part):
        rows = slice(part * ATTN_BLOCK, (part + 1) * ATTN_BLOCK)
        t = jnp.dot(h_scr[rows, :], w_ref[:, first_col:first_col + d_attn],
                    preferred_element_type=f32)
        o_ref[part] = _bf16(t.T)
        return [t[ATTN_BLOCK - SUBLANES:, c * LANES:(c + 1) * LANES] for c in range(d_attn // LANES)]

    side_work = {0: lambda: k_piece(0), 2: lambda: k_piece(1),
                 5: lambda: transposed_piece(0, qt_ref, 0),
                 8: lambda: transposed_piece(0, qt_ref, 1),
                 11: lambda: transposed_piece(2 * d_attn, vt_ref, 0),
                 13: lambda: transposed_piece(2 * d_attn, vt_ref, 1)}
    land_before = {0: 2, 5: 3, 11: 4}
    assert n_sub > max(side_work) and qt_ref.shape[0] == 2 and d_attn == 2 * ATTN_BLOCK

    c_gelu = math.sqrt(2.0 / math.pi)
    for l in range(n_slab):
        for s in range(n_sub):
            gate = jnp.concatenate([piece(n_slab + l, s, c) for c in range(CHAIN)], axis=0)
            gelu_scr[l, s * SUB_ROWS:(s + 1) * SUB_ROWS, :] = (0.5 * gate) * (1.0 + jnp.tanh(
                gate * (c_gelu + (c_gelu * 0.044715) * (gate * gate))))

    c2_row = (0.5 * LRU_C * math.log2(math.e)) * _log_sigmoid(lru_l_ref[...])

    carry = [carry_ref[:, l * LANES:(l + 1) * LANES] for l in range(n_slab)]
    pending_at = -1
    keep_state = zero_ref[...] == 0
    for s in range(n_sub):
        if land and s in land_before:
            land(land_before[s])
        if s in side_work:
            anchor = functools.reduce(jnp.add, side_work[s]())
            pending_at = s + 1
        base = s * SUB_ROWS
        rows = slice(base, base + SUB_ROWS)
        for l in range(n_slab):
            c2 = c2_row[:, l * LANES:(l + 1) * LANES]
            u = u_scr[l, rows, :]
            t_r = jnp.tanh(z_scr[l, rows, :])
            t_i = jnp.tanh(z_scr[n_slab + l, rows, :])
            log2_a = c2 + c2 * t_r
            a = jnp.exp2(log2_a)
            quarter = (jnp.tanh(log2_a * math.log(2.0)) * -0.25) * (a * a + 1.0)
            half_mult = jnp.where(quarter > 0.0, quarter * lax.rsqrt(quarter), 0.0)
            b = (half_mult * u) * (1.0 + t_i)
            gelu = gelu_scr[l, rows, :]

            a_c = [a[c * SUBLANES:(c + 1) * SUBLANES] for c in range(CHAIN)]
            b_c = [b[c * SUBLANES:(c + 1) * SUBLANES] for c in range(CHAIN)]
            h_loc, p_loc = [b_c[0]], [a_c[0]]
            for c in range(1, CHAIN):
                h_loc.append(a_c[c] * h_loc[c - 1] + b_c[c])
                p_loc.append(a_c[c] * p_loc[c - 1])
            p_cum, h_cum = p_loc[CHAIN - 1], h_loc[CHAIN - 1]
            shift = 1
            while shift < SUBLANES:
                valid = row >= shift
                p_prev = jnp.where(valid, pltpu.roll(p_cum, shift, axis=0), 1.0)
                h_prev = jnp.where(valid, pltpu.roll(h_cum, shift, axis=0), 0.0)
                h_cum = p_cum * h_prev + h_cum
                p_cum = p_cum * p_prev
                shift *= 2
            chain_end = h_cum + p_cum * carry[l]
            chain_in = jnp.where(row == 0, carry[l], pltpu.roll(chain_end, 1, axis=0))
            carry[l] = jnp.broadcast_to(chain_end[SUBLANES - 1:SUBLANES, :], (SUBLANES, LANES))
            if s == pending_at:
                carry[l] = jnp.where(keep_state, carry[l], anchor)
            for c in range(CHAIN):
                out = (h_loc[c] + p_loc[c] * chain_in) * gelu[c * SUBLANES:(c + 1) * SUBLANES]
                out_slab[l, pl.ds(base + c, SUBLANES, stride=CHAIN), :] = out
    for l in range(n_slab):
        carry_ref[:, l * LANES:(l + 1) * LANES] = carry[l]
    lru_ref[...] = _bf16(jnp.concatenate([out_slab[l] for l in range(n_slab)], axis=1))


def _in_proj_lru(x, g, w, conv_w, conv_b, w_gates, b_gates, lru_l, later_weights):
    B, S, D = x.shape
    d_lru = lru_l.shape[-1]
    d_attn = (w.shape[1] - 2 * d_lru) // 3
    tm = PROJ_TOKENS
    nblk = tm // ATTN_BLOCK
    n_slab = d_lru // LANES
    const = lambda b, i: (0, 0)
    assert d_lru == d_attn
    n_steps = B * (S // tm)
    assert all(m.shape[0] % (n_steps * 2 * SUBLANES) == 0 for m in later_weights)
    row_block = lambda b, i: (b * (S // tm) + i, 0)
    later_specs = [pl.BlockSpec((m.shape[0] // n_steps, m.shape[1]), row_block)
                   for m in later_weights]
    return pl.pallas_call(
        _in_proj_lru_kernel,
        grid=(B, S // tm),
        in_specs=[
            pl.BlockSpec((None, tm, D), lambda b, i: (b, i, 0)),
            pl.BlockSpec((1, D), const),
            pl.BlockSpec(memory_space=pl.ANY),
            pl.BlockSpec(conv_w.shape, const),
            pl.BlockSpec(conv_b.shape, const),
            pl.BlockSpec(w_gates.shape, lambda b, i: (0, 0, 0)),
            pl.BlockSpec(b_gates.shape, lambda b, i: (0, 0, 0)),
            pl.BlockSpec(lru_l.shape, const),
            pl.BlockSpec((SUBLANES, LANES), const),
        ] + later_specs,
        out_specs=[
            pl.BlockSpec((None, nblk, d_attn, ATTN_BLOCK), lambda b, i: (b, i, 0, 0)),
            pl.BlockSpec((None, tm, d_attn), lambda b, i: (b, i, 0)),
            pl.BlockSpec((None, nblk, d_attn, ATTN_BLOCK), lambda b, i: (b, i, 0, 0)),
            pl.BlockSpec((None, tm, d_lru), lambda b, i: (b, i, 0)),
        ] + later_specs,
        out_shape=[
            jax.ShapeDtypeStruct((B, S // ATTN_BLOCK, d_attn, ATTN_BLOCK), jnp.bfloat16),
            jax.ShapeDtypeStruct((B, S, d_attn), jnp.bfloat16),
            jax.ShapeDtypeStruct((B, S // ATTN_BLOCK, d_attn, ATTN_BLOCK), jnp.bfloat16),
            jax.ShapeDtypeStruct((B, S, d_lru), jnp.bfloat16),
        ] + [jax.ShapeDtypeStruct(m.shape, jnp.bfloat16) for m in later_weights],
        scratch_shapes=[
            pltpu.VMEM(w.shape, jnp.bfloat16),
            pltpu.VMEM((2, D, d_attn), jnp.float32),
            pltpu.SemaphoreType.DMA((2,)),
            pltpu.VMEM((tm, D), jnp.bfloat16),
            pltpu.VMEM((2 * n_slab, tm, LANES), jnp.float32),
            pltpu.VMEM((n_slab, tm, LANES), jnp.float32),
            pltpu.VMEM((tm, d_lru), jnp.bfloat16),
            pltpu.VMEM((2 * n_slab, tm, LANES), jnp.float32),
            pltpu.VMEM((n_slab, tm, LANES), jnp.float32),
            pltpu.VMEM((n_slab, tm, LANES), jnp.float32),
            pltpu.VMEM((CHAIN, SUBLANES, d_lru), jnp.float32),
            pltpu.VMEM((SUBLANES, d_lru), jnp.float32),
        ],
        compiler_params=pltpu.CompilerParams(
            dimension_semantics=("arbitrary", "arbitrary"),
            vmem_limit_bytes=V7X_VMEM_LIMIT_BYTES),
        name="in_proj_lru",
    )(x, g, w, conv_w, conv_b, w_gates, b_gates, lru_l, jnp.zeros((SUBLANES, LANES), jnp.int32),
      *later_weights)


N_POS_COLS = 128
N_ONES_ROWS = 16
SCORE_LEAD = 2
N_SCORE_BUFS = SCORE_LEAD + 1
ATTN_Q_PER_STEP = 9


def _diff_attn_kernel(qt_ref, k_ref, pos_ref, vt_ref, lq1_ref, lk1_ref, lq2_ref, lk2_ref, sg_ref,
                      o_ref,
                      s_ref, m_ref, acc_ref, *, slopes, q_per_step):
    h = pl.program_id(1)
    blk = ATTN_BLOCK
    n_blocks = vt_ref.shape[0]

    slope = jnp.float32(slopes[0])
    for i in range(1, len(slopes)):
        slope = jnp.where(h == i, jnp.float32(slopes[i]), slope)

    r = lax.broadcasted_iota(jnp.int32, (N_POS_COLS, blk), 0)
    lane = lax.broadcasted_iota(jnp.int32, (N_POS_COLS, blk), 1).astype(jnp.float32)
    zeros = jnp.zeros((HEAD_DIM, blk), qt_ref.dtype)
    key_in_blk = lax.broadcasted_iota(jnp.int32, (blk, 2 * blk), 0)
    qry_in_blk = lax.broadcasted_iota(jnp.int32, (blk, 2 * blk), 1) % blk
    causal = key_in_blk <= qry_in_blk
    ones_row = lax.broadcasted_iota(jnp.int32, (N_ONES_ROWS, blk), 0) == 0
    ones_blk = jnp.where(ones_row, 1.0, 0.0).astype(jnp.bfloat16)

    def q_operand(qt, q_index):
        alibi = _bf16(jnp.where(r == 0, slope,
                      jnp.where(r == 1, slope * blk,
                      jnp.where(r == 2, -slope * lane,
                      jnp.where(r == 3, -slope * float(q_index * blk), 0.0)))))
        return jnp.concatenate(
            [jnp.concatenate([qt[:HEAD_DIM], zeros, alibi], axis=0),
             jnp.concatenate([zeros, qt[HEAD_DIM:], alibi], axis=0)], axis=1)

    def score_block(q_aug, j, diagonal):
        rows = slice(j * blk, (j + 1) * blk)
        k_aug = jnp.concatenate([k_ref[rows, :], pos_ref[rows, :]], axis=1)
        s = jnp.dot(k_aug, q_aug, preferred_element_type=jnp.float32)
        return jnp.where(causal, s, -jnp.inf) if diagonal else s

    def finish(t):
        acc = acc_ref[...]
        lam = (jnp.exp(jnp.sum(lq1_ref[...] * lk1_ref[...], axis=-1, keepdims=True))
               - jnp.exp(jnp.sum(lq2_ref[...] * lk2_ref[...], axis=-1, keepdims=True))
               + LAMBDA_INIT)
        heads = acc[:V_HEAD_DIM] / acc[V_HEAD_DIM:V_HEAD_DIM + 1]
        o = heads[:, :blk] - lam * heads[:, blk:]
        o = o * lax.rsqrt(jnp.mean(o * o, axis=0, keepdims=True) + EPS)
        o_ref[t * blk:(t + 1) * blk, :] = _bf16(o.T * (sg_ref[...] * (1.0 - LAMBDA_INIT)))

    order = list(range(0, n_blocks, 2)) + list(range(n_blocks - 1 - n_blocks % 2, 0, -2))
    assert sorted(order) == list(range(n_blocks))

    def score_q_block(q_aug, q_index, slot, j, m_run):
        s = score_block(q_aug, j, j == q_index)
        s_ref[slot, j * blk:(j + 1) * blk, :] = s
        col_max = jnp.max(s, axis=0, keepdims=True)
        return col_max if m_run is None else jnp.maximum(m_run, col_max)

    def sub_step(t):
        if t >= 1:
            finish(order[t - 1])
        if t == n_blocks:
            return
        if t == 0:
            for pos in range(min(SCORE_LEAD, n_blocks)):
                q_index = order[pos]
                q_aug = q_operand(qt_ref[q_index], q_index)
                m_run = None
                for j in range(q_index + 1):
                    m_run = score_q_block(q_aug, q_index, pos % N_SCORE_BUFS, j, m_run)
                m_ref[pos % N_SCORE_BUFS] = m_run
        cur = t % N_SCORE_BUFS
        q_cur = order[t]
        scoring = t + SCORE_LEAD < n_blocks
        q_ahead = order[t + SCORE_LEAD] if scoring else -1
        slot_ahead = (t + SCORE_LEAD) % N_SCORE_BUFS
        q_aug = q_operand(qt_ref[q_ahead], q_ahead) if scoring else None
        m_cur = m_ref[cur]
        m_run = None
        acc = jnp.zeros((V_HEAD_DIM + N_ONES_ROWS, 2 * blk), jnp.float32)
        for j in range(max(q_cur, q_ahead) + 1):
            if j <= q_ahead:
                m_run = score_q_block(q_aug, q_ahead, slot_ahead, j, m_run)
            if j <= q_cur:
                p = _bf16(jnp.exp(s_ref[cur, j * blk:(j + 1) * blk, :] - m_cur))
                vt_aug = jnp.concatenate([vt_ref[j], ones_blk], axis=0)
                acc = acc + jnp.dot(vt_aug, p, preferred_element_type=jnp.float32)
        if scoring:
            m_ref[slot_ahead] = m_run
        acc_ref[...] = acc

    def grid_step(first):
        def run():
            for t in range(first, min(first + q_per_step, n_blocks + 1)):
                sub_step(t)
        return run

    steps = [grid_step(first) for first in range(0, n_blocks + 1, q_per_step)]
    if len(steps) == 1:
        steps[0]()
    else:
        lax.switch(pl.program_id(2), steps)


def _alibi_slopes(n_heads):
    slopes = [2.0 ** (-8.0 * (i + 1) / n_heads) for i in range(n_heads)]
    assert all(math.frexp(s)[0] == 0.5 for s in slopes), slopes
    return slopes


def _key_position_columns(seq_len):
    j = np.arange(seq_len)
    pos = np.zeros((seq_len, N_POS_COLS), np.float32)
    pos[:, 0] = j % ATTN_BLOCK
    pos[:, 1] = j // ATTN_BLOCK
    pos[:, 2:4] = 1.0
    assert ATTN_BLOCK <= 256 and seq_len // ATTN_BLOCK <= 256
    return jnp.asarray(pos, jnp.bfloat16)


def _diff_attn(qt, k, vt, lambdas, subln_g):
    B, nblk, d_attn, blk = qt.shape
    S = k.shape[1]
    H = d_attn // V_HEAD_DIM
    pos = _key_position_columns(S)
    kernel = functools.partial(_diff_attn_kernel, slopes=tuple(_alibi_slopes(H)),
                               q_per_step=ATTN_Q_PER_STEP)
    n_steps = pl.cdiv(nblk + 1, ATTN_Q_PER_STEP)
    return pl.pallas_call(
        kernel,
        grid=(B, H, n_steps),
        in_specs=[
            pl.BlockSpec((None, nblk, V_HEAD_DIM, blk), lambda b, h, q: (b, 0, h, 0)),
            pl.BlockSpec((None, S, V_HEAD_DIM), lambda b, h, q: (b, 0, h)),
            pl.BlockSpec(pos.shape, lambda b, h, q: (0, 0)),
            pl.BlockSpec((None, nblk, V_HEAD_DIM, blk), lambda b, h, q: (b, 0, h, 0)),
            *[pl.BlockSpec(v.shape, lambda b, h, q: (0, 0)) for v in lambdas],
            pl.BlockSpec(subln_g.shape, lambda b, h, q: (0, 0)),
        ],
        out_specs=pl.BlockSpec((None, S, V_HEAD_DIM), lambda b, h, q: (b, 0, h)),
        out_shape=jax.ShapeDtypeStruct((B, S, d_attn), jnp.bfloat16),
        scratch_shapes=[pltpu.VMEM((N_SCORE_BUFS, S, 2 * blk), jnp.float32),
                        pltpu.VMEM((N_SCORE_BUFS, 1, 2 * blk), jnp.float32),
                        pltpu.VMEM((V_HEAD_DIM + N_ONES_ROWS, 2 * blk), jnp.float32)],
        compiler_params=pltpu.CompilerParams(
            dimension_semantics=("parallel", "parallel", "arbitrary"),
            vmem_limit_bytes=V7X_VMEM_LIMIT_BYTES),
        name="diff_attn",
    )(qt, k, pos, vt, *lambdas, subln_g)


def _out_mlp_kernel(x_ref, attn_ref, lru_ref, wo_hbm, gm_ref, wup_hbm, wdn_hbm, gf_ref, y_ref,
                    wo_ref, wup_ref, wdn_ref, sem):
    tm = x_ref.shape[0]
    d_ff = wup_ref.shape[1]
    n_chunks = d_ff // FF_CHUNK
    rows = [slice(g * tm // MLP_SPLIT, (g + 1) * tm // MLP_SPLIT) for g in range(MLP_SPLIT)]

    def ff(c):
        return pl.ds(c * FF_CHUNK, FF_CHUNK)

    sources = [wo_hbm]
    targets = [wo_ref]
    for c in range(n_chunks):
        sources += [wup_hbm.at[:, ff(c)], wdn_hbm.at[ff(c), :]]
        targets += [wup_ref.at[:, ff(c)], wdn_ref.at[ff(c), :]]
    assert len(sources) == sem.shape[0]

    def copy(i):
        return pltpu.make_async_copy(sources[i], targets[i], sem.at[i])

    def land(i):
        copy(i).wait()

    def step_body(first_step):
        if first_step:
            for i in range(len(sources)):
                copy(i).start()
            land(0)
        x1 = []
        for r in rows:
            mix = jnp.concatenate([attn_ref[r, :], lru_ref[r, :]], axis=-1)
            x1.append(x_ref[r, :] + jnp.dot(mix, wo_ref[...], preferred_element_type=jnp.float32))
        hm = [_bf16(_rmsnorm_rows(v, gm_ref[...])) for v in x1]
        mlp = [jnp.zeros_like(v) for v in x1]
        for c in range(n_chunks):
            cols = slice(c * FF_CHUNK, (c + 1) * FF_CHUNK)
            if first_step:
                land(1 + 2 * c)
            act = [_bf16(jnp.square(jnp.maximum(
                jnp.dot(hm[g], wup_ref[:, cols], preferred_element_type=jnp.float32), 0.0)))
                for g in range(MLP_SPLIT)]
            if first_step:
                land(2 + 2 * c)
            for g in range(MLP_SPLIT):
                mlp[g] = mlp[g] + jnp.dot(act[g], wdn_ref[cols, :],
                                          preferred_element_type=jnp.float32)
        for g, r in enumerate(rows):
            y_ref[r, :] = _rmsnorm_rows(x1[g] + mlp[g], gf_ref[...])

    pl.when(pl.program_id(0) == 0)(lambda: step_body(True))
    pl.when(pl.program_id(0) != 0)(lambda: step_body(False))


def _out_mlp(x, attn, lru, w_out, g_mlp, w_up, w_down, g_final):
    B, S, D = x.shape
    T = B * S
    tm = MLP_TOKENS
    x2 = x.reshape(T, D)
    attn2 = attn.reshape(T, attn.shape[-1])
    lru2 = lru.reshape(T, lru.shape[-1])
    const = lambda i: (0, 0)
    in_hbm = pl.BlockSpec(memory_space=pl.ANY)
    assert w_out.shape == (D, D) and w_up.shape[1] % FF_CHUNK == 0
    assert w_out.dtype == w_up.dtype == w_down.dtype == jnp.bfloat16
    y = pl.pallas_call(
        _out_mlp_kernel,
        grid=(T // tm,),
        in_specs=[
            pl.BlockSpec((tm, D), lambda i: (i, 0)),
            pl.BlockSpec((tm, attn2.shape[1]), lambda i: (i, 0)),
            pl.BlockSpec((tm, lru2.shape[1]), lambda i: (i, 0)),
            in_hbm,
            pl.BlockSpec(g_mlp.shape, const),
            in_hbm,
            in_hbm,
            pl.BlockSpec(g_final.shape, const),
        ],
        out_specs=pl.BlockSpec((tm, D), lambda i: (i, 0)),
        out_shape=jax.ShapeDtypeStruct((T, D), jnp.float32),
        scratch_shapes=[
            pltpu.VMEM(w_out.shape, jnp.bfloat16),
            pltpu.VMEM(w_up.shape, jnp.bfloat16),
            pltpu.VMEM(w_down.shape, jnp.bfloat16),
            pltpu.SemaphoreType.DMA((1 + 2 * (w_up.shape[1] // FF_CHUNK),)),
        ],
        compiler_params=pltpu.CompilerParams(
            dimension_semantics=("arbitrary",),
            vmem_limit_bytes=V7X_VMEM_LIMIT_BYTES),
        name="out_mlp",
    )(x2, attn2, lru2, w_out, g_mlp, w_up, w_down, g_final)
    return y.reshape(B, S, D)


def _pair_block_diag(w):
    n, c, d = w.shape
    rows = w.reshape(n // 2, 2 * c, d)
    tiled = jnp.concatenate([rows, rows], axis=-1)
    row_block = np.arange(2 * c)[:, None] // c
    col_block = np.arange(2 * d)[None, :] // d
    return jnp.where(jnp.asarray(row_block == col_block), tiled, 0.0)


def kernel(x, norm_mix_g, w_in, conv_w, conv_b, w_rg, b_rg, w_ig, b_ig, lru_L, lambda_q1, lambda_k1, lambda_q2, lambda_k2, subln_g, w_out, norm_mlp_g, w_up, w_down, final_g):
    B, S, D = x.shape
    d_lru = lru_L.shape[-1]
    d_attn = (w_in.shape[-1] - 2 * d_lru) // 3
    depth = w_in.shape[0]
    n_slab = d_lru // LANES
    assert depth == 1, "LAMBDA_INIT is specialised to a single layer"
    assert S % PROJ_TOKENS == 0 and (B * S) % MLP_TOKENS == 0
    assert PROJ_TOKENS % SUB_ROWS == 0 and d_lru % LANES == 0
    assert 2 * w_rg.shape[-1] == LANES and w_rg.shape[1] * w_rg.shape[2] == d_lru
    l = 0
    row = lambda v: v.reshape(1, -1)

    w_gates = _bf16(0.5 * jnp.concatenate([_pair_block_diag(w_rg[l]), _pair_block_diag(w_ig[l])],
                                          axis=-1))
    b_gates = 0.5 * jnp.concatenate([b_rg[l].reshape(n_slab, 1, LANES),
                                     b_ig[l].reshape(n_slab, 1, LANES)], axis=-1)
    lambdas = [row(v[l]) for v in (lambda_q1, lambda_k1, lambda_q2, lambda_k2)]

    qt, k, vt, lru, w_out_b, w_up_b, w_down_b = _in_proj_lru(
        x, row(norm_mix_g[l]), w_in[l], conv_w[l], row(conv_b[l]), w_gates, b_gates, row(lru_L[l]),
        later_weights=(w_out[l], w_up[l], w_down[l]))
    attn = _diff_attn(qt, k, vt, lambdas, row(subln_g[l]))
    return _out_mlp(x, attn, lru, w_out_b, row(norm_mlp_g[l]), w_up_b, w_down_b, row(final_g))
```

```python
import functools
import math

import numpy as np
import jax
import jax.numpy as jnp
from jax import lax
from jax.experimental import pallas as pl
from jax.experimental.pallas import tpu as pltpu

EPS = 1e-6
CONV_WIDTH = 4
LRU_C = 8.0
HEAD_DIM = 64
V_HEAD_DIM = 2 * HEAD_DIM
LAMBDA_INIT = 0.8 - 0.6 * math.exp(-0.3 * 0)

V7X_MXU_COLS = 256
LANES = 128
SUBLANES = 8
V7X_VMEM_LIMIT_BYTES = 56 * 1024 * 1024

ATTN_BLOCK = V7X_MXU_COLS
PROJ_TOKENS = 512
MLP_TOKENS = 512
FF_CHUNK = 1024
MLP_SPLIT = 2
W_PIECES_IN_FLIGHT = 2
CHAIN = 4
SUB_ROWS = SUBLANES * CHAIN


def _bf16(x):
    return x.astype(jnp.bfloat16)


def _rmsnorm_rows(x, g):
    return x * lax.rsqrt(jnp.mean(x * x, axis=-1, keepdims=True) + EPS) * g


def _log_sigmoid(x):
    return jnp.minimum(x, 0.0) - jnp.log1p(jnp.exp(-jnp.abs(x)))


def _in_proj_lru_kernel(x_ref, g_ref, w_hbm, cw_ref, cb_ref, wg_ref, bg_ref, lru_l_ref, zero_ref,
                        wo_f32, wup_f32, wdn_f32,
                        qt_ref, k_ref, vt_ref, lru_ref, wo_bf16, wup_bf16, wdn_bf16,
                        w_ref, w_stage, w_sem, h_scr, ug_slab, u_scr, ub_scr, z_scr, gelu_scr,
                        out_slab, tail_ref, carry_ref):
    d_lru = lru_ref.shape[-1]
    d_attn = k_ref.shape[-1]

    for src, dst in ((wo_f32, wo_bf16), (wup_f32, wup_bf16), (wdn_f32, wdn_bf16)):
        dst[...] = _bf16(src[...])

    piece_cols = [3 * d_attn, 3 * d_attn + d_lru, d_attn, 0, 2 * d_attn]

    def copy(i):
        return pltpu.make_async_copy(w_hbm.at[:, pl.ds(piece_cols[i], d_attn)],
                                     w_stage.at[i % 2], w_sem.at[i % 2])

    def land(i):
        copy(i).wait()
        chunk = w_stage[i % 2]
        if piece_cols[i] == 0:
            chunk = chunk * HEAD_DIM ** -0.5
        w_ref[:, piece_cols[i]:piece_cols[i] + d_attn] = _bf16(chunk)
        if i + 2 < len(piece_cols):
            copy(i + 2).start()

    @pl.when(pl.program_id(1) == 0)
    def _():
        tail_ref[...] = jnp.zeros(tail_ref.shape, jnp.float32)
        carry_ref[...] = jnp.zeros(carry_ref.shape, jnp.float32)

    step = functools.partial(_in_proj_lru_step, x_ref, g_ref, w_ref, cw_ref, cb_ref, wg_ref, bg_ref,
                             lru_l_ref, zero_ref, qt_ref, k_ref, vt_ref, lru_ref, h_scr, ug_slab,
                             u_scr, ub_scr, z_scr, gelu_scr, out_slab, tail_ref, carry_ref)

    def first_step():
        copy(0).start()
        copy(1).start()
        step(land)

    first_tile = (pl.program_id(0) == 0) & (pl.program_id(1) == 0)
    pl.when(first_tile)(first_step)
    pl.when(jnp.logical_not(first_tile))(lambda: step(None))


def _in_proj_lru_step(x_ref, g_ref, w_ref, cw_ref, cb_ref, wg_ref, bg_ref, lru_l_ref, zero_ref,
                      qt_ref, k_ref, vt_ref, lru_ref, h_scr, ug_slab, u_scr, ub_scr, z_scr, gelu_scr,
                      out_slab, tail_ref, carry_ref, land):
    tm = x_ref.shape[0]
    d_lru = lru_ref.shape[-1]
    n_slab = d_lru // LANES
    n_sub = tm // SUB_ROWS
    d_attn = k_ref.shape[-1]
    f32 = jnp.float32

    h_scr[...] = _bf16(_rmsnorm_rows(x_ref[...], g_ref[...]))

    def ug_half(half):
        if land:
            land(half)
        cols = slice(3 * d_attn + half * d_lru, 3 * d_attn + (half + 1) * d_lru)
        nat = jnp.dot(h_scr[...], w_ref[:, cols], preferred_element_type=f32)
        for l in range(n_slab):
            ug_slab[half * n_slab + l] = nat[:, l * LANES:(l + 1) * LANES]

    ug_half(0)

    def piece(slab, s, c):
        return ug_slab[slab, pl.ds(s * SUB_ROWS + c, SUBLANES, stride=CHAIN), :]

    row = lax.broadcasted_iota(jnp.int32, (SUBLANES, LANES), 0)

    for l in range(n_slab):
        lanes = slice(l * LANES, (l + 1) * LANES)
        cw = cw_ref[:, lanes]
        cb = cb_ref[:, lanes]
        rolled_prev = [pltpu.roll(tail_ref[c, :, lanes], 1, axis=0) for c in range(CHAIN)]
        for s in range(n_sub):
            cur = [piece(l, s, c) for c in range(CHAIN)]
            rolled = [pltpu.roll(cur[c], 1, axis=0) for c in range(CHAIN)]
            shifted = [jnp.where(row == 0, rolled_prev[c], rolled[c]) for c in range(CHAIN)]
            rolled_prev = rolled
            if s == n_sub - 1:
                for c in range(CHAIN):
                    tail_ref[c, :, lanes] = cur[c]
            conv = []
            for c in range(CHAIN):
                acc = cb
                for kk in range(CONV_WIDTH):
                    d = CONV_WIDTH - 1 - kk
                    src = cur[c - d] if c >= d else shifted[c - d + CHAIN]
                    acc = acc + src * cw[kk:kk + 1, :]
                conv.append(acc)
            u_sub = jnp.concatenate(conv, axis=0)
            u_scr[l, s * SUB_ROWS:(s + 1) * SUB_ROWS, :] = u_sub
            ub_scr[s * SUB_ROWS:(s + 1) * SUB_ROWS, lanes] = _bf16(u_sub)

    ug_half(1)
    for l in range(n_slab):
        z = jnp.dot(ub_scr[:, l * LANES:(l + 1) * LANES], wg_ref[l],
                    preferred_element_type=f32) + bg_ref[l]
        z_scr[l] = z[:, :LANES]
        z_scr[n_slab + l] = z[:, LANES:]

    def k_piece(part):
        cols = slice(part * ATTN_BLOCK, (part + 1) * ATTN_BLOCK)
        t = jnp.dot(h_scr[...], w_ref[:, d_attn + part * ATTN_BLOCK:
                                     d_attn + (part + 1) * ATTN_BLOCK], preferred_element_type=f32)
        k_ref[:, cols] = _bf16(t)
        return [t[tm - SUBLANES:, c * LANES:(c + 1) * LANES] for c in range(ATTN_BLOCK // LANES)]

    def transposed_piece(first_col, o_ref, part):
        rows = slice(part * ATTN_BLOCK, (part + 1) * ATTN_BLOCK)
        t = jnp.dot(h_scr[rows, :], w_ref[:, first_col:first_col + d_attn],
                    preferred_element_type=f32)
        o_ref[part] = _bf16(t.T)
        return [t[ATTN_BLOCK - SUBLANES:, c * LANES:(c + 1) * LANES] for c in range(d_attn // LANES)]

    side_work = {0: lambda: k_piece(0), 2: lambda: k_piece(1),
                 5: lambda: transposed_piece(0, qt_ref, 0),
                 8: lambda: transposed_piece(0, qt_ref, 1),
                 11: lambda: transposed_piece(2 * d_attn, vt_ref, 0),
                 13: lambda: transposed_piece(2 * d_attn, vt_ref, 1)}
    land_before = {0: 2, 5: 3, 11: 4}
    assert n_sub > max(side_work) and qt_ref.shape[0] == 2 and d_attn == 2 * ATTN_BLOCK

    c_gelu = math.sqrt(2.0 / math.pi)
    for l in range(n_slab):
        for s in range(n_sub):
            gate = jnp.concatenate([piece(n_slab + l, s, c) for c in range(CHAIN)], axis=0)
            gelu_scr[l, s * SUB_ROWS:(s + 1) * SUB_ROWS, :] = (0.5 * gate) * (1.0 + jnp.tanh(
                gate * (c_gelu + (c_gelu * 0.044715) * (gate * gate))))

    c2_row = (0.5 * LRU_C * math.log2(math.e)) * _log_sigmoid(lru_l_ref[...])

    carry = [carry_ref[:, l * LANES:(l + 1) * LANES] for l in range(n_slab)]
    pending_at = -1
    keep_state = zero_ref[...] == 0
    for s in range(n_sub):
        if land and s in land_before:
            land(land_before[s])
        if s in side_work:
            anchor = functools.reduce(jnp.add, side_work[s]())
            pending_at = s + 1
        base = s * SUB_ROWS
        rows = slice(base, base + SUB_ROWS)
        for l in range(n_slab):
            c2 = c2_row[:, l * LANES:(l + 1) * LANES]
            u = u_scr[l, rows, :]
            t_r = jnp.tanh(z_scr[l, rows, :])
            t_i = jnp.tanh(z_scr[n_slab + l, rows, :])
            log2_a = c2 + c2 * t_r
            a = jnp.exp2(log2_a)
            quarter = (jnp.tanh(log2_a * math.log(2.0)) * -0.25) * (a * a + 1.0)
            half_mult = jnp.where(quarter > 0.0, quarter * lax.rsqrt(quarter), 0.0)
            b = (half_mult * u) * (1.0 + t_i)
            gelu = gelu_scr[l, rows, :]

            a_c = [a[c * SUBLANES:(c + 1) * SUBLANES] for c in range(CHAIN)]
            b_c = [b[c * SUBLANES:(c + 1) * SUBLANES] for c in range(CHAIN)]
            h_loc, p_loc = [b_c[0]], [a_c[0]]
            for c in range(1, CHAIN):
                h_loc.append(a_c[c] * h_loc[c - 1] + b_c[c])
                p_loc.append(a_c[c] * p_loc[c - 1])
            p_cum, h_cum = p_loc[CHAIN - 1], h_loc[CHAIN - 1]
            shift = 1
            while shift < SUBLANES:
                valid = row >= shift
                p_prev = jnp.where(valid, pltpu.roll(p_cum, shift, axis=0), 1.0)
                h_prev = jnp.where(valid, pltpu.roll(h_cum, shift, axis=0), 0.0)
                h_cum = p_cum * h_prev + h_cum
                p_cum = p_cum * p_prev
                shift *= 2
            chain_end = h_cum + p_cum * carry[l]
            chain_in = jnp.where(row == 0, carry[l], pltpu.roll(chain_end, 1, axis=0))
            carry[l] = jnp.broadcast_to(chain_end[SUBLANES - 1:SUBLANES, :], (SUBLANES, LANES))
            if s == pending_at:
                carry[l] = jnp.where(keep_state, carry[l], anchor)
            for c in range(CHAIN):
                out = (h_loc[c] + p_loc[c] * chain_in) * gelu[c * SUBLANES:(c + 1) * SUBLANES]
                out_slab[l, pl.ds(base + c, SUBLANES, stride=CHAIN), :] = out
    for l in range(n_slab):
        carry_ref[:, l * LANES:(l + 1) * LANES] = carry[l]
    lru_ref[...] = _bf16(jnp.concatenate([out_slab[l] for l in range(n_slab)], axis=1))


def _in_proj_lru(x, g, w, conv_w, conv_b, w_gates, b_gates, lru_l, later_weights):
    B, S, D = x.shape
    d_lru = lru_l.shape[-1]
    d_attn = (w.shape[1] - 2 * d_lru) // 3
    tm = PROJ_TOKENS
    nblk = tm // ATTN_BLOCK
    n_slab = d_lru // LANES
    const = lambda b, i: (0, 0)
    assert d_lru == d_attn
    n_steps = B * (S // tm)
    assert all(m.shape[0] % (n_steps * 2 * SUBLANES) == 0 for m in later_weights)
    row_block = lambda b, i: (b * (S // tm) + i, 0)
    later_specs = [pl.BlockSpec((m.shape[0] // n_steps, m.shape[1]), row_block)
                   for m in later_weights]
    return pl.pallas_call(
        _in_proj_lru_kernel,
        grid=(B, S // tm),
        in_specs=[
            pl.BlockSpec((None, tm, D), lambda b, i: (b, i, 0)),
            pl.BlockSpec((1, D), const),
            pl.BlockSpec(memory_space=pl.ANY),
            pl.BlockSpec(conv_w.shape, const),
            pl.BlockSpec(conv_b.shape, const),
            pl.BlockSpec(w_gates.shape, lambda b, i: (0, 0, 0)),
            pl.BlockSpec(b_gates.shape, lambda b, i: (0, 0, 0)),
            pl.BlockSpec(lru_l.shape, const),
            pl.BlockSpec((SUBLANES, LANES), const),
        ] + later_specs,
        out_specs=[
            pl.BlockSpec((None, nblk, d_attn, ATTN_BLOCK), lambda b, i: (b, i, 0, 0)),
            pl.BlockSpec((None, tm, d_attn), lambda b, i: (b, i, 0)),
            pl.BlockSpec((None, nblk, d_attn, ATTN_BLOCK), lambda b, i: (b, i, 0, 0)),
            pl.BlockSpec((None, tm, d_lru), lambda b, i: (b, i, 0)),
        ] + later_specs,
        out_shape=[
            jax.ShapeDtypeStruct((B, S // ATTN_BLOCK, d_attn, ATTN_BLOCK), jnp.bfloat16),
            jax.ShapeDtypeStruct((B, S, d_attn), jnp.bfloat16),
            jax.ShapeDtypeStruct((B, S // ATTN_BLOCK, d_attn, ATTN_BLOCK), jnp.bfloat16),
            jax.ShapeDtypeStruct((B, S, d_lru), jnp.bfloat16),
        ] + [jax.ShapeDtypeStruct(m.shape, jnp.bfloat16) for m in later_weights],
        scratch_shapes=[
            pltpu.VMEM(w.shape, jnp.bfloat16),
            pltpu.VMEM((2, D, d_attn), jnp.float32),
            pltpu.SemaphoreType.DMA((2,)),
            pltpu.VMEM((tm, D), jnp.bfloat16),
            pltpu.VMEM((2 * n_slab, tm, LANES), jnp.float32),
            pltpu.VMEM((n_slab, tm, LANES), jnp.float32),
            pltpu.VMEM((tm, d_lru), jnp.bfloat16),
            pltpu.VMEM((2 * n_slab, tm, LANES), jnp.float32),
            pltpu.VMEM((n_slab, tm, LANES), jnp.float32),
            pltpu.VMEM((n_slab, tm, LANES), jnp.float32),
            pltpu.VMEM((CHAIN, SUBLANES, d_lru), jnp.float32),
            pltpu.VMEM((SUBLANES, d_lru), jnp.float32),
        ],
        compiler_params=pltpu.CompilerParams(
            dimension_semantics=("arbitrary", "arbitrary"),
            vmem_limit_bytes=V7X_VMEM_LIMIT_BYTES),
        name="in_proj_lru",
    )(x, g, w, conv_w, conv_b, w_gates, b_gates, lru_l, jnp.zeros((SUBLANES, LANES), jnp.int32),
      *later_weights)


N_POS_COLS = 128
N_ONES_ROWS = 16
SCORE_LEAD = 2
N_SCORE_BUFS = SCORE_LEAD + 1
ATTN_Q_PER_STEP = 9


def _diff_attn_kernel(qt_ref, k_ref, pos_ref, vt_ref, lq1_ref, lk1_ref, lq2_ref, lk2_ref, sg_ref,
                      o_ref,
                      s_ref, m_ref, acc_ref, *, slopes, q_per_step):
    h = pl.program_id(1)
    blk = ATTN_BLOCK
    n_blocks = vt_ref.shape[0]

    slope = jnp.float32(slopes[0])
    for i in range(1, len(slopes)):
        slope = jnp.where(h == i, jnp.float32(slopes[i]), slope)

    r = lax.broadcasted_iota(jnp.int32, (N_POS_COLS, blk), 0)
    lane = lax.broadcasted_iota(jnp.int32, (N_POS_COLS, blk), 1).astype(jnp.float32)
    zeros = jnp.zeros((HEAD_DIM, blk), qt_ref.dtype)
    key_in_blk = lax.broadcasted_iota(jnp.int32, (blk, 2 * blk), 0)
    qry_in_blk = lax.broadcasted_iota(jnp.int32, (blk, 2 * blk), 1) % blk
    causal = key_in_blk <= qry_in_blk
    ones_row = lax.broadcasted_iota(jnp.int32, (N_ONES_ROWS, blk), 0) == 0
    ones_blk = jnp.where(ones_row, 1.0, 0.0).astype(jnp.bfloat16)

    def q_operand(qt, q_index):
        alibi = _bf16(jnp.where(r == 0, slope,
                      jnp.where(r == 1, slope * blk,
                      jnp.where(r == 2, -slope * lane,
                      jnp.where(r == 3, -slope * float(q_index * blk), 0.0)))))
        return jnp.concatenate(
            [jnp.concatenate([qt[:HEAD_DIM], zeros, alibi], axis=0),
             jnp.concatenate([zeros, qt[HEAD_DIM:], alibi], axis=0)], axis=1)

    def score_block(q_aug, j, diagonal):
        rows = slice(j * blk, (j + 1) * blk)
        k_aug = jnp.concatenate([k_ref[rows, :], pos_ref[rows, :]], axis=1)
        s = jnp.dot(k_aug, q_aug, preferred_element_type=jnp.float32)
        return jnp.where(causal, s, -jnp.inf) if diagonal else s

    def finish(t):
        acc = acc_ref[...]
        lam = (jnp.exp(jnp.sum(lq1_ref[...] * lk1_ref[...], axis=-1, keepdims=True))
               - jnp.exp(jnp.sum(lq2_ref[...] * lk2_ref[...], axis=-1, keepdims=True))
               + LAMBDA_INIT)
        heads = acc[:V_HEAD_DIM] / acc[V_HEAD_DIM:V_HEAD_DIM + 1]
        o = heads[:, :blk] - lam * heads[:, blk:]
        o = o * lax.rsqrt(jnp.mean(o * o, axis=0, keepdims=True) + EPS)
        o_ref[t * blk:(t + 1) * blk, :] = _bf16(o.T * (sg_ref[...] * (1.0 - LAMBDA_INIT)))

    order = list(range(0, n_blocks, 2)) + list(range(n_blocks - 1 - n_blocks % 2, 0, -2))
    assert sorted(order) == list(range(n_blocks))

    def score_q_block(q_aug, q_index, slot, j, m_run):
        s = score_block(q_aug, j, j == q_index)
        s_ref[slot, j * blk:(j + 1) * blk, :] = s
        col_max = jnp.max(s, axis=0, keepdims=True)
        return col_max if m_run is None else jnp.maximum(m_run, col_max)

    def sub_step(t):
        if t >= 1:
            finish(order[t - 1])
        if t == n_blocks:
            return
        if t == 0:
            for pos in range(min(SCORE_LEAD, n_blocks)):
                q_index = order[pos]
                q_aug = q_operand(qt_ref[q_index], q_index)
                m_run = None
                for j in range(q_index + 1):
                    m_run = score_q_block(q_aug, q_index, pos % N_SCORE_BUFS, j, m_run)
                m_ref[pos % N_SCORE_BUFS] = m_run
        cur = t % N_SCORE_BUFS
        q_cur = order[t]
        scoring = t + SCORE_LEAD < n_blocks
        q_ahead = order[t + SCORE_LEAD] if scoring else -1
        slot_ahead = (t + SCORE_LEAD) % N_SCORE_BUFS
        q_aug = q_operand(qt_ref[q_ahead], q_ahead) if scoring else None
        m_cur = m_ref[cur]
        m_run = None
        acc = jnp.zeros((V_HEAD_DIM + N_ONES_ROWS, 2 * blk), jnp.float32)
        for j in range(max(q_cur, q_ahead) + 1):
            if j <= q_ahead:
                m_run = score_q_block(q_aug, q_ahead, slot_ahead, j, m_run)
            if j <= q_cur:
                p = _bf16(jnp.exp(s_ref[cur, j * blk:(j + 1) * blk, :] - m_cur))
                vt_aug = jnp.concatenate([vt_ref[j], ones_blk], axis=0)
                acc = acc + jnp.dot(vt_aug, p, preferred_element_type=jnp.float32)
        if scoring:
            m_ref[slot_ahead] = m_run
        acc_ref[...] = acc

    def grid_step(first):
        def run():
            for t in range(first, min(first + q_per_step, n_blocks + 1)):
                sub_step(t)
        return run

    steps = [grid_step(first) for first in range(0, n_blocks + 1, q_per_step)]
    if len(steps) == 1:
        steps[0]()
    else:
        lax.switch(pl.program_id(2), steps)


def _alibi_slopes(n_heads):
    slopes = [2.0 ** (-8.0 * (i + 1) / n_heads) for i in range(n_heads)]
    assert all(math.frexp(s)[0] == 0.5 for s in slopes), slopes
    return slopes


def _key_position_columns(seq_len):
    j = np.arange(seq_len)
    pos = np.zeros((seq_len, N_POS_COLS), np.float32)
    pos[:, 0] = j % ATTN_BLOCK
    pos[:, 1] = j // ATTN_BLOCK
    pos[:, 2:4] = 1.0
    assert ATTN_BLOCK <= 256 and seq_len // ATTN_BLOCK <= 256
    return jnp.asarray(pos, jnp.bfloat16)


def _diff_attn(qt, k, vt, lambdas, subln_g):
    B, nblk, d_attn, blk = qt.shape
    S = k.shape[1]
    H = d_attn // V_HEAD_DIM
    pos = _key_position_columns(S)
    kernel = functools.partial(_diff_attn_kernel, slopes=tuple(_alibi_slopes(H)),
                               q_per_step=ATTN_Q_PER_STEP)
    n_steps = pl.cdiv(nblk + 1, ATTN_Q_PER_STEP)
    return pl.pallas_call(
        kernel,
        grid=(B, H, n_steps),
        in_specs=[
            pl.BlockSpec((None, nblk, V_HEAD_DIM, blk), lambda b, h, q: (b, 0, h, 0)),
            pl.BlockSpec((None, S, V_HEAD_DIM), lambda b, h, q: (b, 0, h)),
            pl.BlockSpec(pos.shape, lambda b, h, q: (0, 0)),
            pl.BlockSpec((None, nblk, V_HEAD_DIM, blk), lambda b, h, q: (b, 0, h, 0)),
            *[pl.BlockSpec(v.shape, lambda b, h, q: (0, 0)) for v in lambdas],
            pl.BlockSpec(subln_g.shape, lambda b, h, q: (0, 0)),
        ],
        out_specs=pl.BlockSpec((None, S, V_HEAD_DIM), lambda b, h, q: (b, 0, h)),
        out_shape=jax.ShapeDtypeStruct((B, S, d_attn), jnp.bfloat16),
        scratch_shapes=[pltpu.VMEM((N_SCORE_BUFS, S, 2 * blk), jnp.float32),
                        pltpu.VMEM((N_SCORE_BUFS, 1, 2 * blk), jnp.float32),
                        pltpu.VMEM((V_HEAD_DIM + N_ONES_ROWS, 2 * blk), jnp.float32)],
        compiler_params=pltpu.CompilerParams(
            dimension_semantics=("parallel", "parallel", "arbitrary"),
            vmem_limit_bytes=V7X_VMEM_LIMIT_BYTES),
        name="diff_attn",
    )(qt, k, pos, vt, *lambdas, subln_g)


def _out_mlp_kernel(x_ref, attn_ref, lru_ref, wo_hbm, gm_ref, wup_hbm, wdn_hbm, gf_ref, y_ref,
                    wo_ref, wup_ref, wdn_ref, sem):
    tm = x_ref.shape[0]
    d_ff = wup_ref.shape[1]
    n_chunks = d_ff // FF_CHUNK
    rows = [slice(g * tm // MLP_SPLIT, (g + 1) * tm // MLP_SPLIT) for g in range(MLP_SPLIT)]

    def ff(c):
        return pl.ds(c * FF_CHUNK, FF_CHUNK)

    sources = [wo_hbm]
    targets = [wo_ref]
    for c in range(n_chunks):
        sources += [wup_hbm.at[:, ff(c)], wdn_hbm.at[ff(c), :]]
        targets += [wup_ref.at[:, ff(c)], wdn_ref.at[ff(c), :]]
    assert len(sources) == sem.shape[0]

    def copy(i):
        return pltpu.make_async_copy(sources[i], targets[i], sem.at[i])

    def land(i):
        copy(i).wait()
        if i + W_PIECES_IN_FLIGHT < len(sources):
            copy(i + W_PIECES_IN_FLIGHT).start()

    def step_body(first_step):
        if first_step:
            for i in range(W_PIECES_IN_FLIGHT):
                copy(i).start()
            land(0)
        x1 = []
        for r in rows:
            mix = jnp.concatenate([attn_ref[r, :], lru_ref[r, :]], axis=-1)
            x1.append(x_ref[r, :] + jnp.dot(mix, wo_ref[...], preferred_element_type=jnp.float32))
        hm = [_bf16(_rmsnorm_rows(v, gm_ref[...])) for v in x1]
        mlp = [jnp.zeros_like(v) for v in x1]
        for c in range(n_chunks):
            cols = slice(c * FF_CHUNK, (c + 1) * FF_CHUNK)
            if first_step:
                land(1 + 2 * c)
            act = [_bf16(jnp.square(jnp.maximum(
                jnp.dot(hm[g], wup_ref[:, cols], preferred_element_type=jnp.float32), 0.0)))
                for g in range(MLP_SPLIT)]
            if first_step:
                land(2 + 2 * c)
            for g in range(MLP_SPLIT):
                mlp[g] = mlp[g] + jnp.dot(act[g], wdn_ref[cols, :],
                                          preferred_element_type=jnp.float32)
        for g, r in enumerate(rows):
            y_ref[r, :] = _rmsnorm_rows(x1[g] + mlp[g], gf_ref[...])

    pl.when(pl.program_id(0) == 0)(lambda: step_body(True))
    pl.when(pl.program_id(0) != 0)(lambda: step_body(False))


def _out_mlp(x, attn, lru, w_out, g_mlp, w_up, w_down, g_final):
    B, S, D = x.shape
    T = B * S
    tm = MLP_TOKENS
    x2 = x.reshape(T, D)
    attn2 = attn.reshape(T, attn.shape[-1])
    lru2 = lru.reshape(T, lru.shape[-1])
    const = lambda i: (0, 0)
    in_hbm = pl.BlockSpec(memory_space=pl.ANY)
    assert w_out.shape == (D, D) and w_up.shape[1] % FF_CHUNK == 0
    assert w_out.dtype == w_up.dtype == w_down.dtype == jnp.bfloat16
    y = pl.pallas_call(
        _out_mlp_kernel,
        grid=(T // tm,),
        in_specs=[
            pl.BlockSpec((tm, D), lambda i: (i, 0)),
            pl.BlockSpec((tm, attn2.shape[1]), lambda i: (i, 0)),
            pl.BlockSpec((tm, lru2.shape[1]), lambda i: (i, 0)),
            in_hbm,
            pl.BlockSpec(g_mlp.shape, const),
            in_hbm,
            in_hbm,
            pl.BlockSpec(g_final.shape, const),
        ],
        out_specs=pl.BlockSpec((tm, D), lambda i: (i, 0)),
        out_shape=jax.ShapeDtypeStruct((T, D), jnp.float32),
        scratch_shapes=[
            pltpu.VMEM(w_out.shape, jnp.bfloat16),
            pltpu.VMEM(w_up.shape, jnp.bfloat16),
            pltpu.VMEM(w_down.shape, jnp.bfloat16),
            pltpu.SemaphoreType.DMA((1 + 2 * (w_up.shape[1] // FF_CHUNK),)),
        ],
        compiler_params=pltpu.CompilerParams(
            dimension_semantics=("arbitrary",),
            vmem_limit_bytes=V7X_VMEM_LIMIT_BYTES),
        name="out_mlp",
    )(x2, attn2, lru2, w_out, g_mlp, w_up, w_down, g_final)
    return y.reshape(B, S, D)


def _pair_block_diag(w):
    n, c, d = w.shape
    rows = w.reshape(n // 2, 2 * c, d)
    tiled = jnp.concatenate([rows, rows], axis=-1)
    row_block = np.arange(2 * c)[:, None] // c
    col_block = np.arange(2 * d)[None, :] // d
    return jnp.where(jnp.asarray(row_block == col_block), tiled, 0.0)


def kernel(x, norm_mix_g, w_in, conv_w, conv_b, w_rg, b_rg, w_ig, b_ig, lru_L, lambda_q1, lambda_k1, lambda_q2, lambda_k2, subln_g, w_out, norm_mlp_g, w_up, w_down, final_g):
    B, S, D = x.shape
    d_lru = lru_L.shape[-1]
    d_attn = (w_in.shape[-1] - 2 * d_lru) // 3
    depth = w_in.shape[0]
    n_slab = d_lru // LANES
    assert depth == 1, "LAMBDA_INIT is specialised to a single layer"
    assert S % PROJ_TOKENS == 0 and (B * S) % MLP_TOKENS == 0
    assert PROJ_TOKENS % SUB_ROWS == 0 and d_lru % LANES == 0
    assert 2 * w_rg.shape[-1] == LANES and w_rg.shape[1] * w_rg.shape[2] == d_lru
    l = 0
    row = lambda v: v.reshape(1, -1)

    w_gates = _bf16(0.5 * jnp.concatenate([_pair_block_diag(w_rg[l]), _pair_block_diag(w_ig[l])],
                                          axis=-1))
    b_gates = 0.5 * jnp.concatenate([b_rg[l].reshape(n_slab, 1, LANES),
                                     b_ig[l].reshape(n_slab, 1, LANES)], axis=-1)
    lambdas = [row(v[l]) for v in (lambda_q1, lambda_k1, lambda_q2, lambda_k2)]

    qt, k, vt, lru, w_out_b, w_up_b, w_down_b = _in_proj_lru(
        x, row(norm_mix_g[l]), w_in[l], conv_w[l], row(conv_b[l]), w_gates, b_gates, row(lru_L[l]),
        later_weights=(w_out[l], w_up[l], w_down[l]))
    attn = _diff_attn(qt, k, vt, lambdas, row(subln_g[l]))
    return _out_mlp(x, attn, lru, w_out_b, row(norm_mlp_g[l]), w_up_b, w_down_b, row(final_g))
```

```python
import functools
import math

import numpy as np
import jax
import jax.numpy as jnp
from jax import lax
from jax.experimental import pallas as pl
from jax.experimental.pallas import tpu as pltpu

EPS = 1e-6
CONV_WIDTH = 4
LRU_C = 8.0
HEAD_DIM = 64
V_HEAD_DIM = 2 * HEAD_DIM
LAMBDA_INIT = 0.8 - 0.6 * math.exp(-0.3 * 0)

V7X_MXU_COLS = 256
LANES = 128
SUBLANES = 8
V7X_VMEM_LIMIT_BYTES = 56 * 1024 * 1024

ATTN_BLOCK = V7X_MXU_COLS
PROJ_TOKENS = 512
MLP_TOKENS = 512
FF_CHUNK = 1024
MLP_SPLIT = 2
W_PIECES_IN_FLIGHT = 2
CHAIN = 4
SUB_ROWS = SUBLANES * CHAIN


def _bf16(x):
    return x.astype(jnp.bfloat16)


def _rmsnorm_rows(x, g):
    return x * lax.rsqrt(jnp.mean(x * x, axis=-1, keepdims=True) + EPS) * g


def _log_sigmoid(x):
    return jnp.minimum(x, 0.0) - jnp.log1p(jnp.exp(-jnp.abs(x)))


def _in_proj_lru_kernel(x_ref, g_ref, w_hbm, cw_ref, cb_ref, wg_ref, bg_ref, lru_l_ref, zero_ref,
                        wo_f32, wup_f32, wdn_f32,
                        qt_ref, k_ref, vt_ref, lru_ref, wo_bf16, wup_bf16, wdn_bf16,
                        w_ref, w_stage, w_sem, h_scr, ug_slab, u_scr, ub_scr, z_scr, gelu_scr,
                        out_slab, tail_ref, carry_ref):
    d_lru = lru_ref.shape[-1]
    d_attn = k_ref.shape[-1]

    cast_jobs = ((wo_f32, wo_bf16), (wup_f32, wup_bf16), (wdn_f32, wdn_bf16))

    piece_cols = [3 * d_attn, 3 * d_attn + d_lru, d_attn, 0, 2 * d_attn]

    def copy(i):
        return pltpu.make_async_copy(w_hbm.at[:, pl.ds(piece_cols[i], d_attn)],
                                     w_stage.at[i % 2], w_sem.at[i % 2])

    def land(i):
        copy(i).wait()
        chunk = w_stage[i % 2]
        if piece_cols[i] == 0:
            chunk = chunk * HEAD_DIM ** -0.5
        w_ref[:, piece_cols[i]:piece_cols[i] + d_attn] = _bf16(chunk)
        if i + 2 < len(piece_cols):
            copy(i + 2).start()

    @pl.when(pl.program_id(1) == 0)
    def _():
        tail_ref[...] = jnp.zeros(tail_ref.shape, jnp.float32)
        carry_ref[...] = jnp.zeros(carry_ref.shape, jnp.float32)

    step = functools.partial(_in_proj_lru_step, x_ref, g_ref, w_ref, cw_ref, cb_ref, wg_ref, bg_ref,
                             lru_l_ref, zero_ref, qt_ref, k_ref, vt_ref, lru_ref, h_scr, ug_slab,
                             u_scr, ub_scr, z_scr, gelu_scr, out_slab, tail_ref, carry_ref,
                             cast_jobs)

    def first_step():
        copy(0).start()
        copy(1).start()
        step(land)

    first_tile = (pl.program_id(0) == 0) & (pl.program_id(1) == 0)
    pl.when(first_tile)(first_step)
    pl.when(jnp.logical_not(first_tile))(lambda: step(None))


def _in_proj_lru_step(x_ref, g_ref, w_ref, cw_ref, cb_ref, wg_ref, bg_ref, lru_l_ref, zero_ref,
                      qt_ref, k_ref, vt_ref, lru_ref, h_scr, ug_slab, u_scr, ub_scr, z_scr, gelu_scr,
                      out_slab, tail_ref, carry_ref, cast_jobs, land):
    for src, dst in cast_jobs:
        dst[...] = _bf16(src[...])

    tm = x_ref.shape[0]
    d_lru = lru_ref.shape[-1]
    n_slab = d_lru // LANES
    n_sub = tm // SUB_ROWS
    d_attn = k_ref.shape[-1]
    f32 = jnp.float32

    h_scr[...] = _bf16(_rmsnorm_rows(x_ref[...], g_ref[...]))

    def ug_half(half):
        if land:
            land(half)
        cols = slice(3 * d_attn + half * d_lru, 3 * d_attn + (half + 1) * d_lru)
        nat = jnp.dot(h_scr[...], w_ref[:, cols], preferred_element_type=f32)
        for l in range(n_slab):
            ug_slab[half * n_slab + l] = nat[:, l * LANES:(l + 1) * LANES]

    ug_half(0)

    def piece(slab, s, c):
        return ug_slab[slab, pl.ds(s * SUB_ROWS + c, SUBLANES, stride=CHAIN), :]

    row = lax.broadcasted_iota(jnp.int32, (SUBLANES, LANES), 0)

    for l in range(n_slab):
        lanes = slice(l * LANES, (l + 1) * LANES)
        cw = cw_ref[:, lanes]
        cb = cb_ref[:, lanes]
        rolled_prev = [pltpu.roll(tail_ref[c, :, lanes], 1, axis=0) for c in range(CHAIN)]
        for s in range(n_sub):
            cur = [piece(l, s, c) for c in range(CHAIN)]
            rolled = [pltpu.roll(cur[c], 1, axis=0) for c in range(CHAIN)]
            shifted = [jnp.where(row == 0, rolled_prev[c], rolled[c]) for c in range(CHAIN)]
            rolled_prev = rolled
            if s == n_sub - 1:
                for c in range(CHAIN):
                    tail_ref[c, :, lanes] = cur[c]
            conv = []
            for c in range(CHAIN):
                acc = cb
                for kk in range(CONV_WIDTH):
                    d = CONV_WIDTH - 1 - kk
                    src = cur[c - d] if c >= d else shifted[c - d + CHAIN]
                    acc = acc + src * cw[kk:kk + 1, :]
                conv.append(acc)
            u_sub = jnp.concatenate(conv, axis=0)
            u_scr[l, s * SUB_ROWS:(s + 1) * SUB_ROWS, :] = u_sub
            ub_scr[s * SUB_ROWS:(s + 1) * SUB_ROWS, lanes] = _bf16(u_sub)

    ug_half(1)
    for l in range(n_slab):
        z = jnp.dot(ub_scr[:, l * LANES:(l + 1) * LANES], wg_ref[l],
                    preferred_element_type=f32) + bg_ref[l]
        z_scr[l] = z[:, :LANES]
        z_scr[n_slab + l] = z[:, LANES:]

    def k_piece(part):
        cols = slice(part * ATTN_BLOCK, (part + 1) * ATTN_BLOCK)
        t = jnp.dot(h_scr[...], w_ref[:, d_attn + part * ATTN_BLOCK:
                                     d_attn + (part + 1) * ATTN_BLOCK], preferred_element_type=f32)
        k_ref[:, cols] = _bf16(t)
        return [t[tm - SUBLANES:, c * LANES:(c + 1) * LANES] for c in range(ATTN_BLOCK // LANES)]

    def transposed_piece(first_col, o_ref, part):
        rows = slice(part * ATTN_BLOCK, (part + 1) * ATTN_BLOCK)
        t = jnp.dot(h_scr[rows, :], w_ref[:, first_col:first_col + d_attn],
                    preferred_element_type=f32)
        o_ref[part] = _bf16(t.T)
        return [t[ATTN_BLOCK - SUBLANES:, c * LANES:(c + 1) * LANES] for c in range(d_attn // LANES)]

    side_work = {0: lambda: k_piece(0), 2: lambda: k_piece(1),
                 5: lambda: transposed_piece(0, qt_ref, 0),
                 8: lambda: transposed_piece(0, qt_ref, 1),
                 11: lambda: transposed_piece(2 * d_attn, vt_ref, 0),
                 13: lambda: transposed_piece(2 * d_attn, vt_ref, 1)}
    land_before = {0: 2, 5: 3, 11: 4}
    assert n_sub > max(side_work) and qt_ref.shape[0] == 2 and d_attn == 2 * ATTN_BLOCK

    c_gelu = math.sqrt(2.0 / math.pi)
    for l in range(n_slab):
        for s in range(n_sub):
            gate = jnp.concatenate([piece(n_slab + l, s, c) for c in range(CHAIN)], axis=0)
            gelu_scr[l, s * SUB_ROWS:(s + 1) * SUB_ROWS, :] = (0.5 * gate) * (1.0 + jnp.tanh(
                gate * (c_gelu + (c_gelu * 0.044715) * (gate * gate))))

    c2_row = (0.5 * LRU_C * math.log2(math.e)) * _log_sigmoid(lru_l_ref[...])

    carry = [carry_ref[:, l * LANES:(l + 1) * LANES] for l in range(n_slab)]
    pending_at = -1
    keep_state = zero_ref[...] == 0
    for s in range(n_sub):
        if land and s in land_before:
            land(land_before[s])
        if s in side_work:
            anchor = functools.reduce(jnp.add, side_work[s]())
            pending_at = s + 1
        base = s * SUB_ROWS
        rows = slice(base, base + SUB_ROWS)
        for l in range(n_slab):
            c2 = c2_row[:, l * LANES:(l + 1) * LANES]
            u = u_scr[l, rows, :]
            t_r = jnp.tanh(z_scr[l, rows, :])
            t_i = jnp.tanh(z_scr[n_slab + l, rows, :])
            log2_a = c2 + c2 * t_r
            a = jnp.exp2(log2_a)
            quarter = (jnp.tanh(log2_a * math.log(2.0)) * -0.25) * (a * a + 1.0)
            half_mult = jnp.where(quarter > 0.0, quarter * lax.rsqrt(quarter), 0.0)
            b = (half_mult * u) * (1.0 + t_i)
            gelu = gelu_scr[l, rows, :]

            a_c = [a[c * SUBLANES:(c + 1) * SUBLANES] for c in range(CHAIN)]
            b_c = [b[c * SUBLANES:(c + 1) * SUBLANES] for c in range(CHAIN)]
            h_loc, p_loc = [b_c[0]], [a_c[0]]
            for c in range(1, CHAIN):
                h_loc.append(a_c[c] * h_loc[c - 1] + b_c[c])
                p_loc.append(a_c[c] * p_loc[c - 1])
            p_cum, h_cum = p_loc[CHAIN - 1], h_loc[CHAIN - 1]
            shift = 1
            while shift < SUBLANES:
                valid = row >= shift
                p_prev = jnp.where(valid, pltpu.roll(p_cum, shift, axis=0), 1.0)
                h_prev = jnp.where(valid, pltpu.roll(h_cum, shift, axis=0), 0.0)
                h_cum = p_cum * h_prev + h_cum
                p_cum = p_cum * p_prev
                shift *= 2
            chain_end = h_cum + p_cum * carry[l]
            chain_in = jnp.where(row == 0, carry[l], pltpu.roll(chain_end, 1, axis=0))
            carry[l] = jnp.broadcast_to(chain_end[SUBLANES - 1:SUBLANES, :], (SUBLANES, LANES))
            if s == pending_at:
                carry[l] = jnp.where(keep_state, carry[l], anchor)
            for c in range(CHAIN):
                out = (h_loc[c] + p_loc[c] * chain_in) * gelu[c * SUBLANES:(c + 1) * SUBLANES]
                out_slab[l, pl.ds(base + c, SUBLANES, stride=CHAIN), :] = out
    for l in range(n_slab):
        carry_ref[:, l * LANES:(l + 1) * LANES] = carry[l]
    lru_ref[...] = _bf16(jnp.concatenate([out_slab[l] for l in range(n_slab)], axis=1))


def _in_proj_lru(x, g, w, conv_w, conv_b, w_gates, b_gates, lru_l, later_weights):
    B, S, D = x.shape
    d_lru = lru_l.shape[-1]
    d_attn = (w.shape[1] - 2 * d_lru) // 3
    tm = PROJ_TOKENS
    nblk = tm // ATTN_BLOCK
    n_slab = d_lru // LANES
    const = lambda b, i: (0, 0)
    assert d_lru == d_attn
    n_steps = B * (S // tm)
    assert all(m.shape[0] % (n_steps * 2 * SUBLANES) == 0 for m in later_weights)
    row_block = lambda b, i: (b * (S // tm) + i, 0)
    later_specs = [pl.BlockSpec((m.shape[0] // n_steps, m.shape[1]), row_block)
                   for m in later_weights]
    return pl.pallas_call(
        _in_proj_lru_kernel,
        grid=(B, S // tm),
        in_specs=[
            pl.BlockSpec((None, tm, D), lambda b, i: (b, i, 0)),
            pl.BlockSpec((1, D), const),
            pl.BlockSpec(memory_space=pl.ANY),
            pl.BlockSpec(conv_w.shape, const),
            pl.BlockSpec(conv_b.shape, const),
            pl.BlockSpec(w_gates.shape, lambda b, i: (0, 0, 0)),
            pl.BlockSpec(b_gates.shape, lambda b, i: (0, 0, 0)),
            pl.BlockSpec(lru_l.shape, const),
            pl.BlockSpec((SUBLANES, LANES), const),
        ] + later_specs,
        out_specs=[
            pl.BlockSpec((None, nblk, d_attn, ATTN_BLOCK), lambda b, i: (b, i, 0, 0)),
            pl.BlockSpec((None, tm, d_attn), lambda b, i: (b, i, 0)),
            pl.BlockSpec((None, nblk, d_attn, ATTN_BLOCK), lambda b, i: (b, i, 0, 0)),
            pl.BlockSpec((None, tm, d_lru), lambda b, i: (b, i, 0)),
        ] + later_specs,
        out_shape=[
            jax.ShapeDtypeStruct((B, S // ATTN_BLOCK, d_attn, ATTN_BLOCK), jnp.bfloat16),
            jax.ShapeDtypeStruct((B, S, d_attn), jnp.bfloat16),
            jax.ShapeDtypeStruct((B, S // ATTN_BLOCK, d_attn, ATTN_BLOCK), jnp.bfloat16),
            jax.ShapeDtypeStruct((B, S, d_lru), jnp.bfloat16),
        ] + [jax.ShapeDtypeStruct(m.shape, jnp.bfloat16) for m in later_weights],
        scratch_shapes=[
            pltpu.VMEM(w.shape, jnp.bfloat16),
            pltpu.VMEM((2, D, d_attn), jnp.float32),
            pltpu.SemaphoreType.DMA((2,)),
            pltpu.VMEM((tm, D), jnp.bfloat16),
            pltpu.VMEM((2 * n_slab, tm, LANES), jnp.float32),
            pltpu.VMEM((n_slab, tm, LANES), jnp.float32),
            pltpu.VMEM((tm, d_lru), jnp.bfloat16),
            pltpu.VMEM((2 * n_slab, tm, LANES), jnp.float32),
            pltpu.VMEM((n_slab, tm, LANES), jnp.float32),
            pltpu.VMEM((n_slab, tm, LANES), jnp.float32),
            pltpu.VMEM((CHAIN, SUBLANES, d_lru), jnp.float32),
            pltpu.VMEM((SUBLANES, d_lru), jnp.float32),
        ],
        compiler_params=pltpu.CompilerParams(
            dimension_semantics=("arbitrary", "arbitrary"),
            vmem_limit_bytes=V7X_VMEM_LIMIT_BYTES),
        name="in_proj_lru",
    )(x, g, w, conv_w, conv_b, w_gates, b_gates, lru_l, jnp.zeros((SUBLANES, LANES), jnp.int32),
      *later_weights)


N_POS_COLS = 128
N_ONES_ROWS = 16
SCORE_LEAD = 2
N_SCORE_BUFS = SCORE_LEAD + 1
ATTN_Q_PER_STEP = 9


def _diff_attn_kernel(qt_ref, k_ref, pos_ref, vt_ref, lq1_ref, lk1_ref, lq2_ref, lk2_ref, sg_ref,
                      o_ref,
                      s_ref, m_ref, acc_ref, *, slopes, q_per_step):
    h = pl.program_id(1)
    blk = ATTN_BLOCK
    n_blocks = vt_ref.shape[0]

    slope = jnp.float32(slopes[0])
    for i in range(1, len(slopes)):
        slope = jnp.where(h == i, jnp.float32(slopes[i]), slope)

    r = lax.broadcasted_iota(jnp.int32, (N_POS_COLS, blk), 0)
    lane = lax.broadcasted_iota(jnp.int32, (N_POS_COLS, blk), 1).astype(jnp.float32)
    zeros = jnp.zeros((HEAD_DIM, blk), qt_ref.dtype)
    key_in_blk = lax.broadcasted_iota(jnp.int32, (blk, 2 * blk), 0)
    qry_in_blk = lax.broadcasted_iota(jnp.int32, (blk, 2 * blk), 1) % blk
    causal = key_in_blk <= qry_in_blk
    ones_row = lax.broadcasted_iota(jnp.int32, (N_ONES_ROWS, blk), 0) == 0
    ones_blk = jnp.where(ones_row, 1.0, 0.0).astype(jnp.bfloat16)

    def q_operand(qt, q_index):
        alibi = _bf16(jnp.where(r == 0, slope,
                      jnp.where(r == 1, slope * blk,
                      jnp.where(r == 2, -slope * lane,
                      jnp.where(r == 3, -slope * float(q_index * blk), 0.0)))))
        return jnp.concatenate(
            [jnp.concatenate([qt[:HEAD_DIM], zeros, alibi], axis=0),
             jnp.concatenate([zeros, qt[HEAD_DIM:], alibi], axis=0)], axis=1)

    def score_block(q_aug, j, diagonal):
        rows = slice(j * blk, (j + 1) * blk)
        k_aug = jnp.concatenate([k_ref[rows, :], pos_ref[rows, :]], axis=1)
        s = jnp.dot(k_aug, q_aug, preferred_element_type=jnp.float32)
        return jnp.where(causal, s, -jnp.inf) if diagonal else s

    def finish(t):
        acc = acc_ref[...]
        lam = (jnp.exp(jnp.sum(lq1_ref[...] * lk1_ref[...], axis=-1, keepdims=True))
               - jnp.exp(jnp.sum(lq2_ref[...] * lk2_ref[...], axis=-1, keepdims=True))
               + LAMBDA_INIT)
        heads = acc[:V_HEAD_DIM] / acc[V_HEAD_DIM:V_HEAD_DIM + 1]
        o = heads[:, :blk] - lam * heads[:, blk:]
        o = o * lax.rsqrt(jnp.mean(o * o, axis=0, keepdims=True) + EPS)
        o_ref[t * blk:(t + 1) * blk, :] = _bf16(o.T * (sg_ref[...] * (1.0 - LAMBDA_INIT)))

    order = list(range(0, n_blocks, 2)) + list(range(n_blocks - 1 - n_blocks % 2, 0, -2))
    assert sorted(order) == list(range(n_blocks))

    def score_q_block(q_aug, q_index, slot, j, m_run):
        s = score_block(q_aug, j, j == q_index)
        s_ref[slot, j * blk:(j + 1) * blk, :] = s
        col_max = jnp.max(s, axis=0, keepdims=True)
        return col_max if m_run is None else jnp.maximum(m_run, col_max)

    def sub_step(t):
        if t >= 1:
            finish(order[t - 1])
        if t == n_blocks:
            return
        if t == 0:
            for pos in range(min(SCORE_LEAD, n_blocks)):
                q_index = order[pos]
                q_aug = q_operand(qt_ref[q_index], q_index)
                m_run = None
                for j in range(q_index + 1):
                    m_run = score_q_block(q_aug, q_index, pos % N_SCORE_BUFS, j, m_run)
                m_ref[pos % N_SCORE_BUFS] = m_run
        cur = t % N_SCORE_BUFS
        q_cur = order[t]
        scoring = t + SCORE_LEAD < n_blocks
        q_ahead = order[t + SCORE_LEAD] if scoring else -1
        slot_ahead = (t + SCORE_LEAD) % N_SCORE_BUFS
        q_aug = q_operand(qt_ref[q_ahead], q_ahead) if scoring else None
        m_cur = m_ref[cur]
        m_run = None
        acc = jnp.zeros((V_HEAD_DIM + N_ONES_ROWS, 2 * blk), jnp.float32)
        for j in range(max(q_cur, q_ahead) + 1):
            if j <= q_ahead:
                m_run = score_q_block(q_aug, q_ahead, slot_ahead, j, m_run)
            if j <= q_cur:
                p = _bf16(jnp.exp(s_ref[cur, j * blk:(j + 1) * blk, :] - m_cur))
                vt_aug = jnp.concatenate([vt_ref[j], ones_blk], axis=0)
                acc = acc + jnp.dot(vt_aug, p, preferred_element_type=jnp.float32)
        if scoring:
            m_ref[slot_ahead] = m_run
        acc_ref[...] = acc

    def grid_step(first):
        def run():
            for t in range(first, min(first + q_per_step, n_blocks + 1)):
                sub_step(t)
        return run

    steps = [grid_step(first) for first in range(0, n_blocks + 1, q_per_step)]
    if len(steps) == 1:
        steps[0]()
    else:
        lax.switch(pl.program_id(2), steps)


def _alibi_slopes(n_heads):
    slopes = [2.0 ** (-8.0 * (i + 1) / n_heads) for i in range(n_heads)]
    assert all(math.frexp(s)[0] == 0.5 for s in slopes), slopes
    return slopes


def _key_position_columns(seq_len):
    j = np.arange(seq_len)
    pos = np.zeros((seq_len, N_POS_COLS), np.float32)
    pos[:, 0] = j % ATTN_BLOCK
    pos[:, 1] = j // ATTN_BLOCK
    pos[:, 2:4] = 1.0
    assert ATTN_BLOCK <= 256 and seq_len // ATTN_BLOCK <= 256
    return jnp.asarray(pos, jnp.bfloat16)


def _diff_attn(qt, k, vt, lambdas, subln_g):
    B, nblk, d_attn, blk = qt.shape
    S = k.shape[1]
    H = d_attn // V_HEAD_DIM
    pos = _key_position_columns(S)
    kernel = functools.partial(_diff_attn_kernel, slopes=tuple(_alibi_slopes(H)),
                               q_per_step=ATTN_Q_PER_STEP)
    n_steps = pl.cdiv(nblk + 1, ATTN_Q_PER_STEP)
    return pl.pallas_call(
        kernel,
        grid=(B, H, n_steps),
        in_specs=[
            pl.BlockSpec((None, nblk, V_HEAD_DIM, blk), lambda b, h, q: (b, 0, h, 0)),
            pl.BlockSpec((None, S, V_HEAD_DIM), lambda b, h, q: (b, 0, h)),
            pl.BlockSpec(pos.shape, lambda b, h, q: (0, 0)),
            pl.BlockSpec((None, nblk, V_HEAD_DIM, blk), lambda b, h, q: (b, 0, h, 0)),
            *[pl.BlockSpec(v.shape, lambda b, h, q: (0, 0)) for v in lambdas],
            pl.BlockSpec(subln_g.shape, lambda b, h, q: (0, 0)),
        ],
        out_specs=pl.BlockSpec((None, S, V_HEAD_DIM), lambda b, h, q: (b, 0, h)),
        out_shape=jax.ShapeDtypeStruct((B, S, d_attn), jnp.bfloat16),
        scratch_shapes=[pltpu.VMEM((N_SCORE_BUFS, S, 2 * blk), jnp.float32),
                        pltpu.VMEM((N_SCORE_BUFS, 1, 2 * blk), jnp.float32),
                        pltpu.VMEM((V_HEAD_DIM + N_ONES_ROWS, 2 * blk), jnp.float32)],
        compiler_params=pltpu.CompilerParams(
            dimension_semantics=("parallel", "parallel", "arbitrary"),
            vmem_limit_bytes=V7X_VMEM_LIMIT_BYTES),
        name="diff_attn",
    )(qt, k, pos, vt, *lambdas, subln_g)


def _out_mlp_kernel(x_ref, attn_ref, lru_ref, wo_hbm, gm_ref, wup_hbm, wdn_hbm, gf_ref, y_ref,
                    wo_ref, wup_ref, wdn_ref, sem):
    tm = x_ref.shape[0]
    d_ff = wup_ref.shape[1]
    n_chunks = d_ff // FF_CHUNK
    rows = [slice(g * tm // MLP_SPLIT, (g + 1) * tm // MLP_SPLIT) for g in range(MLP_SPLIT)]

    def ff(c):
        return pl.ds(c * FF_CHUNK, FF_CHUNK)

    sources = [wo_hbm]
    targets = [wo_ref]
    for c in range(n_chunks):
        sources += [wup_hbm.at[:, ff(c)], wdn_hbm.at[ff(c), :]]
        targets += [wup_ref.at[:, ff(c)], wdn_ref.at[ff(c), :]]
    assert len(sources) == sem.shape[0]

    def copy(i):
        return pltpu.make_async_copy(sources[i], targets[i], sem.at[i])

    def land(i):
        copy(i).wait()
        if i + W_PIECES_IN_FLIGHT < len(sources):
            copy(i + W_PIECES_IN_FLIGHT).start()

    def step_body(first_step):
        if first_step:
            for i in range(W_PIECES_IN_FLIGHT):
                copy(i).start()
            land(0)
        x1 = []
        for r in rows:
            mix = jnp.concatenate([attn_ref[r, :], lru_ref[r, :]], axis=-1)
            x1.append(x_ref[r, :] + jnp.dot(mix, wo_ref[...], preferred_element_type=jnp.float32))
        hm = [_bf16(_rmsnorm_rows(v, gm_ref[...])) for v in x1]
        mlp = [jnp.zeros_like(v) for v in x1]
        for c in range(n_chunks):
            cols = slice(c * FF_CHUNK, (c + 1) * FF_CHUNK)
            if first_step:
                land(1 + 2 * c)
            act = [_bf16(jnp.square(jnp.maximum(
                jnp.dot(hm[g], wup_ref[:, cols], preferred_element_type=jnp.float32), 0.0)))
                for g in range(MLP_SPLIT)]
            if first_step:
                land(2 + 2 * c)
            for g in range(MLP_SPLIT):
                mlp[g] = mlp[g] + jnp.dot(act[g], wdn_ref[cols, :],
                                          preferred_element_type=jnp.float32)
        for g, r in enumerate(rows):
            y_ref[r, :] = _rmsnorm_rows(x1[g] + mlp[g], gf_ref[...])

    pl.when(pl.program_id(0) == 0)(lambda: step_body(True))
    pl.when(pl.program_id(0) != 0)(lambda: step_body(False))


def _out_mlp(x, attn, lru, w_out, g_mlp, w_up, w_down, g_final):
    B, S, D = x.shape
    T = B * S
    tm = MLP_TOKENS
    x2 = x.reshape(T, D)
    attn2 = attn.reshape(T, attn.shape[-1])
    lru2 = lru.reshape(T, lru.shape[-1])
    const = lambda i: (0, 0)
    in_hbm = pl.BlockSpec(memory_space=pl.ANY)
    assert w_out.shape == (D, D) and w_up.shape[1] % FF_CHUNK == 0
    assert w_out.dtype == w_up.dtype == w_down.dtype == jnp.bfloat16
    y = pl.pallas_call(
        _out_mlp_kernel,
        grid=(T // tm,),
        in_specs=[
            pl.BlockSpec((tm, D), lambda i: (i, 0)),
            pl.BlockSpec((tm, attn2.shape[1]), lambda i: (i, 0)),
            pl.BlockSpec((tm, lru2.shape[1]), lambda i: (i, 0)),
            in_hbm,
            pl.BlockSpec(g_mlp.shape, const),
            in_hbm,
            in_hbm,
            pl.BlockSpec(g_final.shape, const),
        ],
        out_specs=pl.BlockSpec((tm, D), lambda i: (i, 0)),
        out_shape=jax.ShapeDtypeStruct((T, D), jnp.float32),
        scratch_shapes=[
            pltpu.VMEM(w_out.shape, jnp.bfloat16),
            pltpu.VMEM(w_up.shape, jnp.bfloat16),
            pltpu.VMEM(w_down.shape, jnp.bfloat16),
            pltpu.SemaphoreType.DMA((1 + 2 * (w_up.shape[1] // FF_CHUNK),)),
        ],
        compiler_params=pltpu.CompilerParams(
            dimension_semantics=("arbitrary",),
            vmem_limit_bytes=V7X_VMEM_LIMIT_BYTES),
        name="out_mlp",
    )(x2, attn2, lru2, w_out, g_mlp, w_up, w_down, g_final)
    return y.reshape(B, S, D)


def _pair_block_diag(w):
    n, c, d = w.shape
    rows = w.reshape(n // 2, 2 * c, d)
    tiled = jnp.concatenate([rows, rows], axis=-1)
    row_block = np.arange(2 * c)[:, None] // c
    col_block = np.arange(2 * d)[None, :] // d
    return jnp.where(jnp.asarray(row_block == col_block), tiled, 0.0)


def kernel(x, norm_mix_g, w_in, conv_w, conv_b, w_rg, b_rg, w_ig, b_ig, lru_L, lambda_q1, lambda_k1, lambda_q2, lambda_k2, subln_g, w_out, norm_mlp_g, w_up, w_down, final_g):
    B, S, D = x.shape
    d_lru = lru_L.shape[-1]
    d_attn = (w_in.shape[-1] - 2 * d_lru) // 3
    depth = w_in.shape[0]
    n_slab = d_lru // LANES
    assert depth == 1, "LAMBDA_INIT is specialised to a single layer"
    assert S % PROJ_TOKENS == 0 and (B * S) % MLP_TOKENS == 0
    assert PROJ_TOKENS % SUB_ROWS == 0 and d_lru % LANES == 0
    assert 2 * w_rg.shape[-1] == LANES and w_rg.shape[1] * w_rg.shape[2] == d_lru
    l = 0
    row = lambda v: v.reshape(1, -1)

    w_gates = _bf16(0.5 * jnp.concatenate([_pair_block_diag(w_rg[l]), _pair_block_diag(w_ig[l])],
                                          axis=-1))
    b_gates = 0.5 * jnp.concatenate([b_rg[l].reshape(n_slab, 1, LANES),
                                     b_ig[l].reshape(n_slab, 1, LANES)], axis=-1)
    lambdas = [row(v[l]) for v in (lambda_q1, lambda_k1, lambda_q2, lambda_k2)]

    qt, k, vt, lru, w_out_b, w_up_b, w_down_b = _in_proj_lru(
        x, row(norm_mix_g[l]), w_in[l], conv_w[l], row(conv_b[l]), w_gates, b_gates, row(lru_L[l]),
        later_weights=(w_out[l], w_up[l], w_down[l]))
    attn = _diff_attn(qt, k, vt, lambdas, row(subln_g[l]))
    return _out_mlp(x, attn, lru, w_out_b, row(norm_mlp_g[l]), w_up_b, w_down_b, row(final_g))
```
